```python
import math
import jax, jax.numpy as jnp
from jax import lax
import numpy as np

D_MODEL = 1024
BATCH = 8
SEQ = 8192
DEPTH = 4

N_MIXERS = 3
N_A = (DEPTH + 2) // 3
N_B = (DEPTH + 1) // 3
N_C = DEPTH // 3

HEAD_DIM = 64
N_HEADS = D_MODEL // HEAD_DIM
N_KV = 4
GROUP = N_HEADS // N_KV
WINDOW = 128
BLOCK = 128
QKV_DIM = (N_HEADS + 2 * N_KV) * HEAD_DIM

CONV_K = 31

CHUNK = 128
SGU_FFN = 4 * D_MODEL
SGU_HALF = SGU_FFN // 2
SGU_GROUPS = 8
SGU_GROUP_DIM = SGU_HALF // SGU_GROUPS

FFN_DIM = 2816
FFN_CONV_K = 3

NORM_EPS = 1e-6
NEG_INF = -1e30

kernel_name = "hybrid_swa_conformer_sgu_trunk"


def rmsnorm(x, g):
    xf = x.astype(jnp.float32)
    y = xf * lax.rsqrt(jnp.mean(xf * xf, axis=-1, keepdims=True) + NORM_EPS)
    return (y * g.astype(jnp.float32)).astype(x.dtype)


def layernorm(x, g, b):
    xf = x.astype(jnp.float32)
    mu = jnp.mean(xf, axis=-1, keepdims=True)
    xc = xf - mu
    var = jnp.mean(xc * xc, axis=-1, keepdims=True)
    y = xc * lax.rsqrt(var + NORM_EPS)
    return (y * g.astype(jnp.float32) + b.astype(jnp.float32)).astype(x.dtype)


def causal_dwconv(x, w, b):
    K, C = w.shape
    y = lax.conv_general_dilated(
        x, w[:, None, :].astype(x.dtype), window_strides=(1,), padding=[(K - 1, 0)],
        dimension_numbers=("NWC", "WIO", "NWC"), feature_group_count=C)
    return y + b


def alibi_slopes(n):
    return 2.0 ** (-8.0 * jnp.arange(1, n + 1, dtype=jnp.float32) / n)


def sliding_window_attention(h, wqkv, bqkv, sinks, wo, bo):
    B, T, _ = h.shape
    nb = T // BLOCK
    qkv = h @ wqkv + bqkv
    q, k, v = jnp.split(qkv, [N_HEADS * HEAD_DIM, (N_HEADS + N_KV) * HEAD_DIM], axis=-1)
    q = q.reshape(B, nb, BLOCK, N_KV, GROUP, HEAD_DIM)
    k = k.reshape(B, nb, BLOCK, N_KV, HEAD_DIM)
    v = v.reshape(B, nb, BLOCK, N_KV, HEAD_DIM)

    def with_prev(z):
        prev = jnp.pad(z, ((0, 0), (1, 0), (0, 0), (0, 0), (0, 0)))[:, :-1]
        return jnp.concatenate([prev, z], axis=2)

    k2, v2 = with_prev(k), with_prev(v)
    s = jnp.einsum("bnqkgd,bnskd->bnkgqs", q, k2).astype(jnp.float32) * (HEAD_DIM ** -0.5)

    qpos = jnp.arange(BLOCK) + BLOCK
    kpos = jnp.arange(2 * BLOCK)
    dist = qpos[:, None] - kpos[None, :]
    band = (dist >= 0) & (dist < WINDOW)
    blk = jnp.arange(nb)
    valid = band[None] & ~((blk[:, None, None] == 0) & (kpos[None, None, :] < BLOCK))

    slopes = alibi_slopes(N_HEADS).reshape(N_KV, GROUP)
    s = s - slopes[:, :, None, None] * dist.astype(jnp.float32)[None, None]
    s = jnp.where(valid[None, :, None, None], s, NEG_INF)

    sink = jnp.broadcast_to(sinks.astype(jnp.float32).reshape(N_KV, GROUP)[None, None, :, :, None, None],
                            s.shape[:-1] + (1,))
    p = jax.nn.softmax(jnp.concatenate([s, sink], axis=-1), axis=-1)[..., :-1]
    o = jnp.einsum("bnkgqs,bnskd->bnqkgd", p.astype(v2.dtype), v2).reshape(B, T, N_HEADS * HEAD_DIM)
    return o @ wo + bo


def conformer_conv(h, w_in, b_in, dw, dw_b, ln_g, ln_b, w_out, b_out):
    a, g = jnp.split(h @ w_in + b_in, 2, axis=-1)
    z = a * jax.nn.sigmoid(g)
    z = causal_dwconv(z, dw, dw_b)
    z = jax.nn.silu(layernorm(z, ln_g, ln_b))
    return z @ w_out + b_out


def chunked_sgu(h, w_in, b_in, ln_g, ln_b, ws, bs, w_out, b_out):
    B, T, _ = h.shape
    z = jax.nn.gelu(h @ w_in + b_in, approximate=False)
    u, v = jnp.split(z, 2, axis=-1)
    v = layernorm(v, ln_g, ln_b)
    v = v.reshape(B, T // CHUNK, CHUNK, SGU_GROUPS, SGU_GROUP_DIM)
    mask = jnp.tril(jnp.ones((CHUNK, CHUNK), dtype=bool))
    wm = jnp.where(mask[None], ws, jnp.zeros_like(ws))
    v = jnp.einsum("gts,bcsgd->bctgd", wm.astype(v.dtype), v) + bs.T[None, None, :, :, None]
    v = v.reshape(B, T, SGU_HALF)
    return (u * v) @ w_out + b_out


def conv_glu_ffn(h, w_in, dw, dw_b, w_out):
    z = causal_dwconv(h @ w_in, dw, dw_b)
    g, u = jnp.split(z, 2, axis=-1)
    return (jax.nn.silu(g) * u) @ w_out


def _fwd_setup_inputs(seed: int = 0) -> dict:
    key = jax.random.key(seed)
    ks = jax.random.split(key, 40)
    D = D_MODEL

    def nrm(k, shape, scale):
        return jax.random.normal(k, shape, jnp.float32) * scale

    return {
        "x": nrm(ks[0], (BATCH, SEQ, D), 1.0),
        "c": nrm(ks[1], (BATCH, D), 1.0),
        "norm1_g": 1.0 + nrm(ks[2], (DEPTH, D), 0.05),
        "norm2_g": 1.0 + nrm(ks[3], (DEPTH, D), 0.05),
        "ada_w": nrm(ks[4], (DEPTH, D, 6 * D), 0.5 * D ** -0.5),
        "ada_b": nrm(ks[5], (DEPTH, 6 * D), 0.01),
        "attn_wqkv": nrm(ks[6], (N_A, D, QKV_DIM), D ** -0.5),
        "attn_bqkv": nrm(ks[7], (N_A, QKV_DIM), 0.01),
        "attn_sinks": nrm(ks[8], (N_A, N_HEADS), 0.5),
        "attn_wo": nrm(ks[9], (N_A, N_HEADS * HEAD_DIM, D), (N_HEADS * HEAD_DIM) ** -0.5),
        "attn_bo": nrm(ks[10], (N_A, D), 0.01),
        "conv_w_in": nrm(ks[11], (N_B, D, 2 * D), D ** -0.5),
        "conv_b_in": nrm(ks[12], (N_B, 2 * D), 0.01),
        "conv_dw": nrm(ks[13], (N_B, CONV_K, D), CONV_K ** -0.5),
        "conv_dw_b": nrm(ks[14], (N_B, D), 0.01),
        "conv_ln_g": 1.0 + nrm(ks[15], (N_B, D), 0.05),
        "conv_ln_b": nrm(ks[16], (N_B, D), 0.01),
        "conv_w_out": nrm(ks[17], (N_B, D, D), D ** -0.5),
        "conv_b_out": nrm(ks[18], (N_B, D), 0.01),
        "sgu_w_in": nrm(ks[19], (N_C, D, SGU_FFN), D ** -0.5),
        "sgu_b_in": nrm(ks[20], (N_C, SGU_FFN), 0.01),
        "sgu_ln_g": 1.0 + nrm(ks[21], (N_C, SGU_HALF), 0.05),
        "sgu_ln_b": nrm(ks[22], (N_C, SGU_HALF), 0.01),
        "sgu_ws": nrm(ks[23], (N_C, SGU_GROUPS, CHUNK, CHUNK), CHUNK ** -0.5),
        "sgu_bs": 1.0 + nrm(ks[24], (N_C, SGU_GROUPS, CHUNK), 0.01),
        "sgu_w_out": nrm(ks[25], (N_C, SGU_HALF, D), SGU_HALF ** -0.5),
        "sgu_b_out": nrm(ks[26], (N_C, D), 0.01),
        "ffn_w_in": nrm(ks[27], (DEPTH, D, 2 * FFN_DIM), D ** -0.5),
        "ffn_dw": nrm(ks[28], (DEPTH, FFN_CONV_K, 2 * FFN_DIM), FFN_CONV_K ** -0.5),
        "ffn_dw_b": nrm(ks[29], (DEPTH, 2 * FFN_DIM), 0.01),
        "ffn_w_out": nrm(ks[30], (DEPTH, FFN_DIM, D), FFN_DIM ** -0.5),
        "final_g": 1.0 + nrm(ks[31], (D,), 0.05),
    }


def _fwd_reference(x, c, norm1_g, norm2_g, ada_w, ada_b,
              attn_wqkv, attn_bqkv, attn_sinks, attn_wo, attn_bo,
              conv_w_in, conv_b_in, conv_dw, conv_dw_b, conv_ln_g, conv_ln_b, conv_w_out, conv_b_out,
              sgu_w_in, sgu_b_in, sgu_ln_g, sgu_ln_b, sgu_ws, sgu_bs, sgu_w_out, sgu_b_out,
              ffn_w_in, ffn_dw, ffn_dw_b, ffn_w_out, final_g):
    c_act = jax.nn.silu(c)
    for i in range(DEPTH):
        mod = (c_act @ ada_w[i] + ada_b[i])[:, None, :]
        sh1, sc1, g1, sh2, sc2, g2 = jnp.split(mod, 6, axis=-1)

        h = rmsnorm(x, norm1_g[i]) * (1.0 + sc1) + sh1
        kind, j = i % N_MIXERS, i // N_MIXERS
        if kind == 0:
            y = sliding_window_attention(h, attn_wqkv[j], attn_bqkv[j], attn_sinks[j], attn_wo[j], attn_bo[j])
        elif kind == 1:
            y = conformer_conv(h, conv_w_in[j], conv_b_in[j], conv_dw[j], conv_dw_b[j],
                               conv_ln_g[j], conv_ln_b[j], conv_w_out[j], conv_b_out[j])
        else:
            y = chunked_sgu(h, sgu_w_in[j], sgu_b_in[j], sgu_ln_g[j], sgu_ln_b[j],
                            sgu_ws[j], sgu_bs[j], sgu_w_out[j], sgu_b_out[j])
        x = x + g1 * y

        h = rmsnorm(x, norm2_g[i]) * (1.0 + sc2) + sh2
        x = x + g2 * conv_glu_ffn(h, ffn_w_in[i], ffn_dw[i], ffn_dw_b[i], ffn_w_out[i])
    return rmsnorm(x, final_g)


import jax as _jax
import jax.numpy as _jnp

TWIN_FORMAT = 'train_step'
FWD_PARAMS = ['x', 'c', 'norm1_g', 'norm2_g', 'ada_w', 'ada_b', 'attn_wqkv', 'attn_bqkv', 'attn_sinks', 'attn_wo', 'attn_bo', 'conv_w_in', 'conv_b_in', 'conv_dw', 'conv_dw_b', 'conv_ln_g', 'conv_ln_b', 'conv_w_out', 'conv_b_out', 'sgu_w_in', 'sgu_b_in', 'sgu_ln_g', 'sgu_ln_b', 'sgu_ws', 'sgu_bs', 'sgu_w_out', 'sgu_b_out', 'ffn_w_in', 'ffn_dw', 'ffn_dw_b', 'ffn_w_out', 'final_g']
TWIN_WEIGHTS = ['norm1_g', 'norm2_g', 'ada_w', 'ada_b', 'attn_wqkv', 'attn_bqkv', 'attn_sinks', 'attn_wo', 'attn_bo', 'conv_w_in', 'conv_b_in', 'conv_dw', 'conv_dw_b', 'conv_ln_g', 'conv_ln_b', 'conv_w_out', 'conv_b_out', 'sgu_w_in', 'sgu_b_in', 'sgu_ln_g', 'sgu_ln_b', 'sgu_ws', 'sgu_bs', 'sgu_w_out', 'sgu_b_out', 'ffn_w_in', 'ffn_dw', 'ffn_dw_b', 'ffn_w_out', 'final_g']
TWIN_DIFF_INPUT = 'x'
TWIN_INPUTS = ['x', 'c', 'norm1_g', 'norm2_g', 'ada_w', 'ada_b', 'attn_wqkv', 'attn_bqkv', 'attn_sinks', 'attn_wo', 'attn_bo', 'conv_w_in', 'conv_b_in', 'conv_dw', 'conv_dw_b', 'conv_ln_g', 'conv_ln_b', 'conv_w_out', 'conv_b_out', 'sgu_w_in', 'sgu_b_in', 'sgu_ln_g', 'sgu_ln_b', 'sgu_ws', 'sgu_bs', 'sgu_w_out', 'sgu_b_out', 'ffn_w_in', 'ffn_dw', 'ffn_dw_b', 'ffn_w_out', 'final_g', 'loss_target', 'm_norm1_g', 'm_norm2_g', 'm_ada_w', 'm_ada_b', 'm_attn_wqkv', 'm_attn_bqkv', 'm_attn_sinks', 'm_attn_wo', 'm_attn_bo', 'm_conv_w_in', 'm_conv_b_in', 'm_conv_dw', 'm_conv_dw_b', 'm_conv_ln_g', 'm_conv_ln_b', 'm_conv_w_out', 'm_conv_b_out', 'm_sgu_w_in', 'm_sgu_b_in', 'm_sgu_ln_g', 'm_sgu_ln_b', 'm_sgu_ws', 'm_sgu_bs', 'm_sgu_w_out', 'm_sgu_b_out', 'm_ffn_w_in', 'm_ffn_dw', 'm_ffn_dw_b', 'm_ffn_w_out', 'm_final_g', 'v_norm1_g', 'v_norm2_g', 'v_ada_w', 'v_ada_b', 'v_attn_wqkv', 'v_attn_bqkv', 'v_attn_sinks', 'v_attn_wo', 'v_attn_bo', 'v_conv_w_in', 'v_conv_b_in', 'v_conv_dw', 'v_conv_dw_b', 'v_conv_ln_g', 'v_conv_ln_b', 'v_conv_w_out', 'v_conv_b_out', 'v_sgu_w_in', 'v_sgu_b_in', 'v_sgu_ln_g', 'v_sgu_ln_b', 'v_sgu_ws', 'v_sgu_bs', 'v_sgu_w_out', 'v_sgu_b_out', 'v_ffn_w_in', 'v_ffn_dw', 'v_ffn_dw_b', 'v_ffn_w_out', 'v_final_g']
TWIN_OUTPUTS = ['loss', 'grad_x', 'grad_norm1_g', 'grad_norm2_g', 'grad_ada_w', 'grad_ada_b', 'grad_attn_wqkv', 'grad_attn_bqkv', 'grad_attn_sinks', 'grad_attn_wo', 'grad_attn_bo', 'grad_conv_w_in', 'grad_conv_b_in', 'grad_conv_dw', 'grad_conv_dw_b', 'grad_conv_ln_g', 'grad_conv_ln_b', 'grad_conv_w_out', 'grad_conv_b_out', 'grad_sgu_w_in', 'grad_sgu_b_in', 'grad_sgu_ln_g', 'grad_sgu_ln_b', 'grad_sgu_ws', 'grad_sgu_bs', 'grad_sgu_w_out', 'grad_sgu_b_out', 'grad_ffn_w_in', 'grad_ffn_dw', 'grad_ffn_dw_b', 'grad_ffn_w_out', 'grad_final_g', 'delta_norm1_g', 'delta_norm2_g', 'delta_ada_w', 'delta_ada_b', 'delta_attn_wqkv', 'delta_attn_bqkv', 'delta_attn_sinks', 'delta_attn_wo', 'delta_attn_bo', 'delta_conv_w_in', 'delta_conv_b_in', 'delta_conv_dw', 'delta_conv_dw_b', 'delta_conv_ln_g', 'delta_conv_ln_b', 'delta_conv_w_out', 'delta_conv_b_out', 'delta_sgu_w_in', 'delta_sgu_b_in', 'delta_sgu_ln_g', 'delta_sgu_ln_b', 'delta_sgu_ws', 'delta_sgu_bs', 'delta_sgu_w_out', 'delta_sgu_b_out', 'delta_ffn_w_in', 'delta_ffn_dw', 'delta_ffn_dw_b', 'delta_ffn_w_out', 'delta_final_g', 'new_m_norm1_g', 'new_m_norm2_g', 'new_m_ada_w', 'new_m_ada_b', 'new_m_attn_wqkv', 'new_m_attn_bqkv', 'new_m_attn_sinks', 'new_m_attn_wo', 'new_m_attn_bo', 'new_m_conv_w_in', 'new_m_conv_b_in', 'new_m_conv_dw', 'new_m_conv_dw_b', 'new_m_conv_ln_g', 'new_m_conv_ln_b', 'new_m_conv_w_out', 'new_m_conv_b_out', 'new_m_sgu_w_in', 'new_m_sgu_b_in', 'new_m_sgu_ln_g', 'new_m_sgu_ln_b', 'new_m_sgu_ws', 'new_m_sgu_bs', 'new_m_sgu_w_out', 'new_m_sgu_b_out', 'new_m_ffn_w_in', 'new_m_ffn_dw', 'new_m_ffn_dw_b', 'new_m_ffn_w_out', 'new_m_final_g', 'new_v_norm1_g', 'new_v_norm2_g', 'new_v_ada_w', 'new_v_ada_b', 'new_v_attn_wqkv', 'new_v_attn_bqkv', 'new_v_attn_sinks', 'new_v_attn_wo', 'new_v_attn_bo', 'new_v_conv_w_in', 'new_v_conv_b_in', 'new_v_conv_dw', 'new_v_conv_dw_b', 'new_v_conv_ln_g', 'new_v_conv_ln_b', 'new_v_conv_w_out', 'new_v_conv_b_out', 'new_v_sgu_w_in', 'new_v_sgu_b_in', 'new_v_sgu_ln_g', 'new_v_sgu_ln_b', 'new_v_sgu_ws', 'new_v_sgu_bs', 'new_v_sgu_w_out', 'new_v_sgu_b_out', 'new_v_ffn_w_in', 'new_v_ffn_dw', 'new_v_ffn_dw_b', 'new_v_ffn_w_out', 'new_v_final_g']
TWIN_LEAF_KINDS = {'loss': 'loss', 'grad_x': 'grad_x', 'grad_norm1_g': 'grad_w', 'grad_norm2_g': 'grad_w', 'grad_ada_w': 'grad_w', 'grad_ada_b': 'grad_w', 'grad_attn_wqkv': 'grad_w', 'grad_attn_bqkv': 'grad_w', 'grad_attn_sinks': 'grad_w', 'grad_attn_wo': 'grad_w', 'grad_attn_bo': 'grad_w', 'grad_conv_w_in': 'grad_w', 'grad_conv_b_in': 'grad_w', 'grad_conv_dw': 'grad_w', 'grad_conv_dw_b': 'grad_w', 'grad_conv_ln_g': 'grad_w', 'grad_conv_ln_b': 'grad_w', 'grad_conv_w_out': 'grad_w', 'grad_conv_b_out': 'grad_w', 'grad_sgu_w_in': 'grad_w', 'grad_sgu_b_in': 'grad_w', 'grad_sgu_ln_g': 'grad_w', 'grad_sgu_ln_b': 'grad_w', 'grad_sgu_ws': 'grad_w', 'grad_sgu_bs': 'grad_w', 'grad_sgu_w_out': 'grad_w', 'grad_sgu_b_out': 'grad_w', 'grad_ffn_w_in': 'grad_w', 'grad_ffn_dw': 'grad_w', 'grad_ffn_dw_b': 'grad_w', 'grad_ffn_w_out': 'grad_w', 'grad_final_g': 'grad_w', 'delta_norm1_g': 'delta_w', 'delta_norm2_g': 'delta_w', 'delta_ada_w': 'delta_w', 'delta_ada_b': 'delta_w', 'delta_attn_wqkv': 'delta_w', 'delta_attn_bqkv': 'delta_w', 'delta_attn_sinks': 'delta_w', 'delta_attn_wo': 'delta_w', 'delta_attn_bo': 'delta_w', 'delta_conv_w_in': 'delta_w', 'delta_conv_b_in': 'delta_w', 'delta_conv_dw': 'delta_w', 'delta_conv_dw_b': 'delta_w', 'delta_conv_ln_g': 'delta_w', 'delta_conv_ln_b': 'delta_w', 'delta_conv_w_out': 'delta_w', 'delta_conv_b_out': 'delta_w', 'delta_sgu_w_in': 'delta_w', 'delta_sgu_b_in': 'delta_w', 'delta_sgu_ln_g': 'delta_w', 'delta_sgu_ln_b': 'delta_w', 'delta_sgu_ws': 'delta_w', 'delta_sgu_bs': 'delta_w', 'delta_sgu_w_out': 'delta_w', 'delta_sgu_b_out': 'delta_w', 'delta_ffn_w_in': 'delta_w', 'delta_ffn_dw': 'delta_w', 'delta_ffn_dw_b': 'delta_w', 'delta_ffn_w_out': 'delta_w', 'delta_final_g': 'delta_w', 'new_m_norm1_g': 'new_m', 'new_m_norm2_g': 'new_m', 'new_m_ada_w': 'new_m', 'new_m_ada_b': 'new_m', 'new_m_attn_wqkv': 'new_m', 'new_m_attn_bqkv': 'new_m', 'new_m_attn_sinks': 'new_m', 'new_m_attn_wo': 'new_m', 'new_m_attn_bo': 'new_m', 'new_m_conv_w_in': 'new_m', 'new_m_conv_b_in': 'new_m', 'new_m_conv_dw': 'new_m', 'new_m_conv_dw_b': 'new_m', 'new_m_conv_ln_g': 'new_m', 'new_m_conv_ln_b': 'new_m', 'new_m_conv_w_out': 'new_m', 'new_m_conv_b_out': 'new_m', 'new_m_sgu_w_in': 'new_m', 'new_m_sgu_b_in': 'new_m', 'new_m_sgu_ln_g': 'new_m', 'new_m_sgu_ln_b': 'new_m', 'new_m_sgu_ws': 'new_m', 'new_m_sgu_bs': 'new_m', 'new_m_sgu_w_out': 'new_m', 'new_m_sgu_b_out': 'new_m', 'new_m_ffn_w_in': 'new_m', 'new_m_ffn_dw': 'new_m', 'new_m_ffn_dw_b': 'new_m', 'new_m_ffn_w_out': 'new_m', 'new_m_final_g': 'new_m', 'new_v_norm1_g': 'new_v', 'new_v_norm2_g': 'new_v', 'new_v_ada_w': 'new_v', 'new_v_ada_b': 'new_v', 'new_v_attn_wqkv': 'new_v', 'new_v_attn_bqkv': 'new_v', 'new_v_attn_sinks': 'new_v', 'new_v_attn_wo': 'new_v', 'new_v_attn_bo': 'new_v', 'new_v_conv_w_in': 'new_v', 'new_v_conv_b_in': 'new_v', 'new_v_conv_dw': 'new_v', 'new_v_conv_dw_b': 'new_v', 'new_v_conv_ln_g': 'new_v', 'new_v_conv_ln_b': 'new_v', 'new_v_conv_w_out': 'new_v', 'new_v_conv_b_out': 'new_v', 'new_v_sgu_w_in': 'new_v', 'new_v_sgu_b_in': 'new_v', 'new_v_sgu_ln_g': 'new_v', 'new_v_sgu_ln_b': 'new_v', 'new_v_sgu_ws': 'new_v', 'new_v_sgu_bs': 'new_v', 'new_v_sgu_w_out': 'new_v', 'new_v_sgu_b_out': 'new_v', 'new_v_ffn_w_in': 'new_v', 'new_v_ffn_dw': 'new_v', 'new_v_ffn_dw_b': 'new_v', 'new_v_ffn_w_out': 'new_v', 'new_v_final_g': 'new_v'}


def _forward(args):
    return _fwd_reference(*[args[k] for k in FWD_PARAMS])


def _output_shape():
    def fwd():
        inp = _fwd_setup_inputs(0)
        return _fwd_reference(*[inp[k] for k in FWD_PARAMS])
    out = _jax.eval_shape(fwd)
    return out.shape, out.dtype

N_MICROBATCH = 1
ADAM_LR = 0.001
ADAM_B1 = 0.9
ADAM_B2 = 0.999
ADAM_EPS = 1e-08
ADAM_WD = 0.01
ADAM_STEP = 10
PER_EXAMPLE_BATCH_AXIS = {'x': 0, 'c': 0, 'loss_target': 0}
SHARED_INPUTS = []
_WEIGHT_DTYPES = {'norm1_g': _jnp.float32, 'norm2_g': _jnp.float32, 'ada_w': _jnp.float32, 'ada_b': _jnp.float32, 'attn_wqkv': _jnp.float32, 'attn_bqkv': _jnp.float32, 'attn_sinks': _jnp.float32, 'attn_wo': _jnp.float32, 'attn_bo': _jnp.float32, 'conv_w_in': _jnp.float32, 'conv_b_in': _jnp.float32, 'conv_dw': _jnp.float32, 'conv_dw_b': _jnp.float32, 'conv_ln_g': _jnp.float32, 'conv_ln_b': _jnp.float32, 'conv_w_out': _jnp.float32, 'conv_b_out': _jnp.float32, 'sgu_w_in': _jnp.float32, 'sgu_b_in': _jnp.float32, 'sgu_ln_g': _jnp.float32, 'sgu_ln_b': _jnp.float32, 'sgu_ws': _jnp.float32, 'sgu_bs': _jnp.float32, 'sgu_w_out': _jnp.float32, 'sgu_b_out': _jnp.float32, 'ffn_w_in': _jnp.float32, 'ffn_dw': _jnp.float32, 'ffn_dw_b': _jnp.float32, 'ffn_w_out': _jnp.float32, 'final_g': _jnp.float32}
MOMENT_SCALE = {'norm1_g': 5.321346e-02, 'norm2_g': 7.286024e-02, 'ada_w': 8.630748e-02, 'ada_b': 1.583902e-01, 'attn_wqkv': 3.950415e-02, 'attn_bqkv': 9.764315e-02, 'attn_sinks': 3.710070e-02, 'attn_wo': 3.960718e-02, 'attn_bo': 1.146547e-01, 'conv_w_in': 3.750759e-02, 'conv_b_in': 4.550446e-02, 'conv_dw': 4.906584e-02, 'conv_dw_b': 1.006462e-01, 'conv_ln_g': 6.854406e-02, 'conv_ln_b': 6.852052e-02, 'conv_w_out': 5.074144e-02, 'conv_b_out': 1.209428e-01, 'sgu_w_in': 3.974892e-02, 'sgu_b_in': 4.416667e-02, 'sgu_ln_g': 2.618914e-02, 'sgu_ln_b': 2.690569e-02, 'sgu_ws': 3.700726e-02, 'sgu_bs': 5.339789e-02, 'sgu_w_out': 7.317012e-02, 'sgu_b_out': 1.409243e-01, 'ffn_w_in': 3.272761e-02, 'ffn_dw': 3.301839e-02, 'ffn_dw_b': 3.128723e-02, 'ffn_w_out': 5.397822e-02, 'final_g': 6.419644e+01}


def _to_microbatches(a, axis):
    t = _jnp.moveaxis(a, axis, 0)
    t = t.reshape((N_MICROBATCH, t.shape[0] // N_MICROBATCH) + t.shape[1:])
    return _jnp.moveaxis(t, 1, axis + 1)


def setup_inputs(seed: int = 0) -> dict:
    inp = _fwd_setup_inputs(seed)
    key = _jax.random.fold_in(_jax.random.key(seed), 7919)
    shape, _ = _output_shape()
    out = dict(inp)
    out["loss_target"] = _jax.random.normal(_jax.random.fold_in(key, 0), shape, _jnp.float32)
    for i, name in enumerate(TWIN_WEIGHTS):
        w = inp[name].astype(_jnp.float32)
        if MOMENT_SCALE is None:
            s = _jnp.sqrt(_jnp.mean(_jnp.square(w)) + 1e-30)
        else:
            s = MOMENT_SCALE[name]
        km, kv = _jax.random.split(_jax.random.fold_in(key, i + 1))
        out[name] = w
        out["m_" + name] = s * _jax.random.normal(km, w.shape, _jnp.float32)
        out["v_" + name] = (s * s) * _jax.random.uniform(kv, w.shape, _jnp.float32, 0.5, 1.5)
    if N_MICROBATCH > 1:
        for name, axis in PER_EXAMPLE_BATCH_AXIS.items():
            out[name] = _to_microbatches(out[name], axis)
    return {'x': out['x'], 'c': out['c'], 'norm1_g': out['norm1_g'], 'norm2_g': out['norm2_g'], 'ada_w': out['ada_w'], 'ada_b': out['ada_b'], 'attn_wqkv': out['attn_wqkv'], 'attn_bqkv': out['attn_bqkv'], 'attn_sinks': out['attn_sinks'], 'attn_wo': out['attn_wo'], 'attn_bo': out['attn_bo'], 'conv_w_in': out['conv_w_in'], 'conv_b_in': out['conv_b_in'], 'conv_dw': out['conv_dw'], 'conv_dw_b': out['conv_dw_b'], 'conv_ln_g': out['conv_ln_g'], 'conv_ln_b': out['conv_ln_b'], 'conv_w_out': out['conv_w_out'], 'conv_b_out': out['conv_b_out'], 'sgu_w_in': out['sgu_w_in'], 'sgu_b_in': out['sgu_b_in'], 'sgu_ln_g': out['sgu_ln_g'], 'sgu_ln_b': out['sgu_ln_b'], 'sgu_ws': out['sgu_ws'], 'sgu_bs': out['sgu_bs'], 'sgu_w_out': out['sgu_w_out'], 'sgu_b_out': out['sgu_b_out'], 'ffn_w_in': out['ffn_w_in'], 'ffn_dw': out['ffn_dw'], 'ffn_dw_b': out['ffn_dw_b'], 'ffn_w_out': out['ffn_w_out'], 'final_g': out['final_g'], 'loss_target': out['loss_target'], 'm_norm1_g': out['m_norm1_g'], 'm_norm2_g': out['m_norm2_g'], 'm_ada_w': out['m_ada_w'], 'm_ada_b': out['m_ada_b'], 'm_attn_wqkv': out['m_attn_wqkv'], 'm_attn_bqkv': out['m_attn_bqkv'], 'm_attn_sinks': out['m_attn_sinks'], 'm_attn_wo': out['m_attn_wo'], 'm_attn_bo': out['m_attn_bo'], 'm_conv_w_in': out['m_conv_w_in'], 'm_conv_b_in': out['m_conv_b_in'], 'm_conv_dw': out['m_conv_dw'], 'm_conv_dw_b': out['m_conv_dw_b'], 'm_conv_ln_g': out['m_conv_ln_g'], 'm_conv_ln_b': out['m_conv_ln_b'], 'm_conv_w_out': out['m_conv_w_out'], 'm_conv_b_out': out['m_conv_b_out'], 'm_sgu_w_in': out['m_sgu_w_in'], 'm_sgu_b_in': out['m_sgu_b_in'], 'm_sgu_ln_g': out['m_sgu_ln_g'], 'm_sgu_ln_b': out['m_sgu_ln_b'], 'm_sgu_ws': out['m_sgu_ws'], 'm_sgu_bs': out['m_sgu_bs'], 'm_sgu_w_out': out['m_sgu_w_out'], 'm_sgu_b_out': out['m_sgu_b_out'], 'm_ffn_w_in': out['m_ffn_w_in'], 'm_ffn_dw': out['m_ffn_dw'], 'm_ffn_dw_b': out['m_ffn_dw_b'], 'm_ffn_w_out': out['m_ffn_w_out'], 'm_final_g': out['m_final_g'], 'v_norm1_g': out['v_norm1_g'], 'v_norm2_g': out['v_norm2_g'], 'v_ada_w': out['v_ada_w'], 'v_ada_b': out['v_ada_b'], 'v_attn_wqkv': out['v_attn_wqkv'], 'v_attn_bqkv': out['v_attn_bqkv'], 'v_attn_sinks': out['v_attn_sinks'], 'v_attn_wo': out['v_attn_wo'], 'v_attn_bo': out['v_attn_bo'], 'v_conv_w_in': out['v_conv_w_in'], 'v_conv_b_in': out['v_conv_b_in'], 'v_conv_dw': out['v_conv_dw'], 'v_conv_dw_b': out['v_conv_dw_b'], 'v_conv_ln_g': out['v_conv_ln_g'], 'v_conv_ln_b': out['v_conv_ln_b'], 'v_conv_w_out': out['v_conv_w_out'], 'v_conv_b_out': out['v_conv_b_out'], 'v_sgu_w_in': out['v_sgu_w_in'], 'v_sgu_b_in': out['v_sgu_b_in'], 'v_sgu_ln_g': out['v_sgu_ln_g'], 'v_sgu_ln_b': out['v_sgu_ln_b'], 'v_sgu_ws': out['v_sgu_ws'], 'v_sgu_bs': out['v_sgu_bs'], 'v_sgu_w_out': out['v_sgu_w_out'], 'v_sgu_b_out': out['v_sgu_b_out'], 'v_ffn_w_in': out['v_ffn_w_in'], 'v_ffn_dw': out['v_ffn_dw'], 'v_ffn_dw_b': out['v_ffn_dw_b'], 'v_ffn_w_out': out['v_ffn_w_out'], 'v_final_g': out['v_final_g']}


def _loss(weights, diff, rest, loss_target):
    with _jax.named_scope("forward"):
        args = {**rest, TWIN_DIFF_INPUT: diff, **{k: w.astype(_WEIGHT_DTYPES[k]) for k, w in weights.items()}}
        y = _forward(args)
    with _jax.named_scope("loss_head"):
        err = _jnp.square(y.astype(_jnp.float32) - loss_target)
        return 0.5 * _jnp.sum(_jnp.mean(err, axis=-1)) if err.ndim else 0.5 * err


def _adamw(w, g, m, v):
    m = ADAM_B1 * m + (1.0 - ADAM_B1) * g
    v = ADAM_B2 * v + (1.0 - ADAM_B2) * _jnp.square(g)
    m_hat = m / (1.0 - ADAM_B1 ** ADAM_STEP)
    v_hat = v / (1.0 - ADAM_B2 ** ADAM_STEP)
    delta = -ADAM_LR * (m_hat / (_jnp.sqrt(v_hat) + ADAM_EPS) + ADAM_WD * w)
    return delta, m, v


def reference(x, c, norm1_g, norm2_g, ada_w, ada_b, attn_wqkv, attn_bqkv, attn_sinks, attn_wo, attn_bo, conv_w_in, conv_b_in, conv_dw, conv_dw_b, conv_ln_g, conv_ln_b, conv_w_out, conv_b_out, sgu_w_in, sgu_b_in, sgu_ln_g, sgu_ln_b, sgu_ws, sgu_bs, sgu_w_out, sgu_b_out, ffn_w_in, ffn_dw, ffn_dw_b, ffn_w_out, final_g, loss_target, m_norm1_g, m_norm2_g, m_ada_w, m_ada_b, m_attn_wqkv, m_attn_bqkv, m_attn_sinks, m_attn_wo, m_attn_bo, m_conv_w_in, m_conv_b_in, m_conv_dw, m_conv_dw_b, m_conv_ln_g, m_conv_ln_b, m_conv_w_out, m_conv_b_out, m_sgu_w_in, m_sgu_b_in, m_sgu_ln_g, m_sgu_ln_b, m_sgu_ws, m_sgu_bs, m_sgu_w_out, m_sgu_b_out, m_ffn_w_in, m_ffn_dw, m_ffn_dw_b, m_ffn_w_out, m_final_g, v_norm1_g, v_norm2_g, v_ada_w, v_ada_b, v_attn_wqkv, v_attn_bqkv, v_attn_sinks, v_attn_wo, v_attn_bo, v_conv_w_in, v_conv_b_in, v_conv_dw, v_conv_dw_b, v_conv_ln_g, v_conv_ln_b, v_conv_w_out, v_conv_b_out, v_sgu_w_in, v_sgu_b_in, v_sgu_ln_g, v_sgu_ln_b, v_sgu_ws, v_sgu_bs, v_sgu_w_out, v_sgu_b_out, v_ffn_w_in, v_ffn_dw, v_ffn_dw_b, v_ffn_w_out, v_final_g):
    given = dict(x=x, c=c, norm1_g=norm1_g, norm2_g=norm2_g, ada_w=ada_w, ada_b=ada_b, attn_wqkv=attn_wqkv, attn_bqkv=attn_bqkv, attn_sinks=attn_sinks, attn_wo=attn_wo, attn_bo=attn_bo, conv_w_in=conv_w_in, conv_b_in=conv_b_in, conv_dw=conv_dw, conv_dw_b=conv_dw_b, conv_ln_g=conv_ln_g, conv_ln_b=conv_ln_b, conv_w_out=conv_w_out, conv_b_out=conv_b_out, sgu_w_in=sgu_w_in, sgu_b_in=sgu_b_in, sgu_ln_g=sgu_ln_g, sgu_ln_b=sgu_ln_b, sgu_ws=sgu_ws, sgu_bs=sgu_bs, sgu_w_out=sgu_w_out, sgu_b_out=sgu_b_out, ffn_w_in=ffn_w_in, ffn_dw=ffn_dw, ffn_dw_b=ffn_dw_b, ffn_w_out=ffn_w_out, final_g=final_g, loss_target=loss_target, m_norm1_g=m_norm1_g, m_norm2_g=m_norm2_g, m_ada_w=m_ada_w, m_ada_b=m_ada_b, m_attn_wqkv=m_attn_wqkv, m_attn_bqkv=m_attn_bqkv, m_attn_sinks=m_attn_sinks, m_attn_wo=m_attn_wo, m_attn_bo=m_attn_bo, m_conv_w_in=m_conv_w_in, m_conv_b_in=m_conv_b_in, m_conv_dw=m_conv_dw, m_conv_dw_b=m_conv_dw_b, m_conv_ln_g=m_conv_ln_g, m_conv_ln_b=m_conv_ln_b, m_conv_w_out=m_conv_w_out, m_conv_b_out=m_conv_b_out, m_sgu_w_in=m_sgu_w_in, m_sgu_b_in=m_sgu_b_in, m_sgu_ln_g=m_sgu_ln_g, m_sgu_ln_b=m_sgu_ln_b, m_sgu_ws=m_sgu_ws, m_sgu_bs=m_sgu_bs, m_sgu_w_out=m_sgu_w_out, m_sgu_b_out=m_sgu_b_out, m_ffn_w_in=m_ffn_w_in, m_ffn_dw=m_ffn_dw, m_ffn_dw_b=m_ffn_dw_b, m_ffn_w_out=m_ffn_w_out, m_final_g=m_final_g, v_norm1_g=v_norm1_g, v_norm2_g=v_norm2_g, v_ada_w=v_ada_w, v_ada_b=v_ada_b, v_attn_wqkv=v_attn_wqkv, v_attn_bqkv=v_attn_bqkv, v_attn_sinks=v_attn_sinks, v_attn_wo=v_attn_wo, v_attn_bo=v_attn_bo, v_conv_w_in=v_conv_w_in, v_conv_b_in=v_conv_b_in, v_conv_dw=v_conv_dw, v_conv_dw_b=v_conv_dw_b, v_conv_ln_g=v_conv_ln_g, v_conv_ln_b=v_conv_ln_b, v_conv_w_out=v_conv_w_out, v_conv_b_out=v_conv_b_out, v_sgu_w_in=v_sgu_w_in, v_sgu_b_in=v_sgu_b_in, v_sgu_ln_g=v_sgu_ln_g, v_sgu_ln_b=v_sgu_ln_b, v_sgu_ws=v_sgu_ws, v_sgu_bs=v_sgu_bs, v_sgu_w_out=v_sgu_w_out, v_sgu_b_out=v_sgu_b_out, v_ffn_w_in=v_ffn_w_in, v_ffn_dw=v_ffn_dw, v_ffn_dw_b=v_ffn_dw_b, v_ffn_w_out=v_ffn_w_out, v_final_g=v_final_g)
    weights = {n: given[n] for n in TWIN_WEIGHTS}
    shared = {n: given[n] for n in SHARED_INPUTS}
    per_example = {n: given[n] for n in ['x', 'c']}
    grad_fn = _jax.value_and_grad(_loss, argnums=(0, 1))

    def one_microbatch(ex, loss_target):
        ex = dict(ex)
        diff = ex.pop(TWIN_DIFF_INPUT)
        return grad_fn(weights, diff, {**shared, **ex}, loss_target)

    if N_MICROBATCH == 1:
        loss, (grad_w, grad_x) = one_microbatch(per_example, given["loss_target"])
    else:
        def body(carry, xs):
            loss_sum, grad_sum = carry
            l_k, (gw_k, gx_k) = one_microbatch(xs[0], xs[1])
            with _jax.named_scope("update"):
                return (loss_sum + l_k, _jax.tree.map(_jnp.add, grad_sum, gw_k)), gx_k

        init = (_jnp.zeros((), _jnp.float32), _jax.tree.map(_jnp.zeros_like, weights))
        (loss, grad_w), grad_x = _jax.lax.scan(body, init, (per_example, given["loss_target"]))
    with _jax.named_scope("update"):
        delta_w, new_m, new_v = {}, {}, {}
        for n in TWIN_WEIGHTS:
            delta_w[n], new_m[n], new_v[n] = _adamw(weights[n], grad_w[n], given["m_" + n], given["v_" + n])
    return (loss, grad_x, *[grad_w[n] for n in TWIN_WEIGHTS], *[delta_w[n] for n in TWIN_WEIGHTS],
            *[new_m[n] for n in TWIN_WEIGHTS], *[new_v[n] for n in TWIN_WEIGHTS])
```

```python
import math

import jax
import jax.numpy as jnp
from jax import lax
from jax.experimental import pallas as pl
from jax.experimental.pallas import tpu as pltpu

F32, BF16 = jnp.float32, jnp.bfloat16
SDS = jax.ShapeDtypeStruct
MESH_IDS = pl.DeviceIdType.MESH

_VMEM_LIMIT_BYTES = 48 * 1024 * 1024
_LANES = 128
_NORM_EPS = 1e-6
_NEG_INF = -1e30
_HEAD_DIM = 64
_N_KV = 4
_ATTN_BLOCK = 128
_ROW_TILE = 512
_COL_TILE = 512
_CONV_HALO = 32
_FFN_HALO = 16
_ADAM_LR, _ADAM_B1, _ADAM_B2, _ADAM_EPS, _ADAM_WD, _ADAM_STEP = 0.001, 0.9, 0.999, 1e-08, 0.01, 10
_N_CHIPS = 4
_N_DEV = 8


def _tile(n, pref, unit):
    if n <= pref:
        return n
    t = pref - pref % unit
    while t >= unit:
        if n % t == 0:
            return t
        t -= unit
    return n


def _params(n_axes):
    return pltpu.CompilerParams(dimension_semantics=("arbitrary",) * n_axes, vmem_limit_bytes=_VMEM_LIMIT_BYTES)


def _row(n):
    return pl.BlockSpec((1, n), lambda *_: (0, 0))


def _sigmoid(v):
    return 1.0 / (1.0 + jnp.exp(-v))


def _nt(a, b):
    return lax.dot_general(a, b, (((1,), (1,)), ((), ())), preferred_element_type=F32)


def _tn(a, b):
    return lax.dot_general(a, b, (((0,), (0,)), ((), ())), preferred_element_type=F32)


def _norm_mod_matmul(x, gn, sc, sh, w, layer, b, name):
    t, d = x.shape
    n = w.shape[2]
    tm, tn = _tile(t, _ROW_TILE, 16), _tile(n, _COL_TILE, _LANES)

    def body(x_ref, gn_ref, sc_ref, sh_ref, w_ref, b_ref, h_ref, z_ref):
        @pl.when(pl.program_id(1) == 0)
        def _():
            xf = x_ref[...]
            r = lax.rsqrt(jnp.mean(xf * xf, axis=-1, keepdims=True) + _NORM_EPS)
            h_ref[...] = ((xf * r * gn_ref[...]) * (1.0 + sc_ref[...]) + sh_ref[...]).astype(BF16)

        z = jnp.dot(h_ref[...], w_ref[...], preferred_element_type=F32) + b_ref[...]
        z_ref[...] = z.astype(BF16)

    return pl.pallas_call(
        body, name=name, grid=(t // tm, n // tn),
        in_specs=[pl.BlockSpec((tm, d), lambda i, j: (i, 0)), _row(d), _row(d), _row(d),
                  pl.BlockSpec((None, d, tn), lambda i, j: (layer, 0, j)), pl.BlockSpec((1, tn), lambda i, j: (0, j))],
        out_specs=[pl.BlockSpec((tm, d), lambda i, j: (i, 0)), pl.BlockSpec((tm, tn), lambda i, j: (i, j))],
        out_shape=[SDS((t, d), BF16), SDS((t, n), BF16)], compiler_params=_params(2),
    )(x, gn, sc, sh, w, b)


def _matmul_resid(a, w, layer, b, x, gate, name):
    t, k = a.shape
    d = w.shape[2]
    tm, tn = _tile(t, _ROW_TILE, 16), _tile(d, _COL_TILE, _LANES)

    def body(a_ref, w_ref, b_ref, x_ref, g_ref, y_ref, xo_ref):
        y = jnp.dot(a_ref[...], w_ref[...], preferred_element_type=F32) + b_ref[...]
        y_ref[...] = y.astype(BF16)
        xo_ref[...] = x_ref[...] + g_ref[...] * y

    blk = pl.BlockSpec((tm, tn), lambda i, j: (i, j))
    vec = pl.BlockSpec((1, tn), lambda i, j: (0, j))
    return pl.pallas_call(
        body, name=name, grid=(t // tm, d // tn),
        in_specs=[pl.BlockSpec((tm, k), lambda i, j: (i, 0)), pl.BlockSpec((None, k, tn), lambda i, j: (layer, 0, j)),
                  vec, blk, vec],
        out_specs=[blk, blk], out_shape=[SDS((t, d), BF16), SDS((t, d), F32)], compiler_params=_params(2),
    )(a, w, b, x, gate)


def _outproj_bwd(dxo, y, gate, w, layer, name):
    t, d = dxo.shape
    k = w.shape[1]
    tm, tk = _tile(t, _ROW_TILE, 16), _tile(k, _COL_TILE, _LANES)

    def body(dxo_ref, y_ref, g_ref, w_ref, dy_ref, da_ref, dg_ref, db_ref):
        i, j = pl.program_id(0), pl.program_id(1)

        @pl.when((i == 0) & (j == 0))
        def _():
            dg_ref[...] = jnp.zeros_like(dg_ref)
            db_ref[...] = jnp.zeros_like(db_ref)

        @pl.when(j == 0)
        def _():
            dxf = dxo_ref[...]
            dyf = dxf * g_ref[...]
            dy_ref[...] = dyf.astype(BF16)
            dg_ref[...] += jnp.sum(dxf * y_ref[...].astype(F32), axis=0, keepdims=True)
            db_ref[...] += jnp.sum(dyf, axis=0, keepdims=True)

        da_ref[...] = _nt(dy_ref[...], w_ref[...]).astype(BF16)

    full = pl.BlockSpec((tm, d), lambda i, j: (i, 0))
    return pl.pallas_call(
        body, name=name, grid=(t // tm, k // tk),
        in_specs=[full, full, _row(d), pl.BlockSpec((None, tk, d), lambda i, j: (layer, j, 0))],
        out_specs=[full, pl.BlockSpec((tm, tk), lambda i, j: (i, j)), _row(d), _row(d)],
        out_shape=[SDS((t, d), BF16), SDS((t, k), BF16), SDS((1, d), F32), SDS((1, d), F32)],
        compiler_params=_params(2),
    )(dxo, y, gate, w)


def _matmul_tn(a, b, name):
    t, ka = a.shape
    nb = b.shape[1]
    tka, tnb, tt = _tile(ka, 1024, _LANES), _tile(nb, _COL_TILE, _LANES), _tile(t, _ROW_TILE, 16)
    nt = t // tt

    def body(a_ref, b_ref, o_ref, acc):
        s = pl.program_id(2)

        @pl.when(s == 0)
        def _():
            acc[...] = jnp.zeros_like(acc)

        acc[...] += _tn(a_ref[...], b_ref[...])

        @pl.when(s == nt - 1)
        def _():
            o_ref[...] = acc[...].astype(BF16)

    return pl.pallas_call(
        body, name=name, grid=(ka // tka, nb // tnb, nt),
        in_specs=[pl.BlockSpec((tt, tka), lambda i, j, s: (s, i)), pl.BlockSpec((tt, tnb), lambda i, j, s: (s, j))],
        out_specs=pl.BlockSpec((tka, tnb), lambda i, j, s: (i, j)),
        out_shape=SDS((ka, nb), BF16), scratch_shapes=[pltpu.VMEM((tka, tnb), F32)], compiler_params=_params(3),
    )(a, b)


def _inproj_bwd(dz, w, layer, x, dxo, gn, sc, name):
    t, n = dz.shape
    d = x.shape[1]
    tm, tk = _tile(t, _ROW_TILE, 16), _tile(n, _COL_TILE, _LANES)
    nk = n // tk

    def body(dz_ref, w_ref, x_ref, dxo_ref, gn_ref, sc_ref, dx_ref, dgn_ref, dsc_ref, dsh_ref, acc):
        i, k = pl.program_id(0), pl.program_id(1)

        @pl.when((i == 0) & (k == 0))
        def _():
            dgn_ref[...] = jnp.zeros_like(dgn_ref)
            dsc_ref[...] = jnp.zeros_like(dsc_ref)
            dsh_ref[...] = jnp.zeros_like(dsh_ref)

        @pl.when(k == 0)
        def _():
            acc[...] = jnp.zeros_like(acc)

        acc[...] += _nt(dz_ref[...], w_ref[...])

        @pl.when(k == nk - 1)
        def _():
            dh = acc[...]
            xf = x_ref[...]
            r = lax.rsqrt(jnp.mean(xf * xf, axis=-1, keepdims=True) + _NORM_EPS)
            xn = xf * r
            gnv = gn_ref[...]
            dsh_ref[...] += jnp.sum(dh, axis=0, keepdims=True)
            dsc_ref[...] += jnp.sum(dh * (xn * gnv), axis=0, keepdims=True)
            drn = dh * (1.0 + sc_ref[...])
            dgn_ref[...] += jnp.sum(drn * xn, axis=0, keepdims=True)
            dxn = drn * gnv
            dx_ref[...] = dxo_ref[...] + r * (dxn - xn * jnp.mean(dxn * xn, axis=-1, keepdims=True))

    full = pl.BlockSpec((tm, d), lambda i, k: (i, 0))
    return pl.pallas_call(
        body, name=name, grid=(t // tm, nk),
        in_specs=[pl.BlockSpec((tm, tk), lambda i, k: (i, k)), pl.BlockSpec((None, d, tk), lambda i, k: (layer, 0, k)),
                  full, full, _row(d), _row(d)],
        out_specs=[full, _row(d), _row(d), _row(d)],
        out_shape=[SDS((t, d), F32), SDS((1, d), F32), SDS((1, d), F32), SDS((1, d), F32)],
        scratch_shapes=[pltpu.VMEM((tm, d), F32)], compiler_params=_params(2),
    )(dz, w, x, dxo, gn, sc)


def _final_loss(x, g, target, name):
    t, d = x.shape
    tm = _tile(t, _ROW_TILE, 8)

    def body(x_ref, g_ref, t_ref, loss_ref, dx_ref, dg_ref):
        @pl.when(pl.program_id(0) == 0)
        def _():
            loss_ref[...] = jnp.zeros_like(loss_ref)
            dg_ref[...] = jnp.zeros_like(dg_ref)

        xf = x_ref[...]
        r = lax.rsqrt(jnp.mean(xf * xf, axis=-1, keepdims=True) + _NORM_EPS)
        xn = xf * r
        gv = g_ref[...]
        e = xn * gv - t_ref[...]
        per_row = jnp.mean(e * e, axis=-1, keepdims=True)
        loss_ref[...] += 0.5 * jnp.sum(per_row, axis=0, keepdims=True)
        dy = e * (1.0 / d)
        dg_ref[...] += jnp.sum(dy * xn, axis=0, keepdims=True)
        dxn = dy * gv
        dx_ref[...] = r * (dxn - xn * jnp.mean(dxn * xn, axis=-1, keepdims=True))

    full = pl.BlockSpec((tm, d), lambda i: (i, 0))
    return pl.pallas_call(
        body, name=name, grid=(t // tm,), in_specs=[full, _row(d), full],
        out_specs=[_row(_LANES), full, _row(d)],
        out_shape=[SDS((1, _LANES), F32), SDS((t, d), F32), SDS((1, d), F32)], compiler_params=_params(1),
    )(x, g, target)


def _attn_tables(nh):
    group = nh // _N_KV
    slopes = 2.0 ** (-8.0 * jnp.arange(1, nh + 1, dtype=F32) / nh)
    qpos = jnp.arange(_ATTN_BLOCK) + _ATTN_BLOCK
    kpos = jnp.arange(2 * _ATTN_BLOCK)
    dist = qpos[:, None] - kpos[None, :]
    band = (dist >= 0) & (dist < _ATTN_BLOCK)
    bias = jnp.where(band[None], -slopes[:, None, None] * dist.astype(F32)[None], _NEG_INF)
    first = jnp.where((kpos < _ATTN_BLOCK)[None, None, :], _NEG_INF, bias)
    return jnp.stack([first, bias]).reshape(2, _N_KV, group * _ATTN_BLOCK, 2 * _ATTN_BLOCK)


def _attn_probs(q_ref, kp_ref, kc_ref, tab_ref, sink_ref, kv, group):
    hd, blk = _HEAD_DIM, _ATTN_BLOCK
    k2 = jnp.concatenate([kp_ref[:, kv * hd:(kv + 1) * hd], kc_ref[:, kv * hd:(kv + 1) * hd]], axis=0)
    q4 = jnp.concatenate([q_ref[:, (kv * group + g) * hd:(kv * group + g + 1) * hd] for g in range(group)], axis=0)
    s = _nt(q4, k2) * (hd ** -0.5) + tab_ref[kv]
    sink = jnp.concatenate([jnp.full((blk, 1), sink_ref[kv * group + g], F32) for g in range(group)], axis=0)
    m = jnp.maximum(jnp.max(s, axis=-1, keepdims=True), sink)
    e = jnp.exp(s - m)
    es = jnp.exp(sink - m)
    inv = 1.0 / (jnp.sum(e, axis=-1, keepdims=True) + es)
    return q4, k2, e * inv, es * inv


def _attn_specs(nh, order):
    group = nh // _N_KV
    blk, kvw = _ATTN_BLOCK, _N_KV * _HEAD_DIM
    prev = lambda i: jnp.maximum(order(i) - 1, 0)
    return [
        pl.BlockSpec((blk, nh * _HEAD_DIM), lambda i: (order(i), 0)),
        pl.BlockSpec((blk, kvw), lambda i: (prev(i), group)),
        pl.BlockSpec((blk, kvw), lambda i: (order(i), group)),
        pl.BlockSpec((blk, kvw), lambda i: (prev(i), group + 1)),
        pl.BlockSpec((blk, kvw), lambda i: (order(i), group + 1)),
        pl.BlockSpec((None, _N_KV, group * blk, 2 * blk), lambda i: (jnp.minimum(order(i), 1), 0, 0, 0)),
        pl.BlockSpec(memory_space=pltpu.SMEM),
    ]


def _attn_fwd(qkv, tables, sinks, nh, name):
    t = qkv.shape[0]
    group, hd, blk = nh // _N_KV, _HEAD_DIM, _ATTN_BLOCK

    def body(q_ref, kp_ref, kc_ref, vp_ref, vc_ref, tab_ref, sink_ref, o_ref):
        outs = [None] * nh
        for kv in range(_N_KV):
            _, _, p, _ = _attn_probs(q_ref, kp_ref, kc_ref, tab_ref, sink_ref, kv, group)
            v2 = jnp.concatenate([vp_ref[:, kv * hd:(kv + 1) * hd], vc_ref[:, kv * hd:(kv + 1) * hd]], axis=0)
            o4 = jnp.dot(p.astype(BF16), v2, preferred_element_type=F32)
            for g in range(group):
                outs[kv * group + g] = o4[g * blk:(g + 1) * blk, :]
        o_ref[...] = jnp.concatenate(outs, axis=1).astype(BF16)

    return pl.pallas_call(
        body, name=name, grid=(t // blk,), in_specs=_attn_specs(nh, lambda i: i),
        out_specs=pl.BlockSpec((blk, nh * hd), lambda i: (i, 0)), out_shape=SDS((t, nh * hd), BF16),
        compiler_params=_params(1),
    )(qkv, qkv, qkv, qkv, qkv, tables, sinks)


def _attn_bwd(qkv, do, tables, sinks, nh, name):
    t, wq = qkv.shape
    group, hd, blk = nh // _N_KV, _HEAD_DIM, _ATTN_BLOCK
    nb = t // blk
    kvw = _N_KV * hd
    order = lambda i: nb - 1 - i

    def body(q_ref, kp_ref, kc_ref, vp_ref, vc_ref, tab_ref, sink_ref, do_ref, dqkv_ref, dsink_ref, db_ref, ck, cv):
        i = pl.program_id(0)

        @pl.when(i == 0)
        def _():
            ck[...] = jnp.zeros_like(ck)
            cv[...] = jnp.zeros_like(cv)
            dsink_ref[...] = jnp.zeros_like(dsink_ref)
            db_ref[...] = jnp.zeros_like(db_ref)

        lane = lax.broadcasted_iota(jnp.int32, (1, _LANES), 1)
        dq, dk, dv = [None] * nh, [None] * _N_KV, [None] * _N_KV
        dsink = jnp.zeros((1, _LANES), F32)
        for kv in range(_N_KV):
            q4, k2, p, ps = _attn_probs(q_ref, kp_ref, kc_ref, tab_ref, sink_ref, kv, group)
            v2 = jnp.concatenate([vp_ref[:, kv * hd:(kv + 1) * hd], vc_ref[:, kv * hd:(kv + 1) * hd]], axis=0)
            do4 = jnp.concatenate([do_ref[:, (kv * group + g) * hd:(kv * group + g + 1) * hd] for g in range(group)], axis=0)
            dp = _nt(do4, v2)
            dl = jnp.sum(p * dp, axis=-1, keepdims=True)
            ds = (p * (dp - dl)).astype(BF16)
            dsk = -ps * dl
            for g in range(group):
                dsink = dsink + jnp.where(lane == kv * group + g, jnp.sum(dsk[g * blk:(g + 1) * blk, :]), 0.0)
            dq4 = jnp.dot(ds, k2, preferred_element_type=F32) * (hd ** -0.5)
            for g in range(group):
                dq[kv * group + g] = dq4[g * blk:(g + 1) * blk, :]
            dk2 = _tn(ds, q4) * (hd ** -0.5)
            dv2 = _tn(p.astype(BF16), do4)
            dk[kv] = dk2[blk:, :] + ck[:, kv * hd:(kv + 1) * hd]
            dv[kv] = dv2[blk:, :] + cv[:, kv * hd:(kv + 1) * hd]
            ck[:, kv * hd:(kv + 1) * hd] = dk2[:blk, :]
            cv[:, kv * hd:(kv + 1) * hd] = dv2[:blk, :]
        dqkv = jnp.concatenate(dq + dk + dv, axis=1)
        dqkv_ref[...] = dqkv.astype(BF16)
        db_ref[...] += jnp.sum(dqkv, axis=0, keepdims=True)
        dsink_ref[...] += dsink

    return pl.pallas_call(
        body, name=name, grid=(nb,),
        in_specs=_attn_specs(nh, order) + [pl.BlockSpec((blk, nh * hd), lambda i: (order(i), 0))],
        out_specs=[pl.BlockSpec((blk, wq), lambda i: (order(i), 0)), _row(_LANES), _row(wq)],
        out_shape=[SDS((t, wq), BF16), SDS((1, _LANES), F32), SDS((1, wq), F32)],
        scratch_shapes=[pltpu.VMEM((blk, kvw), F32), pltpu.VMEM((blk, kvw), F32)], compiler_params=_params(1),
    )(qkv, qkv, qkv, qkv, qkv, tables, sinks, do)


def _taps(src_ref, w_ref, dst_ref, n_rows, width, offs, rg):
    cg = _tile(width, 512, _LANES)
    for c0 in range(0, width, cg):
        wk = [w_ref[k:k + 1, c0:c0 + cg] for k, _ in offs]
        for r0 in range(0, n_rows, rg):
            acc = None
            for (_, off), wv in zip(offs, wk):
                term = wv * src_ref[r0 + off:r0 + off + rg, c0:c0 + cg]
                acc = term if acc is None else acc + term
            dst_ref[r0:r0 + rg, c0:c0 + cg] = acc


def _tap_grads(dy_ref, z_ref, out_ref, n_rows, width, offs, rg):
    cg = _tile(width, 512, _LANES)
    for c0 in range(0, width, cg):
        for k, off in offs:
            acc = None
            for r0 in range(0, n_rows, rg):
                term = dy_ref[r0:r0 + rg, c0:c0 + cg] * z_ref[r0 + off:r0 + off + rg, c0:c0 + cg]
                acc = term if acc is None else acc + term
            out_ref[k:k + 1, c0:c0 + cg] += jnp.sum(acc, axis=0, keepdims=True)


def _conv_mid_fwd(ag, dw, dwb, lng, lnb, name):
    t, c2 = ag.shape
    c = c2 // 2
    kw = dw.shape[0]
    hl = _CONV_HALO
    tm = _tile(t, 256, hl)
    per = tm // hl

    def body(agp_ref, ag_ref, dw_ref, dwb_ref, lng_ref, lnb_ref, o_ref, zext, yb):
        i = pl.program_id(0)
        glu = lambda ref: ref[:, :c].astype(F32) * _sigmoid(ref[:, c:].astype(F32))
        zext[0:hl, :] = jnp.where(i > 0, glu(agp_ref), 0.0)
        zext[hl:, :] = glu(ag_ref)
        _taps(zext, dw_ref, yb, tm, c, [(k, hl - (kw - 1) + k) for k in range(kw)], 32)
        y = yb[...] + dwb_ref[...]
        mu = jnp.mean(y, axis=-1, keepdims=True)
        yc = y - mu
        rstd = lax.rsqrt(jnp.mean(yc * yc, axis=-1, keepdims=True) + _NORM_EPS)
        ln = yc * rstd * lng_ref[...] + lnb_ref[...]
        o_ref[...] = (ln * _sigmoid(ln)).astype(BF16)

    return pl.pallas_call(
        body, name=name, grid=(t // tm,),
        in_specs=[pl.BlockSpec((hl, c2), lambda i: (jnp.maximum(i * per - 1, 0), 0)), pl.BlockSpec((tm, c2), lambda i: (i, 0)),
                  pl.BlockSpec((kw, c), lambda i: (0, 0)), _row(c), _row(c), _row(c)],
        out_specs=pl.BlockSpec((tm, c), lambda i: (i, 0)), out_shape=SDS((t, c), BF16),
        scratch_shapes=[pltpu.VMEM((hl + tm, c), F32), pltpu.VMEM((tm, c), F32)], compiler_params=_params(1),
    )(ag, ag, dw, dwb, lng, lnb)


def _conv_mid_bwd(ag, dzc, dw, dwb, lng, lnb, name):
    t, c2 = ag.shape
    c = c2 // 2
    kw = dw.shape[0]
    hl = _CONV_HALO
    tm = _tile(t, 256, hl)
    per = tm // hl
    nt = t // tm
    last_halo = t // hl - 1

    def body(agp_ref, ag_ref, agn_ref, dzc_ref, dzcn_ref, dw_ref, dwb_ref, lng_ref, lnb_ref,
             dag_ref, ddw_ref, ddwb_ref, dlng_ref, dlnb_ref, dbin_ref, zext, yext, dyext, dzb):
        i = pl.program_id(0)

        @pl.when(i == 0)
        def _():
            for r in (ddw_ref, ddwb_ref, dlng_ref, dlnb_ref, dbin_ref):
                r[...] = jnp.zeros_like(r)

        glu = lambda ref: ref[:, :c].astype(F32) * _sigmoid(ref[:, c:].astype(F32))
        zext[0:hl, :] = jnp.where(i > 0, glu(agp_ref), 0.0)
        zext[hl:hl + tm, :] = glu(ag_ref)
        zext[hl + tm:, :] = glu(agn_ref)
        fwd_offs = [(k, hl - (kw - 1) + k) for k in range(kw)]
        _taps(zext, dw_ref, yext, tm + hl, c, fwd_offs, 32)
        y = yext[...] + dwb_ref[...]
        mu = jnp.mean(y, axis=-1, keepdims=True)
        yc = y - mu
        rstd = lax.rsqrt(jnp.mean(yc * yc, axis=-1, keepdims=True) + _NORM_EPS)
        xhat = yc * rstd
        lngv = lng_ref[...]
        ln = xhat * lngv + lnb_ref[...]
        sg = _sigmoid(ln)
        dz_out = jnp.concatenate([dzc_ref[...].astype(F32), jnp.where(i < nt - 1, dzcn_ref[...].astype(F32), 0.0)], axis=0)
        dln = dz_out * (sg * (1.0 + ln * (1.0 - sg)))
        dlng_ref[...] += jnp.sum((dln * xhat)[:tm], axis=0, keepdims=True)
        dlnb_ref[...] += jnp.sum(dln[:tm], axis=0, keepdims=True)
        dxh = dln * lngv
        dy = rstd * (dxh - jnp.mean(dxh, axis=-1, keepdims=True) - xhat * jnp.mean(dxh * xhat, axis=-1, keepdims=True))
        dyext[...] = dy
        ddwb_ref[...] += jnp.sum(dy[:tm], axis=0, keepdims=True)
        _taps(dyext, dw_ref, dzb, tm, c, [(k, kw - 1 - k) for k in range(kw)], 32)
        _tap_grads(dyext, zext, ddw_ref, tm, c, fwd_offs, 32)
        a = ag_ref[:, :c].astype(F32)
        sgg = _sigmoid(ag_ref[:, c:].astype(F32))
        dz = dzb[...]
        da = dz * sgg
        dg = dz * a * sgg * (1.0 - sgg)
        dag_ref[:, :c] = da.astype(BF16)
        dag_ref[:, c:] = dg.astype(BF16)
        dbin_ref[:, :c] += jnp.sum(da, axis=0, keepdims=True)
        dbin_ref[:, c:] += jnp.sum(dg, axis=0, keepdims=True)

    prev = lambda i: (jnp.maximum(i * per - 1, 0), 0)
    nxt = lambda i: (jnp.minimum((i + 1) * per, last_halo), 0)
    return pl.pallas_call(
        body, name=name, grid=(nt,),
        in_specs=[pl.BlockSpec((hl, c2), prev), pl.BlockSpec((tm, c2), lambda i: (i, 0)), pl.BlockSpec((hl, c2), nxt),
                  pl.BlockSpec((tm, c), lambda i: (i, 0)), pl.BlockSpec((hl, c), nxt),
                  pl.BlockSpec((kw, c), lambda i: (0, 0)), _row(c), _row(c), _row(c)],
        out_specs=[pl.BlockSpec((tm, c2), lambda i: (i, 0)), pl.BlockSpec((kw, c), lambda i: (0, 0)), _row(c), _row(c), _row(c), _row(c2)],
        out_shape=[SDS((t, c2), BF16), SDS((kw, c), F32), SDS((1, c), F32), SDS((1, c), F32), SDS((1, c), F32), SDS((1, c2), F32)],
        scratch_shapes=[pltpu.VMEM((hl + tm + hl, c), F32), pltpu.VMEM((tm + hl, c), F32), pltpu.VMEM((tm + hl, c), F32),
                        pltpu.VMEM((tm, c), F32)],
        compiler_params=_params(1),
    )(ag, ag, ag, dzc, dzc, dw, dwb, lng, lnb)


def _ffn_mid_fwd(zf, dw, dwb, name):
    t, f2 = zf.shape
    f = f2 // 2
    kw = dw.shape[0]
    hl = _FFN_HALO
    tm = _tile(t, 256, hl)
    per = tm // hl
    tc = _tile(f, 256, _LANES)
    offs = [(k, hl - (kw - 1) + k) for k in range(kw)]

    def body(zp_ref, z_ref, dw_ref, dwb_ref, o_ref, zext, cb):
        i = pl.program_id(0)
        zext[0:hl, :] = jnp.where(i > 0, zp_ref[...].astype(F32), 0.0)
        zext[hl:, :] = z_ref[...].astype(F32)
        _taps(zext, dw_ref, cb, tm, f2, offs, 16)
        for c0 in range(0, f, tc):
            g = cb[:, c0:c0 + tc] + dwb_ref[:, c0:c0 + tc]
            u = cb[:, f + c0:f + c0 + tc] + dwb_ref[:, f + c0:f + c0 + tc]
            o_ref[:, c0:c0 + tc] = (g * _sigmoid(g) * u).astype(BF16)

    return pl.pallas_call(
        body, name=name, grid=(t // tm,),
        in_specs=[pl.BlockSpec((hl, f2), lambda i: (jnp.maximum(i * per - 1, 0), 0)), pl.BlockSpec((tm, f2), lambda i: (i, 0)),
                  pl.BlockSpec((kw, f2), lambda i: (0, 0)), _row(f2)],
        out_specs=pl.BlockSpec((tm, f), lambda i: (i, 0)), out_shape=SDS((t, f), BF16),
        scratch_shapes=[pltpu.VMEM((hl + tm, f2), F32), pltpu.VMEM((tm, f2), F32)], compiler_params=_params(1),
    )(zf, zf, dw, dwb)


def _ffn_mid_bwd(zf, dact, dw, dwb, name):
    t, f2 = zf.shape
    f = f2 // 2
    kw = dw.shape[0]
    hl = _FFN_HALO
    tm = _tile(t, 128, hl)
    per = tm // hl
    nt = t // tm
    last_halo = t // hl - 1
    tc = _tile(f, 256, _LANES)
    fwd_offs = [(k, hl - (kw - 1) + k) for k in range(kw)]

    def body(zp_ref, z_ref, zn_ref, da_ref, dan_ref, dw_ref, dwb_ref, dzf_ref, ddw_ref, ddwb_ref, zext, cext, dcext, dzb):
        i = pl.program_id(0)

        @pl.when(i == 0)
        def _():
            ddw_ref[...] = jnp.zeros_like(ddw_ref)
            ddwb_ref[...] = jnp.zeros_like(ddwb_ref)

        zext[0:hl, :] = jnp.where(i > 0, zp_ref[...].astype(F32), 0.0)
        zext[hl:hl + tm, :] = z_ref[...].astype(F32)
        zext[hl + tm:, :] = zn_ref[...].astype(F32)
        _taps(zext, dw_ref, cext, tm + hl, f2, fwd_offs, 16)
        for c0 in range(0, f, tc):
            g = cext[:, c0:c0 + tc] + dwb_ref[:, c0:c0 + tc]
            u = cext[:, f + c0:f + c0 + tc] + dwb_ref[:, f + c0:f + c0 + tc]
            da = jnp.concatenate([da_ref[:, c0:c0 + tc].astype(F32),
                                  jnp.where(i < nt - 1, dan_ref[:, c0:c0 + tc].astype(F32), 0.0)], axis=0)
            sg = _sigmoid(g)
            dcg = da * u * (sg * (1.0 + g * (1.0 - sg)))
            dcu = da * (g * sg)
            dcext[:, c0:c0 + tc] = dcg
            dcext[:, f + c0:f + c0 + tc] = dcu
            ddwb_ref[:, c0:c0 + tc] += jnp.sum(dcg[:tm], axis=0, keepdims=True)
            ddwb_ref[:, f + c0:f + c0 + tc] += jnp.sum(dcu[:tm], axis=0, keepdims=True)
        _taps(dcext, dw_ref, dzb, tm, f2, [(k, kw - 1 - k) for k in range(kw)], 16)
        _tap_grads(dcext, zext, ddw_ref, tm, f2, fwd_offs, 16)
        dzf_ref[...] = dzb[...].astype(BF16)

    prev = lambda i: (jnp.maximum(i * per - 1, 0), 0)
    nxt = lambda i: (jnp.minimum((i + 1) * per, last_halo), 0)
    return pl.pallas_call(
        body, name=name, grid=(nt,),
        in_specs=[pl.BlockSpec((hl, f2), prev), pl.BlockSpec((tm, f2), lambda i: (i, 0)), pl.BlockSpec((hl, f2), nxt),
                  pl.BlockSpec((tm, f), lambda i: (i, 0)), pl.BlockSpec((hl, f), nxt),
                  pl.BlockSpec((kw, f2), lambda i: (0, 0)), _row(f2)],
        out_specs=[pl.BlockSpec((tm, f2), lambda i: (i, 0)), pl.BlockSpec((kw, f2), lambda i: (0, 0)), _row(f2)],
        out_shape=[SDS((t, f2), BF16), SDS((kw, f2), F32), SDS((1, f2), F32)],
        scratch_shapes=[pltpu.VMEM((hl + tm + hl, f2), F32), pltpu.VMEM((tm + hl, f2), F32), pltpu.VMEM((tm + hl, f2), F32),
                        pltpu.VMEM((tm, f2), F32)],
        compiler_params=_params(1),
    )(zf, zf, zf, dact, dact, dw, dwb)


_INV_SQRT2 = 0.7071067811865476
_INV_SQRT_2PI = 0.3989422804014327


def _sgu_common(zin_ref, lng_ref, lnb_ref, hh):
    z = zin_ref[...].astype(F32)
    cdf = 0.5 * (1.0 + lax.erf(z * _INV_SQRT2))
    ge = z * cdf
    u, v = ge[:, :hh], ge[:, hh:]
    mu = jnp.mean(v, axis=-1, keepdims=True)
    vc = v - mu
    rstd = lax.rsqrt(jnp.mean(vc * vc, axis=-1, keepdims=True) + _NORM_EPS)
    vhat = vc * rstd
    vn = vhat * lng_ref[...] + lnb_ref[...]
    return z, cdf, u, vhat, rstd, vn


def _sgu_wm(ws_ref, g, ch):
    rows = lax.broadcasted_iota(jnp.int32, (ch, ch), 0)
    cols = lax.broadcasted_iota(jnp.int32, (ch, ch), 1)
    return jnp.where(rows >= cols, ws_ref[g], 0.0).astype(BF16)


def _sgu_mid_fwd(zin, lng, lnb, ws, bs_t, name):
    t, h2 = zin.shape
    hh = h2 // 2
    ng, ch = ws.shape[0], ws.shape[1]
    hg = hh // ng
    tm = _tile(t, 256, ch)

    def body(zin_ref, lng_ref, lnb_ref, ws_ref, bs_ref, o_ref):
        _, _, u, _, _, vn = _sgu_common(zin_ref, lng_ref, lnb_ref, hh)
        vnb = vn.astype(BF16)
        for g in range(ng):
            wm = _sgu_wm(ws_ref, g, ch)
            for cc in range(tm // ch):
                rs, cs = slice(cc * ch, (cc + 1) * ch), slice(g * hg, (g + 1) * hg)
                vv = jnp.dot(wm, vnb[rs, cs], preferred_element_type=F32) + bs_ref[:, g:g + 1]
                o_ref[rs, cs] = (u[rs, cs] * vv).astype(BF16)

    return pl.pallas_call(
        body, name=name, grid=(t // tm,),
        in_specs=[pl.BlockSpec((tm, h2), lambda i: (i, 0)), _row(hh), _row(hh),
                  pl.BlockSpec((ng, ch, ch), lambda i: (0, 0, 0)), pl.BlockSpec((ch, ng), lambda i: (0, 0))],
        out_specs=pl.BlockSpec((tm, hh), lambda i: (i, 0)), out_shape=SDS((t, hh), BF16), compiler_params=_params(1),
    )(zin, lng, lnb, ws, bs_t)


def _sgu_mid_bwd(zin, duv, lng, lnb, ws, bs_t, name):
    t, h2 = zin.shape
    hh = h2 // 2
    ng, ch = ws.shape[0], ws.shape[1]
    hg = hh // ng
    tm = _tile(t, 128, ch)

    def body(zin_ref, duv_ref, lng_ref, lnb_ref, ws_ref, bs_ref, dzin_ref, dlng_ref, dlnb_ref, dws_ref, dbs_ref, dbin_ref, dvn_s, du_s):
        @pl.when(pl.program_id(0) == 0)
        def _():
            for r in (dlng_ref, dlnb_ref, dws_ref, dbs_ref, dbin_ref):
                r[...] = jnp.zeros_like(r)

        z, cdf, u, vhat, rstd, vn = _sgu_common(zin_ref, lng_ref, lnb_ref, hh)
        vnb = vn.astype(BF16)
        duv = duv_ref[...].astype(F32)
        dvv = (duv * u).astype(BF16)
        lane = lax.broadcasted_iota(jnp.int32, (1, _LANES), 1)
        rows = lax.broadcasted_iota(jnp.int32, (ch, ch), 0)
        cols = lax.broadcasted_iota(jnp.int32, (ch, ch), 1)
        dbs = jnp.zeros((ch, _LANES), F32)
        for g in range(ng):
            wm = _sgu_wm(ws_ref, g, ch)
            dwm = jnp.zeros((ch, ch), F32)
            for cc in range(tm // ch):
                rs, cs = slice(cc * ch, (cc + 1) * ch), slice(g * hg, (g + 1) * hg)
                vv = jnp.dot(wm, vnb[rs, cs], preferred_element_type=F32) + bs_ref[:, g:g + 1]
                du_s[rs, cs] = duv[rs, cs] * vv
                dvn_s[rs, cs] = _tn(wm, dvv[rs, cs])
                dwm = dwm + _nt(dvv[rs, cs], vnb[rs, cs])
                dbs = dbs + jnp.where(lane == g, jnp.sum(dvv[rs, cs].astype(F32), axis=-1, keepdims=True), 0.0)
            dws_ref[g] += jnp.where(rows >= cols, dwm, 0.0)
        dbs_ref[...] += dbs
        dvn = dvn_s[...]
        dlng_ref[...] += jnp.sum(dvn * vhat, axis=0, keepdims=True)
        dlnb_ref[...] += jnp.sum(dvn, axis=0, keepdims=True)
        dxh = dvn * lng_ref[...]
        dv = rstd * (dxh - jnp.mean(dxh, axis=-1, keepdims=True) - vhat * jnp.mean(dxh * vhat, axis=-1, keepdims=True))
        dgelu = cdf + z * (_INV_SQRT_2PI * jnp.exp(-0.5 * z * z))
        dzu = du_s[...] * dgelu[:, :hh]
        dzv = dv * dgelu[:, hh:]
        dzin_ref[:, :hh] = dzu.astype(BF16)
        dzin_ref[:, hh:] = dzv.astype(BF16)
        dbin_ref[:, :hh] += jnp.sum(dzu, axis=0, keepdims=True)
        dbin_ref[:, hh:] += jnp.sum(dzv, axis=0, keepdims=True)

    return pl.pallas_call(
        body, name=name, grid=(t // tm,),
        in_specs=[pl.BlockSpec((tm, h2), lambda i: (i, 0)), pl.BlockSpec((tm, hh), lambda i: (i, 0)), _row(hh), _row(hh),
                  pl.BlockSpec((ng, ch, ch), lambda i: (0, 0, 0)), pl.BlockSpec((ch, ng), lambda i: (0, 0))],
        out_specs=[pl.BlockSpec((tm, h2), lambda i: (i, 0)), _row(hh), _row(hh), pl.BlockSpec((ng, ch, ch), lambda i: (0, 0, 0)),
                   pl.BlockSpec((ch, _LANES), lambda i: (0, 0)), _row(h2)],
        out_shape=[SDS((t, h2), BF16), SDS((1, hh), F32), SDS((1, hh), F32), SDS((ng, ch, ch), F32), SDS((ch, _LANES), F32),
                   SDS((1, h2), F32)],
        scratch_shapes=[pltpu.VMEM((tm, hh), F32), pltpu.VMEM((tm, hh), F32)], compiler_params=_params(1),
    )(zin, duv, lng, lnb, ws, bs_t)


def _ada_mod(c_all, ada_w, ada_b, name):
    nl, d, n = ada_w.shape
    nb = c_all.shape[0]
    tn = _tile(n, _COL_TILE, _LANES)

    def body(c_ref, w_ref, b_ref, o_ref):
        cv = c_ref[...]
        ca = cv * _sigmoid(cv)
        o_ref[...] = jnp.dot(ca, w_ref[...], preferred_element_type=F32, precision=lax.Precision.HIGHEST) + b_ref[...]

    return pl.pallas_call(
        body, name=name, grid=(nl, n // tn),
        in_specs=[pl.BlockSpec((nb, d), lambda l, j: (0, 0)), pl.BlockSpec((None, d, tn), lambda l, j: (l, 0, j)),
                  pl.BlockSpec((None, 1, tn), lambda l, j: (l, 0, j))],
        out_specs=pl.BlockSpec((None, nb, tn), lambda l, j: (l, 0, j)), out_shape=SDS((nl, nb, n), F32),
        compiler_params=_params(2),
    )(c_all, ada_w, ada_b)


def _ada_wgrad(c_all_t, dmod, name):
    d, nb = c_all_t.shape
    nl, _, n = dmod.shape
    tn = _tile(n, _COL_TILE, _LANES)

    def body(c_ref, dm_ref, o_ref):
        cv = c_ref[...]
        ca = cv * _sigmoid(cv)
        acc = ca[:, 0:1] * dm_ref[0:1, :]
        for b in range(1, nb):
            acc = acc + ca[:, b:b + 1] * dm_ref[b:b + 1, :]
        o_ref[...] = acc

    return pl.pallas_call(
        body, name=name, grid=(nl, n // tn),
        in_specs=[pl.BlockSpec((d, nb), lambda l, j: (0, 0)), pl.BlockSpec((None, nb, tn), lambda l, j: (l, 0, j))],
        out_specs=pl.BlockSpec((None, d, tn), lambda l, j: (l, 0, j)), out_shape=SDS((nl, d, n), F32),
        compiler_params=_params(2),
    )(c_all_t, dmod)


def _as_rows(a):
    return a.reshape(-1, a.shape[-1])


def _row_tile(r, c, n_arrays):
    budget = _VMEM_LIMIT_BYTES // (4 * 2 * n_arrays * 4)
    return _tile(r, max(8, budget // max(c, 1)), 8)


def _cast_bf16(a, name):
    a2 = _as_rows(a)
    r, c = a2.shape
    tr = _row_tile(r, c, 2)

    def body(a_ref, o_ref):
        o_ref[...] = a_ref[...].astype(BF16)

    spec = pl.BlockSpec((tr, c), lambda i: (i, 0))
    out = pl.pallas_call(body, name=name, grid=(r // tr,), in_specs=[spec], out_specs=spec, out_shape=SDS((r, c), BF16),
                         compiler_params=_params(1))(a2)
    return out.reshape(a.shape)


def _sum_slots(parts, name):
    n = parts.shape[0]
    p2 = parts.reshape(n, -1, parts.shape[-1])
    r, c = p2.shape[1:]
    tr = _row_tile(r, c, n + 1)

    def body(p_ref, o_ref):
        acc = p_ref[0].astype(F32)
        for s in range(1, n):
            acc = acc + p_ref[s].astype(F32)
        o_ref[...] = acc

    out = pl.pallas_call(body, name=name, grid=(r // tr,), in_specs=[pl.BlockSpec((n, tr, c), lambda i: (0, i, 0))],
                         out_specs=pl.BlockSpec((tr, c), lambda i: (i, 0)), out_shape=SDS((r, c), F32),
                         compiler_params=_params(1))(p2)
    return out.reshape(parts.shape[1:])


def _adamw(w, g_parts, m, v, name):
    w2, m2, v2 = _as_rows(w), _as_rows(m), _as_rows(v)
    gs = [_as_rows(g) for g in g_parts]
    r, c = w2.shape
    ng = len(gs)
    tr = _row_tile(r, c, 7 + ng)
    c1 = 1.0 - _ADAM_B1 ** _ADAM_STEP
    c2 = 1.0 - _ADAM_B2 ** _ADAM_STEP

    def body(*refs):
        w_ref, m_ref, v_ref = refs[:3]
        g_refs = refs[3:3 + ng]
        go_ref, d_ref, mo_ref, vo_ref = refs[3 + ng:]
        g = g_refs[0][...]
        for gr in g_refs[1:]:
            g = g + gr[...]
        mn = _ADAM_B1 * m_ref[...] + (1.0 - _ADAM_B1) * g
        vn = _ADAM_B2 * v_ref[...] + (1.0 - _ADAM_B2) * (g * g)
        go_ref[...] = g
        mo_ref[...] = mn
        vo_ref[...] = vn
        d_ref[...] = -_ADAM_LR * ((mn / c1) / (jnp.sqrt(vn / c2) + _ADAM_EPS) + _ADAM_WD * w_ref[...])

    spec = pl.BlockSpec((tr, c), lambda i: (i, 0))
    outs = pl.pallas_call(body, name=name, grid=(r // tr,), in_specs=[spec] * (3 + ng), out_specs=[spec] * 4,
                          out_shape=[SDS((r, c), F32)] * 4, compiler_params=_params(1))(w2, m2, v2, *gs)
    return tuple(o.reshape(w.shape) for o in outs)


_HBM = pl.BlockSpec(memory_space=pltpu.HBM)
_CHIP_FLIPS = ((1, 0, 0), (0, 1, 0), (1, 1, 0))
_ALL_FLIPS = tuple((a, b, c) for a in (0, 1) for b in (0, 1) for c in (0, 1))[1:]
_SIBLING_FLIP = ((0, 0, 1),)


def _at(ref, idx):
    return ref.at[idx] if idx else ref


def _exchange(name, flips, srcs, out_shapes, items):
    n_in, n_out = len(srcs), len(out_shapes)
    n_remote = len(items) * len(flips)
    n_local = sum(1 for it in items if it[4])

    def body(*refs):
        src_refs, dst_refs = refs[:n_in], refs[n_in:n_in + n_out]
        send_sems, recv_sems, local_sems = refs[n_in + n_out:]
        me = (lax.axis_index("x"), lax.axis_index("y"), lax.axis_index("c"))
        sends, recvs, locs = [], [], []
        n, nl = 0, 0
        for si, di, src_idx, dst_idx, local in items:
            for flip in flips:
                peer = tuple(1 - m if f else m for m, f in zip(me, flip))
                push = pltpu.make_async_remote_copy(
                    src_ref=_at(src_refs[si], src_idx(*peer)), dst_ref=_at(dst_refs[di], dst_idx(*me)),
                    send_sem=send_sems.at[n], recv_sem=recv_sems.at[n], device_id=peer, device_id_type=MESH_IDS)
                push.start()
                sends.append(push)
                recvs.append(pltpu.make_async_remote_copy(
                    src_ref=_at(src_refs[si], src_idx(*me)), dst_ref=_at(dst_refs[di], dst_idx(*peer)),
                    send_sem=send_sems.at[n], recv_sem=recv_sems.at[n], device_id=peer, device_id_type=MESH_IDS))
                n += 1
            if local:
                cp = pltpu.make_async_copy(_at(src_refs[si], src_idx(*me)), _at(dst_refs[di], dst_idx(*me)), local_sems.at[nl])
                cp.start()
                locs.append(cp)
                nl += 1
        for r in recvs:
            r.wait_recv()
        for s in sends:
            s.wait_send()
        for cp in locs:
            cp.wait()

    return pl.pallas_call(
        body, name=name, in_specs=[_HBM] * n_in, out_specs=[_HBM] * n_out, out_shape=list(out_shapes),
        scratch_shapes=[pltpu.SemaphoreType.DMA((n_remote,)), pltpu.SemaphoreType.DMA((n_remote,)),
                        pltpu.SemaphoreType.DMA((max(n_local, 1),))],
    )(*srcs)


def _chip_of(x, y, c):
    return 2 * x + y


def _dev_of(x, y, c):
    return 4 * x + 2 * y + c


def _window(axis, ndim, size):
    def idx(x, y, c):
        return tuple(pl.ds(_chip_of(x, y, c) * size, size) if a == axis else slice(None) for a in range(ndim))
    return idx


def _whole(x, y, c):
    return ()


def _gather_chips(name, shards_axes):
    srcs, shapes, items = [], [], []
    for k, (a, axis) in enumerate(shards_axes):
        full = list(a.shape)
        full[axis] *= _N_CHIPS
        srcs.append(a)
        shapes.append(SDS(tuple(full), a.dtype))
        items.append((k, k, _whole, _window(axis, a.ndim, a.shape[axis]), True))
    return _exchange(name, _CHIP_FLIPS, srcs, shapes, items)


def _gather_all(name, a):
    slot = lambda x, y, c: (_dev_of(x, y, c),)
    return _exchange(name, _ALL_FLIPS, [a], [SDS((_N_DEV,) + a.shape, a.dtype)], [(0, 0, _whole, slot, True)])[0]


def _swap_chips(name, a):
    slot = lambda x, y, c: (_chip_of(x, y, c),)
    return _exchange(name, _CHIP_FLIPS, [a], [SDS(a.shape, a.dtype)], [(0, 0, slot, slot, True)])[0]


def _scatter_grads(name, grads_axes):
    srcs, shapes, items = [], [], []
    for per_layer, axis in grads_axes:
        a0 = per_layer[0]
        shard = list(a0.shape)
        shard[axis] //= _N_CHIPS
        di = len(shapes)
        shapes.append(SDS((_N_CHIPS, len(per_layer)) + tuple(shard), a0.dtype))
        for layer, a in enumerate(per_layer):
            srcs.append(a)
            dst = (lambda lay: lambda x, y, c: (_chip_of(x, y, c), lay))(layer)
            items.append((len(srcs) - 1, di, _window(axis, a.ndim, shard[axis]), dst, True))
    return _exchange(name, _CHIP_FLIPS, srcs, shapes, items)


def _swap_sibling(name, arrays):
    items = [(k, k, _whole, _whole, False) for k in range(len(arrays))]
    return _exchange(name, _SIBLING_FLIP, list(arrays), [SDS(a.shape, a.dtype) for a in arrays], items)


_BIG = ("attn_wqkv", "attn_wo", "conv_w_in", "conv_w_out", "sgu_w_in", "sgu_w_out", "ffn_w_in", "ffn_w_out")
_BIG_AXIS = {"attn_wqkv": 2, "attn_wo": 1, "conv_w_in": 2, "conv_w_out": 1, "sgu_w_in": 2, "sgu_w_out": 1,
             "ffn_w_in": 2, "ffn_w_out": 1}
_SMALL = {"norm1_g": None, "norm2_g": None, "ada_b": None, "attn_bqkv": 1, "attn_sinks": None, "attn_bo": 1,
          "conv_b_in": None, "conv_dw": 2, "conv_dw_b": None, "conv_ln_g": None, "conv_ln_b": None, "conv_b_out": None,
          "sgu_b_in": 1, "sgu_ln_g": 1, "sgu_ln_b": 1, "sgu_ws": None, "sgu_bs": None, "sgu_b_out": 1,
          "ffn_dw": 2, "ffn_dw_b": None, "final_g": None}
_WEIGHTS = ['norm1_g', 'norm2_g', 'ada_w', 'ada_b', 'attn_wqkv', 'attn_bqkv', 'attn_sinks', 'attn_wo', 'attn_bo',
            'conv_w_in', 'conv_b_in', 'conv_dw', 'conv_dw_b', 'conv_ln_g', 'conv_ln_b', 'conv_w_out', 'conv_b_out',
            'sgu_w_in', 'sgu_b_in', 'sgu_ln_g', 'sgu_ln_b', 'sgu_ws', 'sgu_bs', 'sgu_w_out', 'sgu_b_out',
            'ffn_w_in', 'ffn_dw', 'ffn_dw_b', 'ffn_w_out', 'final_g']
_FLAT_COLS = 1024


def _full_shape(a, axis):
    s = list(a.shape)
    if axis is not None:
        s[axis] *= _N_CHIPS
    return tuple(s)


def _pack(arrays):
    flat = jnp.concatenate([a.reshape(-1).astype(F32) for a in arrays])
    pad = (-flat.shape[0]) % (8 * _FLAT_COLS)
    return jnp.pad(flat, (0, pad)).reshape(-1, _FLAT_COLS)


def _unpack(flat2d, shapes):
    flat = flat2d.reshape(-1)
    out, off = [], 0
    for s in shapes:
        n = math.prod(s)
        out.append(flat[off:off + n].reshape(s))
        off += n
    return out


def _shard_of(full, axis, chip):
    if axis is None:
        return full
    size = full.shape[axis] // _N_CHIPS
    return lax.dynamic_slice_in_dim(full, chip * size, size, axis)


def _unshard(gathered, axis):
    moved = jnp.moveaxis(gathered, 0, axis)
    shape = list(gathered.shape[1:])
    shape[axis] *= _N_CHIPS
    return moved.reshape(shape)


def kernel(x, c, norm1_g, norm2_g, ada_w, ada_b, attn_wqkv, attn_bqkv, attn_sinks, attn_wo, attn_bo, conv_w_in, conv_b_in, conv_dw, conv_dw_b, conv_ln_g, conv_ln_b, conv_w_out, conv_b_out, sgu_w_in, sgu_b_in, sgu_ln_g, sgu_ln_b, sgu_ws, sgu_bs, sgu_w_out, sgu_b_out, ffn_w_in, ffn_dw, ffn_dw_b, ffn_w_out, final_g, loss_target, m_norm1_g, m_norm2_g, m_ada_w, m_ada_b, m_attn_wqkv, m_attn_bqkv, m_attn_sinks, m_attn_wo, m_attn_bo, m_conv_w_in, m_conv_b_in, m_conv_dw, m_conv_dw_b, m_conv_ln_g, m_conv_ln_b, m_conv_w_out, m_conv_b_out, m_sgu_w_in, m_sgu_b_in, m_sgu_ln_g, m_sgu_ln_b, m_sgu_ws, m_sgu_bs, m_sgu_w_out, m_sgu_b_out, m_ffn_w_in, m_ffn_dw, m_ffn_dw_b, m_ffn_w_out, m_final_g, v_norm1_g, v_norm2_g, v_ada_w, v_ada_b, v_attn_wqkv, v_attn_bqkv, v_attn_sinks, v_attn_wo, v_attn_bo, v_conv_w_in, v_conv_b_in, v_conv_dw, v_conv_dw_b, v_conv_ln_g, v_conv_ln_b, v_conv_w_out, v_conv_b_out, v_sgu_w_in, v_sgu_b_in, v_sgu_ln_g, v_sgu_ln_b, v_sgu_ws, v_sgu_bs, v_sgu_w_out, v_sgu_b_out, v_ffn_w_in, v_ffn_dw, v_ffn_dw_b, v_ffn_w_out, v_final_g):
    args = dict(locals())
    wts = {n: args[n] for n in _WEIGHTS}
    mom_m = {n: args["m_" + n] for n in _WEIGHTS}
    mom_v = {n: args["v_" + n] for n in _WEIGHTS}

    ix, iy, ic = lax.axis_index("x"), lax.axis_index("y"), lax.axis_index("c")
    chip = 2 * ix + iy
    xs = x[0]
    tgt = loss_target[0]
    t, d = xs.shape
    nh = d // _HEAD_DIM
    depth = ada_w.shape[0]
    ncols = ada_w.shape[2]
    n_mod = ncols * _N_CHIPS // d

    small_sharded = [n for n in _WEIGHTS if _SMALL.get(n) is not None]
    packed_small = _pack([wts[n] for n in small_sharded])
    big_bf16 = [(_cast_bf16(wts[n], "cast_" + n), _BIG_AXIS[n]) for n in _BIG]
    gathered = _gather_chips("gather_weights", big_bf16 + [(packed_small[None], 0)])
    wfull = dict(zip(_BIG, gathered[:len(_BIG)]))
    small_rows = gathered[len(_BIG)]
    pieces = [_unpack(small_rows[s], [wts[n].shape for n in small_sharded]) for s in range(_N_CHIPS)]
    full = {n: wts[n] for n in _WEIGHTS if n in _SMALL and _SMALL[n] is None}
    for k, n in enumerate(small_sharded):
        full[n] = _unshard(jnp.stack([pieces[s][k] for s in range(_N_CHIPS)]), _SMALL[n])

    c_all = _gather_all("gather_c", c)[:, 0, :]
    ada_b_cols = lax.dynamic_slice_in_dim(ada_b, chip * ncols, ncols, 1)[:, None, :]
    mod_cols = _ada_mod(c_all, ada_w, ada_b_cols, "ada_mod")
    mine = lax.dynamic_index_in_dim(mod_cols.reshape(depth, _N_CHIPS, 2, ncols), ic, 2, keepdims=False)
    got = _swap_chips("swap_mod", jnp.moveaxis(mine, 1, 0))
    mod = jnp.moveaxis(got, 0, 1).reshape(depth, n_mod, 1, d)

    tables = _attn_tables(nh)
    zeros_d = jnp.zeros((1, d), F32)
    zeros_f2 = jnp.zeros((1, ffn_w_in.shape[2] * _N_CHIPS), F32)
    row = lambda a: a.reshape(1, -1)

    saved = []
    xcur = xs
    for i in range(depth):
        sh1, sc1, g1, sh2, sc2, g2 = (mod[i, k] for k in range(n_mod))
        kind, j = i % 3, i // 3
        tag = "L%d_" % i
        if kind == 0:
            w_in, b_in, w_out, b_out = wfull["attn_wqkv"], row(full["attn_bqkv"][j]), wfull["attn_wo"], row(full["attn_bo"][j])
        elif kind == 1:
            w_in, b_in, w_out, b_out = wfull["conv_w_in"], row(full["conv_b_in"][j]), wfull["conv_w_out"], row(full["conv_b_out"][j])
        else:
            w_in, b_in, w_out, b_out = wfull["sgu_w_in"], row(full["sgu_b_in"][j]), wfull["sgu_w_out"], row(full["sgu_b_out"][j])
        h1, z = _norm_mod_matmul(xcur, row(full["norm1_g"][i]), sc1, sh1, w_in, j, b_in, tag + "mixer_in")
        if kind == 0:
            a = _attn_fwd(z, tables, full["attn_sinks"][j], nh, tag + "attn")
        elif kind == 1:
            a = _conv_mid_fwd(z, full["conv_dw"][j], row(full["conv_dw_b"][j]), row(full["conv_ln_g"][j]), row(full["conv_ln_b"][j]),
                              tag + "conv_mid")
        else:
            a = _sgu_mid_fwd(z, row(full["sgu_ln_g"][j]), row(full["sgu_ln_b"][j]), full["sgu_ws"][j], full["sgu_bs"][j].T,
                             tag + "sgu_mid")
        y1, xmid = _matmul_resid(a, w_out, j, b_out, xcur, g1, tag + "mixer_out")
        h2, zf = _norm_mod_matmul(xmid, row(full["norm2_g"][i]), sc2, sh2, wfull["ffn_w_in"], i, zeros_f2, tag + "ffn_in")
        act = _ffn_mid_fwd(zf, full["ffn_dw"][i], row(full["ffn_dw_b"][i]), tag + "ffn_mid")
        y2, xnext = _matmul_resid(act, wfull["ffn_w_out"], i, zeros_d, xmid, g2, tag + "ffn_out")
        saved.append(dict(x=xcur, h1=h1, z=z, a=a, y1=y1, xmid=xmid, h2=h2, zf=zf, act=act, y2=y2, w_in=w_in, w_out=w_out))
        xcur = xnext

    loss_row, dx, d_final_g = _final_loss(xcur, row(final_g), tgt, "final_loss")

    small_g = {n: jnp.zeros(_full_shape(wts[n], _SMALL[n]), F32) for n in _SMALL}
    small_g["final_g"] = d_final_g[0]
    big_g = {n: [None] * wts[n].shape[0] for n in _BIG}
    dmod = [None] * depth

    def put(name, j, val):
        small_g[name] = small_g[name].at[j].set(val.reshape(small_g[name].shape[1:]))

    for i in reversed(range(depth)):
        sv = saved[i]
        sh1, sc1, g1, sh2, sc2, g2 = (mod[i, k] for k in range(n_mod))
        kind, j = i % 3, i // 3
        tag = "L%d_" % i
        dy2, dact, dg2, _ = _outproj_bwd(dx, sv["y2"], g2, wfull["ffn_w_out"], i, tag + "ffn_out_bwd")
        big_g["ffn_w_out"][i] = _matmul_tn(sv["act"], dy2, tag + "ffn_out_wgrad")
        dzf, d_fdw, d_fdwb = _ffn_mid_bwd(sv["zf"], dact, full["ffn_dw"][i], row(full["ffn_dw_b"][i]), tag + "ffn_mid_bwd")
        put("ffn_dw", i, d_fdw)
        put("ffn_dw_b", i, d_fdwb)
        big_g["ffn_w_in"][i] = _matmul_tn(sv["h2"], dzf, tag + "ffn_in_wgrad")
        dxmid, dn2, dsc2, dsh2 = _inproj_bwd(dzf, wfull["ffn_w_in"], i, sv["xmid"], dx, row(full["norm2_g"][i]), sc2, tag + "ffn_in_bwd")
        put("norm2_g", i, dn2)
        dy1, da, dg1, dbo = _outproj_bwd(dxmid, sv["y1"], g1, sv["w_out"], j, tag + "mixer_out_bwd")
        w_out_name, w_in_name = {0: ("attn_wo", "attn_wqkv"), 1: ("conv_w_out", "conv_w_in"), 2: ("sgu_w_out", "sgu_w_in")}[kind]
        big_g[w_out_name][j] = _matmul_tn(sv["a"], dy1, tag + "mixer_out_wgrad")
        if kind == 0:
            dz, dsink, dbin = _attn_bwd(sv["z"], da, tables, full["attn_sinks"][j], nh, tag + "attn_bwd")
            put("attn_sinks", j, dsink[0, :nh])
            put("attn_bqkv", j, dbin)
            put("attn_bo", j, dbo)
        elif kind == 1:
            dz, d_dw, d_dwb, d_lng, d_lnb, dbin = _conv_mid_bwd(
                sv["z"], da, full["conv_dw"][j], row(full["conv_dw_b"][j]), row(full["conv_ln_g"][j]), row(full["conv_ln_b"][j]),
                tag + "conv_mid_bwd")
            for nme, val in (("conv_dw", d_dw), ("conv_dw_b", d_dwb), ("conv_ln_g", d_lng), ("conv_ln_b", d_lnb),
                             ("conv_b_in", dbin), ("conv_b_out", dbo)):
                put(nme, j, val)
        else:
            dz, d_lng, d_lnb, d_ws, d_bst, dbin = _sgu_mid_bwd(
                sv["z"], da, row(full["sgu_ln_g"][j]), row(full["sgu_ln_b"][j]), full["sgu_ws"][j], full["sgu_bs"][j].T,
                tag + "sgu_mid_bwd")
            ng = sgu_ws.shape[1]
            for nme, val in (("sgu_ln_g", d_lng), ("sgu_ln_b", d_lnb), ("sgu_ws", d_ws), ("sgu_bs", d_bst[:, :ng].T),
                             ("sgu_b_in", dbin), ("sgu_b_out", dbo)):
                put(nme, j, val)
        big_g[w_in_name][j] = _matmul_tn(sv["h1"], dz, tag + "mixer_in_wgrad")
        dx, dn1, dsc1, dsh1 = _inproj_bwd(dz, sv["w_in"], j, sv["x"], dxmid, row(full["norm1_g"][i]), sc1, tag + "mixer_in_bwd")
        put("norm1_g", i, dn1)
        dmod[i] = jnp.concatenate([dsh1, dsc1, dg1, dsh2, dsc2, dg2], axis=1)

    grad_x = dx[None]
    loss = lax.psum(loss_row[0, 0], ("x", "y", "c"))

    small_names = [n for n in _WEIGHTS if n in _SMALL and n != "ada_b"]
    dmod_own = jnp.concatenate(dmod, axis=0)
    packed = _pack([small_g[n] for n in small_names] + [dmod_own])
    packed_all = _gather_all("gather_small_grads", packed)
    summed = _sum_slots(packed_all, "sum_small_grads")
    small_full = dict(zip(small_names, _unpack(summed, [small_g[n].shape for n in small_names])))
    n_small = sum(math.prod(small_g[n].shape) for n in small_names)
    dmod_all = packed_all.reshape(_N_DEV, -1)[:, n_small:n_small + dmod_own.size].reshape(_N_DEV, depth, n_mod * d)
    small_full["ada_b"] = _sum_slots(dmod_all, "sum_ada_b_grad")
    dmod_cols = lax.dynamic_slice_in_dim(jnp.moveaxis(dmod_all, 0, 1), chip * ncols, ncols, 2)
    g_ada_w = _ada_wgrad(c_all.T, dmod_cols, "ada_wgrad")

    slots = _scatter_grads("scatter_grads", [(big_g[n], _BIG_AXIS[n] - 1) for n in _BIG])
    partial = [_sum_slots(s, "sum_chips_" + n) for n, s in zip(_BIG, slots)]
    other = _swap_sibling("swap_cores", partial)

    outs = {}
    for n, mine_p, theirs_p in zip(_BIG, partial, other):
        outs[n] = _adamw(wts[n], [mine_p, theirs_p], mom_m[n], mom_v[n], "adamw_" + n)
    outs["ada_w"] = _adamw(ada_w, [g_ada_w], m_ada_w, v_ada_w, "adamw_ada_w")
    sm_names = [n for n in _WEIGHTS if n in _SMALL]
    g_loc = [_shard_of(small_full[n], _SMALL[n], chip) for n in sm_names]
    packs = [_pack([src[n] for n in sm_names]) for src in (wts, mom_m, mom_v)]
    sm_out = _adamw(packs[0], [_pack(g_loc)], packs[1], packs[2], "adamw_small")
    shapes = [wts[n].shape for n in sm_names]
    unpacked = [_unpack(o, shapes) for o in sm_out]
    for k, n in enumerate(sm_names):
        outs[n] = tuple(u[k] for u in unpacked)

    result = [loss, grad_x]
    for which in range(4):
        result += [outs[n][which] for n in _WEIGHTS]
    return tuple(result)
```

```python
import math

import jax
import jax.numpy as jnp
from jax import lax
from jax.experimental import pallas as pl
from jax.experimental.pallas import tpu as pltpu

F32, BF16 = jnp.float32, jnp.bfloat16
SDS = jax.ShapeDtypeStruct
MESH_IDS = pl.DeviceIdType.MESH

_VMEM_LIMIT_BYTES = 48 * 1024 * 1024
_LANES = 128
_NORM_EPS = 1e-6
_NEG_INF = -1e30
_HEAD_DIM = 64
_N_KV = 4
_ATTN_BLOCK = 128
_ROW_TILE = 512
_COL_TILE = 512
_WGRAD_TILE = 1536
_CONV_HALO = 32
_FFN_HALO = 16
_ADAM_LR, _ADAM_B1, _ADAM_B2, _ADAM_EPS, _ADAM_WD, _ADAM_STEP = 0.001, 0.9, 0.999, 1e-08, 0.01, 10
_N_CHIPS = 4
_N_DEV = 8


def _tile(n, pref, unit):
    if n <= pref:
        return n
    t = pref - pref % unit
    while t >= unit:
        if n % t == 0:
            return t
        t -= unit
    return n


def _params(n_axes):
    return pltpu.CompilerParams(dimension_semantics=("arbitrary",) * n_axes, vmem_limit_bytes=_VMEM_LIMIT_BYTES)


def _row(n):
    return pl.BlockSpec((1, n), lambda *_: (0, 0))


def _sigmoid(v):
    return 1.0 / (1.0 + jnp.exp(-v))


def _nt(a, b):
    return lax.dot_general(a, b, (((1,), (1,)), ((), ())), preferred_element_type=F32)


def _tn(a, b):
    return lax.dot_general(a, b, (((0,), (0,)), ((), ())), preferred_element_type=F32)


def _resident(shape, index_map):
    return pl.BlockSpec(shape, index_map, pipeline_mode=pl.Buffered(1))


def _norm_mod_matmul(x, gn, sc, sh, w, layer, b, name):
    t, d = x.shape
    n = w.shape[2]
    tm, tn = _tile(t, _ROW_TILE, 16), _tile(n, _COL_TILE, _LANES)

    def body(x_ref, gn_ref, sc_ref, sh_ref, w_ref, b_ref, h_ref, z_ref):
        xf = x_ref[...]
        r = lax.rsqrt(jnp.mean(xf * xf, axis=-1, keepdims=True) + _NORM_EPS)
        h_ref[...] = ((xf * r * gn_ref[...]) * (1.0 + sc_ref[...]) + sh_ref[...]).astype(BF16)
        for c0 in range(0, n, tn):
            z = jnp.dot(h_ref[...], w_ref[:, c0:c0 + tn], preferred_element_type=F32) + b_ref[:, c0:c0 + tn]
            z_ref[:, c0:c0 + tn] = z.astype(BF16)

    return pl.pallas_call(
        body, name=name, grid=(t // tm,),
        in_specs=[pl.BlockSpec((tm, d), lambda i: (i, 0)), _row(d), _row(d), _row(d),
                  _resident((None, d, n), lambda i: (layer, 0, 0)), _row(n)],
        out_specs=[pl.BlockSpec((tm, d), lambda i: (i, 0)), pl.BlockSpec((tm, n), lambda i: (i, 0))],
        out_shape=[SDS((t, d), BF16), SDS((t, n), BF16)], compiler_params=_params(1),
    )(x, gn, sc, sh, w, b)


def _matmul_resid(a, w, layer, b, x, gate, name):
    t, k = a.shape
    d = w.shape[2]
    tm, tn = _tile(t, _ROW_TILE, 16), _tile(d, _COL_TILE, _LANES)

    def body(a_ref, w_ref, b_ref, x_ref, g_ref, y_ref, xo_ref):
        for c0 in range(0, d, tn):
            cs = slice(c0, c0 + tn)
            y = jnp.dot(a_ref[...], w_ref[:, cs], preferred_element_type=F32) + b_ref[:, cs]
            y_ref[:, cs] = y.astype(BF16)
            xo_ref[:, cs] = x_ref[:, cs] + g_ref[:, cs] * y

    blk = pl.BlockSpec((tm, d), lambda i: (i, 0))
    return pl.pallas_call(
        body, name=name, grid=(t // tm,),
        in_specs=[pl.BlockSpec((tm, k), lambda i: (i, 0)), _resident((None, k, d), lambda i: (layer, 0, 0)), _row(d), blk, _row(d)],
        out_specs=[blk, blk], out_shape=[SDS((t, d), BF16), SDS((t, d), F32)], compiler_params=_params(1),
    )(a, w, b, x, gate)


def _outproj_bwd(dxo, y, gate, w, layer, name):
    t, d = dxo.shape
    k = w.shape[1]
    tm, tk = _tile(t, _ROW_TILE, 16), _tile(k, _COL_TILE, _LANES)

    def body(dxo_ref, y_ref, g_ref, w_ref, dy_ref, da_ref, dg_ref, db_ref):
        @pl.when(pl.program_id(0) == 0)
        def _():
            dg_ref[...] = jnp.zeros_like(dg_ref)
            db_ref[...] = jnp.zeros_like(db_ref)

        dxf = dxo_ref[...]
        dyf = dxf * g_ref[...]
        dy_ref[...] = dyf.astype(BF16)
        dg_ref[...] += jnp.sum(dxf * y_ref[...].astype(F32), axis=0, keepdims=True)
        db_ref[...] += jnp.sum(dyf, axis=0, keepdims=True)
        for c0 in range(0, k, tk):
            da_ref[:, c0:c0 + tk] = _nt(dy_ref[...], w_ref[c0:c0 + tk, :]).astype(BF16)

    full = pl.BlockSpec((tm, d), lambda i: (i, 0))
    return pl.pallas_call(
        body, name=name, grid=(t // tm,),
        in_specs=[full, full, _row(d), _resident((None, k, d), lambda i: (layer, 0, 0))],
        out_specs=[full, pl.BlockSpec((tm, k), lambda i: (i, 0)), _row(d), _row(d)],
        out_shape=[SDS((t, d), BF16), SDS((t, k), BF16), SDS((1, d), F32), SDS((1, d), F32)],
        compiler_params=_params(1),
    )(dxo, y, gate, w)


def _matmul_tn(a, b, name):
    t, ka = a.shape
    nb = b.shape[1]
    tka, tnb, tt = _tile(ka, _WGRAD_TILE, _LANES), _tile(nb, _WGRAD_TILE, _LANES), _tile(t, 2 * _ROW_TILE, 16)
    nt = t // tt

    def body(a_ref, b_ref, o_ref, acc):
        s = pl.program_id(2)

        @pl.when(s == 0)
        def _():
            acc[...] = jnp.zeros_like(acc)

        acc[...] += _tn(a_ref[...], b_ref[...])

        @pl.when(s == nt - 1)
        def _():
            o_ref[...] = acc[...].astype(BF16)

    return pl.pallas_call(
        body, name=name, grid=(ka // tka, nb // tnb, nt),
        in_specs=[pl.BlockSpec((tt, tka), lambda i, j, s: (s, i)), pl.BlockSpec((tt, tnb), lambda i, j, s: (s, j))],
        out_specs=pl.BlockSpec((tka, tnb), lambda i, j, s: (i, j)),
        out_shape=SDS((ka, nb), BF16), scratch_shapes=[pltpu.VMEM((tka, tnb), F32)], compiler_params=_params(3),
    )(a, b)


def _inproj_bwd(dz, w, layer, x, dxo, gn, sc, name):
    t, n = dz.shape
    d = x.shape[1]
    tm, tk = _tile(t, _ROW_TILE, 16), _tile(n, _COL_TILE, _LANES)

    def body(dz_ref, w_ref, x_ref, dxo_ref, gn_ref, sc_ref, dx_ref, dgn_ref, dsc_ref, dsh_ref, acc):
        @pl.when(pl.program_id(0) == 0)
        def _():
            dgn_ref[...] = jnp.zeros_like(dgn_ref)
            dsc_ref[...] = jnp.zeros_like(dsc_ref)
            dsh_ref[...] = jnp.zeros_like(dsh_ref)

        for c0 in range(0, n, tk):
            part = _nt(dz_ref[:, c0:c0 + tk], w_ref[:, c0:c0 + tk])
            if c0 == 0:
                acc[...] = part
            else:
                acc[...] += part
        dh = acc[...]
        xf = x_ref[...]
        r = lax.rsqrt(jnp.mean(xf * xf, axis=-1, keepdims=True) + _NORM_EPS)
        xn = xf * r
        gnv = gn_ref[...]
        dsh_ref[...] += jnp.sum(dh, axis=0, keepdims=True)
        dsc_ref[...] += jnp.sum(dh * (xn * gnv), axis=0, keepdims=True)
        drn = dh * (1.0 + sc_ref[...])
        dgn_ref[...] += jnp.sum(drn * xn, axis=0, keepdims=True)
        dxn = drn * gnv
        dx_ref[...] = dxo_ref[...] + r * (dxn - xn * jnp.mean(dxn * xn, axis=-1, keepdims=True))

    full = pl.BlockSpec((tm, d), lambda i: (i, 0))
    return pl.pallas_call(
        body, name=name, grid=(t // tm,),
        in_specs=[pl.BlockSpec((tm, n), lambda i: (i, 0)), _resident((None, d, n), lambda i: (layer, 0, 0)),
                  full, full, _row(d), _row(d)],
        out_specs=[full, _row(d), _row(d), _row(d)],
        out_shape=[SDS((t, d), F32), SDS((1, d), F32), SDS((1, d), F32), SDS((1, d), F32)],
        scratch_shapes=[pltpu.VMEM((tm, d), F32)], compiler_params=_params(1),
    )(dz, w, x, dxo, gn, sc)


def _final_loss(x, g, target, name):
    t, d = x.shape
    tm = _tile(t, _ROW_TILE, 8)

    def body(x_ref, g_ref, t_ref, loss_ref, dx_ref, dg_ref):
        @pl.when(pl.program_id(0) == 0)
        def _():
            loss_ref[...] = jnp.zeros_like(loss_ref)
            dg_ref[...] = jnp.zeros_like(dg_ref)

        xf = x_ref[...]
        r = lax.rsqrt(jnp.mean(xf * xf, axis=-1, keepdims=True) + _NORM_EPS)
        xn = xf * r
        gv = g_ref[...]
        e = xn * gv - t_ref[...]
        per_row = jnp.mean(e * e, axis=-1, keepdims=True)
        loss_ref[...] += 0.5 * jnp.sum(per_row, axis=0, keepdims=True)
        dy = e * (1.0 / d)
        dg_ref[...] += jnp.sum(dy * xn, axis=0, keepdims=True)
        dxn = dy * gv
        dx_ref[...] = r * (dxn - xn * jnp.mean(dxn * xn, axis=-1, keepdims=True))

    full = pl.BlockSpec((tm, d), lambda i: (i, 0))
    return pl.pallas_call(
        body, name=name, grid=(t // tm,), in_specs=[full, _row(d), full],
        out_specs=[_row(_LANES), full, _row(d)],
        out_shape=[SDS((1, _LANES), F32), SDS((t, d), F32), SDS((1, d), F32)], compiler_params=_params(1),
    )(x, g, target)


def _attn_tables(nh):
    group = nh // _N_KV
    slopes = 2.0 ** (-8.0 * jnp.arange(1, nh + 1, dtype=F32) / nh)
    qpos = jnp.arange(_ATTN_BLOCK) + _ATTN_BLOCK
    kpos = jnp.arange(2 * _ATTN_BLOCK)
    dist = qpos[:, None] - kpos[None, :]
    band = (dist >= 0) & (dist < _ATTN_BLOCK)
    bias = jnp.where(band[None], -slopes[:, None, None] * dist.astype(F32)[None], _NEG_INF)
    first = jnp.where((kpos < _ATTN_BLOCK)[None, None, :], _NEG_INF, bias)
    return jnp.stack([first, bias]).reshape(2, _N_KV, group * _ATTN_BLOCK, 2 * _ATTN_BLOCK)


def _attn_probs(q_ref, kp_ref, kc_ref, tab_ref, sink_ref, kv, group):
    hd, blk = _HEAD_DIM, _ATTN_BLOCK
    k2 = jnp.concatenate([kp_ref[:, kv * hd:(kv + 1) * hd], kc_ref[:, kv * hd:(kv + 1) * hd]], axis=0)
    q4 = jnp.concatenate([q_ref[:, (kv * group + g) * hd:(kv * group + g + 1) * hd] for g in range(group)], axis=0)
    s = _nt(q4, k2) * (hd ** -0.5) + tab_ref[kv]
    sink = jnp.concatenate([jnp.full((blk, 1), sink_ref[kv * group + g], F32) for g in range(group)], axis=0)
    m = jnp.maximum(jnp.max(s, axis=-1, keepdims=True), sink)
    e = jnp.exp(s - m)
    es = jnp.exp(sink - m)
    inv = 1.0 / (jnp.sum(e, axis=-1, keepdims=True) + es)
    return q4, k2, e * inv, es * inv


def _attn_specs(nh, order):
    group = nh // _N_KV
    blk, kvw = _ATTN_BLOCK, _N_KV * _HEAD_DIM
    prev = lambda i: jnp.maximum(order(i) - 1, 0)
    return [
        pl.BlockSpec((blk, nh * _HEAD_DIM), lambda i: (order(i), 0)),
        pl.BlockSpec((blk, kvw), lambda i: (prev(i), group)),
        pl.BlockSpec((blk, kvw), lambda i: (order(i), group)),
        pl.BlockSpec((blk, kvw), lambda i: (prev(i), group + 1)),
        pl.BlockSpec((blk, kvw), lambda i: (order(i), group + 1)),
        pl.BlockSpec((None, _N_KV, group * blk, 2 * blk), lambda i: (jnp.minimum(order(i), 1), 0, 0, 0)),
        pl.BlockSpec(memory_space=pltpu.SMEM),
    ]


def _attn_fwd(qkv, tables, sinks, nh, name):
    t = qkv.shape[0]
    group, hd, blk = nh // _N_KV, _HEAD_DIM, _ATTN_BLOCK

    def body(q_ref, kp_ref, kc_ref, vp_ref, vc_ref, tab_ref, sink_ref, o_ref):
        outs = [None] * nh
        for kv in range(_N_KV):
            _, _, p, _ = _attn_probs(q_ref, kp_ref, kc_ref, tab_ref, sink_ref, kv, group)
            v2 = jnp.concatenate([vp_ref[:, kv * hd:(kv + 1) * hd], vc_ref[:, kv * hd:(kv + 1) * hd]], axis=0)
            o4 = jnp.dot(p.astype(BF16), v2, preferred_element_type=F32)
            for g in range(group):
                outs[kv * group + g] = o4[g * blk:(g + 1) * blk, :]
        o_ref[...] = jnp.concatenate(outs, axis=1).astype(BF16)

    return pl.pallas_call(
        body, name=name, grid=(t // blk,), in_specs=_attn_specs(nh, lambda i: i),
        out_specs=pl.BlockSpec((blk, nh * hd), lambda i: (i, 0)), out_shape=SDS((t, nh * hd), BF16),
        compiler_params=_params(1),
    )(qkv, qkv, qkv, qkv, qkv, tables, sinks)


def _attn_bwd(qkv, do, tables, sinks, nh, name):
    t, wq = qkv.shape
    group, hd, blk = nh // _N_KV, _HEAD_DIM, _ATTN_BLOCK
    nb = t // blk
    kvw = _N_KV * hd
    order = lambda i: nb - 1 - i

    def body(q_ref, kp_ref, kc_ref, vp_ref, vc_ref, tab_ref, sink_ref, do_ref, dqkv_ref, dsink_ref, db_ref, ck, cv):
        i = pl.program_id(0)

        @pl.when(i == 0)
        def _():
            ck[...] = jnp.zeros_like(ck)
            cv[...] = jnp.zeros_like(cv)
            dsink_ref[...] = jnp.zeros_like(dsink_ref)
            db_ref[...] = jnp.zeros_like(db_ref)

        lane = lax.broadcasted_iota(jnp.int32, (1, _LANES), 1)
        dq, dk, dv = [None] * nh, [None] * _N_KV, [None] * _N_KV
        dsink = jnp.zeros((1, _LANES), F32)
        for kv in range(_N_KV):
            q4, k2, p, ps = _attn_probs(q_ref, kp_ref, kc_ref, tab_ref, sink_ref, kv, group)
            v2 = jnp.concatenate([vp_ref[:, kv * hd:(kv + 1) * hd], vc_ref[:, kv * hd:(kv + 1) * hd]], axis=0)
            do4 = jnp.concatenate([do_ref[:, (kv * group + g) * hd:(kv * group + g + 1) * hd] for g in range(group)], axis=0)
            dp = _nt(do4, v2)
            dl = jnp.sum(p * dp, axis=-1, keepdims=True)
            ds = (p * (dp - dl)).astype(BF16)
            dsk = -ps * dl
            for g in range(group):
                dsink = dsink + jnp.where(lane == kv * group + g, jnp.sum(dsk[g * blk:(g + 1) * blk, :]), 0.0)
            dq4 = jnp.dot(ds, k2, preferred_element_type=F32) * (hd ** -0.5)
            for g in range(group):
                dq[kv * group + g] = dq4[g * blk:(g + 1) * blk, :]
            dk2 = _tn(ds, q4) * (hd ** -0.5)
            dv2 = _tn(p.astype(BF16), do4)
            dk[kv] = dk2[blk:, :] + ck[:, kv * hd:(kv + 1) * hd]
            dv[kv] = dv2[blk:, :] + cv[:, kv * hd:(kv + 1) * hd]
            ck[:, kv * hd:(kv + 1) * hd] = dk2[:blk, :]
            cv[:, kv * hd:(kv + 1) * hd] = dv2[:blk, :]
        dqkv = jnp.concatenate(dq + dk + dv, axis=1)
        dqkv_ref[...] = dqkv.astype(BF16)
        db_ref[...] += jnp.sum(dqkv, axis=0, keepdims=True)
        dsink_ref[...] += dsink

    return pl.pallas_call(
        body, name=name, grid=(nb,),
        in_specs=_attn_specs(nh, order) + [pl.BlockSpec((blk, nh * hd), lambda i: (order(i), 0))],
        out_specs=[pl.BlockSpec((blk, wq), lambda i: (order(i), 0)), _row(_LANES), _row(wq)],
        out_shape=[SDS((t, wq), BF16), SDS((1, _LANES), F32), SDS((1, wq), F32)],
        scratch_shapes=[pltpu.VMEM((blk, kvw), F32), pltpu.VMEM((blk, kvw), F32)], compiler_params=_params(1),
    )(qkv, qkv, qkv, qkv, qkv, tables, sinks, do)


def _taps(src_ref, w_ref, dst_ref, n_rows, width, offs, rg):
    cg = _tile(width, 512, _LANES)
    for c0 in range(0, width, cg):
        wk = [w_ref[k:k + 1, c0:c0 + cg] for k, _ in offs]
        for r0 in range(0, n_rows, rg):
            acc = None
            for (_, off), wv in zip(offs, wk):
                term = wv * src_ref[r0 + off:r0 + off + rg, c0:c0 + cg]
                acc = term if acc is None else acc + term
            dst_ref[r0:r0 + rg, c0:c0 + cg] = acc


def _tap_grads(dy_ref, z_ref, out_ref, n_rows, width, offs, rg):
    cg = _tile(width, 512, _LANES)
    for c0 in range(0, width, cg):
        for k, off in offs:
            acc = None
            for r0 in range(0, n_rows, rg):
                term = dy_ref[r0:r0 + rg, c0:c0 + cg] * z_ref[r0 + off:r0 + off + rg, c0:c0 + cg]
                acc = term if acc is None else acc + term
            out_ref[k:k + 1, c0:c0 + cg] += jnp.sum(acc, axis=0, keepdims=True)


def _conv_mid_fwd(ag, dw, dwb, lng, lnb, name):
    t, c2 = ag.shape
    c = c2 // 2
    kw = dw.shape[0]
    hl = _CONV_HALO
    tm = _tile(t, 256, hl)
    per = tm // hl

    def body(agp_ref, ag_ref, dw_ref, dwb_ref, lng_ref, lnb_ref, o_ref, zext, yb):
        i = pl.program_id(0)
        glu = lambda ref: ref[:, :c].astype(F32) * _sigmoid(ref[:, c:].astype(F32))
        zext[0:hl, :] = jnp.where(i > 0, glu(agp_ref), 0.0)
        zext[hl:, :] = glu(ag_ref)
        _taps(zext, dw_ref, yb, tm, c, [(k, hl - (kw - 1) + k) for k in range(kw)], 32)
        y = yb[...] + dwb_ref[...]
        mu = jnp.mean(y, axis=-1, keepdims=True)
        yc = y - mu
        rstd = lax.rsqrt(jnp.mean(yc * yc, axis=-1, keepdims=True) + _NORM_EPS)
        ln = yc * rstd * lng_ref[...] + lnb_ref[...]
        o_ref[...] = (ln * _sigmoid(ln)).astype(BF16)

    return pl.pallas_call(
        body, name=name, grid=(t // tm,),
        in_specs=[pl.BlockSpec((hl, c2), lambda i: (jnp.maximum(i * per - 1, 0), 0)), pl.BlockSpec((tm, c2), lambda i: (i, 0)),
                  pl.BlockSpec((kw, c), lambda i: (0, 0)), _row(c), _row(c), _row(c)],
        out_specs=pl.BlockSpec((tm, c), lambda i: (i, 0)), out_shape=SDS((t, c), BF16),
        scratch_shapes=[pltpu.VMEM((hl + tm, c), F32), pltpu.VMEM((tm, c), F32)], compiler_params=_params(1),
    )(ag, ag, dw, dwb, lng, lnb)


def _conv_mid_bwd(ag, dzc, dw, dwb, lng, lnb, name):
    t, c2 = ag.shape
    c = c2 // 2
    kw = dw.shape[0]
    hl = _CONV_HALO
    tm = _tile(t, 256, hl)
    per = tm // hl
    nt = t // tm
    last_halo = t // hl - 1

    def body(agp_ref, ag_ref, agn_ref, dzc_ref, dzcn_ref, dw_ref, dwb_ref, lng_ref, lnb_ref,
             dag_ref, ddw_ref, ddwb_ref, dlng_ref, dlnb_ref, dbin_ref, zext, yext, dyext, dzb):
        i = pl.program_id(0)

        @pl.when(i == 0)
        def _():
            for r in (ddw_ref, ddwb_ref, dlng_ref, dlnb_ref, dbin_ref):
                r[...] = jnp.zeros_like(r)

        glu = lambda ref: ref[:, :c].astype(F32) * _sigmoid(ref[:, c:].astype(F32))
        zext[0:hl, :] = jnp.where(i > 0, glu(agp_ref), 0.0)
        zext[hl:hl + tm, :] = glu(ag_ref)
        zext[hl + tm:, :] = glu(agn_ref)
        fwd_offs = [(k, hl - (kw - 1) + k) for k in range(kw)]
        _taps(zext, dw_ref, yext, tm + hl, c, fwd_offs, 32)
        y = yext[...] + dwb_ref[...]
        mu = jnp.mean(y, axis=-1, keepdims=True)
        yc = y - mu
        rstd = lax.rsqrt(jnp.mean(yc * yc, axis=-1, keepdims=True) + _NORM_EPS)
        xhat = yc * rstd
        lngv = lng_ref[...]
        ln = xhat * lngv + lnb_ref[...]
        sg = _sigmoid(ln)
        dz_out = jnp.concatenate([dzc_ref[...].astype(F32), jnp.where(i < nt - 1, dzcn_ref[...].astype(F32), 0.0)], axis=0)
        dln = dz_out * (sg * (1.0 + ln * (1.0 - sg)))
        dlng_ref[...] += jnp.sum((dln * xhat)[:tm], axis=0, keepdims=True)
        dlnb_ref[...] += jnp.sum(dln[:tm], axis=0, keepdims=True)
        dxh = dln * lngv
        dy = rstd * (dxh - jnp.mean(dxh, axis=-1, keepdims=True) - xhat * jnp.mean(dxh * xhat, axis=-1, keepdims=True))
        dyext[...] = dy
        ddwb_ref[...] += jnp.sum(dy[:tm], axis=0, keepdims=True)
        _taps(dyext, dw_ref, dzb, tm, c, [(k, kw - 1 - k) for k in range(kw)], 32)
        _tap_grads(dyext, zext, ddw_ref, tm, c, fwd_offs, 32)
        a = ag_ref[:, :c].astype(F32)
        sgg = _sigmoid(ag_ref[:, c:].astype(F32))
        dz = dzb[...]
        da = dz * sgg
        dg = dz * a * sgg * (1.0 - sgg)
        dag_ref[:, :c] = da.astype(BF16)
        dag_ref[:, c:] = dg.astype(BF16)
        dbin_ref[:, :c] += jnp.sum(da, axis=0, keepdims=True)
        dbin_ref[:, c:] += jnp.sum(dg, axis=0, keepdims=True)

    prev = lambda i: (jnp.maximum(i * per - 1, 0), 0)
    nxt = lambda i: (jnp.minimum((i + 1) * per, last_halo), 0)
    return pl.pallas_call(
        body, name=name, grid=(nt,),
        in_specs=[pl.BlockSpec((hl, c2), prev), pl.BlockSpec((tm, c2), lambda i: (i, 0)), pl.BlockSpec((hl, c2), nxt),
                  pl.BlockSpec((tm, c), lambda i: (i, 0)), pl.BlockSpec((hl, c), nxt),
                  pl.BlockSpec((kw, c), lambda i: (0, 0)), _row(c), _row(c), _row(c)],
        out_specs=[pl.BlockSpec((tm, c2), lambda i: (i, 0)), pl.BlockSpec((kw, c), lambda i: (0, 0)), _row(c), _row(c), _row(c), _row(c2)],
        out_shape=[SDS((t, c2), BF16), SDS((kw, c), F32), SDS((1, c), F32), SDS((1, c), F32), SDS((1, c), F32), SDS((1, c2), F32)],
        scratch_shapes=[pltpu.VMEM((hl + tm + hl, c), F32), pltpu.VMEM((tm + hl, c), F32), pltpu.VMEM((tm + hl, c), F32),
                        pltpu.VMEM((tm, c), F32)],
        compiler_params=_params(1),
    )(ag, ag, ag, dzc, dzc, dw, dwb, lng, lnb)


def _ffn_mid_fwd(zf, dw, dwb, name):
    t, f2 = zf.shape
    f = f2 // 2
    kw = dw.shape[0]
    hl = _FFN_HALO
    tm = _tile(t, 256, hl)
    per = tm // hl
    tc = _tile(f, 256, _LANES)
    offs = [(k, hl - (kw - 1) + k) for k in range(kw)]

    def body(zp_ref, z_ref, dw_ref, dwb_ref, o_ref, zext, cb):
        i = pl.program_id(0)
        zext[0:hl, :] = jnp.where(i > 0, zp_ref[...].astype(F32), 0.0)
        zext[hl:, :] = z_ref[...].astype(F32)
        _taps(zext, dw_ref, cb, tm, f2, offs, 16)
        for c0 in range(0, f, tc):
            g = cb[:, c0:c0 + tc] + dwb_ref[:, c0:c0 + tc]
            u = cb[:, f + c0:f + c0 + tc] + dwb_ref[:, f + c0:f + c0 + tc]
            o_ref[:, c0:c0 + tc] = (g * _sigmoid(g) * u).astype(BF16)

    return pl.pallas_call(
        body, name=name, grid=(t // tm,),
        in_specs=[pl.BlockSpec((hl, f2), lambda i: (jnp.maximum(i * per - 1, 0), 0)), pl.BlockSpec((tm, f2), lambda i: (i, 0)),
                  pl.BlockSpec((kw, f2), lambda i: (0, 0)), _row(f2)],
        out_specs=pl.BlockSpec((tm, f), lambda i: (i, 0)), out_shape=SDS((t, f), BF16),
        scratch_shapes=[pltpu.VMEM((hl + tm, f2), F32), pltpu.VMEM((tm, f2), F32)], compiler_params=_params(1),
    )(zf, zf, dw, dwb)


def _ffn_mid_bwd(zf, dact, dw, dwb, name):
    t, f2 = zf.shape
    f = f2 // 2
    kw = dw.shape[0]
    hl = _FFN_HALO
    tm = _tile(t, 128, hl)
    per = tm // hl
    nt = t // tm
    last_halo = t // hl - 1
    tc = _tile(f, 256, _LANES)
    fwd_offs = [(k, hl - (kw - 1) + k) for k in range(kw)]

    def body(zp_ref, z_ref, zn_ref, da_ref, dan_ref, dw_ref, dwb_ref, dzf_ref, ddw_ref, ddwb_ref, zext, cext, dcext, dzb):
        i = pl.program_id(0)

        @pl.when(i == 0)
        def _():
            ddw_ref[...] = jnp.zeros_like(ddw_ref)
            ddwb_ref[...] = jnp.zeros_like(ddwb_ref)

        zext[0:hl, :] = jnp.where(i > 0, zp_ref[...].astype(F32), 0.0)
        zext[hl:hl + tm, :] = z_ref[...].astype(F32)
        zext[hl + tm:, :] = zn_ref[...].astype(F32)
        _taps(zext, dw_ref, cext, tm + hl, f2, fwd_offs, 16)
        for c0 in range(0, f, tc):
            g = cext[:, c0:c0 + tc] + dwb_ref[:, c0:c0 + tc]
            u = cext[:, f + c0:f + c0 + tc] + dwb_ref[:, f + c0:f + c0 + tc]
            da = jnp.concatenate([da_ref[:, c0:c0 + tc].astype(F32),
                                  jnp.where(i < nt - 1, dan_ref[:, c0:c0 + tc].astype(F32), 0.0)], axis=0)
            sg = _sigmoid(g)
            dcg = da * u * (sg * (1.0 + g * (1.0 - sg)))
            dcu = da * (g * sg)
            dcext[:, c0:c0 + tc] = dcg
            dcext[:, f + c0:f + c0 + tc] = dcu
            ddwb_ref[:, c0:c0 + tc] += jnp.sum(dcg[:tm], axis=0, keepdims=True)
            ddwb_ref[:, f + c0:f + c0 + tc] += jnp.sum(dcu[:tm], axis=0, keepdims=True)
        _taps(dcext, dw_ref, dzb, tm, f2, [(k, kw - 1 - k) for k in range(kw)], 16)
        _tap_grads(dcext, zext, ddw_ref, tm, f2, fwd_offs, 16)
        dzf_ref[...] = dzb[...].astype(BF16)

    prev = lambda i: (jnp.maximum(i * per - 1, 0), 0)
    nxt = lambda i: (jnp.minimum((i + 1) * per, last_halo), 0)
    return pl.pallas_call(
        body, name=name, grid=(nt,),
        in_specs=[pl.BlockSpec((hl, f2), prev), pl.BlockSpec((tm, f2), lambda i: (i, 0)), pl.BlockSpec((hl, f2), nxt),
                  pl.BlockSpec((tm, f), lambda i: (i, 0)), pl.BlockSpec((hl, f), nxt),
                  pl.BlockSpec((kw, f2), lambda i: (0, 0)), _row(f2)],
        out_specs=[pl.BlockSpec((tm, f2), lambda i: (i, 0)), pl.BlockSpec((kw, f2), lambda i: (0, 0)), _row(f2)],
        out_shape=[SDS((t, f2), BF16), SDS((kw, f2), F32), SDS((1, f2), F32)],
        scratch_shapes=[pltpu.VMEM((hl + tm + hl, f2), F32), pltpu.VMEM((tm + hl, f2), F32), pltpu.VMEM((tm + hl, f2), F32),
                        pltpu.VMEM((tm, f2), F32)],
        compiler_params=_params(1),
    )(zf, zf, zf, dact, dact, dw, dwb)


_INV_SQRT2 = 0.7071067811865476
_INV_SQRT_2PI = 0.3989422804014327


def _sgu_common(zin_ref, lng_ref, lnb_ref, hh):
    z = zin_ref[...].astype(F32)
    cdf = 0.5 * (1.0 + lax.erf(z * _INV_SQRT2))
    ge = z * cdf
    u, v = ge[:, :hh], ge[:, hh:]
    mu = jnp.mean(v, axis=-1, keepdims=True)
    vc = v - mu
    rstd = lax.rsqrt(jnp.mean(vc * vc, axis=-1, keepdims=True) + _NORM_EPS)
    vhat = vc * rstd
    vn = vhat * lng_ref[...] + lnb_ref[...]
    return z, cdf, u, vhat, rstd, vn


def _sgu_wm(ws_ref, g, ch):
    rows = lax.broadcasted_iota(jnp.int32, (ch, ch), 0)
    cols = lax.broadcasted_iota(jnp.int32, (ch, ch), 1)
    return jnp.where(rows >= cols, ws_ref[g], 0.0).astype(BF16)


def _sgu_mid_fwd(zin, lng, lnb, ws, bs_t, name):
    t, h2 = zin.shape
    hh = h2 // 2
    ng, ch = ws.shape[0], ws.shape[1]
    hg = hh // ng
    tm = _tile(t, 256, ch)

    def body(zin_ref, lng_ref, lnb_ref, ws_ref, bs_ref, o_ref):
        _, _, u, _, _, vn = _sgu_common(zin_ref, lng_ref, lnb_ref, hh)
        vnb = vn.astype(BF16)
        for g in range(ng):
            wm = _sgu_wm(ws_ref, g, ch)
            for cc in range(tm // ch):
                rs, cs = slice(cc * ch, (cc + 1) * ch), slice(g * hg, (g + 1) * hg)
                vv = jnp.dot(wm, vnb[rs, cs], preferred_element_type=F32) + bs_ref[:, g:g + 1]
                o_ref[rs, cs] = (u[rs, cs] * vv).astype(BF16)

    return pl.pallas_call(
        body, name=name, grid=(t // tm,),
        in_specs=[pl.BlockSpec((tm, h2), lambda i: (i, 0)), _row(hh), _row(hh),
                  pl.BlockSpec((ng, ch, ch), lambda i: (0, 0, 0)), pl.BlockSpec((ch, ng), lambda i: (0, 0))],
        out_specs=pl.BlockSpec((tm, hh), lambda i: (i, 0)), out_shape=SDS((t, hh), BF16), compiler_params=_params(1),
    )(zin, lng, lnb, ws, bs_t)


def _sgu_mid_bwd(zin, duv, lng, lnb, ws, bs_t, name):
    t, h2 = zin.shape
    hh = h2 // 2
    ng, ch = ws.shape[0], ws.shape[1]
    hg = hh // ng
    tm = _tile(t, 128, ch)

    def body(zin_ref, duv_ref, lng_ref, lnb_ref, ws_ref, bs_ref, dzin_ref, dlng_ref, dlnb_ref, dws_ref, dbs_ref, dbin_ref, dvn_s, du_s):
        @pl.when(pl.program_id(0) == 0)
        def _():
            for r in (dlng_ref, dlnb_ref, dws_ref, dbs_ref, dbin_ref):
                r[...] = jnp.zeros_like(r)

        z, cdf, u, vhat, rstd, vn = _sgu_common(zin_ref, lng_ref, lnb_ref, hh)
        vnb = vn.astype(BF16)
        duv = duv_ref[...].astype(F32)
        dvv = (duv * u).astype(BF16)
        lane = lax.broadcasted_iota(jnp.int32, (1, _LANES), 1)
        rows = lax.broadcasted_iota(jnp.int32, (ch, ch), 0)
        cols = lax.broadcasted_iota(jnp.int32, (ch, ch), 1)
        dbs = jnp.zeros((ch, _LANES), F32)
        for g in range(ng):
            wm = _sgu_wm(ws_ref, g, ch)
            dwm = jnp.zeros((ch, ch), F32)
            for cc in range(tm // ch):
                rs, cs = slice(cc * ch, (cc + 1) * ch), slice(g * hg, (g + 1) * hg)
                vv = jnp.dot(wm, vnb[rs, cs], preferred_element_type=F32) + bs_ref[:, g:g + 1]
                du_s[rs, cs] = duv[rs, cs] * vv
                dvn_s[rs, cs] = _tn(wm, dvv[rs, cs])
                dwm = dwm + _nt(dvv[rs, cs], vnb[rs, cs])
                dbs = dbs + jnp.where(lane == g, jnp.sum(dvv[rs, cs].astype(F32), axis=-1, keepdims=True), 0.0)
            dws_ref[g] += jnp.where(rows >= cols, dwm, 0.0)
        dbs_ref[...] += dbs
        dvn = dvn_s[...]
        dlng_ref[...] += jnp.sum(dvn * vhat, axis=0, keepdims=True)
        dlnb_ref[...] += jnp.sum(dvn, axis=0, keepdims=True)
        dxh = dvn * lng_ref[...]
        dv = rstd * (dxh - jnp.mean(dxh, axis=-1, keepdims=True) - vhat * jnp.mean(dxh * vhat, axis=-1, keepdims=True))
        dgelu = cdf + z * (_INV_SQRT_2PI * jnp.exp(-0.5 * z * z))
        dzu = du_s[...] * dgelu[:, :hh]
        dzv = dv * dgelu[:, hh:]
        dzin_ref[:, :hh] = dzu.astype(BF16)
        dzin_ref[:, hh:] = dzv.astype(BF16)
        dbin_ref[:, :hh] += jnp.sum(dzu, axis=0, keepdims=True)
        dbin_ref[:, hh:] += jnp.sum(dzv, axis=0, keepdims=True)

    return pl.pallas_call(
        body, name=name, grid=(t // tm,),
        in_specs=[pl.BlockSpec((tm, h2), lambda i: (i, 0)), pl.BlockSpec((tm, hh), lambda i: (i, 0)), _row(hh), _row(hh),
                  pl.BlockSpec((ng, ch, ch), lambda i: (0, 0, 0)), pl.BlockSpec((ch, ng), lambda i: (0, 0))],
        out_specs=[pl.BlockSpec((tm, h2), lambda i: (i, 0)), _row(hh), _row(hh), pl.BlockSpec((ng, ch, ch), lambda i: (0, 0, 0)),
                   pl.BlockSpec((ch, _LANES), lambda i: (0, 0)), _row(h2)],
        out_shape=[SDS((t, h2), BF16), SDS((1, hh), F32), SDS((1, hh), F32), SDS((ng, ch, ch), F32), SDS((ch, _LANES), F32),
                   SDS((1, h2), F32)],
        scratch_shapes=[pltpu.VMEM((tm, hh), F32), pltpu.VMEM((tm, hh), F32)], compiler_params=_params(1),
    )(zin, duv, lng, lnb, ws, bs_t)


def _ada_mod(c_all, ada_w, ada_b, name):
    nl, d, n = ada_w.shape
    nb = c_all.shape[0]
    tn = _tile(n, _COL_TILE, _LANES)

    def body(c_ref, w_ref, b_ref, o_ref):
        cv = c_ref[...]
        ca = cv * _sigmoid(cv)
        o_ref[...] = jnp.dot(ca, w_ref[...], preferred_element_type=F32, precision=lax.Precision.HIGHEST) + b_ref[...]

    return pl.pallas_call(
        body, name=name, grid=(nl, n // tn),
        in_specs=[pl.BlockSpec((nb, d), lambda l, j: (0, 0)), pl.BlockSpec((None, d, tn), lambda l, j: (l, 0, j)),
                  pl.BlockSpec((None, 1, tn), lambda l, j: (l, 0, j))],
        out_specs=pl.BlockSpec((None, nb, tn), lambda l, j: (l, 0, j)), out_shape=SDS((nl, nb, n), F32),
        compiler_params=_params(2),
    )(c_all, ada_w, ada_b)


def _ada_wgrad(c_all_t, dmod, name):
    d, nb = c_all_t.shape
    nl, _, n = dmod.shape
    tn = _tile(n, _COL_TILE, _LANES)

    def body(c_ref, dm_ref, o_ref):
        cv = c_ref[...]
        ca = cv * _sigmoid(cv)
        acc = ca[:, 0:1] * dm_ref[0:1, :]
        for b in range(1, nb):
            acc = acc + ca[:, b:b + 1] * dm_ref[b:b + 1, :]
        o_ref[...] = acc

    return pl.pallas_call(
        body, name=name, grid=(nl, n // tn),
        in_specs=[pl.BlockSpec((d, nb), lambda l, j: (0, 0)), pl.BlockSpec((None, nb, tn), lambda l, j: (l, 0, j))],
        out_specs=pl.BlockSpec((None, d, tn), lambda l, j: (l, 0, j)), out_shape=SDS((nl, d, n), F32),
        compiler_params=_params(2),
    )(c_all_t, dmod)


def _as_rows(a):
    return a.reshape(-1, a.shape[-1])


def _row_tile(r, c, n_arrays):
    budget = _VMEM_LIMIT_BYTES // (4 * 2 * n_arrays * 4)
    return _tile(r, max(8, budget // max(c, 1)), 8)


def _cast_bf16(a, name):
    a2 = _as_rows(a)
    r, c = a2.shape
    tr = _row_tile(r, c, 2)

    def body(a_ref, o_ref):
        o_ref[...] = a_ref[...].astype(BF16)

    spec = pl.BlockSpec((tr, c), lambda i: (i, 0))
    out = pl.pallas_call(body, name=name, grid=(r // tr,), in_specs=[spec], out_specs=spec, out_shape=SDS((r, c), BF16),
                         compiler_params=_params(1))(a2)
    return out.reshape(a.shape)


def _sum_slots(parts, name):
    n = parts.shape[0]
    p2 = parts.reshape(n, -1, parts.shape[-1])
    r, c = p2.shape[1:]
    tr = _row_tile(r, c, n + 1)

    def body(p_ref, o_ref):
        acc = p_ref[0].astype(F32)
        for s in range(1, n):
            acc = acc + p_ref[s].astype(F32)
        o_ref[...] = acc

    out = pl.pallas_call(body, name=name, grid=(r // tr,), in_specs=[pl.BlockSpec((n, tr, c), lambda i: (0, i, 0))],
                         out_specs=pl.BlockSpec((tr, c), lambda i: (i, 0)), out_shape=SDS((r, c), F32),
                         compiler_params=_params(1))(p2)
    return out.reshape(parts.shape[1:])


def _adamw(w, g_parts, m, v, name):
    w2, m2, v2 = _as_rows(w), _as_rows(m), _as_rows(v)
    gs = [_as_rows(g) for g in g_parts]
    r, c = w2.shape
    ng = len(gs)
    tr = _row_tile(r, c, 7 + ng)
    c1 = 1.0 - _ADAM_B1 ** _ADAM_STEP
    c2 = 1.0 - _ADAM_B2 ** _ADAM_STEP

    def body(*refs):
        w_ref, m_ref, v_ref = refs[:3]
        g_refs = refs[3:3 + ng]
        go_ref, d_ref, mo_ref, vo_ref = refs[3 + ng:]
        g = g_refs[0][...]
        for gr in g_refs[1:]:
            g = g + gr[...]
        mn = _ADAM_B1 * m_ref[...] + (1.0 - _ADAM_B1) * g
        vn = _ADAM_B2 * v_ref[...] + (1.0 - _ADAM_B2) * (g * g)
        go_ref[...] = g
        mo_ref[...] = mn
        vo_ref[...] = vn
        d_ref[...] = -_ADAM_LR * ((mn / c1) / (jnp.sqrt(vn / c2) + _ADAM_EPS) + _ADAM_WD * w_ref[...])

    spec = pl.BlockSpec((tr, c), lambda i: (i, 0))
    outs = pl.pallas_call(body, name=name, grid=(r // tr,), in_specs=[spec] * (3 + ng), out_specs=[spec] * 4,
                          out_shape=[SDS((r, c), F32)] * 4, compiler_params=_params(1))(w2, m2, v2, *gs)
    return tuple(o.reshape(w.shape) for o in outs)


_HBM = pl.BlockSpec(memory_space=pltpu.HBM)
_CHIP_FLIPS = ((1, 0, 0), (0, 1, 0), (1, 1, 0))
_ALL_FLIPS = tuple((a, b, c) for a in (0, 1) for b in (0, 1) for c in (0, 1))[1:]
_SIBLING_FLIP = ((0, 0, 1),)


def _at(ref, idx):
    return ref.at[idx] if idx else ref


def _exchange(name, flips, srcs, out_shapes, items):
    n_in, n_out = len(srcs), len(out_shapes)
    n_remote = len(items) * len(flips)
    n_local = sum(1 for it in items if it[4])

    def body(*refs):
        src_refs, dst_refs = refs[:n_in], refs[n_in:n_in + n_out]
        send_sems, recv_sems, local_sems = refs[n_in + n_out:]
        me = (lax.axis_index("x"), lax.axis_index("y"), lax.axis_index("c"))
        sends, recvs, locs = [], [], []
        n, nl = 0, 0
        for si, di, src_idx, dst_idx, local in items:
            for flip in flips:
                peer = tuple(1 - m if f else m for m, f in zip(me, flip))
                push = pltpu.make_async_remote_copy(
                    src_ref=_at(src_refs[si], src_idx(*peer)), dst_ref=_at(dst_refs[di], dst_idx(*me)),
                    send_sem=send_sems.at[n], recv_sem=recv_sems.at[n], device_id=peer, device_id_type=MESH_IDS)
                push.start()
                sends.append(push)
                recvs.append(pltpu.make_async_remote_copy(
                    src_ref=_at(src_refs[si], src_idx(*me)), dst_ref=_at(dst_refs[di], dst_idx(*peer)),
                    send_sem=send_sems.at[n], recv_sem=recv_sems.at[n], device_id=peer, device_id_type=MESH_IDS))
                n += 1
            if local:
                cp = pltpu.make_async_copy(_at(src_refs[si], src_idx(*me)), _at(dst_refs[di], dst_idx(*me)), local_sems.at[nl])
                cp.start()
                locs.append(cp)
                nl += 1
        for r in recvs:
            r.wait_recv()
        for s in sends:
            s.wait_send()
        for cp in locs:
            cp.wait()

    return pl.pallas_call(
        body, name=name, in_specs=[_HBM] * n_in, out_specs=[_HBM] * n_out, out_shape=list(out_shapes),
        scratch_shapes=[pltpu.SemaphoreType.DMA((n_remote,)), pltpu.SemaphoreType.DMA((n_remote,)),
                        pltpu.SemaphoreType.DMA((max(n_local, 1),))],
    )(*srcs)


def _chip_of(x, y, c):
    return 2 * x + y


def _dev_of(x, y, c):
    return 4 * x + 2 * y + c


def _window(axis, ndim, size):
    def idx(x, y, c):
        return tuple(pl.ds(_chip_of(x, y, c) * size, size) if a == axis else slice(None) for a in range(ndim))
    return idx


def _whole(x, y, c):
    return ()


def _gather_chips(name, shards_axes):
    srcs, shapes, items = [], [], []
    for k, (a, axis) in enumerate(shards_axes):
        full = list(a.shape)
        full[axis] *= _N_CHIPS
        srcs.append(a)
        shapes.append(SDS(tuple(full), a.dtype))
        items.append((k, k, _whole, _window(axis, a.ndim, a.shape[axis]), True))
    return _exchange(name, _CHIP_FLIPS, srcs, shapes, items)


def _gather_all(name, a):
    slot = lambda x, y, c: (_dev_of(x, y, c),)
    return _exchange(name, _ALL_FLIPS, [a], [SDS((_N_DEV,) + a.shape, a.dtype)], [(0, 0, _whole, slot, True)])[0]


def _swap_chips(name, a):
    slot = lambda x, y, c: (_chip_of(x, y, c),)
    return _exchange(name, _CHIP_FLIPS, [a], [SDS(a.shape, a.dtype)], [(0, 0, slot, slot, True)])[0]


def _scatter_grads(name, grads_axes):
    srcs, shapes, items = [], [], []
    for per_layer, axis in grads_axes:
        a0 = per_layer[0]
        shard = list(a0.shape)
        shard[axis] //= _N_CHIPS
        di = len(shapes)
        shapes.append(SDS((_N_CHIPS, len(per_layer)) + tuple(shard), a0.dtype))
        for layer, a in enumerate(per_layer):
            srcs.append(a)
            dst = (lambda lay: lambda x, y, c: (_chip_of(x, y, c), lay))(layer)
            items.append((len(srcs) - 1, di, _window(axis, a.ndim, shard[axis]), dst, True))
    return _exchange(name, _CHIP_FLIPS, srcs, shapes, items)


def _swap_sibling(name, arrays):
    items = [(k, k, _whole, _whole, False) for k in range(len(arrays))]
    return _exchange(name, _SIBLING_FLIP, list(arrays), [SDS(a.shape, a.dtype) for a in arrays], items)


_BIG = ("attn_wqkv", "attn_wo", "conv_w_in", "conv_w_out", "sgu_w_in", "sgu_w_out", "ffn_w_in", "ffn_w_out")
_BIG_AXIS = {"attn_wqkv": 2, "attn_wo": 1, "conv_w_in": 2, "conv_w_out": 1, "sgu_w_in": 2, "sgu_w_out": 1,
             "ffn_w_in": 2, "ffn_w_out": 1}
_SMALL = {"norm1_g": None, "norm2_g": None, "ada_b": None, "attn_bqkv": 1, "attn_sinks": None, "attn_bo": 1,
          "conv_b_in": None, "conv_dw": 2, "conv_dw_b": None, "conv_ln_g": None, "conv_ln_b": None, "conv_b_out": None,
          "sgu_b_in": 1, "sgu_ln_g": 1, "sgu_ln_b": 1, "sgu_ws": None, "sgu_bs": None, "sgu_b_out": 1,
          "ffn_dw": 2, "ffn_dw_b": None, "final_g": None}
_WEIGHTS = ['norm1_g', 'norm2_g', 'ada_w', 'ada_b', 'attn_wqkv', 'attn_bqkv', 'attn_sinks', 'attn_wo', 'attn_bo',
            'conv_w_in', 'conv_b_in', 'conv_dw', 'conv_dw_b', 'conv_ln_g', 'conv_ln_b', 'conv_w_out', 'conv_b_out',
            'sgu_w_in', 'sgu_b_in', 'sgu_ln_g', 'sgu_ln_b', 'sgu_ws', 'sgu_bs', 'sgu_w_out', 'sgu_b_out',
            'ffn_w_in', 'ffn_dw', 'ffn_dw_b', 'ffn_w_out', 'final_g']
_FLAT_COLS = 1024


def _full_shape(a, axis):
    s = list(a.shape)
    if axis is not None:
        s[axis] *= _N_CHIPS
    return tuple(s)


def _pack(arrays):
    flat = jnp.concatenate([a.reshape(-1).astype(F32) for a in arrays])
    pad = (-flat.shape[0]) % (8 * _FLAT_COLS)
    return jnp.pad(flat, (0, pad)).reshape(-1, _FLAT_COLS)


def _unpack(flat2d, shapes):
    flat = flat2d.reshape(-1)
    out, off = [], 0
    for s in shapes:
        n = math.prod(s)
        out.append(flat[off:off + n].reshape(s))
        off += n
    return out


def _shard_of(full, axis, chip):
    if axis is None:
        return full
    size = full.shape[axis] // _N_CHIPS
    return lax.dynamic_slice_in_dim(full, chip * size, size, axis)


def _unshard(gathered, axis):
    moved = jnp.moveaxis(gathered, 0, axis)
    shape = list(gathered.shape[1:])
    shape[axis] *= _N_CHIPS
    return moved.reshape(shape)


def kernel(x, c, norm1_g, norm2_g, ada_w, ada_b, attn_wqkv, attn_bqkv, attn_sinks, attn_wo, attn_bo, conv_w_in, conv_b_in, conv_dw, conv_dw_b, conv_ln_g, conv_ln_b, conv_w_out, conv_b_out, sgu_w_in, sgu_b_in, sgu_ln_g, sgu_ln_b, sgu_ws, sgu_bs, sgu_w_out, sgu_b_out, ffn_w_in, ffn_dw, ffn_dw_b, ffn_w_out, final_g, loss_target, m_norm1_g, m_norm2_g, m_ada_w, m_ada_b, m_attn_wqkv, m_attn_bqkv, m_attn_sinks, m_attn_wo, m_attn_bo, m_conv_w_in, m_conv_b_in, m_conv_dw, m_conv_dw_b, m_conv_ln_g, m_conv_ln_b, m_conv_w_out, m_conv_b_out, m_sgu_w_in, m_sgu_b_in, m_sgu_ln_g, m_sgu_ln_b, m_sgu_ws, m_sgu_bs, m_sgu_w_out, m_sgu_b_out, m_ffn_w_in, m_ffn_dw, m_ffn_dw_b, m_ffn_w_out, m_final_g, v_norm1_g, v_norm2_g, v_ada_w, v_ada_b, v_attn_wqkv, v_attn_bqkv, v_attn_sinks, v_attn_wo, v_attn_bo, v_conv_w_in, v_conv_b_in, v_conv_dw, v_conv_dw_b, v_conv_ln_g, v_conv_ln_b, v_conv_w_out, v_conv_b_out, v_sgu_w_in, v_sgu_b_in, v_sgu_ln_g, v_sgu_ln_b, v_sgu_ws, v_sgu_bs, v_sgu_w_out, v_sgu_b_out, v_ffn_w_in, v_ffn_dw, v_ffn_dw_b, v_ffn_w_out, v_final_g):
    args = dict(locals())
    wts = {n: args[n] for n in _WEIGHTS}
    mom_m = {n: args["m_" + n] for n in _WEIGHTS}
    mom_v = {n: args["v_" + n] for n in _WEIGHTS}

    ix, iy, ic = lax.axis_index("x"), lax.axis_index("y"), lax.axis_index("c")
    chip = 2 * ix + iy
    xs = x[0]
    tgt = loss_target[0]
    t, d = xs.shape
    nh = d // _HEAD_DIM
    depth = ada_w.shape[0]
    ncols = ada_w.shape[2]
    n_mod = ncols * _N_CHIPS // d

    small_sharded = [n for n in _WEIGHTS if _SMALL.get(n) is not None]
    packed_small = _pack([wts[n] for n in small_sharded])
    big_bf16 = [(_cast_bf16(wts[n], "cast_" + n), _BIG_AXIS[n]) for n in _BIG]
    gathered = _gather_chips("gather_weights", big_bf16 + [(packed_small[None], 0)])
    wfull = dict(zip(_BIG, gathered[:len(_BIG)]))
    small_rows = gathered[len(_BIG)]
    pieces = [_unpack(small_rows[s], [wts[n].shape for n in small_sharded]) for s in range(_N_CHIPS)]
    full = {n: wts[n] for n in _WEIGHTS if n in _SMALL and _SMALL[n] is None}
    for k, n in enumerate(small_sharded):
        full[n] = _unshard(jnp.stack([pieces[s][k] for s in range(_N_CHIPS)]), _SMALL[n])

    c_all = _gather_all("gather_c", c)[:, 0, :]
    ada_b_cols = lax.dynamic_slice_in_dim(ada_b, chip * ncols, ncols, 1)[:, None, :]
    mod_cols = _ada_mod(c_all, ada_w, ada_b_cols, "ada_mod")
    mine = lax.dynamic_index_in_dim(mod_cols.reshape(depth, _N_CHIPS, 2, ncols), ic, 2, keepdims=False)
    got = _swap_chips("swap_mod", jnp.moveaxis(mine, 1, 0))
    mod = jnp.moveaxis(got, 0, 1).reshape(depth, n_mod, 1, d)

    tables = _attn_tables(nh)
    zeros_d = jnp.zeros((1, d), F32)
    zeros_f2 = jnp.zeros((1, ffn_w_in.shape[2] * _N_CHIPS), F32)
    row = lambda a: a.reshape(1, -1)

    saved = []
    xcur = xs
    for i in range(depth):
        sh1, sc1, g1, sh2, sc2, g2 = (mod[i, k] for k in range(n_mod))
        kind, j = i % 3, i // 3
        tag = "L%d_" % i
        if kind == 0:
            w_in, b_in, w_out, b_out = wfull["attn_wqkv"], row(full["attn_bqkv"][j]), wfull["attn_wo"], row(full["attn_bo"][j])
        elif kind == 1:
            w_in, b_in, w_out, b_out = wfull["conv_w_in"], row(full["conv_b_in"][j]), wfull["conv_w_out"], row(full["conv_b_out"][j])
        else:
            w_in, b_in, w_out, b_out = wfull["sgu_w_in"], row(full["sgu_b_in"][j]), wfull["sgu_w_out"], row(full["sgu_b_out"][j])
        h1, z = _norm_mod_matmul(xcur, row(full["norm1_g"][i]), sc1, sh1, w_in, j, b_in, tag + "mixer_in")
        if kind == 0:
            a = _attn_fwd(z, tables, full["attn_sinks"][j], nh, tag + "attn")
        elif kind == 1:
            a = _conv_mid_fwd(z, full["conv_dw"][j], row(full["conv_dw_b"][j]), row(full["conv_ln_g"][j]), row(full["conv_ln_b"][j]),
                              tag + "conv_mid")
        else:
            a = _sgu_mid_fwd(z, row(full["sgu_ln_g"][j]), row(full["sgu_ln_b"][j]), full["sgu_ws"][j], full["sgu_bs"][j].T,
                             tag + "sgu_mid")
        y1, xmid = _matmul_resid(a, w_out, j, b_out, xcur, g1, tag + "mixer_out")
        h2, zf = _norm_mod_matmul(xmid, row(full["norm2_g"][i]), sc2, sh2, wfull["ffn_w_in"], i, zeros_f2, tag + "ffn_in")
        act = _ffn_mid_fwd(zf, full["ffn_dw"][i], row(full["ffn_dw_b"][i]), tag + "ffn_mid")
        y2, xnext = _matmul_resid(act, wfull["ffn_w_out"], i, zeros_d, xmid, g2, tag + "ffn_out")
        saved.append(dict(x=xcur, h1=h1, z=z, a=a, y1=y1, xmid=xmid, h2=h2, zf=zf, act=act, y2=y2, w_in=w_in, w_out=w_out))
        xcur = xnext

    loss_row, dx, d_final_g = _final_loss(xcur, row(final_g), tgt, "final_loss")

    small_g = {n: jnp.zeros(_full_shape(wts[n], _SMALL[n]), F32) for n in _SMALL}
    small_g["final_g"] = d_final_g[0]
    big_g = {n: [None] * wts[n].shape[0] for n in _BIG}
    dmod = [None] * depth

    def put(name, j, val):
        small_g[name] = small_g[name].at[j].set(val.reshape(small_g[name].shape[1:]))

    for i in reversed(range(depth)):
        sv = saved[i]
        sh1, sc1, g1, sh2, sc2, g2 = (mod[i, k] for k in range(n_mod))
        kind, j = i % 3, i // 3
        tag = "L%d_" % i
        dy2, dact, dg2, _ = _outproj_bwd(dx, sv["y2"], g2, wfull["ffn_w_out"], i, tag + "ffn_out_bwd")
        big_g["ffn_w_out"][i] = _matmul_tn(sv["act"], dy2, tag + "ffn_out_wgrad")
        dzf, d_fdw, d_fdwb = _ffn_mid_bwd(sv["zf"], dact, full["ffn_dw"][i], row(full["ffn_dw_b"][i]), tag + "ffn_mid_bwd")
        put("ffn_dw", i, d_fdw)
        put("ffn_dw_b", i, d_fdwb)
        big_g["ffn_w_in"][i] = _matmul_tn(sv["h2"], dzf, tag + "ffn_in_wgrad")
        dxmid, dn2, dsc2, dsh2 = _inproj_bwd(dzf, wfull["ffn_w_in"], i, sv["xmid"], dx, row(full["norm2_g"][i]), sc2, tag + "ffn_in_bwd")
        put("norm2_g", i, dn2)
        dy1, da, dg1, dbo = _outproj_bwd(dxmid, sv["y1"], g1, sv["w_out"], j, tag + "mixer_out_bwd")
        w_out_name, w_in_name = {0: ("attn_wo", "attn_wqkv"), 1: ("conv_w_out", "conv_w_in"), 2: ("sgu_w_out", "sgu_w_in")}[kind]
        big_g[w_out_name][j] = _matmul_tn(sv["a"], dy1, tag + "mixer_out_wgrad")
        if kind == 0:
            dz, dsink, dbin = _attn_bwd(sv["z"], da, tables, full["attn_sinks"][j], nh, tag + "attn_bwd")
            put("attn_sinks", j, dsink[0, :nh])
            put("attn_bqkv", j, dbin)
            put("attn_bo", j, dbo)
        elif kind == 1:
            dz, d_dw, d_dwb, d_lng, d_lnb, dbin = _conv_mid_bwd(
                sv["z"], da, full["conv_dw"][j], row(full["conv_dw_b"][j]), row(full["conv_ln_g"][j]), row(full["conv_ln_b"][j]),
                tag + "conv_mid_bwd")
            for nme, val in (("conv_dw", d_dw), ("conv_dw_b", d_dwb), ("conv_ln_g", d_lng), ("conv_ln_b", d_lnb),
                             ("conv_b_in", dbin), ("conv_b_out", dbo)):
                put(nme, j, val)
        else:
            dz, d_lng, d_lnb, d_ws, d_bst, dbin = _sgu_mid_bwd(
                sv["z"], da, row(full["sgu_ln_g"][j]), row(full["sgu_ln_b"][j]), full["sgu_ws"][j], full["sgu_bs"][j].T,
                tag + "sgu_mid_bwd")
            ng = sgu_ws.shape[1]
            for nme, val in (("sgu_ln_g", d_lng), ("sgu_ln_b", d_lnb), ("sgu_ws", d_ws), ("sgu_bs", d_bst[:, :ng].T),
                             ("sgu_b_in", dbin), ("sgu_b_out", dbo)):
                put(nme, j, val)
        big_g[w_in_name][j] = _matmul_tn(sv["h1"], dz, tag + "mixer_in_wgrad")
        dx, dn1, dsc1, dsh1 = _inproj_bwd(dz, sv["w_in"], j, sv["x"], dxmid, row(full["norm1_g"][i]), sc1, tag + "mixer_in_bwd")
        put("norm1_g", i, dn1)
        dmod[i] = jnp.concatenate([dsh1, dsc1, dg1, dsh2, dsc2, dg2], axis=1)

    grad_x = dx[None]
    loss = lax.psum(loss_row[0, 0], ("x", "y", "c"))

    small_names = [n for n in _WEIGHTS if n in _SMALL and n != "ada_b"]
    dmod_own = jnp.concatenate(dmod, axis=0)
    packed = _pack([small_g[n] for n in small_names] + [dmod_own])
    packed_all = _gather_all("gather_small_grads", packed)
    summed = _sum_slots(packed_all, "sum_small_grads")
    small_full = dict(zip(small_names, _unpack(summed, [small_g[n].shape for n in small_names])))
    n_small = sum(math.prod(small_g[n].shape) for n in small_names)
    dmod_all = packed_all.reshape(_N_DEV, -1)[:, n_small:n_small + dmod_own.size].reshape(_N_DEV, depth, n_mod * d)
    small_full["ada_b"] = _sum_slots(dmod_all, "sum_ada_b_grad")
    dmod_cols = lax.dynamic_slice_in_dim(jnp.moveaxis(dmod_all, 0, 1), chip * ncols, ncols, 2)
    g_ada_w = _ada_wgrad(c_all.T, dmod_cols, "ada_wgrad")

    slots = _scatter_grads("scatter_grads", [(big_g[n], _BIG_AXIS[n] - 1) for n in _BIG])
    partial = [_sum_slots(s, "sum_chips_" + n) for n, s in zip(_BIG, slots)]
    other = _swap_sibling("swap_cores", partial)

    outs = {}
    for n, mine_p, theirs_p in zip(_BIG, partial, other):
        outs[n] = _adamw(wts[n], [mine_p, theirs_p], mom_m[n], mom_v[n], "adamw_" + n)
    outs["ada_w"] = _adamw(ada_w, [g_ada_w], m_ada_w, v_ada_w, "adamw_ada_w")
    sm_names = [n for n in _WEIGHTS if n in _SMALL]
    g_loc = [_shard_of(small_full[n], _SMALL[n], chip) for n in sm_names]
    packs = [_pack([src[n] for n in sm_names]) for src in (wts, mom_m, mom_v)]
    sm_out = _adamw(packs[0], [_pack(g_loc)], packs[1], packs[2], "adamw_small")
    shapes = [wts[n].shape for n in sm_names]
    unpacked = [_unpack(o, shapes) for o in sm_out]
    for k, n in enumerate(sm_names):
        outs[n] = tuple(u[k] for u in unpacked)

    result = [loss, grad_x]
    for which in range(4):
        result += [outs[n][which] for n in _WEIGHTS]
    return tuple(result)
```

```python
import math

import jax
import jax.numpy as jnp
from jax import lax
from jax.experimental import pallas as pl
from jax.experimental.pallas import tpu as pltpu

F32, BF16 = jnp.float32, jnp.bfloat16
SDS = jax.ShapeDtypeStruct
MESH_IDS = pl.DeviceIdType.MESH

_VMEM_LIMIT_BYTES = 48 * 1024 * 1024
_LANES = 128
_NORM_EPS = 1e-6
_NEG_INF = -1e30
_HEAD_DIM = 64
_N_KV = 4
_ATTN_BLOCK = 128
_ROW_TILE = 512
_COL_TILE = 512
_WGRAD_TILE = 1536
_CONV_HALO = 32
_FFN_HALO = 16
_ADAM_LR, _ADAM_B1, _ADAM_B2, _ADAM_EPS, _ADAM_WD, _ADAM_STEP = 0.001, 0.9, 0.999, 1e-08, 0.01, 10
_N_CHIPS = 4
_N_DEV = 8


def _tile(n, pref, unit):
    if n <= pref:
        return n
    t = pref - pref % unit
    while t >= unit:
        if n % t == 0:
            return t
        t -= unit
    return n


def _params(n_axes):
    return pltpu.CompilerParams(dimension_semantics=("arbitrary",) * n_axes, vmem_limit_bytes=_VMEM_LIMIT_BYTES)


def _row(n):
    return pl.BlockSpec((1, n), lambda *_: (0, 0))


def _sigmoid(v):
    return 1.0 / (1.0 + jnp.exp(-v))


def _nt(a, b):
    return lax.dot_general(a, b, (((1,), (1,)), ((), ())), preferred_element_type=F32)


def _tn(a, b):
    return lax.dot_general(a, b, (((0,), (0,)), ((), ())), preferred_element_type=F32)


def _resident(shape, index_map):
    return pl.BlockSpec(shape, index_map, pipeline_mode=pl.Buffered(1))


def _norm_mod_matmul(x, gn, sc, sh, w, layer, b, name):
    t, d = x.shape
    n = w.shape[2]
    tm, tn = _tile(t, _ROW_TILE, 16), _tile(n, _COL_TILE, _LANES)

    def body(x_ref, gn_ref, sc_ref, sh_ref, w_ref, b_ref, h_ref, z_ref):
        xf = x_ref[...]
        r = lax.rsqrt(jnp.mean(xf * xf, axis=-1, keepdims=True) + _NORM_EPS)
        h_ref[...] = ((xf * r * gn_ref[...]) * (1.0 + sc_ref[...]) + sh_ref[...]).astype(BF16)
        for c0 in range(0, n, tn):
            z = jnp.dot(h_ref[...], w_ref[:, c0:c0 + tn], preferred_element_type=F32) + b_ref[:, c0:c0 + tn]
            z_ref[:, c0:c0 + tn] = z.astype(BF16)

    return pl.pallas_call(
        body, name=name, grid=(t // tm,),
        in_specs=[pl.BlockSpec((tm, d), lambda i: (i, 0)), _row(d), _row(d), _row(d),
                  _resident((None, d, n), lambda i: (layer, 0, 0)), _row(n)],
        out_specs=[pl.BlockSpec((tm, d), lambda i: (i, 0)), pl.BlockSpec((tm, n), lambda i: (i, 0))],
        out_shape=[SDS((t, d), BF16), SDS((t, n), BF16)], compiler_params=_params(1),
    )(x, gn, sc, sh, w, b)


def _matmul_resid(a, w, layer, b, x, gate, name):
    t, k = a.shape
    d = w.shape[2]
    tm, tn = _tile(t, _ROW_TILE, 16), _tile(d, _COL_TILE, _LANES)

    def body(a_ref, w_ref, b_ref, x_ref, g_ref, y_ref, xo_ref):
        for c0 in range(0, d, tn):
            cs = slice(c0, c0 + tn)
            y = jnp.dot(a_ref[...], w_ref[:, cs], preferred_element_type=F32) + b_ref[:, cs]
            y_ref[:, cs] = y.astype(BF16)
            xo_ref[:, cs] = x_ref[:, cs] + g_ref[:, cs] * y

    blk = pl.BlockSpec((tm, d), lambda i: (i, 0))
    return pl.pallas_call(
        body, name=name, grid=(t // tm,),
        in_specs=[pl.BlockSpec((tm, k), lambda i: (i, 0)), _resident((None, k, d), lambda i: (layer, 0, 0)), _row(d), blk, _row(d)],
        out_specs=[blk, blk], out_shape=[SDS((t, d), BF16), SDS((t, d), F32)], compiler_params=_params(1),
    )(a, w, b, x, gate)


def _outproj_bwd(dxo, y, gate, w, layer, name):
    t, d = dxo.shape
    k = w.shape[1]
    tm, tk = _tile(t, _ROW_TILE, 16), _tile(k, _COL_TILE, _LANES)

    def body(dxo_ref, y_ref, g_ref, w_ref, dy_ref, da_ref, dg_ref, db_ref):
        @pl.when(pl.program_id(0) == 0)
        def _():
            dg_ref[...] = jnp.zeros_like(dg_ref)
            db_ref[...] = jnp.zeros_like(db_ref)

        dxf = dxo_ref[...]
        dyf = dxf * g_ref[...]
        dy_ref[...] = dyf.astype(BF16)
        dg_ref[...] += jnp.sum(dxf * y_ref[...].astype(F32), axis=0, keepdims=True)
        db_ref[...] += jnp.sum(dyf, axis=0, keepdims=True)
        for c0 in range(0, k, tk):
            da_ref[:, c0:c0 + tk] = _nt(dy_ref[...], w_ref[c0:c0 + tk, :]).astype(BF16)

    full = pl.BlockSpec((tm, d), lambda i: (i, 0))
    return pl.pallas_call(
        body, name=name, grid=(t // tm,),
        in_specs=[full, full, _row(d), _resident((None, k, d), lambda i: (layer, 0, 0))],
        out_specs=[full, pl.BlockSpec((tm, k), lambda i: (i, 0)), _row(d), _row(d)],
        out_shape=[SDS((t, d), BF16), SDS((t, k), BF16), SDS((1, d), F32), SDS((1, d), F32)],
        compiler_params=_params(1),
    )(dxo, y, gate, w)


def _matmul_tn(a, b, name):
    t, ka = a.shape
    nb = b.shape[1]
    tka, tnb, tt = _tile(ka, _WGRAD_TILE, _LANES), _tile(nb, _WGRAD_TILE, _LANES), _tile(t, 2 * _ROW_TILE, 16)
    nt = t // tt

    def body(a_ref, b_ref, o_ref, acc):
        s = pl.program_id(2)

        @pl.when(s == 0)
        def _():
            acc[...] = jnp.zeros_like(acc)

        acc[...] += _tn(a_ref[...], b_ref[...])

        @pl.when(s == nt - 1)
        def _():
            o_ref[...] = acc[...].astype(BF16)

    return pl.pallas_call(
        body, name=name, grid=(ka // tka, nb // tnb, nt),
        in_specs=[pl.BlockSpec((tt, tka), lambda i, j, s: (s, i)), pl.BlockSpec((tt, tnb), lambda i, j, s: (s, j))],
        out_specs=pl.BlockSpec((tka, tnb), lambda i, j, s: (i, j)),
        out_shape=SDS((ka, nb), BF16), scratch_shapes=[pltpu.VMEM((tka, tnb), F32)], compiler_params=_params(3),
    )(a, b)


def _inproj_bwd(dz, w, layer, x, dxo, gn, sc, name):
    t, n = dz.shape
    d = x.shape[1]
    tm, tk = _tile(t, _ROW_TILE, 16), _tile(n, _COL_TILE, _LANES)

    def body(dz_ref, w_ref, x_ref, dxo_ref, gn_ref, sc_ref, dx_ref, dgn_ref, dsc_ref, dsh_ref, acc):
        @pl.when(pl.program_id(0) == 0)
        def _():
            dgn_ref[...] = jnp.zeros_like(dgn_ref)
            dsc_ref[...] = jnp.zeros_like(dsc_ref)
            dsh_ref[...] = jnp.zeros_like(dsh_ref)

        for c0 in range(0, n, tk):
            part = _nt(dz_ref[:, c0:c0 + tk], w_ref[:, c0:c0 + tk])
            if c0 == 0:
                acc[...] = part
            else:
                acc[...] += part
        dh = acc[...]
        xf = x_ref[...]
        r = lax.rsqrt(jnp.mean(xf * xf, axis=-1, keepdims=True) + _NORM_EPS)
        xn = xf * r
        gnv = gn_ref[...]
        dsh_ref[...] += jnp.sum(dh, axis=0, keepdims=True)
        dsc_ref[...] += jnp.sum(dh * (xn * gnv), axis=0, keepdims=True)
        drn = dh * (1.0 + sc_ref[...])
        dgn_ref[...] += jnp.sum(drn * xn, axis=0, keepdims=True)
        dxn = drn * gnv
        dx_ref[...] = dxo_ref[...] + r * (dxn - xn * jnp.mean(dxn * xn, axis=-1, keepdims=True))

    full = pl.BlockSpec((tm, d), lambda i: (i, 0))
    return pl.pallas_call(
        body, name=name, grid=(t // tm,),
        in_specs=[pl.BlockSpec((tm, n), lambda i: (i, 0)), _resident((None, d, n), lambda i: (layer, 0, 0)),
                  full, full, _row(d), _row(d)],
        out_specs=[full, _row(d), _row(d), _row(d)],
        out_shape=[SDS((t, d), F32), SDS((1, d), F32), SDS((1, d), F32), SDS((1, d), F32)],
        scratch_shapes=[pltpu.VMEM((tm, d), F32)], compiler_params=_params(1),
    )(dz, w, x, dxo, gn, sc)


def _final_loss(x, g, target, name):
    t, d = x.shape
    tm = _tile(t, _ROW_TILE, 8)

    def body(x_ref, g_ref, t_ref, loss_ref, dx_ref, dg_ref):
        @pl.when(pl.program_id(0) == 0)
        def _():
            loss_ref[...] = jnp.zeros_like(loss_ref)
            dg_ref[...] = jnp.zeros_like(dg_ref)

        xf = x_ref[...]
        r = lax.rsqrt(jnp.mean(xf * xf, axis=-1, keepdims=True) + _NORM_EPS)
        xn = xf * r
        gv = g_ref[...]
        e = xn * gv - t_ref[...]
        per_row = jnp.mean(e * e, axis=-1, keepdims=True)
        loss_ref[...] += 0.5 * jnp.sum(per_row, axis=0, keepdims=True)
        dy = e * (1.0 / d)
        dg_ref[...] += jnp.sum(dy * xn, axis=0, keepdims=True)
        dxn = dy * gv
        dx_ref[...] = r * (dxn - xn * jnp.mean(dxn * xn, axis=-1, keepdims=True))

    full = pl.BlockSpec((tm, d), lambda i: (i, 0))
    return pl.pallas_call(
        body, name=name, grid=(t // tm,), in_specs=[full, _row(d), full],
        out_specs=[_row(_LANES), full, _row(d)],
        out_shape=[SDS((1, _LANES), F32), SDS((t, d), F32), SDS((1, d), F32)], compiler_params=_params(1),
    )(x, g, target)


def _attn_tables(nh):
    group = nh // _N_KV
    slopes = 2.0 ** (-8.0 * jnp.arange(1, nh + 1, dtype=F32) / nh)
    qpos = jnp.arange(_ATTN_BLOCK) + _ATTN_BLOCK
    kpos = jnp.arange(2 * _ATTN_BLOCK)
    dist = qpos[:, None] - kpos[None, :]
    band = (dist >= 0) & (dist < _ATTN_BLOCK)
    bias = jnp.where(band[None], -slopes[:, None, None] * dist.astype(F32)[None], _NEG_INF)
    first = jnp.where((kpos < _ATTN_BLOCK)[None, None, :], _NEG_INF, bias)
    return jnp.stack([first, bias]).reshape(2, _N_KV, group * _ATTN_BLOCK, 2 * _ATTN_BLOCK)


def _attn_probs(q_ref, kp_ref, kc_ref, tab_ref, sink_ref, kv, group):
    hd, blk = _HEAD_DIM, _ATTN_BLOCK
    k2 = jnp.concatenate([kp_ref[:, kv * hd:(kv + 1) * hd], kc_ref[:, kv * hd:(kv + 1) * hd]], axis=0)
    q4 = jnp.concatenate([q_ref[:, (kv * group + g) * hd:(kv * group + g + 1) * hd] for g in range(group)], axis=0)
    s = _nt(q4, k2) * (hd ** -0.5) + tab_ref[kv]
    sink = jnp.concatenate([jnp.full((blk, 1), sink_ref[kv * group + g], F32) for g in range(group)], axis=0)
    m = jnp.maximum(jnp.max(s, axis=-1, keepdims=True), sink)
    e = jnp.exp(s - m)
    es = jnp.exp(sink - m)
    inv = 1.0 / (jnp.sum(e, axis=-1, keepdims=True) + es)
    return q4, k2, e * inv, es * inv


def _attn_specs(nh, order):
    group = nh // _N_KV
    blk, kvw = _ATTN_BLOCK, _N_KV * _HEAD_DIM
    prev = lambda i: jnp.maximum(order(i) - 1, 0)
    return [
        pl.BlockSpec((blk, nh * _HEAD_DIM), lambda i: (order(i), 0)),
        pl.BlockSpec((blk, kvw), lambda i: (prev(i), group)),
        pl.BlockSpec((blk, kvw), lambda i: (order(i), group)),
        pl.BlockSpec((blk, kvw), lambda i: (prev(i), group + 1)),
        pl.BlockSpec((blk, kvw), lambda i: (order(i), group + 1)),
        pl.BlockSpec((None, _N_KV, group * blk, 2 * blk), lambda i: (jnp.minimum(order(i), 1), 0, 0, 0)),
        pl.BlockSpec(memory_space=pltpu.SMEM),
    ]


def _attn_fwd(qkv, tables, sinks, nh, name):
    t = qkv.shape[0]
    group, hd, blk = nh // _N_KV, _HEAD_DIM, _ATTN_BLOCK

    def body(q_ref, kp_ref, kc_ref, vp_ref, vc_ref, tab_ref, sink_ref, o_ref):
        outs = [None] * nh
        for kv in range(_N_KV):
            _, _, p, _ = _attn_probs(q_ref, kp_ref, kc_ref, tab_ref, sink_ref, kv, group)
            v2 = jnp.concatenate([vp_ref[:, kv * hd:(kv + 1) * hd], vc_ref[:, kv * hd:(kv + 1) * hd]], axis=0)
            o4 = jnp.dot(p.astype(BF16), v2, preferred_element_type=F32)
            for g in range(group):
                outs[kv * group + g] = o4[g * blk:(g + 1) * blk, :]
        o_ref[...] = jnp.concatenate(outs, axis=1).astype(BF16)

    return pl.pallas_call(
        body, name=name, grid=(t // blk,), in_specs=_attn_specs(nh, lambda i: i),
        out_specs=pl.BlockSpec((blk, nh * hd), lambda i: (i, 0)), out_shape=SDS((t, nh * hd), BF16),
        compiler_params=_params(1),
    )(qkv, qkv, qkv, qkv, qkv, tables, sinks)


def _attn_bwd(qkv, do, tables, sinks, nh, name):
    t, wq = qkv.shape
    group, hd, blk = nh // _N_KV, _HEAD_DIM, _ATTN_BLOCK
    nb = t // blk
    kvw = _N_KV * hd
    order = lambda i: nb - 1 - i

    def body(q_ref, kp_ref, kc_ref, vp_ref, vc_ref, tab_ref, sink_ref, do_ref, dqkv_ref, dsink_ref, db_ref, ck, cv):
        i = pl.program_id(0)

        @pl.when(i == 0)
        def _():
            ck[...] = jnp.zeros_like(ck)
            cv[...] = jnp.zeros_like(cv)
            dsink_ref[...] = jnp.zeros_like(dsink_ref)
            db_ref[...] = jnp.zeros_like(db_ref)

        lane = lax.broadcasted_iota(jnp.int32, (1, _LANES), 1)
        dq, dk, dv = [None] * nh, [None] * _N_KV, [None] * _N_KV
        dsink = jnp.zeros((1, _LANES), F32)
        for kv in range(_N_KV):
            q4, k2, p, ps = _attn_probs(q_ref, kp_ref, kc_ref, tab_ref, sink_ref, kv, group)
            v2 = jnp.concatenate([vp_ref[:, kv * hd:(kv + 1) * hd], vc_ref[:, kv * hd:(kv + 1) * hd]], axis=0)
            do4 = jnp.concatenate([do_ref[:, (kv * group + g) * hd:(kv * group + g + 1) * hd] for g in range(group)], axis=0)
            dp = _nt(do4, v2)
            dl = jnp.sum(p * dp, axis=-1, keepdims=True)
            ds = (p * (dp - dl)).astype(BF16)
            dsk = -ps * dl
            for g in range(group):
                dsink = dsink + jnp.where(lane == kv * group + g, jnp.sum(dsk[g * blk:(g + 1) * blk, :]), 0.0)
            dq4 = jnp.dot(ds, k2, preferred_element_type=F32) * (hd ** -0.5)
            for g in range(group):
                dq[kv * group + g] = dq4[g * blk:(g + 1) * blk, :]
            dk2 = _tn(ds, q4) * (hd ** -0.5)
            dv2 = _tn(p.astype(BF16), do4)
            dk[kv] = dk2[blk:, :] + ck[:, kv * hd:(kv + 1) * hd]
            dv[kv] = dv2[blk:, :] + cv[:, kv * hd:(kv + 1) * hd]
            ck[:, kv * hd:(kv + 1) * hd] = dk2[:blk, :]
            cv[:, kv * hd:(kv + 1) * hd] = dv2[:blk, :]
        dqkv = jnp.concatenate(dq + dk + dv, axis=1)
        dqkv_ref[...] = dqkv.astype(BF16)
        db_ref[...] += jnp.sum(dqkv, axis=0, keepdims=True)
        dsink_ref[...] += dsink

    return pl.pallas_call(
        body, name=name, grid=(nb,),
        in_specs=_attn_specs(nh, order) + [pl.BlockSpec((blk, nh * hd), lambda i: (order(i), 0))],
        out_specs=[pl.BlockSpec((blk, wq), lambda i: (order(i), 0)), _row(_LANES), _row(wq)],
        out_shape=[SDS((t, wq), BF16), SDS((1, _LANES), F32), SDS((1, wq), F32)],
        scratch_shapes=[pltpu.VMEM((blk, kvw), F32), pltpu.VMEM((blk, kvw), F32)], compiler_params=_params(1),
    )(qkv, qkv, qkv, qkv, qkv, tables, sinks, do)


def _taps(src_ref, w_ref, dst_ref, n_rows, width, offs, rg):
    cg = _tile(width, 512, _LANES)
    for c0 in range(0, width, cg):
        wk = [w_ref[k:k + 1, c0:c0 + cg] for k, _ in offs]
        for r0 in range(0, n_rows, rg):
            acc = None
            for (_, off), wv in zip(offs, wk):
                term = wv * src_ref[r0 + off:r0 + off + rg, c0:c0 + cg]
                acc = term if acc is None else acc + term
            dst_ref[r0:r0 + rg, c0:c0 + cg] = acc


def _tap_grads(dy_ref, z_ref, out_ref, n_rows, width, offs, rg):
    cg = _tile(width, 512, _LANES)
    for c0 in range(0, width, cg):
        for k, off in offs:
            acc = None
            for r0 in range(0, n_rows, rg):
                term = dy_ref[r0:r0 + rg, c0:c0 + cg] * z_ref[r0 + off:r0 + off + rg, c0:c0 + cg]
                acc = term if acc is None else acc + term
            out_ref[k:k + 1, c0:c0 + cg] += jnp.sum(acc, axis=0, keepdims=True)


def _conv_mid_fwd(ag, dw, dwb, lng, lnb, name):
    t, c2 = ag.shape
    c = c2 // 2
    kw = dw.shape[0]
    hl = _CONV_HALO
    tm = _tile(t, 256, hl)
    per = tm // hl

    def body(agp_ref, ag_ref, dw_ref, dwb_ref, lng_ref, lnb_ref, o_ref, zext, yb):
        i = pl.program_id(0)
        glu = lambda ref: ref[:, :c].astype(F32) * _sigmoid(ref[:, c:].astype(F32))
        zext[0:hl, :] = jnp.where(i > 0, glu(agp_ref), 0.0)
        zext[hl:, :] = glu(ag_ref)
        _taps(zext, dw_ref, yb, tm, c, [(k, hl - (kw - 1) + k) for k in range(kw)], 32)
        y = yb[...] + dwb_ref[...]
        mu = jnp.mean(y, axis=-1, keepdims=True)
        yc = y - mu
        rstd = lax.rsqrt(jnp.mean(yc * yc, axis=-1, keepdims=True) + _NORM_EPS)
        ln = yc * rstd * lng_ref[...] + lnb_ref[...]
        o_ref[...] = (ln * _sigmoid(ln)).astype(BF16)

    return pl.pallas_call(
        body, name=name, grid=(t // tm,),
        in_specs=[pl.BlockSpec((hl, c2), lambda i: (jnp.maximum(i * per - 1, 0), 0)), pl.BlockSpec((tm, c2), lambda i: (i, 0)),
                  pl.BlockSpec((kw, c), lambda i: (0, 0)), _row(c), _row(c), _row(c)],
        out_specs=pl.BlockSpec((tm, c), lambda i: (i, 0)), out_shape=SDS((t, c), BF16),
        scratch_shapes=[pltpu.VMEM((hl + tm, c), F32), pltpu.VMEM((tm, c), F32)], compiler_params=_params(1),
    )(ag, ag, dw, dwb, lng, lnb)


def _conv_mid_bwd(ag, dzc, dw, dwb, lng, lnb, name):
    t, c2 = ag.shape
    c = c2 // 2
    kw = dw.shape[0]
    hl = _CONV_HALO
    tm = _tile(t, 256, hl)
    per = tm // hl
    nt = t // tm
    last_halo = t // hl - 1

    def body(agp_ref, ag_ref, agn_ref, dzc_ref, dzcn_ref, dw_ref, dwb_ref, lng_ref, lnb_ref,
             dag_ref, ddw_ref, ddwb_ref, dlng_ref, dlnb_ref, dbin_ref, zext, yext, dyext, dzb):
        i = pl.program_id(0)

        @pl.when(i == 0)
        def _():
            for r in (ddw_ref, ddwb_ref, dlng_ref, dlnb_ref, dbin_ref):
                r[...] = jnp.zeros_like(r)

        glu = lambda ref: ref[:, :c].astype(F32) * _sigmoid(ref[:, c:].astype(F32))
        zext[0:hl, :] = jnp.where(i > 0, glu(agp_ref), 0.0)
        zext[hl:hl + tm, :] = glu(ag_ref)
        zext[hl + tm:, :] = glu(agn_ref)
        fwd_offs = [(k, hl - (kw - 1) + k) for k in range(kw)]
        _taps(zext, dw_ref, yext, tm + hl, c, fwd_offs, 32)
        y = yext[...] + dwb_ref[...]
        mu = jnp.mean(y, axis=-1, keepdims=True)
        yc = y - mu
        rstd = lax.rsqrt(jnp.mean(yc * yc, axis=-1, keepdims=True) + _NORM_EPS)
        xhat = yc * rstd
        lngv = lng_ref[...]
        ln = xhat * lngv + lnb_ref[...]
        sg = _sigmoid(ln)
        dz_out = jnp.concatenate([dzc_ref[...].astype(F32), jnp.where(i < nt - 1, dzcn_ref[...].astype(F32), 0.0)], axis=0)
        dln = dz_out * (sg * (1.0 + ln * (1.0 - sg)))
        dlng_ref[...] += jnp.sum((dln * xhat)[:tm], axis=0, keepdims=True)
        dlnb_ref[...] += jnp.sum(dln[:tm], axis=0, keepdims=True)
        dxh = dln * lngv
        dy = rstd * (dxh - jnp.mean(dxh, axis=-1, keepdims=True) - xhat * jnp.mean(dxh * xhat, axis=-1, keepdims=True))
        dyext[...] = dy
        ddwb_ref[...] += jnp.sum(dy[:tm], axis=0, keepdims=True)
        _taps(dyext, dw_ref, dzb, tm, c, [(k, kw - 1 - k) for k in range(kw)], 32)
        _tap_grads(dyext, zext, ddw_ref, tm, c, fwd_offs, 32)
        a = ag_ref[:, :c].astype(F32)
        sgg = _sigmoid(ag_ref[:, c:].astype(F32))
        dz = dzb[...]
        da = dz * sgg
        dg = dz * a * sgg * (1.0 - sgg)
        dag_ref[:, :c] = da.astype(BF16)
        dag_ref[:, c:] = dg.astype(BF16)
        dbin_ref[:, :c] += jnp.sum(da, axis=0, keepdims=True)
        dbin_ref[:, c:] += jnp.sum(dg, axis=0, keepdims=True)

    prev = lambda i: (jnp.maximum(i * per - 1, 0), 0)
    nxt = lambda i: (jnp.minimum((i + 1) * per, last_halo), 0)
    return pl.pallas_call(
        body, name=name, grid=(nt,),
        in_specs=[pl.BlockSpec((hl, c2), prev), pl.BlockSpec((tm, c2), lambda i: (i, 0)), pl.BlockSpec((hl, c2), nxt),
                  pl.BlockSpec((tm, c), lambda i: (i, 0)), pl.BlockSpec((hl, c), nxt),
                  pl.BlockSpec((kw, c), lambda i: (0, 0)), _row(c), _row(c), _row(c)],
        out_specs=[pl.BlockSpec((tm, c2), lambda i: (i, 0)), pl.BlockSpec((kw, c), lambda i: (0, 0)), _row(c), _row(c), _row(c), _row(c2)],
        out_shape=[SDS((t, c2), BF16), SDS((kw, c), F32), SDS((1, c), F32), SDS((1, c), F32), SDS((1, c), F32), SDS((1, c2), F32)],
        scratch_shapes=[pltpu.VMEM((hl + tm + hl, c), F32), pltpu.VMEM((tm + hl, c), F32), pltpu.VMEM((tm + hl, c), F32),
                        pltpu.VMEM((tm, c), F32)],
        compiler_params=_params(1),
    )(ag, ag, ag, dzc, dzc, dw, dwb, lng, lnb)


def _ffn_mid_fwd(zf, dw, dwb, name):
    t, f2 = zf.shape
    f = f2 // 2
    kw = dw.shape[0]
    hl = _FFN_HALO
    tm = _tile(t, 256, hl)
    per = tm // hl
    tc = _tile(f, 256, _LANES)
    offs = [(k, hl - (kw - 1) + k) for k in range(kw)]

    def body(zp_ref, z_ref, dw_ref, dwb_ref, o_ref, zext, cb):
        i = pl.program_id(0)
        zext[0:hl, :] = jnp.where(i > 0, zp_ref[...].astype(F32), 0.0)
        zext[hl:, :] = z_ref[...].astype(F32)
        _taps(zext, dw_ref, cb, tm, f2, offs, 16)
        for c0 in range(0, f, tc):
            g = cb[:, c0:c0 + tc] + dwb_ref[:, c0:c0 + tc]
            u = cb[:, f + c0:f + c0 + tc] + dwb_ref[:, f + c0:f + c0 + tc]
            o_ref[:, c0:c0 + tc] = (g * _sigmoid(g) * u).astype(BF16)

    return pl.pallas_call(
        body, name=name, grid=(t // tm,),
        in_specs=[pl.BlockSpec((hl, f2), lambda i: (jnp.maximum(i * per - 1, 0), 0)), pl.BlockSpec((tm, f2), lambda i: (i, 0)),
                  pl.BlockSpec((kw, f2), lambda i: (0, 0)), _row(f2)],
        out_specs=pl.BlockSpec((tm, f), lambda i: (i, 0)), out_shape=SDS((t, f), BF16),
        scratch_shapes=[pltpu.VMEM((hl + tm, f2), F32), pltpu.VMEM((tm, f2), F32)], compiler_params=_params(1),
    )(zf, zf, dw, dwb)


def _ffn_mid_bwd(zf, dact, dw, dwb, name):
    t, f2 = zf.shape
    f = f2 // 2
    kw = dw.shape[0]
    hl = _FFN_HALO
    tm = _tile(t, 128, hl)
    per = tm // hl
    nt = t // tm
    last_halo = t // hl - 1
    tc = _tile(f, 256, _LANES)
    fwd_offs = [(k, hl - (kw - 1) + k) for k in range(kw)]

    def body(zp_ref, z_ref, zn_ref, da_ref, dan_ref, dw_ref, dwb_ref, dzf_ref, ddw_ref, ddwb_ref, zext, cext, dcext, dzb):
        i = pl.program_id(0)

        @pl.when(i == 0)
        def _():
            ddw_ref[...] = jnp.zeros_like(ddw_ref)
            ddwb_ref[...] = jnp.zeros_like(ddwb_ref)

        zext[0:hl, :] = jnp.where(i > 0, zp_ref[...].astype(F32), 0.0)
        zext[hl:hl + tm, :] = z_ref[...].astype(F32)
        zext[hl + tm:, :] = zn_ref[...].astype(F32)
        _taps(zext, dw_ref, cext, tm + hl, f2, fwd_offs, 16)
        for c0 in range(0, f, tc):
            g = cext[:, c0:c0 + tc] + dwb_ref[:, c0:c0 + tc]
            u = cext[:, f + c0:f + c0 + tc] + dwb_ref[:, f + c0:f + c0 + tc]
            da = jnp.concatenate([da_ref[:, c0:c0 + tc].astype(F32),
                                  jnp.where(i < nt - 1, dan_ref[:, c0:c0 + tc].astype(F32), 0.0)], axis=0)
            sg = _sigmoid(g)
            dcg = da * u * (sg * (1.0 + g * (1.0 - sg)))
            dcu = da * (g * sg)
            dcext[:, c0:c0 + tc] = dcg
            dcext[:, f + c0:f + c0 + tc] = dcu
            ddwb_ref[:, c0:c0 + tc] += jnp.sum(dcg[:tm], axis=0, keepdims=True)
            ddwb_ref[:, f + c0:f + c0 + tc] += jnp.sum(dcu[:tm], axis=0, keepdims=True)
        _taps(dcext, dw_ref, dzb, tm, f2, [(k, kw - 1 - k) for k in range(kw)], 16)
        _tap_grads(dcext, zext, ddw_ref, tm, f2, fwd_offs, 16)
        dzf_ref[...] = dzb[...].astype(BF16)

    prev = lambda i: (jnp.maximum(i * per - 1, 0), 0)
    nxt = lambda i: (jnp.minimum((i + 1) * per, last_halo), 0)
    return pl.pallas_call(
        body, name=name, grid=(nt,),
        in_specs=[pl.BlockSpec((hl, f2), prev), pl.BlockSpec((tm, f2), lambda i: (i, 0)), pl.BlockSpec((hl, f2), nxt),
                  pl.BlockSpec((tm, f), lambda i: (i, 0)), pl.BlockSpec((hl, f), nxt),
                  pl.BlockSpec((kw, f2), lambda i: (0, 0)), _row(f2)],
        out_specs=[pl.BlockSpec((tm, f2), lambda i: (i, 0)), pl.BlockSpec((kw, f2), lambda i: (0, 0)), _row(f2)],
        out_shape=[SDS((t, f2), BF16), SDS((kw, f2), F32), SDS((1, f2), F32)],
        scratch_shapes=[pltpu.VMEM((hl + tm + hl, f2), F32), pltpu.VMEM((tm + hl, f2), F32), pltpu.VMEM((tm + hl, f2), F32),
                        pltpu.VMEM((tm, f2), F32)],
        compiler_params=_params(1),
    )(zf, zf, zf, dact, dact, dw, dwb)


_INV_SQRT2 = 0.7071067811865476
_INV_SQRT_2PI = 0.3989422804014327


def _sgu_common(zin_ref, lng_ref, lnb_ref, hh):
    z = zin_ref[...].astype(F32)
    cdf = 0.5 * (1.0 + lax.erf(z * _INV_SQRT2))
    ge = z * cdf
    u, v = ge[:, :hh], ge[:, hh:]
    mu = jnp.mean(v, axis=-1, keepdims=True)
    vc = v - mu
    rstd = lax.rsqrt(jnp.mean(vc * vc, axis=-1, keepdims=True) + _NORM_EPS)
    vhat = vc * rstd
    vn = vhat * lng_ref[...] + lnb_ref[...]
    return z, cdf, u, vhat, rstd, vn


def _sgu_wm(ws_ref, g, ch):
    rows = lax.broadcasted_iota(jnp.int32, (ch, ch), 0)
    cols = lax.broadcasted_iota(jnp.int32, (ch, ch), 1)
    return jnp.where(rows >= cols, ws_ref[g], 0.0).astype(BF16)


def _sgu_mid_fwd(zin, lng, lnb, ws, bs_t, name):
    t, h2 = zin.shape
    hh = h2 // 2
    ng, ch = ws.shape[0], ws.shape[1]
    hg = hh // ng
    tm = _tile(t, 256, ch)

    def body(zin_ref, lng_ref, lnb_ref, ws_ref, bs_ref, o_ref):
        _, _, u, _, _, vn = _sgu_common(zin_ref, lng_ref, lnb_ref, hh)
        vnb = vn.astype(BF16)
        for g in range(ng):
            wm = _sgu_wm(ws_ref, g, ch)
            for cc in range(tm // ch):
                rs, cs = slice(cc * ch, (cc + 1) * ch), slice(g * hg, (g + 1) * hg)
                vv = jnp.dot(wm, vnb[rs, cs], preferred_element_type=F32) + bs_ref[:, g:g + 1]
                o_ref[rs, cs] = (u[rs, cs] * vv).astype(BF16)

    return pl.pallas_call(
        body, name=name, grid=(t // tm,),
        in_specs=[pl.BlockSpec((tm, h2), lambda i: (i, 0)), _row(hh), _row(hh),
                  pl.BlockSpec((ng, ch, ch), lambda i: (0, 0, 0)), pl.BlockSpec((ch, ng), lambda i: (0, 0))],
        out_specs=pl.BlockSpec((tm, hh), lambda i: (i, 0)), out_shape=SDS((t, hh), BF16), compiler_params=_params(1),
    )(zin, lng, lnb, ws, bs_t)


def _sgu_mid_bwd(zin, duv, lng, lnb, ws, bs_t, name):
    t, h2 = zin.shape
    hh = h2 // 2
    ng, ch = ws.shape[0], ws.shape[1]
    hg = hh // ng
    tm = _tile(t, 128, ch)

    def body(zin_ref, duv_ref, lng_ref, lnb_ref, ws_ref, bs_ref, dzin_ref, dlng_ref, dlnb_ref, dws_ref, dbs_ref, dbin_ref, dvn_s, du_s):
        @pl.when(pl.program_id(0) == 0)
        def _():
            for r in (dlng_ref, dlnb_ref, dws_ref, dbs_ref, dbin_ref):
                r[...] = jnp.zeros_like(r)

        z, cdf, u, vhat, rstd, vn = _sgu_common(zin_ref, lng_ref, lnb_ref, hh)
        vnb = vn.astype(BF16)
        duv = duv_ref[...].astype(F32)
        dvv = (duv * u).astype(BF16)
        lane = lax.broadcasted_iota(jnp.int32, (1, _LANES), 1)
        rows = lax.broadcasted_iota(jnp.int32, (ch, ch), 0)
        cols = lax.broadcasted_iota(jnp.int32, (ch, ch), 1)
        dbs = jnp.zeros((ch, _LANES), F32)
        for g in range(ng):
            wm = _sgu_wm(ws_ref, g, ch)
            dwm = jnp.zeros((ch, ch), F32)
            for cc in range(tm // ch):
                rs, cs = slice(cc * ch, (cc + 1) * ch), slice(g * hg, (g + 1) * hg)
                vv = jnp.dot(wm, vnb[rs, cs], preferred_element_type=F32) + bs_ref[:, g:g + 1]
                du_s[rs, cs] = duv[rs, cs] * vv
                dvn_s[rs, cs] = _tn(wm, dvv[rs, cs])
                dwm = dwm + _nt(dvv[rs, cs], vnb[rs, cs])
                dbs = dbs + jnp.where(lane == g, jnp.sum(dvv[rs, cs].astype(F32), axis=-1, keepdims=True), 0.0)
            dws_ref[g] += jnp.where(rows >= cols, dwm, 0.0)
        dbs_ref[...] += dbs
        dvn = dvn_s[...]
        dlng_ref[...] += jnp.sum(dvn * vhat, axis=0, keepdims=True)
        dlnb_ref[...] += jnp.sum(dvn, axis=0, keepdims=True)
        dxh = dvn * lng_ref[...]
        dv = rstd * (dxh - jnp.mean(dxh, axis=-1, keepdims=True) - vhat * jnp.mean(dxh * vhat, axis=-1, keepdims=True))
        dgelu = cdf + z * (_INV_SQRT_2PI * jnp.exp(-0.5 * z * z))
        dzu = du_s[...] * dgelu[:, :hh]
        dzv = dv * dgelu[:, hh:]
        dzin_ref[:, :hh] = dzu.astype(BF16)
        dzin_ref[:, hh:] = dzv.astype(BF16)
        dbin_ref[:, :hh] += jnp.sum(dzu, axis=0, keepdims=True)
        dbin_ref[:, hh:] += jnp.sum(dzv, axis=0, keepdims=True)

    return pl.pallas_call(
        body, name=name, grid=(t // tm,),
        in_specs=[pl.BlockSpec((tm, h2), lambda i: (i, 0)), pl.BlockSpec((tm, hh), lambda i: (i, 0)), _row(hh), _row(hh),
                  pl.BlockSpec((ng, ch, ch), lambda i: (0, 0, 0)), pl.BlockSpec((ch, ng), lambda i: (0, 0))],
        out_specs=[pl.BlockSpec((tm, h2), lambda i: (i, 0)), _row(hh), _row(hh), pl.BlockSpec((ng, ch, ch), lambda i: (0, 0, 0)),
                   pl.BlockSpec((ch, _LANES), lambda i: (0, 0)), _row(h2)],
        out_shape=[SDS((t, h2), BF16), SDS((1, hh), F32), SDS((1, hh), F32), SDS((ng, ch, ch), F32), SDS((ch, _LANES), F32),
                   SDS((1, h2), F32)],
        scratch_shapes=[pltpu.VMEM((tm, hh), F32), pltpu.VMEM((tm, hh), F32)], compiler_params=_params(1),
    )(zin, duv, lng, lnb, ws, bs_t)


def _ada_mod(c_all, ada_w, ada_b, name):
    nl, d, n = ada_w.shape
    nb = c_all.shape[0]
    tn = _tile(n, _COL_TILE, _LANES)

    def body(c_ref, w_ref, b_ref, o_ref):
        cv = c_ref[...]
        ca = cv * _sigmoid(cv)
        o_ref[...] = jnp.dot(ca, w_ref[...], preferred_element_type=F32, precision=lax.Precision.HIGHEST) + b_ref[...]

    return pl.pallas_call(
        body, name=name, grid=(nl, n // tn),
        in_specs=[pl.BlockSpec((nb, d), lambda l, j: (0, 0)), pl.BlockSpec((None, d, tn), lambda l, j: (l, 0, j)),
                  pl.BlockSpec((None, 1, tn), lambda l, j: (l, 0, j))],
        out_specs=pl.BlockSpec((None, nb, tn), lambda l, j: (l, 0, j)), out_shape=SDS((nl, nb, n), F32),
        compiler_params=_params(2),
    )(c_all, ada_w, ada_b)


def _ada_wgrad(c_all_t, dmod, name):
    d, nb = c_all_t.shape
    nl, _, n = dmod.shape
    tn = _tile(n, _COL_TILE, _LANES)

    def body(c_ref, dm_ref, o_ref):
        cv = c_ref[...]
        ca = cv * _sigmoid(cv)
        acc = ca[:, 0:1] * dm_ref[0:1, :]
        for b in range(1, nb):
            acc = acc + ca[:, b:b + 1] * dm_ref[b:b + 1, :]
        o_ref[...] = acc

    return pl.pallas_call(
        body, name=name, grid=(nl, n // tn),
        in_specs=[pl.BlockSpec((d, nb), lambda l, j: (0, 0)), pl.BlockSpec((None, nb, tn), lambda l, j: (l, 0, j))],
        out_specs=pl.BlockSpec((None, d, tn), lambda l, j: (l, 0, j)), out_shape=SDS((nl, d, n), F32),
        compiler_params=_params(2),
    )(c_all_t, dmod)


def _as_rows(a):
    return a.reshape(-1, a.shape[-1])


def _row_tile(r, c, n_arrays):
    budget = _VMEM_LIMIT_BYTES // (4 * 2 * n_arrays * 4)
    return _tile(r, max(8, budget // max(c, 1)), 8)


def _cast_bf16_layer(a, layer, name):
    _, r, c = a.shape
    tr = _row_tile(r, c, 2)

    def body(a_ref, o_ref):
        o_ref[...] = a_ref[...].astype(BF16)

    return pl.pallas_call(body, name=name, grid=(r // tr,), in_specs=[pl.BlockSpec((None, tr, c), lambda i: (layer, i, 0))],
                          out_specs=pl.BlockSpec((tr, c), lambda i: (i, 0)), out_shape=SDS((r, c), BF16),
                          compiler_params=_params(1))(a)


def _sum_slot_layers(parts, name):
    nl = len(parts)
    n, r, c = parts[0].shape
    tr = _row_tile(r, c, nl * (n + 1))

    def body(*refs):
        o_ref = refs[nl]
        for lay in range(nl):
            acc = refs[lay][0].astype(F32)
            for s in range(1, n):
                acc = acc + refs[lay][s].astype(F32)
            o_ref[lay] = acc

    return pl.pallas_call(body, name=name, grid=(r // tr,), in_specs=[pl.BlockSpec((n, tr, c), lambda i: (0, i, 0))] * nl,
                          out_specs=pl.BlockSpec((nl, tr, c), lambda i: (0, i, 0)), out_shape=SDS((nl, r, c), F32),
                          compiler_params=_params(1))(*parts)


def _sum_slots(parts, name):
    n = parts.shape[0]
    p2 = parts.reshape(n, -1, parts.shape[-1])
    r, c = p2.shape[1:]
    tr = _row_tile(r, c, n + 1)

    def body(p_ref, o_ref):
        acc = p_ref[0].astype(F32)
        for s in range(1, n):
            acc = acc + p_ref[s].astype(F32)
        o_ref[...] = acc

    out = pl.pallas_call(body, name=name, grid=(r // tr,), in_specs=[pl.BlockSpec((n, tr, c), lambda i: (0, i, 0))],
                         out_specs=pl.BlockSpec((tr, c), lambda i: (i, 0)), out_shape=SDS((r, c), F32),
                         compiler_params=_params(1))(p2)
    return out.reshape(parts.shape[1:])


def _adamw(w, g_parts, m, v, name):
    w2, m2, v2 = _as_rows(w), _as_rows(m), _as_rows(v)
    gs = [_as_rows(g) for g in g_parts]
    r, c = w2.shape
    ng = len(gs)
    tr = _row_tile(r, c, 7 + ng)
    c1 = 1.0 - _ADAM_B1 ** _ADAM_STEP
    c2 = 1.0 - _ADAM_B2 ** _ADAM_STEP

    def body(*refs):
        w_ref, m_ref, v_ref = refs[:3]
        g_refs = refs[3:3 + ng]
        go_ref, d_ref, mo_ref, vo_ref = refs[3 + ng:]
        g = g_refs[0][...]
        for gr in g_refs[1:]:
            g = g + gr[...]
        mn = _ADAM_B1 * m_ref[...] + (1.0 - _ADAM_B1) * g
        vn = _ADAM_B2 * v_ref[...] + (1.0 - _ADAM_B2) * (g * g)
        go_ref[...] = g
        mo_ref[...] = mn
        vo_ref[...] = vn
        d_ref[...] = -_ADAM_LR * ((mn / c1) / (jnp.sqrt(vn / c2) + _ADAM_EPS) + _ADAM_WD * w_ref[...])

    spec = pl.BlockSpec((tr, c), lambda i: (i, 0))
    outs = pl.pallas_call(body, name=name, grid=(r // tr,), in_specs=[spec] * (3 + ng), out_specs=[spec] * 4,
                          out_shape=[SDS((r, c), F32)] * 4, compiler_params=_params(1))(w2, m2, v2, *gs)
    return tuple(o.reshape(w.shape) for o in outs)


_HBM = pl.BlockSpec(memory_space=pltpu.HBM)
_CHIP_FLIPS = ((1, 0, 0), (0, 1, 0), (1, 1, 0))
_ALL_FLIPS = tuple((a, b, c) for a in (0, 1) for b in (0, 1) for c in (0, 1))[1:]
_SIBLING_FLIP = ((0, 0, 1),)


def _at(ref, idx):
    return ref.at[idx] if idx else ref


def _exchange(name, flips, srcs, out_shapes, items):
    n_in, n_out = len(srcs), len(out_shapes)
    n_remote = len(items) * len(flips)
    n_local = sum(1 for it in items if it[4])

    def body(*refs):
        src_refs, dst_refs = refs[:n_in], refs[n_in:n_in + n_out]
        send_sems, recv_sems, local_sems = refs[n_in + n_out:]
        me = (lax.axis_index("x"), lax.axis_index("y"), lax.axis_index("c"))
        sends, recvs, locs = [], [], []
        n, nl = 0, 0
        for si, di, src_idx, dst_idx, local in items:
            for flip in flips:
                peer = tuple(1 - m if f else m for m, f in zip(me, flip))
                push = pltpu.make_async_remote_copy(
                    src_ref=_at(src_refs[si], src_idx(*peer)), dst_ref=_at(dst_refs[di], dst_idx(*me)),
                    send_sem=send_sems.at[n], recv_sem=recv_sems.at[n], device_id=peer, device_id_type=MESH_IDS)
                push.start()
                sends.append(push)
                recvs.append(pltpu.make_async_remote_copy(
                    src_ref=_at(src_refs[si], src_idx(*me)), dst_ref=_at(dst_refs[di], dst_idx(*peer)),
                    send_sem=send_sems.at[n], recv_sem=recv_sems.at[n], device_id=peer, device_id_type=MESH_IDS))
                n += 1
            if local:
                cp = pltpu.make_async_copy(_at(src_refs[si], src_idx(*me)), _at(dst_refs[di], dst_idx(*me)), local_sems.at[nl])
                cp.start()
                locs.append(cp)
                nl += 1
        for r in recvs:
            r.wait_recv()
        for s in sends:
            s.wait_send()
        for cp in locs:
            cp.wait()

    return pl.pallas_call(
        body, name=name, in_specs=[_HBM] * n_in, out_specs=[_HBM] * n_out, out_shape=list(out_shapes),
        scratch_shapes=[pltpu.SemaphoreType.DMA((n_remote,)), pltpu.SemaphoreType.DMA((n_remote,)),
                        pltpu.SemaphoreType.DMA((max(n_local, 1),))],
    )(*srcs)


_SEM = pl.BlockSpec(memory_space=pltpu.SEMAPHORE)
_SIDE_EFFECTS = pltpu.SideEffectType.DATAFLOW_SIDE_EFFECTING


def _split_copies(src_refs, land_refs, send_sems, recv_sems, items):
    me = (lax.axis_index("x"), lax.axis_index("y"), lax.axis_index("c"))
    pushes, arrivals, n = [], [], 0
    for si, di, src_idx, dst_idx in items:
        for flip in _CHIP_FLIPS:
            peer = tuple(1 - m if f else m for m, f in zip(me, flip))
            pushes.append(pltpu.make_async_remote_copy(
                src_ref=_at(src_refs[si], src_idx(*peer)), dst_ref=_at(land_refs[di], dst_idx(*me)),
                send_sem=send_sems.at[n], recv_sem=recv_sems.at[n], device_id=peer, device_id_type=MESH_IDS))
            arrivals.append(pltpu.make_async_remote_copy(
                src_ref=_at(src_refs[si], src_idx(*me)), dst_ref=_at(land_refs[di], dst_idx(*peer)),
                send_sem=send_sems.at[n], recv_sem=recv_sems.at[n], device_id=peer, device_id_type=MESH_IDS))
            n += 1
    return me, pushes, arrivals


def _push_start(name, srcs, land_shapes, items):
    n_src, n_land = len(srcs), len(land_shapes)
    n_buf = n_src + n_land
    n_remote = len(items) * len(_CHIP_FLIPS)

    def body(*refs):
        src_refs, land_refs = refs[:n_src], refs[n_src:n_buf]
        send_sems, recv_sems = refs[n_buf], refs[n_buf + 1]
        token, local_sems = refs[2 * n_buf + 2], refs[2 * n_buf + 3]
        me, pushes, _ = _split_copies(src_refs, land_refs, send_sems, recv_sems, items)
        for p in pushes:
            p.start()
        own = [pltpu.make_async_copy(_at(src_refs[si], src_idx(*me)), _at(land_refs[di], dst_idx(*me)), local_sems.at[k])
               for k, (si, di, src_idx, dst_idx) in enumerate(items)]
        for cp in own:
            cp.start()
        for cp in own:
            cp.wait()
        token[...] = jnp.zeros_like(token)

    bufs = list(srcs) + [lax.empty(s.shape, s.dtype) for s in land_shapes]
    outs = pl.pallas_call(
        body, name=name, in_specs=[_HBM] * n_buf,
        out_shape=[pltpu.SemaphoreType.DMA((n_remote,)), pltpu.SemaphoreType.DMA((n_remote,))]
        + [pltpu.HBM(a.shape, a.dtype) for a in bufs] + [SDS((8, _LANES), F32)],
        out_specs=[_SEM, _SEM] + [_HBM] * n_buf + [pl.BlockSpec(memory_space=pltpu.VMEM)],
        input_output_aliases={k: 2 + k for k in range(n_buf)},
        scratch_shapes=[pltpu.SemaphoreType.DMA((len(items),))],
        compiler_params=pltpu.CompilerParams(has_side_effects=_SIDE_EFFECTS),
    )(*[pltpu.with_memory_space_constraint(a, pltpu.HBM) for a in bufs])
    return dict(send=outs[0], recv=outs[1], bufs=list(outs[2:2 + n_buf]), token=outs[2 + n_buf][0:1, 0:1],
                n_src=n_src, items=items)


def _push_wait(name, handle, after):
    n_src, items = handle["n_src"], handle["items"]
    n_buf = len(handle["bufs"])

    def body(*refs):
        src_refs, land_refs = refs[:n_src], refs[n_src:n_buf]
        send_sems, recv_sems = refs[n_buf], refs[n_buf + 1]
        _, pushes, arrivals = _split_copies(src_refs, land_refs, send_sems, recv_sems, items)
        for p in pushes:
            p.wait_send()
        for a in arrivals:
            a.wait_recv()

    outs = pl.pallas_call(
        body, name=name, in_specs=[_HBM] * n_buf + [_SEM, _SEM, pl.BlockSpec(memory_space=pl.ANY)],
        out_shape=[pltpu.HBM(a.shape, a.dtype) for a in handle["bufs"]], out_specs=[_HBM] * n_buf,
        input_output_aliases={k: k for k in range(n_buf)},
        compiler_params=pltpu.CompilerParams(has_side_effects=_SIDE_EFFECTS),
    )(*handle["bufs"], handle["send"], handle["recv"], after)
    return list(outs[n_src:])


def _chip_of(x, y, c):
    return 2 * x + y


def _dev_of(x, y, c):
    return 4 * x + 2 * y + c


def _window(axis, ndim, size):
    def idx(x, y, c):
        return tuple(pl.ds(_chip_of(x, y, c) * size, size) if a == axis else slice(None) for a in range(ndim))
    return idx


def _whole(x, y, c):
    return ()


def _start_gather(name, shards_axes):
    srcs, shapes, items = [], [], []
    for k, (a, axis) in enumerate(shards_axes):
        full = list(a.shape)
        full[axis] *= _N_CHIPS
        srcs.append(a)
        shapes.append(SDS((1,) + tuple(full), a.dtype))
        window = _window(axis, a.ndim, a.shape[axis])
        items.append((k, k, _whole, (lambda w: lambda x, y, c: (0,) + w(x, y, c))(window)))
    return _push_start(name, srcs, shapes, items)


def _start_scatter(name, grads_axes):
    srcs, shapes, items = [], [], []
    slot = lambda x, y, c: (_chip_of(x, y, c),)
    for k, (a, axis) in enumerate(grads_axes):
        shard = list(a.shape)
        shard[axis] //= _N_CHIPS
        srcs.append(a)
        shapes.append(SDS((_N_CHIPS,) + tuple(shard), a.dtype))
        items.append((k, k, _window(axis, a.ndim, shard[axis]), slot))
    return _push_start(name, srcs, shapes, items)


def _gather_all(name, a):
    slot = lambda x, y, c: (_dev_of(x, y, c),)
    return _exchange(name, _ALL_FLIPS, [a], [SDS((_N_DEV,) + a.shape, a.dtype)], [(0, 0, _whole, slot, True)])[0]


def _swap_chips(name, a):
    slot = lambda x, y, c: (_chip_of(x, y, c),)
    return _exchange(name, _CHIP_FLIPS, [a], [SDS(a.shape, a.dtype)], [(0, 0, slot, slot, True)])[0]


def _swap_sibling(name, arrays):
    items = [(k, k, _whole, _whole, False) for k in range(len(arrays))]
    return _exchange(name, _SIBLING_FLIP, list(arrays), [SDS(a.shape, a.dtype) for a in arrays], items)


_BIG = ("attn_wqkv", "attn_wo", "conv_w_in", "conv_w_out", "sgu_w_in", "sgu_w_out", "ffn_w_in", "ffn_w_out")
_BIG_AXIS = {"attn_wqkv": 2, "attn_wo": 1, "conv_w_in": 2, "conv_w_out": 1, "sgu_w_in": 2, "sgu_w_out": 1,
             "ffn_w_in": 2, "ffn_w_out": 1}
_SMALL = {"norm1_g": None, "norm2_g": None, "ada_b": None, "attn_bqkv": 1, "attn_sinks": None, "attn_bo": 1,
          "conv_b_in": None, "conv_dw": 2, "conv_dw_b": None, "conv_ln_g": None, "conv_ln_b": None, "conv_b_out": None,
          "sgu_b_in": 1, "sgu_ln_g": 1, "sgu_ln_b": 1, "sgu_ws": None, "sgu_bs": None, "sgu_b_out": 1,
          "ffn_dw": 2, "ffn_dw_b": None, "final_g": None}
_WEIGHTS = ['norm1_g', 'norm2_g', 'ada_w', 'ada_b', 'attn_wqkv', 'attn_bqkv', 'attn_sinks', 'attn_wo', 'attn_bo',
            'conv_w_in', 'conv_b_in', 'conv_dw', 'conv_dw_b', 'conv_ln_g', 'conv_ln_b', 'conv_w_out', 'conv_b_out',
            'sgu_w_in', 'sgu_b_in', 'sgu_ln_g', 'sgu_ln_b', 'sgu_ws', 'sgu_bs', 'sgu_w_out', 'sgu_b_out',
            'ffn_w_in', 'ffn_dw', 'ffn_dw_b', 'ffn_w_out', 'final_g']
_FLAT_COLS = 1024


def _full_shape(a, axis):
    s = list(a.shape)
    if axis is not None:
        s[axis] *= _N_CHIPS
    return tuple(s)


def _pack(arrays):
    flat = jnp.concatenate([a.reshape(-1).astype(F32) for a in arrays])
    pad = (-flat.shape[0]) % (8 * _FLAT_COLS)
    return jnp.pad(flat, (0, pad)).reshape(-1, _FLAT_COLS)


def _unpack(flat2d, shapes):
    flat = flat2d.reshape(-1)
    out, off = [], 0
    for s in shapes:
        n = math.prod(s)
        out.append(flat[off:off + n].reshape(s))
        off += n
    return out


def _shard_of(full, axis, chip):
    if axis is None:
        return full
    size = full.shape[axis] // _N_CHIPS
    return lax.dynamic_slice_in_dim(full, chip * size, size, axis)


def _unshard(gathered, axis):
    moved = jnp.moveaxis(gathered, 0, axis)
    shape = list(gathered.shape[1:])
    shape[axis] *= _N_CHIPS
    return moved.reshape(shape)


def kernel(x, c, norm1_g, norm2_g, ada_w, ada_b, attn_wqkv, attn_bqkv, attn_sinks, attn_wo, attn_bo, conv_w_in, conv_b_in, conv_dw, conv_dw_b, conv_ln_g, conv_ln_b, conv_w_out, conv_b_out, sgu_w_in, sgu_b_in, sgu_ln_g, sgu_ln_b, sgu_ws, sgu_bs, sgu_w_out, sgu_b_out, ffn_w_in, ffn_dw, ffn_dw_b, ffn_w_out, final_g, loss_target, m_norm1_g, m_norm2_g, m_ada_w, m_ada_b, m_attn_wqkv, m_attn_bqkv, m_attn_sinks, m_attn_wo, m_attn_bo, m_conv_w_in, m_conv_b_in, m_conv_dw, m_conv_dw_b, m_conv_ln_g, m_conv_ln_b, m_conv_w_out, m_conv_b_out, m_sgu_w_in, m_sgu_b_in, m_sgu_ln_g, m_sgu_ln_b, m_sgu_ws, m_sgu_bs, m_sgu_w_out, m_sgu_b_out, m_ffn_w_in, m_ffn_dw, m_ffn_dw_b, m_ffn_w_out, m_final_g, v_norm1_g, v_norm2_g, v_ada_w, v_ada_b, v_attn_wqkv, v_attn_bqkv, v_attn_sinks, v_attn_wo, v_attn_bo, v_conv_w_in, v_conv_b_in, v_conv_dw, v_conv_dw_b, v_conv_ln_g, v_conv_ln_b, v_conv_w_out, v_conv_b_out, v_sgu_w_in, v_sgu_b_in, v_sgu_ln_g, v_sgu_ln_b, v_sgu_ws, v_sgu_bs, v_sgu_w_out, v_sgu_b_out, v_ffn_w_in, v_ffn_dw, v_ffn_dw_b, v_ffn_w_out, v_final_g):
    args = dict(locals())
    wts = {n: args[n] for n in _WEIGHTS}
    mom_m = {n: args["m_" + n] for n in _WEIGHTS}
    mom_v = {n: args["v_" + n] for n in _WEIGHTS}

    ix, iy, ic = lax.axis_index("x"), lax.axis_index("y"), lax.axis_index("c")
    chip = 2 * ix + iy
    xs = x[0]
    tgt = loss_target[0]
    t, d = xs.shape
    nh = d // _HEAD_DIM
    depth = ada_w.shape[0]
    ncols = ada_w.shape[2]
    n_mod = ncols * _N_CHIPS // d

    small_sharded = [n for n in _WEIGHTS if _SMALL.get(n) is not None]
    packed_small = _pack([wts[n] for n in small_sharded])
    mixers = {0: ("attn_wqkv", "attn_wo"), 1: ("conv_w_in", "conv_w_out"), 2: ("sgu_w_in", "sgu_w_out")}
    groups = [[(n, 0) for n in mixers[0]], [("ffn_w_in", 0), ("ffn_w_out", 0)]]
    for i in range(1, depth):
        groups.append([(n, i // 3) for n in mixers[i % 3]] + [("ffn_w_in", i), ("ffn_w_out", i)])
    gathers = []
    for gi, keys in enumerate(groups):
        shards = [(_cast_bf16_layer(wts[n], j, "cast_%s_%d" % (n, j)), _BIG_AXIS[n] - 1) for n, j in keys]
        gathers.append(_start_gather("start_gather_%d" % gi, shards + ([(packed_small, 0)] if gi == 0 else [])))
    started = sum(h["token"] for h in gathers)
    wfull = {}

    def finish_gather(gi, after):
        lands = _push_wait("wait_gather_%d" % gi, gathers[gi], after)
        wfull.update(zip(groups[gi], lands))
        return lands

    c_all = _gather_all("gather_c", c)[:, 0, :]
    ada_b_cols = lax.dynamic_slice_in_dim(ada_b, chip * ncols, ncols, 1)[:, None, :]
    mod_cols = _ada_mod(c_all, ada_w, ada_b_cols, "ada_mod")
    mine = lax.dynamic_index_in_dim(mod_cols.reshape(depth, _N_CHIPS, 2, ncols), ic, 2, keepdims=False)
    got = _swap_chips("swap_mod", jnp.moveaxis(mine, 1, 0))
    mod = jnp.moveaxis(got, 0, 1).reshape(depth, n_mod, 1, d) + started

    small_rows = finish_gather(0, mod)[-1].reshape(_N_CHIPS, -1, _FLAT_COLS)
    pieces = [_unpack(small_rows[s], [wts[n].shape for n in small_sharded]) for s in range(_N_CHIPS)]
    full = {n: wts[n] for n in _WEIGHTS if n in _SMALL and _SMALL[n] is None}
    for k, n in enumerate(small_sharded):
        full[n] = _unshard(jnp.stack([pieces[s][k] for s in range(_N_CHIPS)]), _SMALL[n])

    tables = _attn_tables(nh)
    zeros_d = jnp.zeros((1, d), F32)
    zeros_f2 = jnp.zeros((1, ffn_w_in.shape[2] * _N_CHIPS), F32)
    row = lambda a: a.reshape(1, -1)

    saved = []
    xcur = xs
    for i in range(depth):
        sh1, sc1, g1, sh2, sc2, g2 = (mod[i, k] for k in range(n_mod))
        kind, j = i % 3, i // 3
        tag = "L%d_" % i
        if i > 0:
            finish_gather(i + 1, xcur)
        w_in, w_out = wfull[(mixers[kind][0], j)], wfull[(mixers[kind][1], j)]
        if kind == 0:
            b_in, b_out = row(full["attn_bqkv"][j]), row(full["attn_bo"][j])
        elif kind == 1:
            b_in, b_out = row(full["conv_b_in"][j]), row(full["conv_b_out"][j])
        else:
            b_in, b_out = row(full["sgu_b_in"][j]), row(full["sgu_b_out"][j])
        h1, z = _norm_mod_matmul(xcur, row(full["norm1_g"][i]), sc1, sh1, w_in, 0, b_in, tag + "mixer_in")
        if i == 0:
            finish_gather(1, z)
        if kind == 0:
            a = _attn_fwd(z, tables, full["attn_sinks"][j], nh, tag + "attn")
        elif kind == 1:
            a = _conv_mid_fwd(z, full["conv_dw"][j], row(full["conv_dw_b"][j]), row(full["conv_ln_g"][j]), row(full["conv_ln_b"][j]),
                              tag + "conv_mid")
        else:
            a = _sgu_mid_fwd(z, row(full["sgu_ln_g"][j]), row(full["sgu_ln_b"][j]), full["sgu_ws"][j], full["sgu_bs"][j].T,
                             tag + "sgu_mid")
        y1, xmid = _matmul_resid(a, w_out, 0, b_out, xcur, g1, tag + "mixer_out")
        h2, zf = _norm_mod_matmul(xmid, row(full["norm2_g"][i]), sc2, sh2, wfull[("ffn_w_in", i)], 0, zeros_f2, tag + "ffn_in")
        act = _ffn_mid_fwd(zf, full["ffn_dw"][i], row(full["ffn_dw_b"][i]), tag + "ffn_mid")
        y2, xnext = _matmul_resid(act, wfull[("ffn_w_out", i)], 0, zeros_d, xmid, g2, tag + "ffn_out")
        saved.append(dict(x=xcur, h1=h1, z=z, a=a, y1=y1, xmid=xmid, h2=h2, zf=zf, act=act, y2=y2, w_in=w_in, w_out=w_out))
        xcur = xnext

    loss_row, dx, d_final_g = _final_loss(xcur, row(final_g), tgt, "final_loss")

    small_g = {n: jnp.zeros(_full_shape(wts[n], _SMALL[n]), F32) for n in _SMALL}
    small_g["final_g"] = d_final_g[0]
    dmod = [None] * depth
    scatters = []
    pushed = 0.0

    def put(name, j, val):
        small_g[name] = small_g[name].at[j].set(val.reshape(small_g[name].shape[1:]))

    for i in reversed(range(depth)):
        sv = saved[i]
        sh1, sc1, g1, sh2, sc2, g2 = (mod[i, k] for k in range(n_mod))
        kind, j = i % 3, i // 3
        tag = "L%d_" % i
        dy2, dact, dg2, _ = _outproj_bwd(dx, sv["y2"], g2 + pushed, wfull[("ffn_w_out", i)], 0, tag + "ffn_out_bwd")
        g_ffn_out = _matmul_tn(sv["act"], dy2, tag + "ffn_out_wgrad")
        dzf, d_fdw, d_fdwb = _ffn_mid_bwd(sv["zf"], dact, full["ffn_dw"][i], row(full["ffn_dw_b"][i]), tag + "ffn_mid_bwd")
        put("ffn_dw", i, d_fdw)
        put("ffn_dw_b", i, d_fdwb)
        g_ffn_in = _matmul_tn(sv["h2"], dzf, tag + "ffn_in_wgrad")
        dxmid, dn2, dsc2, dsh2 = _inproj_bwd(dzf, wfull[("ffn_w_in", i)], 0, sv["xmid"], dx, row(full["norm2_g"][i]), sc2,
                                             tag + "ffn_in_bwd")
        put("norm2_g", i, dn2)
        handle = _start_scatter("start_scatter_ffn_%d" % i, [(g_ffn_in, 1), (g_ffn_out, 0)])
        scatters.append(([("ffn_w_in", i), ("ffn_w_out", i)], handle))
        dy1, da, dg1, dbo = _outproj_bwd(dxmid, sv["y1"], g1 + handle["token"], sv["w_out"], 0, tag + "mixer_out_bwd")
        w_in_name, w_out_name = mixers[kind]
        g_mix_out = _matmul_tn(sv["a"], dy1, tag + "mixer_out_wgrad")
        if kind == 0:
            dz, dsink, dbin = _attn_bwd(sv["z"], da, tables, full["attn_sinks"][j], nh, tag + "attn_bwd")
            put("attn_sinks", j, dsink[0, :nh])
            put("attn_bqkv", j, dbin)
            put("attn_bo", j, dbo)
        elif kind == 1:
            dz, d_dw, d_dwb, d_lng, d_lnb, dbin = _conv_mid_bwd(
                sv["z"], da, full["conv_dw"][j], row(full["conv_dw_b"][j]), row(full["conv_ln_g"][j]), row(full["conv_ln_b"][j]),
                tag + "conv_mid_bwd")
            for nme, val in (("conv_dw", d_dw), ("conv_dw_b", d_dwb), ("conv_ln_g", d_lng), ("conv_ln_b", d_lnb),
                             ("conv_b_in", dbin), ("conv_b_out", dbo)):
                put(nme, j, val)
        else:
            dz, d_lng, d_lnb, d_ws, d_bst, dbin = _sgu_mid_bwd(
                sv["z"], da, row(full["sgu_ln_g"][j]), row(full["sgu_ln_b"][j]), full["sgu_ws"][j], full["sgu_bs"][j].T,
                tag + "sgu_mid_bwd")
            ng = sgu_ws.shape[1]
            for nme, val in (("sgu_ln_g", d_lng), ("sgu_ln_b", d_lnb), ("sgu_ws", d_ws), ("sgu_bs", d_bst[:, :ng].T),
                             ("sgu_b_in", dbin), ("sgu_b_out", dbo)):
                put(nme, j, val)
        g_mix_in = _matmul_tn(sv["h1"], dz, tag + "mixer_in_wgrad")
        dx, dn1, dsc1, dsh1 = _inproj_bwd(dz, sv["w_in"], 0, sv["x"], dxmid, row(full["norm1_g"][i]), sc1, tag + "mixer_in_bwd")
        put("norm1_g", i, dn1)
        dmod[i] = jnp.concatenate([dsh1, dsc1, dg1, dsh2, dsc2, dg2], axis=1)
        handle = _start_scatter("start_scatter_mix_%d" % i, [(g_mix_in, 1), (g_mix_out, 0)])
        scatters.append(([(w_in_name, j), (w_out_name, j)], handle))
        pushed = handle["token"]

    grad_x = dx[None]
    loss = lax.psum(loss_row[0, 0], ("x", "y", "c"))

    small_names = [n for n in _WEIGHTS if n in _SMALL and n != "ada_b"]
    dmod_own = jnp.concatenate(dmod, axis=0)
    packed = _pack([small_g[n] for n in small_names] + [dmod_own]) + pushed
    packed_all = _gather_all("gather_small_grads", packed)
    summed = _sum_slots(packed_all, "sum_small_grads")
    small_full = dict(zip(small_names, _unpack(summed, [small_g[n].shape for n in small_names])))
    n_small = sum(math.prod(small_g[n].shape) for n in small_names)
    dmod_all = packed_all.reshape(_N_DEV, -1)[:, n_small:n_small + dmod_own.size].reshape(_N_DEV, depth, n_mod * d)
    small_full["ada_b"] = _sum_slots(dmod_all, "sum_ada_b_grad")
    dmod_cols = lax.dynamic_slice_in_dim(jnp.moveaxis(dmod_all, 0, 1), chip * ncols, ncols, 2)
    g_ada_w = _ada_wgrad(c_all.T, dmod_cols, "ada_wgrad")

    slots = {}
    for k, (keys, handle) in enumerate(scatters):
        slots.update(zip(keys, _push_wait("wait_scatter_%d" % k, handle, g_ada_w)))
    partial = [_sum_slot_layers([slots[(n, j)] for j in range(wts[n].shape[0])], "sum_chips_" + n) for n in _BIG]
    other = _swap_sibling("swap_cores", partial)

    outs = {}
    for n, mine_p, theirs_p in zip(_BIG, partial, other):
        outs[n] = _adamw(wts[n], [mine_p, theirs_p], mom_m[n], mom_v[n], "adamw_" + n)
    outs["ada_w"] = _adamw(ada_w, [g_ada_w], m_ada_w, v_ada_w, "adamw_ada_w")
    sm_names = [n for n in _WEIGHTS if n in _SMALL]
    g_loc = [_shard_of(small_full[n], _SMALL[n], chip) for n in sm_names]
    packs = [_pack([src[n] for n in sm_names]) for src in (wts, mom_m, mom_v)]
    sm_out = _adamw(packs[0], [_pack(g_loc)], packs[1], packs[2], "adamw_small")
    shapes = [wts[n].shape for n in sm_names]
    unpacked = [_unpack(o, shapes) for o in sm_out]
    for k, n in enumerate(sm_names):
        outs[n] = tuple(u[k] for u in unpacked)

    result = [loss, grad_x]
    for which in range(4):
        result += [outs[n][which] for n in _WEIGHTS]
    return tuple(result)
```

```python
import math

import jax
import jax.numpy as jnp
from jax import lax
from jax.experimental import pallas as pl
from jax.experimental.pallas import tpu as pltpu

F32, BF16 = jnp.float32, jnp.bfloat16
SDS = jax.ShapeDtypeStruct
MESH_IDS = pl.DeviceIdType.MESH

_VMEM_LIMIT_BYTES = 48 * 1024 * 1024
_LANES = 128
_NORM_EPS = 1e-6
_NEG_INF = -1e30
_HEAD_DIM = 64
_N_KV = 4
_ATTN_BLOCK = 128
_ATTN_BLOCKS_PER_STEP = 2
_ROW_TILE = 512
_COL_TILE = 512
_WGRAD_TILE = 1536
_CONV_HALO = 32
_FFN_HALO = 16
_ADAM_LR, _ADAM_B1, _ADAM_B2, _ADAM_EPS, _ADAM_WD, _ADAM_STEP = 0.001, 0.9, 0.999, 1e-08, 0.01, 10
_N_CHIPS = 4
_N_DEV = 8


def _tile(n, pref, unit):
    if n <= pref:
        return n
    t = pref - pref % unit
    while t >= unit:
        if n % t == 0:
            return t
        t -= unit
    return n


def _params(n_axes):
    return pltpu.CompilerParams(dimension_semantics=("arbitrary",) * n_axes, vmem_limit_bytes=_VMEM_LIMIT_BYTES)


def _row(n):
    return pl.BlockSpec((1, n), lambda *_: (0, 0))


def _sigmoid(v):
    return 1.0 / (1.0 + jnp.exp(-v))


def _nt(a, b):
    return lax.dot_general(a, b, (((1,), (1,)), ((), ())), preferred_element_type=F32)


def _tn(a, b):
    return lax.dot_general(a, b, (((0,), (0,)), ((), ())), preferred_element_type=F32)


def _resident(shape, index_map):
    return pl.BlockSpec(shape, index_map, pipeline_mode=pl.Buffered(1))


def _norm_mod_matmul(x, gn, sc, sh, w, layer, b, name):
    t, d = x.shape
    n = w.shape[2]
    tm, tn = _tile(t, _ROW_TILE, 16), _tile(n, _COL_TILE, _LANES)

    def body(x_ref, gn_ref, sc_ref, sh_ref, w_ref, b_ref, h_ref, z_ref):
        xf = x_ref[...]
        r = lax.rsqrt(jnp.mean(xf * xf, axis=-1, keepdims=True) + _NORM_EPS)
        h_ref[...] = ((xf * r * gn_ref[...]) * (1.0 + sc_ref[...]) + sh_ref[...]).astype(BF16)
        for c0 in range(0, n, tn):
            z = jnp.dot(h_ref[...], w_ref[:, c0:c0 + tn], preferred_element_type=F32) + b_ref[:, c0:c0 + tn]
            z_ref[:, c0:c0 + tn] = z.astype(BF16)

    return pl.pallas_call(
        body, name=name, grid=(t // tm,),
        in_specs=[pl.BlockSpec((tm, d), lambda i: (i, 0)), _row(d), _row(d), _row(d),
                  _resident((None, d, n), lambda i: (layer, 0, 0)), _row(n)],
        out_specs=[pl.BlockSpec((tm, d), lambda i: (i, 0)), pl.BlockSpec((tm, n), lambda i: (i, 0))],
        out_shape=[SDS((t, d), BF16), SDS((t, n), BF16)], compiler_params=_params(1),
    )(x, gn, sc, sh, w, b)


def _matmul_resid(a, w, layer, b, x, gate, name):
    t, k = a.shape
    d = w.shape[2]
    tm, tn = _tile(t, _ROW_TILE, 16), _tile(d, _COL_TILE, _LANES)

    def body(a_ref, w_ref, b_ref, x_ref, g_ref, y_ref, xo_ref):
        for c0 in range(0, d, tn):
            cs = slice(c0, c0 + tn)
            y = jnp.dot(a_ref[...], w_ref[:, cs], preferred_element_type=F32) + b_ref[:, cs]
            y_ref[:, cs] = y.astype(BF16)
            xo_ref[:, cs] = x_ref[:, cs] + g_ref[:, cs] * y

    blk = pl.BlockSpec((tm, d), lambda i: (i, 0))
    return pl.pallas_call(
        body, name=name, grid=(t // tm,),
        in_specs=[pl.BlockSpec((tm, k), lambda i: (i, 0)), _resident((None, k, d), lambda i: (layer, 0, 0)), _row(d), blk, _row(d)],
        out_specs=[blk, blk], out_shape=[SDS((t, d), BF16), SDS((t, d), F32)], compiler_params=_params(1),
    )(a, w, b, x, gate)


def _outproj_bwd(dxo, y, gate, w, layer, name):
    t, d = dxo.shape
    k = w.shape[1]
    tm, tk = _tile(t, _ROW_TILE, 16), _tile(k, _COL_TILE, _LANES)

    def body(dxo_ref, y_ref, g_ref, w_ref, dy_ref, da_ref, dg_ref, db_ref):
        @pl.when(pl.program_id(0) == 0)
        def _():
            dg_ref[...] = jnp.zeros_like(dg_ref)
            db_ref[...] = jnp.zeros_like(db_ref)

        dxf = dxo_ref[...]
        dyf = dxf * g_ref[...]
        dy_ref[...] = dyf.astype(BF16)
        dg_ref[...] += jnp.sum(dxf * y_ref[...].astype(F32), axis=0, keepdims=True)
        db_ref[...] += jnp.sum(dyf, axis=0, keepdims=True)
        for c0 in range(0, k, tk):
            da_ref[:, c0:c0 + tk] = _nt(dy_ref[...], w_ref[c0:c0 + tk, :]).astype(BF16)

    full = pl.BlockSpec((tm, d), lambda i: (i, 0))
    return pl.pallas_call(
        body, name=name, grid=(t // tm,),
        in_specs=[full, full, _row(d), _resident((None, k, d), lambda i: (layer, 0, 0))],
        out_specs=[full, pl.BlockSpec((tm, k), lambda i: (i, 0)), _row(d), _row(d)],
        out_shape=[SDS((t, d), BF16), SDS((t, k), BF16), SDS((1, d), F32), SDS((1, d), F32)],
        compiler_params=_params(1),
    )(dxo, y, gate, w)


def _matmul_tn(a, b, name):
    t, ka = a.shape
    nb = b.shape[1]
    tka, tnb, tt = _tile(ka, _WGRAD_TILE, _LANES), _tile(nb, _WGRAD_TILE, _LANES), _tile(t, 2 * _ROW_TILE, 16)
    nt = t // tt

    def body(a_ref, b_ref, o_ref, acc):
        s = pl.program_id(2)

        @pl.when(s == 0)
        def _():
            acc[...] = jnp.zeros_like(acc)

        acc[...] += _tn(a_ref[...], b_ref[...])

        @pl.when(s == nt - 1)
        def _():
            o_ref[...] = acc[...].astype(BF16)

    return pl.pallas_call(
        body, name=name, grid=(ka // tka, nb // tnb, nt),
        in_specs=[pl.BlockSpec((tt, tka), lambda i, j, s: (s, i)), pl.BlockSpec((tt, tnb), lambda i, j, s: (s, j))],
        out_specs=pl.BlockSpec((tka, tnb), lambda i, j, s: (i, j)),
        out_shape=SDS((ka, nb), BF16), scratch_shapes=[pltpu.VMEM((tka, tnb), F32)], compiler_params=_params(3),
    )(a, b)


def _inproj_bwd(dz, w, layer, x, dxo, gn, sc, name):
    t, n = dz.shape
    d = x.shape[1]
    tm, tk = _tile(t, _ROW_TILE, 16), _tile(n, _COL_TILE, _LANES)

    def body(dz_ref, w_ref, x_ref, dxo_ref, gn_ref, sc_ref, dx_ref, dgn_ref, dsc_ref, dsh_ref, acc):
        @pl.when(pl.program_id(0) == 0)
        def _():
            dgn_ref[...] = jnp.zeros_like(dgn_ref)
            dsc_ref[...] = jnp.zeros_like(dsc_ref)
            dsh_ref[...] = jnp.zeros_like(dsh_ref)

        for c0 in range(0, n, tk):
            part = _nt(dz_ref[:, c0:c0 + tk], w_ref[:, c0:c0 + tk])
            if c0 == 0:
                acc[...] = part
            else:
                acc[...] += part
        dh = acc[...]
        xf = x_ref[...]
        r = lax.rsqrt(jnp.mean(xf * xf, axis=-1, keepdims=True) + _NORM_EPS)
        xn = xf * r
        gnv = gn_ref[...]
        dsh_ref[...] += jnp.sum(dh, axis=0, keepdims=True)
        dsc_ref[...] += jnp.sum(dh * (xn * gnv), axis=0, keepdims=True)
        drn = dh * (1.0 + sc_ref[...])
        dgn_ref[...] += jnp.sum(drn * xn, axis=0, keepdims=True)
        dxn = drn * gnv
        dx_ref[...] = dxo_ref[...] + r * (dxn - xn * jnp.mean(dxn * xn, axis=-1, keepdims=True))

    full = pl.BlockSpec((tm, d), lambda i: (i, 0))
    return pl.pallas_call(
        body, name=name, grid=(t // tm,),
        in_specs=[pl.BlockSpec((tm, n), lambda i: (i, 0)), _resident((None, d, n), lambda i: (layer, 0, 0)),
                  full, full, _row(d), _row(d)],
        out_specs=[full, _row(d), _row(d), _row(d)],
        out_shape=[SDS((t, d), F32), SDS((1, d), F32), SDS((1, d), F32), SDS((1, d), F32)],
        scratch_shapes=[pltpu.VMEM((tm, d), F32)], compiler_params=_params(1),
    )(dz, w, x, dxo, gn, sc)


def _final_loss(x, g, target, name):
    t, d = x.shape
    tm = _tile(t, _ROW_TILE, 8)

    def body(x_ref, g_ref, t_ref, loss_ref, dx_ref, dg_ref):
        @pl.when(pl.program_id(0) == 0)
        def _():
            loss_ref[...] = jnp.zeros_like(loss_ref)
            dg_ref[...] = jnp.zeros_like(dg_ref)

        xf = x_ref[...]
        r = lax.rsqrt(jnp.mean(xf * xf, axis=-1, keepdims=True) + _NORM_EPS)
        xn = xf * r
        gv = g_ref[...]
        e = xn * gv - t_ref[...]
        per_row = jnp.mean(e * e, axis=-1, keepdims=True)
        loss_ref[...] += 0.5 * jnp.sum(per_row, axis=0, keepdims=True)
        dy = e * (1.0 / d)
        dg_ref[...] += jnp.sum(dy * xn, axis=0, keepdims=True)
        dxn = dy * gv
        dx_ref[...] = r * (dxn - xn * jnp.mean(dxn * xn, axis=-1, keepdims=True))

    full = pl.BlockSpec((tm, d), lambda i: (i, 0))
    return pl.pallas_call(
        body, name=name, grid=(t // tm,), in_specs=[full, _row(d), full],
        out_specs=[_row(_LANES), full, _row(d)],
        out_shape=[SDS((1, _LANES), F32), SDS((t, d), F32), SDS((1, d), F32)], compiler_params=_params(1),
    )(x, g, target)


def _attn_tables(nh):
    group = nh // _N_KV
    slopes = 2.0 ** (-8.0 * jnp.arange(1, nh + 1, dtype=F32) / nh)
    qpos = jnp.arange(_ATTN_BLOCK) + _ATTN_BLOCK
    kpos = jnp.arange(2 * _ATTN_BLOCK)
    dist = qpos[:, None] - kpos[None, :]
    band = (dist >= 0) & (dist < _ATTN_BLOCK)
    bias = jnp.where(band[None], -slopes[:, None, None] * dist.astype(F32)[None], _NEG_INF)
    first = jnp.where((kpos < _ATTN_BLOCK)[None, None, :], _NEG_INF, bias)
    return jnp.stack([first, bias]).reshape(2, _N_KV, group * _ATTN_BLOCK, 2 * _ATTN_BLOCK)


def _attn_probs(q4, k2, tab, sink_ref, kv, group):
    hd, blk = _HEAD_DIM, _ATTN_BLOCK
    s = _nt(q4, k2) * (hd ** -0.5) + tab
    sink = jnp.concatenate([jnp.full((blk, 1), sink_ref[kv * group + g], F32) for g in range(group)], axis=0)
    m = jnp.maximum(jnp.max(s, axis=-1, keepdims=True), sink)
    e = jnp.exp(s - m)
    es = jnp.exp(sink - m)
    inv = 1.0 / (jnp.sum(e, axis=-1, keepdims=True) + es)
    return e * inv, es * inv


def _attn_operands(refs, b, kv, group):
    q_ref, kp_ref, kc_ref, vp_ref, vc_ref = refs
    hd, blk = _HEAD_DIM, _ATTN_BLOCK
    rows, cs = slice(b * blk, (b + 1) * blk), slice(kv * hd, (kv + 1) * hd)
    before = slice((b - 1) * blk, b * blk)
    k2 = jnp.concatenate([kp_ref[:, cs] if b == 0 else kc_ref[before, cs], kc_ref[rows, cs]], axis=0)
    v2 = jnp.concatenate([vp_ref[:, cs] if b == 0 else vc_ref[before, cs], vc_ref[rows, cs]], axis=0)
    q4 = jnp.concatenate([q_ref[rows, (kv * group + g) * hd:(kv * group + g + 1) * hd] for g in range(group)], axis=0)
    return q4, k2, v2


def _attn_specs(nh, nblk, step):
    group = nh // _N_KV
    blk, kvw = _ATTN_BLOCK, _N_KV * _HEAD_DIM
    prev = lambda i: jnp.maximum(step(i) * nblk - 1, 0)
    return [
        pl.BlockSpec((nblk * blk, nh * _HEAD_DIM), lambda i: (step(i), 0)),
        pl.BlockSpec((blk, kvw), lambda i: (prev(i), group)),
        pl.BlockSpec((nblk * blk, kvw), lambda i: (step(i), group)),
        pl.BlockSpec((blk, kvw), lambda i: (prev(i), group + 1)),
        pl.BlockSpec((nblk * blk, kvw), lambda i: (step(i), group + 1)),
        pl.BlockSpec((2, _N_KV, group * blk, 2 * blk), lambda i: (0, 0, 0, 0)),
        pl.BlockSpec(memory_space=pltpu.SMEM),
    ]


def _attn_fwd(qkv, tables, sinks, nh, name):
    t = qkv.shape[0]
    group, hd, blk = nh // _N_KV, _HEAD_DIM, _ATTN_BLOCK
    nblk = _ATTN_BLOCKS_PER_STEP if t % (_ATTN_BLOCKS_PER_STEP * blk) == 0 else 1

    def body(q_ref, kp_ref, kc_ref, vp_ref, vc_ref, tab_ref, sink_ref, o_ref):
        i = pl.program_id(0)
        for b in range(nblk):
            tab = tab_ref.at[jnp.minimum(i * nblk + b, 1)]
            outs = [None] * nh
            for kv in range(_N_KV):
                q4, k2, v2 = _attn_operands((q_ref, kp_ref, kc_ref, vp_ref, vc_ref), b, kv, group)
                p, _ = _attn_probs(q4, k2, tab[kv], sink_ref, kv, group)
                o4 = jnp.dot(p.astype(BF16), v2, preferred_element_type=F32)
                for g in range(group):
                    outs[kv * group + g] = o4[g * blk:(g + 1) * blk, :]
            o_ref[b * blk:(b + 1) * blk, :] = jnp.concatenate(outs, axis=1).astype(BF16)

    return pl.pallas_call(
        body, name=name, grid=(t // (nblk * blk),), in_specs=_attn_specs(nh, nblk, lambda i: i),
        out_specs=pl.BlockSpec((nblk * blk, nh * hd), lambda i: (i, 0)), out_shape=SDS((t, nh * hd), BF16),
        compiler_params=_params(1),
    )(qkv, qkv, qkv, qkv, qkv, tables, sinks)


def _attn_bwd(qkv, do, tables, sinks, nh, name):
    t, wq = qkv.shape
    group, hd, blk = nh // _N_KV, _HEAD_DIM, _ATTN_BLOCK
    nblk = _ATTN_BLOCKS_PER_STEP if t % (_ATTN_BLOCKS_PER_STEP * blk) == 0 else 1
    nsteps = t // (nblk * blk)
    kvw = _N_KV * hd
    step = lambda i: nsteps - 1 - i

    def body(q_ref, kp_ref, kc_ref, vp_ref, vc_ref, tab_ref, sink_ref, do_ref, dqkv_ref, dsink_ref, db_ref, ck, cv):
        i = pl.program_id(0)

        @pl.when(i == 0)
        def _():
            ck[...] = jnp.zeros_like(ck)
            cv[...] = jnp.zeros_like(cv)
            dsink_ref[...] = jnp.zeros_like(dsink_ref)
            db_ref[...] = jnp.zeros_like(db_ref)

        lane = lax.broadcasted_iota(jnp.int32, (1, _LANES), 1)
        carry_k = [ck[:, kv * hd:(kv + 1) * hd] for kv in range(_N_KV)]
        carry_v = [cv[:, kv * hd:(kv + 1) * hd] for kv in range(_N_KV)]
        dsink = jnp.zeros((1, _LANES), F32)
        dbias = jnp.zeros((1, wq), F32)
        for b in reversed(range(nblk)):
            rows = slice(b * blk, (b + 1) * blk)
            tab = tab_ref.at[jnp.minimum(step(i) * nblk + b, 1)]
            dq, dk, dv = [None] * nh, [None] * _N_KV, [None] * _N_KV
            for kv in range(_N_KV):
                q4, k2, v2 = _attn_operands((q_ref, kp_ref, kc_ref, vp_ref, vc_ref), b, kv, group)
                p, ps = _attn_probs(q4, k2, tab[kv], sink_ref, kv, group)
                do4 = jnp.concatenate([do_ref[rows, (kv * group + g) * hd:(kv * group + g + 1) * hd] for g in range(group)], axis=0)
                dp = _nt(do4, v2)
                dl = jnp.sum(p * dp, axis=-1, keepdims=True)
                ds = (p * (dp - dl)).astype(BF16)
                dsk = -ps * dl
                for g in range(group):
                    dsink = dsink + jnp.where(lane == kv * group + g, jnp.sum(dsk[g * blk:(g + 1) * blk, :]), 0.0)
                dq4 = jnp.dot(ds, k2, preferred_element_type=F32) * (hd ** -0.5)
                for g in range(group):
                    dq[kv * group + g] = dq4[g * blk:(g + 1) * blk, :]
                dk2 = _tn(q4, ds).T * (hd ** -0.5)
                dv2 = _tn(do4, p.astype(BF16)).T
                dk[kv] = dk2[blk:, :] + carry_k[kv]
                dv[kv] = dv2[blk:, :] + carry_v[kv]
                carry_k[kv], carry_v[kv] = dk2[:blk, :], dv2[:blk, :]
            dqkv = jnp.concatenate(dq + dk + dv, axis=1)
            dqkv_ref[rows, :] = dqkv.astype(BF16)
            dbias = dbias + jnp.sum(dqkv, axis=0, keepdims=True)
        for kv in range(_N_KV):
            ck[:, kv * hd:(kv + 1) * hd] = carry_k[kv]
            cv[:, kv * hd:(kv + 1) * hd] = carry_v[kv]
        db_ref[...] += dbias
        dsink_ref[...] += dsink

    return pl.pallas_call(
        body, name=name, grid=(nsteps,),
        in_specs=_attn_specs(nh, nblk, step) + [pl.BlockSpec((nblk * blk, nh * hd), lambda i: (step(i), 0))],
        out_specs=[pl.BlockSpec((nblk * blk, wq), lambda i: (step(i), 0)), _row(_LANES), _row(wq)],
        out_shape=[SDS((t, wq), BF16), SDS((1, _LANES), F32), SDS((1, wq), F32)],
        scratch_shapes=[pltpu.VMEM((blk, kvw), F32), pltpu.VMEM((blk, kvw), F32)], compiler_params=_params(1),
    )(qkv, qkv, qkv, qkv, qkv, tables, sinks, do)


_SUBLANES = 8


_ALL_SHIFTS = tuple(range(1, _SUBLANES))


def _shift_copies(src_ref, sh_ref, shifts=_ALL_SHIFTS):
    rows = src_ref.shape[0]
    full = src_ref[...]
    for n, b in enumerate(shifts):
        sh_ref[n] = pltpu.roll(full, rows - b, axis=0)
    return sh_ref, shifts


def _rows_at(src_ref, shifted, off, r0, rg, cs):
    b = off % _SUBLANES
    if shifted is None or b not in shifted[1]:
        return src_ref[r0 + off:r0 + off + rg, cs]
    return shifted[0][shifted[1].index(b), r0 + off - b:r0 + off - b + rg, cs]


def _taps(src_ref, w_ref, dst_ref, n_rows, width, offs, rg, shifted=None):
    cg = _tile(width, 512, _LANES)
    for c0 in range(0, width, cg):
        cs = slice(c0, c0 + cg)
        wk = [w_ref[k:k + 1, cs] for k, _ in offs]
        for r0 in range(0, n_rows, rg):
            acc = None
            for (_, off), wv in zip(offs, wk):
                term = wv * _rows_at(src_ref, shifted, off, r0, rg, cs)
                acc = term if acc is None else acc + term
            dst_ref[r0:r0 + rg, cs] = acc


def _tap_grads(dy_ref, z_ref, out_ref, n_rows, width, offs, rg, shifted=None):
    cg = _tile(width, 512, _LANES)
    for c0 in range(0, width, cg):
        cs = slice(c0, c0 + cg)
        for k, off in offs:
            acc = None
            for r0 in range(0, n_rows, rg):
                term = dy_ref[r0:r0 + rg, cs] * _rows_at(z_ref, shifted, off, r0, rg, cs)
                acc = term if acc is None else acc + term
            out_ref[k:k + 1, cs] += jnp.sum(acc, axis=0, keepdims=True)


def _conv_mid_fwd(ag, dw, dwb, lng, lnb, name):
    t, c2 = ag.shape
    c = c2 // 2
    kw = dw.shape[0]
    hl = _CONV_HALO
    tm = _tile(t, 256, hl)
    per = tm // hl

    def body(agp_ref, ag_ref, dw_ref, dwb_ref, lng_ref, lnb_ref, o_ref, zext, yb, zsh):
        i = pl.program_id(0)
        glu = lambda ref: ref[:, :c].astype(F32) * _sigmoid(ref[:, c:].astype(F32))
        zext[0:hl, :] = jnp.where(i > 0, glu(agp_ref), 0.0)
        zext[hl:, :] = glu(ag_ref)
        _taps(zext, dw_ref, yb, tm, c, [(k, hl - (kw - 1) + k) for k in range(kw)], 32, _shift_copies(zext, zsh))
        y = yb[...] + dwb_ref[...]
        mu = jnp.mean(y, axis=-1, keepdims=True)
        yc = y - mu
        rstd = lax.rsqrt(jnp.mean(yc * yc, axis=-1, keepdims=True) + _NORM_EPS)
        ln = yc * rstd * lng_ref[...] + lnb_ref[...]
        o_ref[...] = (ln * _sigmoid(ln)).astype(BF16)

    return pl.pallas_call(
        body, name=name, grid=(t // tm,),
        in_specs=[pl.BlockSpec((hl, c2), lambda i: (jnp.maximum(i * per - 1, 0), 0)), pl.BlockSpec((tm, c2), lambda i: (i, 0)),
                  pl.BlockSpec((kw, c), lambda i: (0, 0)), _row(c), _row(c), _row(c)],
        out_specs=pl.BlockSpec((tm, c), lambda i: (i, 0)), out_shape=SDS((t, c), BF16),
        scratch_shapes=[pltpu.VMEM((hl + tm, c), F32), pltpu.VMEM((tm, c), F32), pltpu.VMEM((_SUBLANES - 1, hl + tm, c), F32)],
        compiler_params=_params(1),
    )(ag, ag, dw, dwb, lng, lnb)


def _conv_mid_bwd(ag, dzc, dw, dwb, lng, lnb, name):
    t, c2 = ag.shape
    c = c2 // 2
    kw = dw.shape[0]
    hl = _CONV_HALO
    tm = _tile(t, 256, hl)
    per = tm // hl
    nt = t // tm
    last_halo = t // hl - 1

    def body(agp_ref, ag_ref, agn_ref, dzc_ref, dzcn_ref, dw_ref, dwb_ref, lng_ref, lnb_ref,
             dag_ref, ddw_ref, ddwb_ref, dlng_ref, dlnb_ref, dbin_ref, zext, yext, dyext, dzb, zsh, dysh):
        i = pl.program_id(0)

        @pl.when(i == 0)
        def _():
            for r in (ddw_ref, ddwb_ref, dlng_ref, dlnb_ref, dbin_ref):
                r[...] = jnp.zeros_like(r)

        glu = lambda ref: ref[:, :c].astype(F32) * _sigmoid(ref[:, c:].astype(F32))
        zext[0:hl, :] = jnp.where(i > 0, glu(agp_ref), 0.0)
        zext[hl:hl + tm, :] = glu(ag_ref)
        zext[hl + tm:, :] = glu(agn_ref)
        fwd_offs = [(k, hl - (kw - 1) + k) for k in range(kw)]
        z_shifted = _shift_copies(zext, zsh)
        _taps(zext, dw_ref, yext, tm + hl, c, fwd_offs, 32, z_shifted)
        y = yext[...] + dwb_ref[...]
        mu = jnp.mean(y, axis=-1, keepdims=True)
        yc = y - mu
        rstd = lax.rsqrt(jnp.mean(yc * yc, axis=-1, keepdims=True) + _NORM_EPS)
        xhat = yc * rstd
        lngv = lng_ref[...]
        ln = xhat * lngv + lnb_ref[...]
        sg = _sigmoid(ln)
        dz_out = jnp.concatenate([dzc_ref[...].astype(F32), jnp.where(i < nt - 1, dzcn_ref[...].astype(F32), 0.0)], axis=0)
        dln = dz_out * (sg * (1.0 + ln * (1.0 - sg)))
        dlng_ref[...] += jnp.sum((dln * xhat)[:tm], axis=0, keepdims=True)
        dlnb_ref[...] += jnp.sum(dln[:tm], axis=0, keepdims=True)
        dxh = dln * lngv
        dy = rstd * (dxh - jnp.mean(dxh, axis=-1, keepdims=True) - xhat * jnp.mean(dxh * xhat, axis=-1, keepdims=True))
        dyext[...] = dy
        ddwb_ref[...] += jnp.sum(dy[:tm], axis=0, keepdims=True)
        _taps(dyext, dw_ref, dzb, tm, c, [(k, kw - 1 - k) for k in range(kw)], 32, _shift_copies(dyext, dysh))
        _tap_grads(dyext, zext, ddw_ref, tm, c, fwd_offs, 32, z_shifted)
        a = ag_ref[:, :c].astype(F32)
        sgg = _sigmoid(ag_ref[:, c:].astype(F32))
        dz = dzb[...]
        da = dz * sgg
        dg = dz * a * sgg * (1.0 - sgg)
        dag_ref[:, :c] = da.astype(BF16)
        dag_ref[:, c:] = dg.astype(BF16)
        dbin_ref[:, :c] += jnp.sum(da, axis=0, keepdims=True)
        dbin_ref[:, c:] += jnp.sum(dg, axis=0, keepdims=True)

    prev = lambda i: (jnp.maximum(i * per - 1, 0), 0)
    nxt = lambda i: (jnp.minimum((i + 1) * per, last_halo), 0)
    return pl.pallas_call(
        body, name=name, grid=(nt,),
        in_specs=[pl.BlockSpec((hl, c2), prev), pl.BlockSpec((tm, c2), lambda i: (i, 0)), pl.BlockSpec((hl, c2), nxt),
                  pl.BlockSpec((tm, c), lambda i: (i, 0)), pl.BlockSpec((hl, c), nxt),
                  pl.BlockSpec((kw, c), lambda i: (0, 0)), _row(c), _row(c), _row(c)],
        out_specs=[pl.BlockSpec((tm, c2), lambda i: (i, 0)), pl.BlockSpec((kw, c), lambda i: (0, 0)), _row(c), _row(c), _row(c), _row(c2)],
        out_shape=[SDS((t, c2), BF16), SDS((kw, c), F32), SDS((1, c), F32), SDS((1, c), F32), SDS((1, c), F32), SDS((1, c2), F32)],
        scratch_shapes=[pltpu.VMEM((hl + tm + hl, c), F32), pltpu.VMEM((tm + hl, c), F32), pltpu.VMEM((tm + hl, c), F32),
                        pltpu.VMEM((tm, c), F32), pltpu.VMEM((_SUBLANES - 1, hl + tm + hl, c), F32),
                        pltpu.VMEM((_SUBLANES - 1, tm + hl, c), F32)],
        compiler_params=_params(1),
    )(ag, ag, ag, dzc, dzc, dw, dwb, lng, lnb)


def _ffn_mid_fwd(zf, dw, dwb, name):
    t, f2 = zf.shape
    f = f2 // 2
    kw = dw.shape[0]
    hl = _FFN_HALO
    tm = _tile(t, 256, hl)
    per = tm // hl
    tc = _tile(f, 256, _LANES)
    offs = [(k, hl - (kw - 1) + k) for k in range(kw)]
    shifts = tuple(sorted({off % _SUBLANES for _, off in offs} - {0}))

    def body(zp_ref, z_ref, dw_ref, dwb_ref, o_ref, zext, cb, zsh):
        i = pl.program_id(0)
        zext[0:hl, :] = jnp.where(i > 0, zp_ref[...].astype(F32), 0.0)
        zext[hl:, :] = z_ref[...].astype(F32)
        _taps(zext, dw_ref, cb, tm, f2, offs, 16, _shift_copies(zext, zsh, shifts))
        for c0 in range(0, f, tc):
            g = cb[:, c0:c0 + tc] + dwb_ref[:, c0:c0 + tc]
            u = cb[:, f + c0:f + c0 + tc] + dwb_ref[:, f + c0:f + c0 + tc]
            o_ref[:, c0:c0 + tc] = (g * _sigmoid(g) * u).astype(BF16)

    return pl.pallas_call(
        body, name=name, grid=(t // tm,),
        in_specs=[pl.BlockSpec((hl, f2), lambda i: (jnp.maximum(i * per - 1, 0), 0)), pl.BlockSpec((tm, f2), lambda i: (i, 0)),
                  pl.BlockSpec((kw, f2), lambda i: (0, 0)), _row(f2)],
        out_specs=pl.BlockSpec((tm, f), lambda i: (i, 0)), out_shape=SDS((t, f), BF16),
        scratch_shapes=[pltpu.VMEM((hl + tm, f2), F32), pltpu.VMEM((tm, f2), F32), pltpu.VMEM((len(shifts), hl + tm, f2), F32)],
        compiler_params=_params(1),
    )(zf, zf, dw, dwb)


def _ffn_mid_bwd(zf, dact, dw, dwb, name):
    t, f2 = zf.shape
    f = f2 // 2
    kw = dw.shape[0]
    hl = _FFN_HALO
    tm = _tile(t, 128, hl)
    per = tm // hl
    nt = t // tm
    last_halo = t // hl - 1
    tc = _tile(f, 256, _LANES)
    fwd_offs = [(k, hl - (kw - 1) + k) for k in range(kw)]
    bwd_offs = [(k, kw - 1 - k) for k in range(kw)]
    fwd_shifts = tuple(sorted({off % _SUBLANES for _, off in fwd_offs} - {0}))
    bwd_shifts = tuple(sorted({off % _SUBLANES for _, off in bwd_offs} - {0}))

    def body(zp_ref, z_ref, zn_ref, da_ref, dan_ref, dw_ref, dwb_ref, dzf_ref, ddw_ref, ddwb_ref, zext, cext, dcext, dzb,
             zsh, dcsh):
        i = pl.program_id(0)

        @pl.when(i == 0)
        def _():
            ddw_ref[...] = jnp.zeros_like(ddw_ref)
            ddwb_ref[...] = jnp.zeros_like(ddwb_ref)

        zext[0:hl, :] = jnp.where(i > 0, zp_ref[...].astype(F32), 0.0)
        zext[hl:hl + tm, :] = z_ref[...].astype(F32)
        zext[hl + tm:, :] = zn_ref[...].astype(F32)
        z_shifted = _shift_copies(zext, zsh, fwd_shifts)
        _taps(zext, dw_ref, cext, tm + hl, f2, fwd_offs, 16, z_shifted)
        for c0 in range(0, f, tc):
            g = cext[:, c0:c0 + tc] + dwb_ref[:, c0:c0 + tc]
            u = cext[:, f + c0:f + c0 + tc] + dwb_ref[:, f + c0:f + c0 + tc]
            da = jnp.concatenate([da_ref[:, c0:c0 + tc].astype(F32),
                                  jnp.where(i < nt - 1, dan_ref[:, c0:c0 + tc].astype(F32), 0.0)], axis=0)
            sg = _sigmoid(g)
            dcg = da * u * (sg * (1.0 + g * (1.0 - sg)))
            dcu = da * (g * sg)
            dcext[:, c0:c0 + tc] = dcg
            dcext[:, f + c0:f + c0 + tc] = dcu
            ddwb_ref[:, c0:c0 + tc] += jnp.sum(dcg[:tm], axis=0, keepdims=True)
            ddwb_ref[:, f + c0:f + c0 + tc] += jnp.sum(dcu[:tm], axis=0, keepdims=True)
        _taps(dcext, dw_ref, dzb, tm, f2, bwd_offs, 16, _shift_copies(dcext, dcsh, bwd_shifts))
        _tap_grads(dcext, zext, ddw_ref, tm, f2, fwd_offs, 16, z_shifted)
        dzf_ref[...] = dzb[...].astype(BF16)

    prev = lambda i: (jnp.maximum(i * per - 1, 0), 0)
    nxt = lambda i: (jnp.minimum((i + 1) * per, last_halo), 0)
    return pl.pallas_call(
        body, name=name, grid=(nt,),
        in_specs=[pl.BlockSpec((hl, f2), prev), pl.BlockSpec((tm, f2), lambda i: (i, 0)), pl.BlockSpec((hl, f2), nxt),
                  pl.BlockSpec((tm, f), lambda i: (i, 0)), pl.BlockSpec((hl, f), nxt),
                  pl.BlockSpec((kw, f2), lambda i: (0, 0)), _row(f2)],
        out_specs=[pl.BlockSpec((tm, f2), lambda i: (i, 0)), pl.BlockSpec((kw, f2), lambda i: (0, 0)), _row(f2)],
        out_shape=[SDS((t, f2), BF16), SDS((kw, f2), F32), SDS((1, f2), F32)],
        scratch_shapes=[pltpu.VMEM((hl + tm + hl, f2), F32), pltpu.VMEM((tm + hl, f2), F32), pltpu.VMEM((tm + hl, f2), F32),
                        pltpu.VMEM((tm, f2), F32), pltpu.VMEM((len(fwd_shifts), hl + tm + hl, f2), F32),
                        pltpu.VMEM((len(bwd_shifts), tm + hl, f2), F32)],
        compiler_params=_params(1),
    )(zf, zf, zf, dact, dact, dw, dwb)


_INV_SQRT2 = 0.7071067811865476
_INV_SQRT_2PI = 0.3989422804014327


def _sgu_common(zin_ref, lng_ref, lnb_ref, hh):
    z = zin_ref[...].astype(F32)
    cdf = 0.5 * (1.0 + lax.erf(z * _INV_SQRT2))
    ge = z * cdf
    u, v = ge[:, :hh], ge[:, hh:]
    mu = jnp.mean(v, axis=-1, keepdims=True)
    vc = v - mu
    rstd = lax.rsqrt(jnp.mean(vc * vc, axis=-1, keepdims=True) + _NORM_EPS)
    vhat = vc * rstd
    vn = vhat * lng_ref[...] + lnb_ref[...]
    return z, cdf, u, vhat, rstd, vn


def _sgu_wm(ws_ref, g, ch):
    rows = lax.broadcasted_iota(jnp.int32, (ch, ch), 0)
    cols = lax.broadcasted_iota(jnp.int32, (ch, ch), 1)
    return jnp.where(rows >= cols, ws_ref[g], 0.0).astype(BF16)


def _sgu_mid_fwd(zin, lng, lnb, ws, bs_t, name):
    t, h2 = zin.shape
    hh = h2 // 2
    ng, ch = ws.shape[0], ws.shape[1]
    hg = hh // ng
    tm = _tile(t, 256, ch)

    def body(zin_ref, lng_ref, lnb_ref, ws_ref, bs_ref, o_ref):
        _, _, u, _, _, vn = _sgu_common(zin_ref, lng_ref, lnb_ref, hh)
        vnb = vn.astype(BF16)
        for g in range(ng):
            wm = _sgu_wm(ws_ref, g, ch)
            for cc in range(tm // ch):
                rs, cs = slice(cc * ch, (cc + 1) * ch), slice(g * hg, (g + 1) * hg)
                vv = jnp.dot(wm, vnb[rs, cs], preferred_element_type=F32) + bs_ref[:, g:g + 1]
                o_ref[rs, cs] = (u[rs, cs] * vv).astype(BF16)

    return pl.pallas_call(
        body, name=name, grid=(t // tm,),
        in_specs=[pl.BlockSpec((tm, h2), lambda i: (i, 0)), _row(hh), _row(hh),
                  pl.BlockSpec((ng, ch, ch), lambda i: (0, 0, 0)), pl.BlockSpec((ch, ng), lambda i: (0, 0))],
        out_specs=pl.BlockSpec((tm, hh), lambda i: (i, 0)), out_shape=SDS((t, hh), BF16), compiler_params=_params(1),
    )(zin, lng, lnb, ws, bs_t)


def _sgu_mid_bwd(zin, duv, lng, lnb, ws, bs_t, name):
    t, h2 = zin.shape
    hh = h2 // 2
    ng, ch = ws.shape[0], ws.shape[1]
    hg = hh // ng
    tm = _tile(t, 128, ch)

    def body(zin_ref, duv_ref, lng_ref, lnb_ref, ws_ref, bs_ref, dzin_ref, dlng_ref, dlnb_ref, dws_ref, dbs_ref, dbin_ref, dvn_s, du_s):
        @pl.when(pl.program_id(0) == 0)
        def _():
            for r in (dlng_ref, dlnb_ref, dws_ref, dbs_ref, dbin_ref):
                r[...] = jnp.zeros_like(r)

        z, cdf, u, vhat, rstd, vn = _sgu_common(zin_ref, lng_ref, lnb_ref, hh)
        vnb = vn.astype(BF16)
        duv = duv_ref[...].astype(F32)
        dvv = (duv * u).astype(BF16)
        lane = lax.broadcasted_iota(jnp.int32, (1, _LANES), 1)
        rows = lax.broadcasted_iota(jnp.int32, (ch, ch), 0)
        cols = lax.broadcasted_iota(jnp.int32, (ch, ch), 1)
        dbs = jnp.zeros((ch, _LANES), F32)
        for g in range(ng):
            wm = _sgu_wm(ws_ref, g, ch)
            dwm = jnp.zeros((ch, ch), F32)
            for cc in range(tm // ch):
                rs, cs = slice(cc * ch, (cc + 1) * ch), slice(g * hg, (g + 1) * hg)
                vv = jnp.dot(wm, vnb[rs, cs], preferred_element_type=F32) + bs_ref[:, g:g + 1]
                du_s[rs, cs] = duv[rs, cs] * vv
                dvn_s[rs, cs] = _tn(wm, dvv[rs, cs])
                dwm = dwm + _nt(dvv[rs, cs], vnb[rs, cs])
                dbs = dbs + jnp.where(lane == g, jnp.sum(dvv[rs, cs].astype(F32), axis=-1, keepdims=True), 0.0)
            dws_ref[g] += jnp.where(rows >= cols, dwm, 0.0)
        dbs_ref[...] += dbs
        dvn = dvn_s[...]
        dlng_ref[...] += jnp.sum(dvn * vhat, axis=0, keepdims=True)
        dlnb_ref[...] += jnp.sum(dvn, axis=0, keepdims=True)
        dxh = dvn * lng_ref[...]
        dv = rstd * (dxh - jnp.mean(dxh, axis=-1, keepdims=True) - vhat * jnp.mean(dxh * vhat, axis=-1, keepdims=True))
        dgelu = cdf + z * (_INV_SQRT_2PI * jnp.exp(-0.5 * z * z))
        dzu = du_s[...] * dgelu[:, :hh]
        dzv = dv * dgelu[:, hh:]
        dzin_ref[:, :hh] = dzu.astype(BF16)
        dzin_ref[:, hh:] = dzv.astype(BF16)
        dbin_ref[:, :hh] += jnp.sum(dzu, axis=0, keepdims=True)
        dbin_ref[:, hh:] += jnp.sum(dzv, axis=0, keepdims=True)

    return pl.pallas_call(
        body, name=name, grid=(t // tm,),
        in_specs=[pl.BlockSpec((tm, h2), lambda i: (i, 0)), pl.BlockSpec((tm, hh), lambda i: (i, 0)), _row(hh), _row(hh),
                  pl.BlockSpec((ng, ch, ch), lambda i: (0, 0, 0)), pl.BlockSpec((ch, ng), lambda i: (0, 0))],
        out_specs=[pl.BlockSpec((tm, h2), lambda i: (i, 0)), _row(hh), _row(hh), pl.BlockSpec((ng, ch, ch), lambda i: (0, 0, 0)),
                   pl.BlockSpec((ch, _LANES), lambda i: (0, 0)), _row(h2)],
        out_shape=[SDS((t, h2), BF16), SDS((1, hh), F32), SDS((1, hh), F32), SDS((ng, ch, ch), F32), SDS((ch, _LANES), F32),
                   SDS((1, h2), F32)],
        scratch_shapes=[pltpu.VMEM((tm, hh), F32), pltpu.VMEM((tm, hh), F32)], compiler_params=_params(1),
    )(zin, duv, lng, lnb, ws, bs_t)


def _ada_mod(c_all, ada_w, ada_b, name):
    nl, d, n = ada_w.shape
    nb = c_all.shape[0]
    tn = _tile(n, _COL_TILE, _LANES)

    def body(c_ref, w_ref, b_ref, o_ref):
        cv = c_ref[...]
        ca = cv * _sigmoid(cv)
        o_ref[...] = jnp.dot(ca, w_ref[...], preferred_element_type=F32, precision=lax.Precision.HIGHEST) + b_ref[...]

    return pl.pallas_call(
        body, name=name, grid=(nl, n // tn),
        in_specs=[pl.BlockSpec((nb, d), lambda l, j: (0, 0)), pl.BlockSpec((None, d, tn), lambda l, j: (l, 0, j)),
                  pl.BlockSpec((None, 1, tn), lambda l, j: (l, 0, j))],
        out_specs=pl.BlockSpec((None, nb, tn), lambda l, j: (l, 0, j)), out_shape=SDS((nl, nb, n), F32),
        compiler_params=_params(2),
    )(c_all, ada_w, ada_b)


def _ada_wgrad(c_all_t, dmod, name):
    d, nb = c_all_t.shape
    nl, _, n = dmod.shape
    tn = _tile(n, _COL_TILE, _LANES)

    def body(c_ref, dm_ref, o_ref):
        cv = c_ref[...]
        ca = cv * _sigmoid(cv)
        acc = ca[:, 0:1] * dm_ref[0:1, :]
        for b in range(1, nb):
            acc = acc + ca[:, b:b + 1] * dm_ref[b:b + 1, :]
        o_ref[...] = acc

    return pl.pallas_call(
        body, name=name, grid=(nl, n // tn),
        in_specs=[pl.BlockSpec((d, nb), lambda l, j: (0, 0)), pl.BlockSpec((None, nb, tn), lambda l, j: (l, 0, j))],
        out_specs=pl.BlockSpec((None, d, tn), lambda l, j: (l, 0, j)), out_shape=SDS((nl, d, n), F32),
        compiler_params=_params(2),
    )(c_all_t, dmod)


def _as_rows(a):
    return a.reshape(-1, a.shape[-1])


def _row_tile(r, c, n_arrays):
    budget = _VMEM_LIMIT_BYTES // (4 * 2 * n_arrays * 4)
    return _tile(r, max(8, budget // max(c, 1)), 8)


def _cast_bf16_layer(a, layer, name):
    _, r, c = a.shape
    tr = _row_tile(r, c, 2)

    def body(a_ref, o_ref):
        o_ref[...] = a_ref[...].astype(BF16)

    return pl.pallas_call(body, name=name, grid=(r // tr,), in_specs=[pl.BlockSpec((None, tr, c), lambda i: (layer, i, 0))],
                          out_specs=pl.BlockSpec((tr, c), lambda i: (i, 0)), out_shape=SDS((r, c), BF16),
                          compiler_params=_params(1))(a)


def _sum_slot_layers(parts, name):
    nl = len(parts)
    n, r, c = parts[0].shape
    tr = _row_tile(r, c, nl * (n + 1))

    def body(*refs):
        o_ref = refs[nl]
        for lay in range(nl):
            acc = refs[lay][0].astype(F32)
            for s in range(1, n):
                acc = acc + refs[lay][s].astype(F32)
            o_ref[lay] = acc

    return pl.pallas_call(body, name=name, grid=(r // tr,), in_specs=[pl.BlockSpec((n, tr, c), lambda i: (0, i, 0))] * nl,
                          out_specs=pl.BlockSpec((nl, tr, c), lambda i: (0, i, 0)), out_shape=SDS((nl, r, c), F32),
                          compiler_params=_params(1))(*parts)


def _sum_slots(parts, name):
    n = parts.shape[0]
    p2 = parts.reshape(n, -1, parts.shape[-1])
    r, c = p2.shape[1:]
    tr = _row_tile(r, c, n + 1)

    def body(p_ref, o_ref):
        acc = p_ref[0].astype(F32)
        for s in range(1, n):
            acc = acc + p_ref[s].astype(F32)
        o_ref[...] = acc

    out = pl.pallas_call(body, name=name, grid=(r // tr,), in_specs=[pl.BlockSpec((n, tr, c), lambda i: (0, i, 0))],
                         out_specs=pl.BlockSpec((tr, c), lambda i: (i, 0)), out_shape=SDS((r, c), F32),
                         compiler_params=_params(1))(p2)
    return out.reshape(parts.shape[1:])


def _adamw(w, g_parts, m, v, name):
    w2, m2, v2 = _as_rows(w), _as_rows(m), _as_rows(v)
    gs = [_as_rows(g) for g in g_parts]
    r, c = w2.shape
    ng = len(gs)
    tr = _row_tile(r, c, 7 + ng)
    c1 = 1.0 - _ADAM_B1 ** _ADAM_STEP
    c2 = 1.0 - _ADAM_B2 ** _ADAM_STEP

    def body(*refs):
        w_ref, m_ref, v_ref = refs[:3]
        g_refs = refs[3:3 + ng]
        go_ref, d_ref, mo_ref, vo_ref = refs[3 + ng:]
        g = g_refs[0][...]
        for gr in g_refs[1:]:
            g = g + gr[...]
        mn = _ADAM_B1 * m_ref[...] + (1.0 - _ADAM_B1) * g
        vn = _ADAM_B2 * v_ref[...] + (1.0 - _ADAM_B2) * (g * g)
        go_ref[...] = g
        mo_ref[...] = mn
        vo_ref[...] = vn
        d_ref[...] = -_ADAM_LR * ((mn / c1) / (jnp.sqrt(vn / c2) + _ADAM_EPS) + _ADAM_WD * w_ref[...])

    spec = pl.BlockSpec((tr, c), lambda i: (i, 0))
    outs = pl.pallas_call(body, name=name, grid=(r // tr,), in_specs=[spec] * (3 + ng), out_specs=[spec] * 4,
                          out_shape=[SDS((r, c), F32)] * 4, compiler_params=_params(1))(w2, m2, v2, *gs)
    return tuple(o.reshape(w.shape) for o in outs)


_HBM = pl.BlockSpec(memory_space=pltpu.HBM)
_CHIP_FLIPS = ((1, 0, 0), (0, 1, 0), (1, 1, 0))
_ALL_FLIPS = tuple((a, b, c) for a in (0, 1) for b in (0, 1) for c in (0, 1))[1:]
_SIBLING_FLIP = ((0, 0, 1),)


def _at(ref, idx):
    return ref.at[idx] if idx else ref


def _exchange(name, flips, srcs, out_shapes, items):
    n_in, n_out = len(srcs), len(out_shapes)
    n_remote = len(items) * len(flips)
    n_local = sum(1 for it in items if it[4])

    def body(*refs):
        src_refs, dst_refs = refs[:n_in], refs[n_in:n_in + n_out]
        send_sems, recv_sems, local_sems = refs[n_in + n_out:]
        me = (lax.axis_index("x"), lax.axis_index("y"), lax.axis_index("c"))
        sends, recvs, locs = [], [], []
        n, nl = 0, 0
        for si, di, src_idx, dst_idx, local in items:
            for flip in flips:
                peer = tuple(1 - m if f else m for m, f in zip(me, flip))
                push = pltpu.make_async_remote_copy(
                    src_ref=_at(src_refs[si], src_idx(*peer)), dst_ref=_at(dst_refs[di], dst_idx(*me)),
                    send_sem=send_sems.at[n], recv_sem=recv_sems.at[n], device_id=peer, device_id_type=MESH_IDS)
                push.start()
                sends.append(push)
                recvs.append(pltpu.make_async_remote_copy(
                    src_ref=_at(src_refs[si], src_idx(*me)), dst_ref=_at(dst_refs[di], dst_idx(*peer)),
                    send_sem=send_sems.at[n], recv_sem=recv_sems.at[n], device_id=peer, device_id_type=MESH_IDS))
                n += 1
            if local:
                cp = pltpu.make_async_copy(_at(src_refs[si], src_idx(*me)), _at(dst_refs[di], dst_idx(*me)), local_sems.at[nl])
                cp.start()
                locs.append(cp)
                nl += 1
        for r in recvs:
            r.wait_recv()
        for s in sends:
            s.wait_send()
        for cp in locs:
            cp.wait()

    return pl.pallas_call(
        body, name=name, in_specs=[_HBM] * n_in, out_specs=[_HBM] * n_out, out_shape=list(out_shapes),
        scratch_shapes=[pltpu.SemaphoreType.DMA((n_remote,)), pltpu.SemaphoreType.DMA((n_remote,)),
                        pltpu.SemaphoreType.DMA((max(n_local, 1),))],
    )(*srcs)


_SEM = pl.BlockSpec(memory_space=pltpu.SEMAPHORE)
_SIDE_EFFECTS = pltpu.SideEffectType.DATAFLOW_SIDE_EFFECTING


def _split_copies(src_refs, land_refs, send_sems, recv_sems, items):
    me = (lax.axis_index("x"), lax.axis_index("y"), lax.axis_index("c"))
    pushes, arrivals, n = [], [], 0
    for si, di, src_idx, dst_idx in items:
        for flip in _CHIP_FLIPS:
            peer = tuple(1 - m if f else m for m, f in zip(me, flip))
            pushes.append(pltpu.make_async_remote_copy(
                src_ref=_at(src_refs[si], src_idx(*peer)), dst_ref=_at(land_refs[di], dst_idx(*me)),
                send_sem=send_sems.at[n], recv_sem=recv_sems.at[n], device_id=peer, device_id_type=MESH_IDS))
            arrivals.append(pltpu.make_async_remote_copy(
                src_ref=_at(src_refs[si], src_idx(*me)), dst_ref=_at(land_refs[di], dst_idx(*peer)),
                send_sem=send_sems.at[n], recv_sem=recv_sems.at[n], device_id=peer, device_id_type=MESH_IDS))
            n += 1
    return me, pushes, arrivals


def _push_start(name, srcs, land_shapes, items, after):
    n_src, n_land = len(srcs), len(land_shapes)
    n_buf = n_src + n_land
    n_remote = len(items) * len(_CHIP_FLIPS)

    def body(*refs):
        src_refs, land_refs = refs[:n_src], refs[n_src:n_buf]
        send_sems, recv_sems = refs[n_buf + 1], refs[n_buf + 2]
        token, local_sems = refs[2 * n_buf + 3], refs[2 * n_buf + 4]
        me, pushes, _ = _split_copies(src_refs, land_refs, send_sems, recv_sems, items)
        own = [pltpu.make_async_copy(_at(src_refs[si], src_idx(*me)), _at(land_refs[di], dst_idx(*me)), local_sems.at[k])
               for k, (si, di, src_idx, dst_idx) in enumerate(items)]
        for cp in own:
            cp.start()
        for cp in own:
            cp.wait()
        for p in pushes:
            p.start()
        token[...] = jnp.zeros_like(token)

    bufs = list(srcs) + [lax.empty(s.shape, s.dtype) for s in land_shapes]
    outs = pl.pallas_call(
        body, name=name, in_specs=[_HBM] * n_buf + [pl.BlockSpec(memory_space=pl.ANY)],
        out_shape=[pltpu.SemaphoreType.DMA((n_remote,)), pltpu.SemaphoreType.DMA((n_remote,))]
        + [pltpu.HBM(a.shape, a.dtype) for a in bufs] + [SDS((8, _LANES), F32)],
        out_specs=[_SEM, _SEM] + [_HBM] * n_buf + [pl.BlockSpec(memory_space=pltpu.VMEM)],
        input_output_aliases={k: 2 + k for k in range(n_buf)},
        scratch_shapes=[pltpu.SemaphoreType.DMA((len(items),))],
        compiler_params=pltpu.CompilerParams(has_side_effects=_SIDE_EFFECTS),
    )(*[pltpu.with_memory_space_constraint(a, pltpu.HBM) for a in bufs], after)
    return dict(send=outs[0], recv=outs[1], bufs=list(outs[2:2 + n_buf]), token=outs[2 + n_buf][0:1, 0:1],
                n_src=n_src, items=items)


def _push_wait(name, handle, after):
    n_src, items = handle["n_src"], handle["items"]
    n_buf = len(handle["bufs"])

    def body(*refs):
        src_refs, land_refs = refs[:n_src], refs[n_src:n_buf]
        send_sems, recv_sems = refs[n_buf], refs[n_buf + 1]
        _, pushes, arrivals = _split_copies(src_refs, land_refs, send_sems, recv_sems, items)
        for p in pushes:
            p.wait_send()
        for a in arrivals:
            a.wait_recv()

    outs = pl.pallas_call(
        body, name=name, in_specs=[_HBM] * n_buf + [_SEM, _SEM, pl.BlockSpec(memory_space=pl.ANY)],
        out_shape=[pltpu.HBM(a.shape, a.dtype) for a in handle["bufs"]], out_specs=[_HBM] * n_buf,
        input_output_aliases={k: k for k in range(n_buf)},
        compiler_params=pltpu.CompilerParams(has_side_effects=_SIDE_EFFECTS),
    )(*handle["bufs"], handle["send"], handle["recv"], after)
    return list(outs[n_src:])


def _chip_of(x, y, c):
    return 2 * x + y


def _dev_of(x, y, c):
    return 4 * x + 2 * y + c


def _window(axis, ndim, size):
    def idx(x, y, c):
        return tuple(pl.ds(_chip_of(x, y, c) * size, size) if a == axis else slice(None) for a in range(ndim))
    return idx


def _whole(x, y, c):
    return ()


def _start_gather(name, shards_axes, after):
    srcs, shapes, items = [], [], []
    for k, (a, axis) in enumerate(shards_axes):
        full = list(a.shape)
        full[axis] *= _N_CHIPS
        srcs.append(a)
        shapes.append(SDS((1,) + tuple(full), a.dtype))
        window = _window(axis, a.ndim, a.shape[axis])
        items.append((k, k, _whole, (lambda w: lambda x, y, c: (0,) + w(x, y, c))(window)))
    return _push_start(name, srcs, shapes, items, after)


def _start_scatter(name, grads_axes, after):
    srcs, shapes, items = [], [], []
    slot = lambda x, y, c: (_chip_of(x, y, c),)
    for k, (a, axis) in enumerate(grads_axes):
        shard = list(a.shape)
        shard[axis] //= _N_CHIPS
        srcs.append(a)
        shapes.append(SDS((_N_CHIPS,) + tuple(shard), a.dtype))
        items.append((k, k, _window(axis, a.ndim, shard[axis]), slot))
    return _push_start(name, srcs, shapes, items, after)


def _gather_all(name, a):
    slot = lambda x, y, c: (_dev_of(x, y, c),)
    return _exchange(name, _ALL_FLIPS, [a], [SDS((_N_DEV,) + a.shape, a.dtype)], [(0, 0, _whole, slot, True)])[0]


def _swap_chips(name, a):
    slot = lambda x, y, c: (_chip_of(x, y, c),)
    return _exchange(name, _CHIP_FLIPS, [a], [SDS(a.shape, a.dtype)], [(0, 0, slot, slot, True)])[0]


def _swap_sibling(name, arrays):
    items = [(k, k, _whole, _whole, False) for k in range(len(arrays))]
    return _exchange(name, _SIBLING_FLIP, list(arrays), [SDS(a.shape, a.dtype) for a in arrays], items)


_BIG = ("attn_wqkv", "attn_wo", "conv_w_in", "conv_w_out", "sgu_w_in", "sgu_w_out", "ffn_w_in", "ffn_w_out")
_BIG_AXIS = {"attn_wqkv": 2, "attn_wo": 1, "conv_w_in": 2, "conv_w_out": 1, "sgu_w_in": 2, "sgu_w_out": 1,
             "ffn_w_in": 2, "ffn_w_out": 1}
_SMALL = {"norm1_g": None, "norm2_g": None, "ada_b": None, "attn_bqkv": 1, "attn_sinks": None, "attn_bo": 1,
          "conv_b_in": None, "conv_dw": 2, "conv_dw_b": None, "conv_ln_g": None, "conv_ln_b": None, "conv_b_out": None,
          "sgu_b_in": 1, "sgu_ln_g": 1, "sgu_ln_b": 1, "sgu_ws": None, "sgu_bs": None, "sgu_b_out": 1,
          "ffn_dw": 2, "ffn_dw_b": None, "final_g": None}
_WEIGHTS = ['norm1_g', 'norm2_g', 'ada_w', 'ada_b', 'attn_wqkv', 'attn_bqkv', 'attn_sinks', 'attn_wo', 'attn_bo',
            'conv_w_in', 'conv_b_in', 'conv_dw', 'conv_dw_b', 'conv_ln_g', 'conv_ln_b', 'conv_w_out', 'conv_b_out',
            'sgu_w_in', 'sgu_b_in', 'sgu_ln_g', 'sgu_ln_b', 'sgu_ws', 'sgu_bs', 'sgu_w_out', 'sgu_b_out',
            'ffn_w_in', 'ffn_dw', 'ffn_dw_b', 'ffn_w_out', 'final_g']
_FLAT_COLS = 1024


def _full_shape(a, axis):
    s = list(a.shape)
    if axis is not None:
        s[axis] *= _N_CHIPS
    return tuple(s)


def _pack(arrays):
    flat = jnp.concatenate([a.reshape(-1).astype(F32) for a in arrays])
    pad = (-flat.shape[0]) % (8 * _FLAT_COLS)
    return jnp.pad(flat, (0, pad)).reshape(-1, _FLAT_COLS)


def _unpack(flat2d, shapes):
    flat = flat2d.reshape(-1)
    out, off = [], 0
    for s in shapes:
        n = math.prod(s)
        out.append(flat[off:off + n].reshape(s))
        off += n
    return out


def _shard_of(full, axis, chip):
    if axis is None:
        return full
    size = full.shape[axis] // _N_CHIPS
    return lax.dynamic_slice_in_dim(full, chip * size, size, axis)


def _unshard(gathered, axis):
    moved = jnp.moveaxis(gathered, 0, axis)
    shape = list(gathered.shape[1:])
    shape[axis] *= _N_CHIPS
    return moved.reshape(shape)


def kernel(x, c, norm1_g, norm2_g, ada_w, ada_b, attn_wqkv, attn_bqkv, attn_sinks, attn_wo, attn_bo, conv_w_in, conv_b_in, conv_dw, conv_dw_b, conv_ln_g, conv_ln_b, conv_w_out, conv_b_out, sgu_w_in, sgu_b_in, sgu_ln_g, sgu_ln_b, sgu_ws, sgu_bs, sgu_w_out, sgu_b_out, ffn_w_in, ffn_dw, ffn_dw_b, ffn_w_out, final_g, loss_target, m_norm1_g, m_norm2_g, m_ada_w, m_ada_b, m_attn_wqkv, m_attn_bqkv, m_attn_sinks, m_attn_wo, m_attn_bo, m_conv_w_in, m_conv_b_in, m_conv_dw, m_conv_dw_b, m_conv_ln_g, m_conv_ln_b, m_conv_w_out, m_conv_b_out, m_sgu_w_in, m_sgu_b_in, m_sgu_ln_g, m_sgu_ln_b, m_sgu_ws, m_sgu_bs, m_sgu_w_out, m_sgu_b_out, m_ffn_w_in, m_ffn_dw, m_ffn_dw_b, m_ffn_w_out, m_final_g, v_norm1_g, v_norm2_g, v_ada_w, v_ada_b, v_attn_wqkv, v_attn_bqkv, v_attn_sinks, v_attn_wo, v_attn_bo, v_conv_w_in, v_conv_b_in, v_conv_dw, v_conv_dw_b, v_conv_ln_g, v_conv_ln_b, v_conv_w_out, v_conv_b_out, v_sgu_w_in, v_sgu_b_in, v_sgu_ln_g, v_sgu_ln_b, v_sgu_ws, v_sgu_bs, v_sgu_w_out, v_sgu_b_out, v_ffn_w_in, v_ffn_dw, v_ffn_dw_b, v_ffn_w_out, v_final_g):
    args = dict(locals())
    wts = {n: args[n] for n in _WEIGHTS}
    mom_m = {n: args["m_" + n] for n in _WEIGHTS}
    mom_v = {n: args["v_" + n] for n in _WEIGHTS}

    ix, iy, ic = lax.axis_index("x"), lax.axis_index("y"), lax.axis_index("c")
    chip = 2 * ix + iy
    xs = x[0]
    tgt = loss_target[0]
    t, d = xs.shape
    nh = d // _HEAD_DIM
    depth = ada_w.shape[0]
    ncols = ada_w.shape[2]
    n_mod = ncols * _N_CHIPS // d

    small_sharded = [n for n in _WEIGHTS if _SMALL.get(n) is not None]
    packed_small = _pack([wts[n] for n in small_sharded])
    mixers = {0: ("attn_wqkv", "attn_wo"), 1: ("conv_w_in", "conv_w_out"), 2: ("sgu_w_in", "sgu_w_out")}
    groups = [[(n, 0) for n in mixers[0]], [("ffn_w_in", 0), ("ffn_w_out", 0)]]
    for i in range(1, depth):
        groups.append([(n, i // 3) for n in mixers[i % 3]] + [("ffn_w_in", i), ("ffn_w_out", i)])
    c_all = _gather_all("gather_c", c)[:, 0, :]
    ada_b_cols = lax.dynamic_slice_in_dim(ada_b, chip * ncols, ncols, 1)[:, None, :]
    mod_cols = _ada_mod(c_all, ada_w, ada_b_cols, "ada_mod")
    mine = lax.dynamic_index_in_dim(mod_cols.reshape(depth, _N_CHIPS, 2, ncols), ic, 2, keepdims=False)
    got = _swap_chips("swap_mod", jnp.moveaxis(mine, 1, 0))

    gathers = []
    for gi, keys in enumerate(groups):
        shards = [(_cast_bf16_layer(wts[n], j, "cast_%s_%d" % (n, j)), _BIG_AXIS[n] - 1) for n, j in keys]
        gathers.append(_start_gather("start_gather_%d" % gi, shards + ([(packed_small, 0)] if gi == 0 else []), got))
    started = sum(h["token"] for h in gathers)
    mod = jnp.moveaxis(got, 0, 1).reshape(depth, n_mod, 1, d) + started
    wfull = {}

    def finish_gather(gi, after):
        lands = _push_wait("wait_gather_%d" % gi, gathers[gi], after)
        wfull.update(zip(groups[gi], lands))
        return lands

    small_rows = finish_gather(0, mod)[-1].reshape(_N_CHIPS, -1, _FLAT_COLS)
    pieces = [_unpack(small_rows[s], [wts[n].shape for n in small_sharded]) for s in range(_N_CHIPS)]
    full = {n: wts[n] for n in _WEIGHTS if n in _SMALL and _SMALL[n] is None}
    for k, n in enumerate(small_sharded):
        full[n] = _unshard(jnp.stack([pieces[s][k] for s in range(_N_CHIPS)]), _SMALL[n])

    tables = _attn_tables(nh)
    zeros_d = jnp.zeros((1, d), F32)
    zeros_f2 = jnp.zeros((1, ffn_w_in.shape[2] * _N_CHIPS), F32)
    row = lambda a: a.reshape(1, -1)

    saved = []
    xcur = xs
    for i in range(depth):
        sh1, sc1, g1, sh2, sc2, g2 = (mod[i, k] for k in range(n_mod))
        kind, j = i % 3, i // 3
        tag = "L%d_" % i
        if i > 0:
            finish_gather(i + 1, xcur)
        w_in, w_out = wfull[(mixers[kind][0], j)], wfull[(mixers[kind][1], j)]
        if kind == 0:
            b_in, b_out = row(full["attn_bqkv"][j]), row(full["attn_bo"][j])
        elif kind == 1:
            b_in, b_out = row(full["conv_b_in"][j]), row(full["conv_b_out"][j])
        else:
            b_in, b_out = row(full["sgu_b_in"][j]), row(full["sgu_b_out"][j])
        h1, z = _norm_mod_matmul(xcur, row(full["norm1_g"][i]), sc1, sh1, w_in, 0, b_in, tag + "mixer_in")
        if i == 0:
            finish_gather(1, z)
        if kind == 0:
            a = _attn_fwd(z, tables, full["attn_sinks"][j], nh, tag + "attn")
        elif kind == 1:
            a = _conv_mid_fwd(z, full["conv_dw"][j], row(full["conv_dw_b"][j]), row(full["conv_ln_g"][j]), row(full["conv_ln_b"][j]),
                              tag + "conv_mid")
        else:
            a = _sgu_mid_fwd(z, row(full["sgu_ln_g"][j]), row(full["sgu_ln_b"][j]), full["sgu_ws"][j], full["sgu_bs"][j].T,
                             tag + "sgu_mid")
        y1, xmid = _matmul_resid(a, w_out, 0, b_out, xcur, g1, tag + "mixer_out")
        h2, zf = _norm_mod_matmul(xmid, row(full["norm2_g"][i]), sc2, sh2, wfull[("ffn_w_in", i)], 0, zeros_f2, tag + "ffn_in")
        act = _ffn_mid_fwd(zf, full["ffn_dw"][i], row(full["ffn_dw_b"][i]), tag + "ffn_mid")
        y2, xnext = _matmul_resid(act, wfull[("ffn_w_out", i)], 0, zeros_d, xmid, g2, tag + "ffn_out")
        saved.append(dict(x=xcur, h1=h1, z=z, a=a, y1=y1, xmid=xmid, h2=h2, zf=zf, act=act, y2=y2, w_in=w_in, w_out=w_out))
        xcur = xnext

    loss_row, dx, d_final_g = _final_loss(xcur, row(final_g), tgt, "final_loss")

    small_g = {n: jnp.zeros(_full_shape(wts[n], _SMALL[n]), F32) for n in _SMALL}
    small_g["final_g"] = d_final_g[0]
    dmod = [None] * depth
    scatters = []
    pushed = 0.0

    def put(name, j, val):
        small_g[name] = small_g[name].at[j].set(val.reshape(small_g[name].shape[1:]))

    for i in reversed(range(depth)):
        sv = saved[i]
        sh1, sc1, g1, sh2, sc2, g2 = (mod[i, k] for k in range(n_mod))
        kind, j = i % 3, i // 3
        tag = "L%d_" % i
        dy2, dact, dg2, _ = _outproj_bwd(dx, sv["y2"], g2 + pushed, wfull[("ffn_w_out", i)], 0, tag + "ffn_out_bwd")
        g_ffn_out = _matmul_tn(sv["act"], dy2, tag + "ffn_out_wgrad")
        dzf, d_fdw, d_fdwb = _ffn_mid_bwd(sv["zf"], dact, full["ffn_dw"][i], row(full["ffn_dw_b"][i]), tag + "ffn_mid_bwd")
        put("ffn_dw", i, d_fdw)
        put("ffn_dw_b", i, d_fdwb)
        g_ffn_in = _matmul_tn(sv["h2"], dzf, tag + "ffn_in_wgrad")
        dxmid, dn2, dsc2, dsh2 = _inproj_bwd(dzf, wfull[("ffn_w_in", i)], 0, sv["xmid"], dx, row(full["norm2_g"][i]), sc2,
                                             tag + "ffn_in_bwd")
        put("norm2_g", i, dn2)
        handle = _start_scatter("start_scatter_ffn_%d" % i, [(g_ffn_in, 1), (g_ffn_out, 0)], dxmid)
        scatters.append(([("ffn_w_in", i), ("ffn_w_out", i)], handle))
        dy1, da, dg1, dbo = _outproj_bwd(dxmid, sv["y1"], g1 + handle["token"], sv["w_out"], 0, tag + "mixer_out_bwd")
        w_in_name, w_out_name = mixers[kind]
        g_mix_out = _matmul_tn(sv["a"], dy1, tag + "mixer_out_wgrad")
        if kind == 0:
            dz, dsink, dbin = _attn_bwd(sv["z"], da, tables, full["attn_sinks"][j], nh, tag + "attn_bwd")
            put("attn_sinks", j, dsink[0, :nh])
            put("attn_bqkv", j, dbin)
            put("attn_bo", j, dbo)
        elif kind == 1:
            dz, d_dw, d_dwb, d_lng, d_lnb, dbin = _conv_mid_bwd(
                sv["z"], da, full["conv_dw"][j], row(full["conv_dw_b"][j]), row(full["conv_ln_g"][j]), row(full["conv_ln_b"][j]),
                tag + "conv_mid_bwd")
            for nme, val in (("conv_dw", d_dw), ("conv_dw_b", d_dwb), ("conv_ln_g", d_lng), ("conv_ln_b", d_lnb),
                             ("conv_b_in", dbin), ("conv_b_out", dbo)):
                put(nme, j, val)
        else:
            dz, d_lng, d_lnb, d_ws, d_bst, dbin = _sgu_mid_bwd(
                sv["z"], da, row(full["sgu_ln_g"][j]), row(full["sgu_ln_b"][j]), full["sgu_ws"][j], full["sgu_bs"][j].T,
                tag + "sgu_mid_bwd")
            ng = sgu_ws.shape[1]
            for nme, val in (("sgu_ln_g", d_lng), ("sgu_ln_b", d_lnb), ("sgu_ws", d_ws), ("sgu_bs", d_bst[:, :ng].T),
                             ("sgu_b_in", dbin), ("sgu_b_out", dbo)):
                put(nme, j, val)
        g_mix_in = _matmul_tn(sv["h1"], dz, tag + "mixer_in_wgrad")
        dx, dn1, dsc1, dsh1 = _inproj_bwd(dz, sv["w_in"], 0, sv["x"], dxmid, row(full["norm1_g"][i]), sc1, tag + "mixer_in_bwd")
        put("norm1_g", i, dn1)
        dmod[i] = jnp.concatenate([dsh1, dsc1, dg1, dsh2, dsc2, dg2], axis=1)
        handle = _start_scatter("start_scatter_mix_%d" % i, [(g_mix_in, 1), (g_mix_out, 0)], dx)
        scatters.append(([(w_in_name, j), (w_out_name, j)], handle))
        pushed = handle["token"]

    grad_x = dx[None]
    loss = lax.psum(loss_row[0, 0], ("x", "y", "c"))

    small_names = [n for n in _WEIGHTS if n in _SMALL and n != "ada_b"]
    dmod_own = jnp.concatenate(dmod, axis=0)
    packed = _pack([small_g[n] for n in small_names] + [dmod_own]) + pushed
    packed_all = _gather_all("gather_small_grads", packed)
    summed = _sum_slots(packed_all, "sum_small_grads")
    small_full = dict(zip(small_names, _unpack(summed, [small_g[n].shape for n in small_names])))
    n_small = sum(math.prod(small_g[n].shape) for n in small_names)
    dmod_all = packed_all.reshape(_N_DEV, -1)[:, n_small:n_small + dmod_own.size].reshape(_N_DEV, depth, n_mod * d)
    small_full["ada_b"] = _sum_slots(dmod_all, "sum_ada_b_grad")
    dmod_cols = lax.dynamic_slice_in_dim(jnp.moveaxis(dmod_all, 0, 1), chip * ncols, ncols, 2)
    g_ada_w = _ada_wgrad(c_all.T, dmod_cols, "ada_wgrad")

    slots = {}
    for k, (keys, handle) in enumerate(scatters):
        slots.update(zip(keys, _push_wait("wait_scatter_%d" % k, handle, g_ada_w)))
    partial = [_sum_slot_layers([slots[(n, j)] for j in range(wts[n].shape[0])], "sum_chips_" + n) for n in _BIG]
    other = _swap_sibling("swap_cores", partial)

    outs = {}
    for n, mine_p, theirs_p in zip(_BIG, partial, other):
        outs[n] = _adamw(wts[n], [mine_p, theirs_p], mom_m[n], mom_v[n], "adamw_" + n)
    outs["ada_w"] = _adamw(ada_w, [g_ada_w], m_ada_w, v_ada_w, "adamw_ada_w")
    sm_names = [n for n in _WEIGHTS if n in _SMALL]
    g_loc = [_shard_of(small_full[n], _SMALL[n], chip) for n in sm_names]
    packs = [_pack([src[n] for n in sm_names]) for src in (wts, mom_m, mom_v)]
    sm_out = _adamw(packs[0], [_pack(g_loc)], packs[1], packs[2], "adamw_small")
    shapes = [wts[n].shape for n in sm_names]
    unpacked = [_unpack(o, shapes) for o in sm_out]
    for k, n in enumerate(sm_names):
        outs[n] = tuple(u[k] for u in unpacked)

    result = [loss, grad_x]
    for which in range(4):
        result += [outs[n][which] for n in _WEIGHTS]
    return tuple(result)
```

```python
import math

import jax
import jax.numpy as jnp
from jax import lax
from jax.experimental import pallas as pl
from jax.experimental.pallas import tpu as pltpu

F32, BF16 = jnp.float32, jnp.bfloat16
SDS = jax.ShapeDtypeStruct
MESH_IDS = pl.DeviceIdType.MESH

_VMEM_LIMIT_BYTES = 48 * 1024 * 1024
_LANES = 128
_NORM_EPS = 1e-6
_NEG_INF = -1e30
_HEAD_DIM = 64
_N_KV = 4
_ATTN_BLOCK = 128
_ATTN_BLOCKS_PER_STEP = 2
_ROW_TILE = 512
_COL_TILE = 512
_WGRAD_TILE = 1536
_CONV_HALO = 32
_FFN_HALO = 16
_ADAM_LR, _ADAM_B1, _ADAM_B2, _ADAM_EPS, _ADAM_WD, _ADAM_STEP = 0.001, 0.9, 0.999, 1e-08, 0.01, 10
_N_CHIPS = 4
_N_DEV = 8


def _tile(n, pref, unit):
    if n <= pref:
        return n
    t = pref - pref % unit
    while t >= unit:
        if n % t == 0:
            return t
        t -= unit
    return n


def _params(n_axes):
    return pltpu.CompilerParams(dimension_semantics=("arbitrary",) * n_axes, vmem_limit_bytes=_VMEM_LIMIT_BYTES)


def _row(n):
    return pl.BlockSpec((1, n), lambda *_: (0, 0))


def _sigmoid(v):
    return 1.0 / (1.0 + jnp.exp(-v))


def _nt(a, b):
    return lax.dot_general(a, b, (((1,), (1,)), ((), ())), preferred_element_type=F32)


def _tn(a, b):
    return lax.dot_general(a, b, (((0,), (0,)), ((), ())), preferred_element_type=F32)


def _resident(shape, index_map):
    return pl.BlockSpec(shape, index_map, pipeline_mode=pl.Buffered(1))


def _norm_mod_matmul(x, gn, sc, sh, w, layer, b, name):
    t, d = x.shape
    n = w.shape[2]
    tm, tn = _tile(t, _ROW_TILE, 16), _tile(n, _COL_TILE, _LANES)

    def body(x_ref, gn_ref, sc_ref, sh_ref, w_ref, b_ref, h_ref, z_ref):
        xf = x_ref[...]
        r = lax.rsqrt(jnp.mean(xf * xf, axis=-1, keepdims=True) + _NORM_EPS)
        h_ref[...] = ((xf * r * gn_ref[...]) * (1.0 + sc_ref[...]) + sh_ref[...]).astype(BF16)
        for c0 in range(0, n, tn):
            z = jnp.dot(h_ref[...], w_ref[:, c0:c0 + tn], preferred_element_type=F32) + b_ref[:, c0:c0 + tn]
            z_ref[:, c0:c0 + tn] = z.astype(BF16)

    return pl.pallas_call(
        body, name=name, grid=(t // tm,),
        in_specs=[pl.BlockSpec((tm, d), lambda i: (i, 0)), _row(d), _row(d), _row(d),
                  _resident((None, d, n), lambda i: (layer, 0, 0)), _row(n)],
        out_specs=[pl.BlockSpec((tm, d), lambda i: (i, 0)), pl.BlockSpec((tm, n), lambda i: (i, 0))],
        out_shape=[SDS((t, d), BF16), SDS((t, n), BF16)], compiler_params=_params(1),
    )(x, gn, sc, sh, w, b)


def _matmul_resid(a, w, layer, b, x, gate, name):
    t, k = a.shape
    d = w.shape[2]
    tm, tn = _tile(t, _ROW_TILE, 16), _tile(d, _COL_TILE, _LANES)

    def body(a_ref, w_ref, b_ref, x_ref, g_ref, y_ref, xo_ref):
        for c0 in range(0, d, tn):
            cs = slice(c0, c0 + tn)
            y = jnp.dot(a_ref[...], w_ref[:, cs], preferred_element_type=F32) + b_ref[:, cs]
            y_ref[:, cs] = y.astype(BF16)
            xo_ref[:, cs] = x_ref[:, cs] + g_ref[:, cs] * y

    blk = pl.BlockSpec((tm, d), lambda i: (i, 0))
    return pl.pallas_call(
        body, name=name, grid=(t // tm,),
        in_specs=[pl.BlockSpec((tm, k), lambda i: (i, 0)), _resident((None, k, d), lambda i: (layer, 0, 0)), _row(d), blk, _row(d)],
        out_specs=[blk, blk], out_shape=[SDS((t, d), BF16), SDS((t, d), F32)], compiler_params=_params(1),
    )(a, w, b, x, gate)


def _outproj_bwd(dxo, y, gate, w, layer, name):
    t, d = dxo.shape
    k = w.shape[1]
    tm, tk = _tile(t, _ROW_TILE, 16), _tile(k, _COL_TILE, _LANES)

    def body(dxo_ref, y_ref, g_ref, w_ref, dy_ref, da_ref, dg_ref, db_ref):
        @pl.when(pl.program_id(0) == 0)
        def _():
            dg_ref[...] = jnp.zeros_like(dg_ref)
            db_ref[...] = jnp.zeros_like(db_ref)

        dxf = dxo_ref[...]
        dyf = dxf * g_ref[...]
        dy_ref[...] = dyf.astype(BF16)
        dg_ref[...] += jnp.sum(dxf * y_ref[...].astype(F32), axis=0, keepdims=True)
        db_ref[...] += jnp.sum(dyf, axis=0, keepdims=True)
        for c0 in range(0, k, tk):
            da_ref[:, c0:c0 + tk] = _nt(dy_ref[...], w_ref[c0:c0 + tk, :]).astype(BF16)

    full = pl.BlockSpec((tm, d), lambda i: (i, 0))
    return pl.pallas_call(
        body, name=name, grid=(t // tm,),
        in_specs=[full, full, _row(d), _resident((None, k, d), lambda i: (layer, 0, 0))],
        out_specs=[full, pl.BlockSpec((tm, k), lambda i: (i, 0)), _row(d), _row(d)],
        out_shape=[SDS((t, d), BF16), SDS((t, k), BF16), SDS((1, d), F32), SDS((1, d), F32)],
        compiler_params=_params(1),
    )(dxo, y, gate, w)


def _matmul_tn(a, b, name):
    t, ka = a.shape
    nb = b.shape[1]
    tka, tnb, tt = _tile(ka, _WGRAD_TILE, _LANES), _tile(nb, _WGRAD_TILE, _LANES), _tile(t, 2 * _ROW_TILE, 16)
    nt = t // tt

    def body(a_ref, b_ref, o_ref, acc):
        s = pl.program_id(2)

        @pl.when(s == 0)
        def _():
            acc[...] = jnp.zeros_like(acc)

        acc[...] += _tn(a_ref[...], b_ref[...])

        @pl.when(s == nt - 1)
        def _():
            o_ref[...] = acc[...].astype(BF16)

    return pl.pallas_call(
        body, name=name, grid=(ka // tka, nb // tnb, nt),
        in_specs=[pl.BlockSpec((tt, tka), lambda i, j, s: (s, i)), pl.BlockSpec((tt, tnb), lambda i, j, s: (s, j))],
        out_specs=pl.BlockSpec((tka, tnb), lambda i, j, s: (i, j)),
        out_shape=SDS((ka, nb), BF16), scratch_shapes=[pltpu.VMEM((tka, tnb), F32)], compiler_params=_params(3),
    )(a, b)


def _inproj_bwd(dz, w, layer, x, dxo, gn, sc, name):
    t, n = dz.shape
    d = x.shape[1]
    tm, tk = _tile(t, _ROW_TILE, 16), _tile(n, _COL_TILE, _LANES)

    def body(dz_ref, w_ref, x_ref, dxo_ref, gn_ref, sc_ref, dx_ref, dgn_ref, dsc_ref, dsh_ref, acc):
        @pl.when(pl.program_id(0) == 0)
        def _():
            dgn_ref[...] = jnp.zeros_like(dgn_ref)
            dsc_ref[...] = jnp.zeros_like(dsc_ref)
            dsh_ref[...] = jnp.zeros_like(dsh_ref)

        for c0 in range(0, n, tk):
            part = _nt(dz_ref[:, c0:c0 + tk], w_ref[:, c0:c0 + tk])
            if c0 == 0:
                acc[...] = part
            else:
                acc[...] += part
        dh = acc[...]
        xf = x_ref[...]
        r = lax.rsqrt(jnp.mean(xf * xf, axis=-1, keepdims=True) + _NORM_EPS)
        xn = xf * r
        gnv = gn_ref[...]
        dsh_ref[...] += jnp.sum(dh, axis=0, keepdims=True)
        dsc_ref[...] += jnp.sum(dh * (xn * gnv), axis=0, keepdims=True)
        drn = dh * (1.0 + sc_ref[...])
        dgn_ref[...] += jnp.sum(drn * xn, axis=0, keepdims=True)
        dxn = drn * gnv
        dx_ref[...] = dxo_ref[...] + r * (dxn - xn * jnp.mean(dxn * xn, axis=-1, keepdims=True))

    full = pl.BlockSpec((tm, d), lambda i: (i, 0))
    return pl.pallas_call(
        body, name=name, grid=(t // tm,),
        in_specs=[pl.BlockSpec((tm, n), lambda i: (i, 0)), _resident((None, d, n), lambda i: (layer, 0, 0)),
                  full, full, _row(d), _row(d)],
        out_specs=[full, _row(d), _row(d), _row(d)],
        out_shape=[SDS((t, d), F32), SDS((1, d), F32), SDS((1, d), F32), SDS((1, d), F32)],
        scratch_shapes=[pltpu.VMEM((tm, d), F32)], compiler_params=_params(1),
    )(dz, w, x, dxo, gn, sc)


def _final_loss(x, g, target, name):
    t, d = x.shape
    tm = _tile(t, _ROW_TILE, 8)

    def body(x_ref, g_ref, t_ref, loss_ref, dx_ref, dg_ref):
        @pl.when(pl.program_id(0) == 0)
        def _():
            loss_ref[...] = jnp.zeros_like(loss_ref)
            dg_ref[...] = jnp.zeros_like(dg_ref)

        xf = x_ref[...]
        r = lax.rsqrt(jnp.mean(xf * xf, axis=-1, keepdims=True) + _NORM_EPS)
        xn = xf * r
        gv = g_ref[...]
        e = xn * gv - t_ref[...]
        per_row = jnp.mean(e * e, axis=-1, keepdims=True)
        loss_ref[...] += 0.5 * jnp.sum(per_row, axis=0, keepdims=True)
        dy = e * (1.0 / d)
        dg_ref[...] += jnp.sum(dy * xn, axis=0, keepdims=True)
        dxn = dy * gv
        dx_ref[...] = r * (dxn - xn * jnp.mean(dxn * xn, axis=-1, keepdims=True))

    full = pl.BlockSpec((tm, d), lambda i: (i, 0))
    return pl.pallas_call(
        body, name=name, grid=(t // tm,), in_specs=[full, _row(d), full],
        out_specs=[_row(_LANES), full, _row(d)],
        out_shape=[SDS((1, _LANES), F32), SDS((t, d), F32), SDS((1, d), F32)], compiler_params=_params(1),
    )(x, g, target)


def _attn_tables(nh):
    group = nh // _N_KV
    slopes = 2.0 ** (-8.0 * jnp.arange(1, nh + 1, dtype=F32) / nh)
    qpos = jnp.arange(_ATTN_BLOCK) + _ATTN_BLOCK
    kpos = jnp.arange(2 * _ATTN_BLOCK)
    dist = qpos[:, None] - kpos[None, :]
    band = (dist >= 0) & (dist < _ATTN_BLOCK)
    bias = jnp.where(band[None], -slopes[:, None, None] * dist.astype(F32)[None], _NEG_INF)
    first = jnp.where((kpos < _ATTN_BLOCK)[None, None, :], _NEG_INF, bias)
    return jnp.stack([first, bias]).reshape(2, _N_KV, group * _ATTN_BLOCK, 2 * _ATTN_BLOCK)


def _attn_probs(q4, k2, tab, sink_ref, kv, group):
    hd, blk = _HEAD_DIM, _ATTN_BLOCK
    s = _nt(q4, k2) * (hd ** -0.5) + tab
    sink = jnp.concatenate([jnp.full((blk, 1), sink_ref[kv * group + g], F32) for g in range(group)], axis=0)
    m = jnp.maximum(jnp.max(s, axis=-1, keepdims=True), sink)
    e = jnp.exp(s - m)
    es = jnp.exp(sink - m)
    inv = 1.0 / (jnp.sum(e, axis=-1, keepdims=True) + es)
    return e * inv, es * inv


def _attn_operands(refs, b, kv, group):
    q_ref, kp_ref, kc_ref, vp_ref, vc_ref = refs
    hd, blk = _HEAD_DIM, _ATTN_BLOCK
    rows, cs = slice(b * blk, (b + 1) * blk), slice(kv * hd, (kv + 1) * hd)
    before = slice((b - 1) * blk, b * blk)
    k2 = jnp.concatenate([kp_ref[:, cs] if b == 0 else kc_ref[before, cs], kc_ref[rows, cs]], axis=0)
    v2 = jnp.concatenate([vp_ref[:, cs] if b == 0 else vc_ref[before, cs], vc_ref[rows, cs]], axis=0)
    q4 = jnp.concatenate([q_ref[rows, (kv * group + g) * hd:(kv * group + g + 1) * hd] for g in range(group)], axis=0)
    return q4, k2, v2


def _attn_specs(nh, nblk, step):
    group = nh // _N_KV
    blk, kvw = _ATTN_BLOCK, _N_KV * _HEAD_DIM
    prev = lambda i: jnp.maximum(step(i) * nblk - 1, 0)
    return [
        pl.BlockSpec((nblk * blk, nh * _HEAD_DIM), lambda i: (step(i), 0)),
        pl.BlockSpec((blk, kvw), lambda i: (prev(i), group)),
        pl.BlockSpec((nblk * blk, kvw), lambda i: (step(i), group)),
        pl.BlockSpec((blk, kvw), lambda i: (prev(i), group + 1)),
        pl.BlockSpec((nblk * blk, kvw), lambda i: (step(i), group + 1)),
        pl.BlockSpec((2, _N_KV, group * blk, 2 * blk), lambda i: (0, 0, 0, 0)),
        pl.BlockSpec(memory_space=pltpu.SMEM),
    ]


def _attn_fwd(qkv, tables, sinks, nh, name):
    t = qkv.shape[0]
    group, hd, blk = nh // _N_KV, _HEAD_DIM, _ATTN_BLOCK
    nblk = _ATTN_BLOCKS_PER_STEP if t % (_ATTN_BLOCKS_PER_STEP * blk) == 0 else 1

    def body(q_ref, kp_ref, kc_ref, vp_ref, vc_ref, tab_ref, sink_ref, o_ref):
        i = pl.program_id(0)
        for b in range(nblk):
            tab = tab_ref.at[jnp.minimum(i * nblk + b, 1)]
            outs = [None] * nh
            for kv in range(_N_KV):
                q4, k2, v2 = _attn_operands((q_ref, kp_ref, kc_ref, vp_ref, vc_ref), b, kv, group)
                p, _ = _attn_probs(q4, k2, tab[kv], sink_ref, kv, group)
                o4 = jnp.dot(p.astype(BF16), v2, preferred_element_type=F32)
                for g in range(group):
                    outs[kv * group + g] = o4[g * blk:(g + 1) * blk, :]
            o_ref[b * blk:(b + 1) * blk, :] = jnp.concatenate(outs, axis=1).astype(BF16)

    return pl.pallas_call(
        body, name=name, grid=(t // (nblk * blk),), in_specs=_attn_specs(nh, nblk, lambda i: i),
        out_specs=pl.BlockSpec((nblk * blk, nh * hd), lambda i: (i, 0)), out_shape=SDS((t, nh * hd), BF16),
        compiler_params=_params(1),
    )(qkv, qkv, qkv, qkv, qkv, tables, sinks)


def _attn_bwd(qkv, do, tables, sinks, nh, name):
    t, wq = qkv.shape
    group, hd, blk = nh // _N_KV, _HEAD_DIM, _ATTN_BLOCK
    nblk = _ATTN_BLOCKS_PER_STEP if t % (_ATTN_BLOCKS_PER_STEP * blk) == 0 else 1
    nsteps = t // (nblk * blk)
    kvw = _N_KV * hd
    step = lambda i: nsteps - 1 - i

    def body(q_ref, kp_ref, kc_ref, vp_ref, vc_ref, tab_ref, sink_ref, do_ref, dqkv_ref, dsink_ref, db_ref, ck, cv):
        i = pl.program_id(0)

        @pl.when(i == 0)
        def _():
            ck[...] = jnp.zeros_like(ck)
            cv[...] = jnp.zeros_like(cv)
            dsink_ref[...] = jnp.zeros_like(dsink_ref)
            db_ref[...] = jnp.zeros_like(db_ref)

        lane = lax.broadcasted_iota(jnp.int32, (1, _LANES), 1)
        carry_k = [ck[:, kv * hd:(kv + 1) * hd] for kv in range(_N_KV)]
        carry_v = [cv[:, kv * hd:(kv + 1) * hd] for kv in range(_N_KV)]
        dsink = jnp.zeros((1, _LANES), F32)
        dbias = jnp.zeros((1, wq), F32)
        for b in reversed(range(nblk)):
            rows = slice(b * blk, (b + 1) * blk)
            tab = tab_ref.at[jnp.minimum(step(i) * nblk + b, 1)]
            dq, dk, dv = [None] * nh, [None] * _N_KV, [None] * _N_KV
            for kv in range(_N_KV):
                q4, k2, v2 = _attn_operands((q_ref, kp_ref, kc_ref, vp_ref, vc_ref), b, kv, group)
                p, ps = _attn_probs(q4, k2, tab[kv], sink_ref, kv, group)
                do4 = jnp.concatenate([do_ref[rows, (kv * group + g) * hd:(kv * group + g + 1) * hd] for g in range(group)], axis=0)
                dp = _nt(do4, v2)
                dl = jnp.sum(p * dp, axis=-1, keepdims=True)
                ds = (p * (dp - dl)).astype(BF16)
                dsk = -ps * dl
                for g in range(group):
                    dsink = dsink + jnp.where(lane == kv * group + g, jnp.sum(dsk[g * blk:(g + 1) * blk, :]), 0.0)
                dq4 = jnp.dot(ds, k2, preferred_element_type=F32) * (hd ** -0.5)
                for g in range(group):
                    dq[kv * group + g] = dq4[g * blk:(g + 1) * blk, :]
                dk2 = _tn(q4, ds).T * (hd ** -0.5)
                dv2 = _tn(do4, p.astype(BF16)).T
                dk[kv] = dk2[blk:, :] + carry_k[kv]
                dv[kv] = dv2[blk:, :] + carry_v[kv]
                carry_k[kv], carry_v[kv] = dk2[:blk, :], dv2[:blk, :]
            dqkv = jnp.concatenate(dq + dk + dv, axis=1)
            dqkv_ref[rows, :] = dqkv.astype(BF16)
            dbias = dbias + jnp.sum(dqkv, axis=0, keepdims=True)
        for kv in range(_N_KV):
            ck[:, kv * hd:(kv + 1) * hd] = carry_k[kv]
            cv[:, kv * hd:(kv + 1) * hd] = carry_v[kv]
        db_ref[...] += dbias
        dsink_ref[...] += dsink

    return pl.pallas_call(
        body, name=name, grid=(nsteps,),
        in_specs=_attn_specs(nh, nblk, step) + [pl.BlockSpec((nblk * blk, nh * hd), lambda i: (step(i), 0))],
        out_specs=[pl.BlockSpec((nblk * blk, wq), lambda i: (step(i), 0)), _row(_LANES), _row(wq)],
        out_shape=[SDS((t, wq), BF16), SDS((1, _LANES), F32), SDS((1, wq), F32)],
        scratch_shapes=[pltpu.VMEM((blk, kvw), F32), pltpu.VMEM((blk, kvw), F32)], compiler_params=_params(1),
    )(qkv, qkv, qkv, qkv, qkv, tables, sinks, do)


_SUBLANES = 8


_ALL_SHIFTS = tuple(range(1, _SUBLANES))


def _shift_copies(src_ref, sh_ref, shifts=_ALL_SHIFTS):
    rows = src_ref.shape[0]
    full = src_ref[...]
    for n, b in enumerate(shifts):
        sh_ref[n] = pltpu.roll(full, rows - b, axis=0)
    return sh_ref, shifts


def _rows_at(src_ref, shifted, off, r0, rg, cs):
    b = off % _SUBLANES
    if shifted is None or b not in shifted[1]:
        return src_ref[r0 + off:r0 + off + rg, cs]
    return shifted[0][shifted[1].index(b), r0 + off - b:r0 + off - b + rg, cs]


def _taps(src_ref, w_ref, dst_ref, n_rows, width, offs, rg, shifted=None):
    cg = _tile(width, 512, _LANES)
    for c0 in range(0, width, cg):
        cs = slice(c0, c0 + cg)
        wk = [w_ref[k:k + 1, cs] for k, _ in offs]
        for r0 in range(0, n_rows, rg):
            acc = None
            for (_, off), wv in zip(offs, wk):
                term = wv * _rows_at(src_ref, shifted, off, r0, rg, cs)
                acc = term if acc is None else acc + term
            dst_ref[r0:r0 + rg, cs] = acc


def _tap_grads(dy_ref, z_ref, out_ref, n_rows, width, offs, rg, shifted=None):
    cg = _tile(width, 512, _LANES)
    for c0 in range(0, width, cg):
        cs = slice(c0, c0 + cg)
        for k, off in offs:
            acc = None
            for r0 in range(0, n_rows, rg):
                term = dy_ref[r0:r0 + rg, cs] * _rows_at(z_ref, shifted, off, r0, rg, cs)
                acc = term if acc is None else acc + term
            out_ref[k:k + 1, cs] += jnp.sum(acc, axis=0, keepdims=True)


def _conv_mid_fwd(ag, dw, dwb, lng, lnb, name):
    t, c2 = ag.shape
    c = c2 // 2
    kw = dw.shape[0]
    hl = _CONV_HALO
    tm = _tile(t, 256, hl)
    per = tm // hl

    def body(agp_ref, ag_ref, dw_ref, dwb_ref, lng_ref, lnb_ref, o_ref, zext, yb, zsh):
        i = pl.program_id(0)
        glu = lambda ref: ref[:, :c].astype(F32) * _sigmoid(ref[:, c:].astype(F32))
        zext[0:hl, :] = jnp.where(i > 0, glu(agp_ref), 0.0)
        zext[hl:, :] = glu(ag_ref)
        _taps(zext, dw_ref, yb, tm, c, [(k, hl - (kw - 1) + k) for k in range(kw)], 32, _shift_copies(zext, zsh))
        y = yb[...] + dwb_ref[...]
        mu = jnp.mean(y, axis=-1, keepdims=True)
        yc = y - mu
        rstd = lax.rsqrt(jnp.mean(yc * yc, axis=-1, keepdims=True) + _NORM_EPS)
        ln = yc * rstd * lng_ref[...] + lnb_ref[...]
        o_ref[...] = (ln * _sigmoid(ln)).astype(BF16)

    return pl.pallas_call(
        body, name=name, grid=(t // tm,),
        in_specs=[pl.BlockSpec((hl, c2), lambda i: (jnp.maximum(i * per - 1, 0), 0)), pl.BlockSpec((tm, c2), lambda i: (i, 0)),
                  pl.BlockSpec((kw, c), lambda i: (0, 0)), _row(c), _row(c), _row(c)],
        out_specs=pl.BlockSpec((tm, c), lambda i: (i, 0)), out_shape=SDS((t, c), BF16),
        scratch_shapes=[pltpu.VMEM((hl + tm, c), F32), pltpu.VMEM((tm, c), F32), pltpu.VMEM((_SUBLANES - 1, hl + tm, c), F32)],
        compiler_params=_params(1),
    )(ag, ag, dw, dwb, lng, lnb)


def _conv_mid_bwd(ag, dzc, dw, dwb, lng, lnb, name):
    t, c2 = ag.shape
    c = c2 // 2
    kw = dw.shape[0]
    hl = _CONV_HALO
    tm = _tile(t, 256, hl)
    per = tm // hl
    nt = t // tm
    last_halo = t // hl - 1

    def body(agp_ref, ag_ref, agn_ref, dzc_ref, dzcn_ref, dw_ref, dwb_ref, lng_ref, lnb_ref,
             dag_ref, ddw_ref, ddwb_ref, dlng_ref, dlnb_ref, dbin_ref, zext, yext, dyext, dzb, zsh, dysh):
        i = pl.program_id(0)

        @pl.when(i == 0)
        def _():
            for r in (ddw_ref, ddwb_ref, dlng_ref, dlnb_ref, dbin_ref):
                r[...] = jnp.zeros_like(r)

        glu = lambda ref: ref[:, :c].astype(F32) * _sigmoid(ref[:, c:].astype(F32))
        zext[0:hl, :] = jnp.where(i > 0, glu(agp_ref), 0.0)
        zext[hl:hl + tm, :] = glu(ag_ref)
        zext[hl + tm:, :] = glu(agn_ref)
        fwd_offs = [(k, hl - (kw - 1) + k) for k in range(kw)]
        z_shifted = _shift_copies(zext, zsh)
        _taps(zext, dw_ref, yext, tm + hl, c, fwd_offs, 32, z_shifted)
        y = yext[...] + dwb_ref[...]
        mu = jnp.mean(y, axis=-1, keepdims=True)
        yc = y - mu
        rstd = lax.rsqrt(jnp.mean(yc * yc, axis=-1, keepdims=True) + _NORM_EPS)
        xhat = yc * rstd
        lngv = lng_ref[...]
        ln = xhat * lngv + lnb_ref[...]
        sg = _sigmoid(ln)
        dz_out = jnp.concatenate([dzc_ref[...].astype(F32), jnp.where(i < nt - 1, dzcn_ref[...].astype(F32), 0.0)], axis=0)
        dln = dz_out * (sg * (1.0 + ln * (1.0 - sg)))
        dlng_ref[...] += jnp.sum((dln * xhat)[:tm], axis=0, keepdims=True)
        dlnb_ref[...] += jnp.sum(dln[:tm], axis=0, keepdims=True)
        dxh = dln * lngv
        dy = rstd * (dxh - jnp.mean(dxh, axis=-1, keepdims=True) - xhat * jnp.mean(dxh * xhat, axis=-1, keepdims=True))
        dyext[...] = dy
        ddwb_ref[...] += jnp.sum(dy[:tm], axis=0, keepdims=True)
        _taps(dyext, dw_ref, dzb, tm, c, [(k, kw - 1 - k) for k in range(kw)], 32, _shift_copies(dyext, dysh))
        _tap_grads(dyext, zext, ddw_ref, tm, c, fwd_offs, 32, z_shifted)
        a = ag_ref[:, :c].astype(F32)
        sgg = _sigmoid(ag_ref[:, c:].astype(F32))
        dz = dzb[...]
        da = dz * sgg
        dg = dz * a * sgg * (1.0 - sgg)
        dag_ref[:, :c] = da.astype(BF16)
        dag_ref[:, c:] = dg.astype(BF16)
        dbin_ref[:, :c] += jnp.sum(da, axis=0, keepdims=True)
        dbin_ref[:, c:] += jnp.sum(dg, axis=0, keepdims=True)

    prev = lambda i: (jnp.maximum(i * per - 1, 0), 0)
    nxt = lambda i: (jnp.minimum((i + 1) * per, last_halo), 0)
    return pl.pallas_call(
        body, name=name, grid=(nt,),
        in_specs=[pl.BlockSpec((hl, c2), prev), pl.BlockSpec((tm, c2), lambda i: (i, 0)), pl.BlockSpec((hl, c2), nxt),
                  pl.BlockSpec((tm, c), lambda i: (i, 0)), pl.BlockSpec((hl, c), nxt),
                  pl.BlockSpec((kw, c), lambda i: (0, 0)), _row(c), _row(c), _row(c)],
        out_specs=[pl.BlockSpec((tm, c2), lambda i: (i, 0)), pl.BlockSpec((kw, c), lambda i: (0, 0)), _row(c), _row(c), _row(c), _row(c2)],
        out_shape=[SDS((t, c2), BF16), SDS((kw, c), F32), SDS((1, c), F32), SDS((1, c), F32), SDS((1, c), F32), SDS((1, c2), F32)],
        scratch_shapes=[pltpu.VMEM((hl + tm + hl, c), F32), pltpu.VMEM((tm + hl, c), F32), pltpu.VMEM((tm + hl, c), F32),
                        pltpu.VMEM((tm, c), F32), pltpu.VMEM((_SUBLANES - 1, hl + tm + hl, c), F32),
                        pltpu.VMEM((_SUBLANES - 1, tm + hl, c), F32)],
        compiler_params=_params(1),
    )(ag, ag, ag, dzc, dzc, dw, dwb, lng, lnb)


def _ffn_mid_fwd(zf, dw, dwb, name):
    t, f2 = zf.shape
    f = f2 // 2
    kw = dw.shape[0]
    hl = _FFN_HALO
    tm = _tile(t, 256, hl)
    per = tm // hl
    tc = _tile(f, 256, _LANES)
    offs = [(k, hl - (kw - 1) + k) for k in range(kw)]
    shifts = tuple(sorted({off % _SUBLANES for _, off in offs} - {0}))

    def body(zp_ref, z_ref, dw_ref, dwb_ref, o_ref, zext, cb, zsh):
        i = pl.program_id(0)
        zext[0:hl, :] = jnp.where(i > 0, zp_ref[...].astype(F32), 0.0)
        zext[hl:, :] = z_ref[...].astype(F32)
        _taps(zext, dw_ref, cb, tm, f2, offs, 16, _shift_copies(zext, zsh, shifts))
        for c0 in range(0, f, tc):
            g = cb[:, c0:c0 + tc] + dwb_ref[:, c0:c0 + tc]
            u = cb[:, f + c0:f + c0 + tc] + dwb_ref[:, f + c0:f + c0 + tc]
            o_ref[:, c0:c0 + tc] = (g * _sigmoid(g) * u).astype(BF16)

    return pl.pallas_call(
        body, name=name, grid=(t // tm,),
        in_specs=[pl.BlockSpec((hl, f2), lambda i: (jnp.maximum(i * per - 1, 0), 0)), pl.BlockSpec((tm, f2), lambda i: (i, 0)),
                  pl.BlockSpec((kw, f2), lambda i: (0, 0)), _row(f2)],
        out_specs=pl.BlockSpec((tm, f), lambda i: (i, 0)), out_shape=SDS((t, f), BF16),
        scratch_shapes=[pltpu.VMEM((hl + tm, f2), F32), pltpu.VMEM((tm, f2), F32), pltpu.VMEM((len(shifts), hl + tm, f2), F32)],
        compiler_params=_params(1),
    )(zf, zf, dw, dwb)


def _ffn_mid_bwd(zf, dact, dw, dwb, name):
    t, f2 = zf.shape
    f = f2 // 2
    kw = dw.shape[0]
    hl = _FFN_HALO
    tm = _tile(t, 128, hl)
    per = tm // hl
    nt = t // tm
    last_halo = t // hl - 1
    tc = _tile(f, 256, _LANES)
    fwd_offs = [(k, hl - (kw - 1) + k) for k in range(kw)]
    bwd_offs = [(k, kw - 1 - k) for k in range(kw)]
    fwd_shifts = tuple(sorted({off % _SUBLANES for _, off in fwd_offs} - {0}))
    bwd_shifts = tuple(sorted({off % _SUBLANES for _, off in bwd_offs} - {0}))

    def body(zp_ref, z_ref, zn_ref, da_ref, dan_ref, dw_ref, dwb_ref, dzf_ref, ddw_ref, ddwb_ref, zext, cext, dcext, dzb,
             zsh, dcsh):
        i = pl.program_id(0)

        @pl.when(i == 0)
        def _():
            ddw_ref[...] = jnp.zeros_like(ddw_ref)
            ddwb_ref[...] = jnp.zeros_like(ddwb_ref)

        zext[0:hl, :] = jnp.where(i > 0, zp_ref[...].astype(F32), 0.0)
        zext[hl:hl + tm, :] = z_ref[...].astype(F32)
        zext[hl + tm:, :] = zn_ref[...].astype(F32)
        z_shifted = _shift_copies(zext, zsh, fwd_shifts)
        _taps(zext, dw_ref, cext, tm + hl, f2, fwd_offs, 16, z_shifted)
        for c0 in range(0, f, tc):
            g = cext[:, c0:c0 + tc] + dwb_ref[:, c0:c0 + tc]
            u = cext[:, f + c0:f + c0 + tc] + dwb_ref[:, f + c0:f + c0 + tc]
            da = jnp.concatenate([da_ref[:, c0:c0 + tc].astype(F32),
                                  jnp.where(i < nt - 1, dan_ref[:, c0:c0 + tc].astype(F32), 0.0)], axis=0)
            sg = _sigmoid(g)
            dcg = da * u * (sg * (1.0 + g * (1.0 - sg)))
            dcu = da * (g * sg)
            dcext[:, c0:c0 + tc] = dcg
            dcext[:, f + c0:f + c0 + tc] = dcu
            ddwb_ref[:, c0:c0 + tc] += jnp.sum(dcg[:tm], axis=0, keepdims=True)
            ddwb_ref[:, f + c0:f + c0 + tc] += jnp.sum(dcu[:tm], axis=0, keepdims=True)
        _taps(dcext, dw_ref, dzb, tm, f2, bwd_offs, 16, _shift_copies(dcext, dcsh, bwd_shifts))
        _tap_grads(dcext, zext, ddw_ref, tm, f2, fwd_offs, 16, z_shifted)
        dzf_ref[...] = dzb[...].astype(BF16)

    prev = lambda i: (jnp.maximum(i * per - 1, 0), 0)
    nxt = lambda i: (jnp.minimum((i + 1) * per, last_halo), 0)
    return pl.pallas_call(
        body, name=name, grid=(nt,),
        in_specs=[pl.BlockSpec((hl, f2), prev), pl.BlockSpec((tm, f2), lambda i: (i, 0)), pl.BlockSpec((hl, f2), nxt),
                  pl.BlockSpec((tm, f), lambda i: (i, 0)), pl.BlockSpec((hl, f), nxt),
                  pl.BlockSpec((kw, f2), lambda i: (0, 0)), _row(f2)],
        out_specs=[pl.BlockSpec((tm, f2), lambda i: (i, 0)), pl.BlockSpec((kw, f2), lambda i: (0, 0)), _row(f2)],
        out_shape=[SDS((t, f2), BF16), SDS((kw, f2), F32), SDS((1, f2), F32)],
        scratch_shapes=[pltpu.VMEM((hl + tm + hl, f2), F32), pltpu.VMEM((tm + hl, f2), F32), pltpu.VMEM((tm + hl, f2), F32),
                        pltpu.VMEM((tm, f2), F32), pltpu.VMEM((len(fwd_shifts), hl + tm + hl, f2), F32),
                        pltpu.VMEM((len(bwd_shifts), tm + hl, f2), F32)],
        compiler_params=_params(1),
    )(zf, zf, zf, dact, dact, dw, dwb)


_INV_SQRT2 = 0.7071067811865476
_INV_SQRT_2PI = 0.3989422804014327


def _sgu_common(zin_ref, lng_ref, lnb_ref, hh):
    z = zin_ref[...].astype(F32)
    cdf = 0.5 * (1.0 + lax.erf(z * _INV_SQRT2))
    ge = z * cdf
    u, v = ge[:, :hh], ge[:, hh:]
    mu = jnp.mean(v, axis=-1, keepdims=True)
    vc = v - mu
    rstd = lax.rsqrt(jnp.mean(vc * vc, axis=-1, keepdims=True) + _NORM_EPS)
    vhat = vc * rstd
    vn = vhat * lng_ref[...] + lnb_ref[...]
    return z, cdf, u, vhat, rstd, vn


def _sgu_wm(ws_ref, g, ch):
    rows = lax.broadcasted_iota(jnp.int32, (ch, ch), 0)
    cols = lax.broadcasted_iota(jnp.int32, (ch, ch), 1)
    return jnp.where(rows >= cols, ws_ref[g], 0.0).astype(BF16)


def _sgu_mid_fwd(zin, lng, lnb, ws, bs_t, name):
    t, h2 = zin.shape
    hh = h2 // 2
    ng, ch = ws.shape[0], ws.shape[1]
    hg = hh // ng
    tm = _tile(t, 256, ch)

    def body(zin_ref, lng_ref, lnb_ref, ws_ref, bs_ref, o_ref):
        _, _, u, _, _, vn = _sgu_common(zin_ref, lng_ref, lnb_ref, hh)
        vnb = vn.astype(BF16)
        for g in range(ng):
            wm = _sgu_wm(ws_ref, g, ch)
            for cc in range(tm // ch):
                rs, cs = slice(cc * ch, (cc + 1) * ch), slice(g * hg, (g + 1) * hg)
                vv = jnp.dot(wm, vnb[rs, cs], preferred_element_type=F32) + bs_ref[:, g:g + 1]
                o_ref[rs, cs] = (u[rs, cs] * vv).astype(BF16)

    return pl.pallas_call(
        body, name=name, grid=(t // tm,),
        in_specs=[pl.BlockSpec((tm, h2), lambda i: (i, 0)), _row(hh), _row(hh),
                  pl.BlockSpec((ng, ch, ch), lambda i: (0, 0, 0)), pl.BlockSpec((ch, ng), lambda i: (0, 0))],
        out_specs=pl.BlockSpec((tm, hh), lambda i: (i, 0)), out_shape=SDS((t, hh), BF16), compiler_params=_params(1),
    )(zin, lng, lnb, ws, bs_t)


def _sgu_mid_bwd(zin, duv, lng, lnb, ws, bs_t, name):
    t, h2 = zin.shape
    hh = h2 // 2
    ng, ch = ws.shape[0], ws.shape[1]
    hg = hh // ng
    tm = _tile(t, 128, ch)

    def body(zin_ref, duv_ref, lng_ref, lnb_ref, ws_ref, bs_ref, dzin_ref, dlng_ref, dlnb_ref, dws_ref, dbs_ref, dbin_ref, dvn_s, du_s):
        @pl.when(pl.program_id(0) == 0)
        def _():
            for r in (dlng_ref, dlnb_ref, dws_ref, dbs_ref, dbin_ref):
                r[...] = jnp.zeros_like(r)

        z, cdf, u, vhat, rstd, vn = _sgu_common(zin_ref, lng_ref, lnb_ref, hh)
        vnb = vn.astype(BF16)
        duv = duv_ref[...].astype(F32)
        dvv = (duv * u).astype(BF16)
        lane = lax.broadcasted_iota(jnp.int32, (1, _LANES), 1)
        rows = lax.broadcasted_iota(jnp.int32, (ch, ch), 0)
        cols = lax.broadcasted_iota(jnp.int32, (ch, ch), 1)
        dbs = jnp.zeros((ch, _LANES), F32)
        for g in range(ng):
            wm = _sgu_wm(ws_ref, g, ch)
            dwm = jnp.zeros((ch, ch), F32)
            for cc in range(tm // ch):
                rs, cs = slice(cc * ch, (cc + 1) * ch), slice(g * hg, (g + 1) * hg)
                vv = jnp.dot(wm, vnb[rs, cs], preferred_element_type=F32) + bs_ref[:, g:g + 1]
                du_s[rs, cs] = duv[rs, cs] * vv
                dvn_s[rs, cs] = _tn(wm, dvv[rs, cs])
                dwm = dwm + _nt(dvv[rs, cs], vnb[rs, cs])
                dbs = dbs + jnp.where(lane == g, jnp.sum(dvv[rs, cs].astype(F32), axis=-1, keepdims=True), 0.0)
            dws_ref[g] += jnp.where(rows >= cols, dwm, 0.0)
        dbs_ref[...] += dbs
        dvn = dvn_s[...]
        dlng_ref[...] += jnp.sum(dvn * vhat, axis=0, keepdims=True)
        dlnb_ref[...] += jnp.sum(dvn, axis=0, keepdims=True)
        dxh = dvn * lng_ref[...]
        dv = rstd * (dxh - jnp.mean(dxh, axis=-1, keepdims=True) - vhat * jnp.mean(dxh * vhat, axis=-1, keepdims=True))
        dgelu = cdf + z * (_INV_SQRT_2PI * jnp.exp(-0.5 * z * z))
        dzu = du_s[...] * dgelu[:, :hh]
        dzv = dv * dgelu[:, hh:]
        dzin_ref[:, :hh] = dzu.astype(BF16)
        dzin_ref[:, hh:] = dzv.astype(BF16)
        dbin_ref[:, :hh] += jnp.sum(dzu, axis=0, keepdims=True)
        dbin_ref[:, hh:] += jnp.sum(dzv, axis=0, keepdims=True)

    return pl.pallas_call(
        body, name=name, grid=(t // tm,),
        in_specs=[pl.BlockSpec((tm, h2), lambda i: (i, 0)), pl.BlockSpec((tm, hh), lambda i: (i, 0)), _row(hh), _row(hh),
                  pl.BlockSpec((ng, ch, ch), lambda i: (0, 0, 0)), pl.BlockSpec((ch, ng), lambda i: (0, 0))],
        out_specs=[pl.BlockSpec((tm, h2), lambda i: (i, 0)), _row(hh), _row(hh), pl.BlockSpec((ng, ch, ch), lambda i: (0, 0, 0)),
                   pl.BlockSpec((ch, _LANES), lambda i: (0, 0)), _row(h2)],
        out_shape=[SDS((t, h2), BF16), SDS((1, hh), F32), SDS((1, hh), F32), SDS((ng, ch, ch), F32), SDS((ch, _LANES), F32),
                   SDS((1, h2), F32)],
        scratch_shapes=[pltpu.VMEM((tm, hh), F32), pltpu.VMEM((tm, hh), F32)], compiler_params=_params(1),
    )(zin, duv, lng, lnb, ws, bs_t)


def _ada_mod(c_all, ada_w, ada_b, name):
    nl, d, n = ada_w.shape
    nb = c_all.shape[0]
    tn = _tile(n, _COL_TILE, _LANES)

    def body(c_ref, w_ref, b_ref, o_ref):
        cv = c_ref[...]
        ca = cv * _sigmoid(cv)
        o_ref[...] = jnp.dot(ca, w_ref[...], preferred_element_type=F32, precision=lax.Precision.HIGHEST) + b_ref[...]

    return pl.pallas_call(
        body, name=name, grid=(nl, n // tn),
        in_specs=[pl.BlockSpec((nb, d), lambda l, j: (0, 0)), pl.BlockSpec((None, d, tn), lambda l, j: (l, 0, j)),
                  pl.BlockSpec((None, 1, tn), lambda l, j: (l, 0, j))],
        out_specs=pl.BlockSpec((None, nb, tn), lambda l, j: (l, 0, j)), out_shape=SDS((nl, nb, n), F32),
        compiler_params=_params(2),
    )(c_all, ada_w, ada_b)


def _ada_wgrad(c_all_t, dmod, name):
    d, nb = c_all_t.shape
    nl, _, n = dmod.shape
    tn = _tile(n, _COL_TILE, _LANES)

    def body(c_ref, dm_ref, o_ref):
        cv = c_ref[...]
        ca = cv * _sigmoid(cv)
        acc = ca[:, 0:1] * dm_ref[0:1, :]
        for b in range(1, nb):
            acc = acc + ca[:, b:b + 1] * dm_ref[b:b + 1, :]
        o_ref[...] = acc

    return pl.pallas_call(
        body, name=name, grid=(nl, n // tn),
        in_specs=[pl.BlockSpec((d, nb), lambda l, j: (0, 0)), pl.BlockSpec((None, nb, tn), lambda l, j: (l, 0, j))],
        out_specs=pl.BlockSpec((None, d, tn), lambda l, j: (l, 0, j)), out_shape=SDS((nl, d, n), F32),
        compiler_params=_params(2),
    )(c_all_t, dmod)


def _as_rows(a):
    return a.reshape(-1, a.shape[-1])


def _row_tile(r, c, n_arrays):
    budget = _VMEM_LIMIT_BYTES // (4 * 2 * n_arrays * 4)
    return _tile(r, max(8, budget // max(c, 1)), 8)


def _cast_bf16_layer(a, layer, name):
    _, r, c = a.shape
    tr = _row_tile(r, c, 2)

    def body(a_ref, o_ref):
        o_ref[...] = a_ref[...].astype(BF16)

    return pl.pallas_call(body, name=name, grid=(r // tr,), in_specs=[pl.BlockSpec((None, tr, c), lambda i: (layer, i, 0))],
                          out_specs=pl.BlockSpec((tr, c), lambda i: (i, 0)), out_shape=SDS((r, c), BF16),
                          compiler_params=_params(1))(a)


def _sum_slot_layers(parts, name):
    nl = len(parts)
    n, r, c = parts[0].shape
    tr = _row_tile(r, c, nl * (n + 1))

    def body(*refs):
        o_ref = refs[nl]
        for lay in range(nl):
            acc = refs[lay][0].astype(F32)
            for s in range(1, n):
                acc = acc + refs[lay][s].astype(F32)
            o_ref[lay] = acc

    return pl.pallas_call(body, name=name, grid=(r // tr,), in_specs=[pl.BlockSpec((n, tr, c), lambda i: (0, i, 0))] * nl,
                          out_specs=pl.BlockSpec((nl, tr, c), lambda i: (0, i, 0)), out_shape=SDS((nl, r, c), F32),
                          compiler_params=_params(1))(*parts)


def _sum_slots(parts, name):
    n = parts.shape[0]
    p2 = parts.reshape(n, -1, parts.shape[-1])
    r, c = p2.shape[1:]
    tr = _row_tile(r, c, n + 1)

    def body(p_ref, o_ref):
        acc = p_ref[0].astype(F32)
        for s in range(1, n):
            acc = acc + p_ref[s].astype(F32)
        o_ref[...] = acc

    out = pl.pallas_call(body, name=name, grid=(r // tr,), in_specs=[pl.BlockSpec((n, tr, c), lambda i: (0, i, 0))],
                         out_specs=pl.BlockSpec((tr, c), lambda i: (i, 0)), out_shape=SDS((r, c), F32),
                         compiler_params=_params(1))(p2)
    return out.reshape(parts.shape[1:])


def _adamw(w, g_parts, m, v, name):
    w2, m2, v2 = _as_rows(w), _as_rows(m), _as_rows(v)
    gs = [_as_rows(g) for g in g_parts]
    r, c = w2.shape
    ng = len(gs)
    tr = _row_tile(r, c, 7 + ng)
    c1 = 1.0 - _ADAM_B1 ** _ADAM_STEP
    c2 = 1.0 - _ADAM_B2 ** _ADAM_STEP

    def body(*refs):
        w_ref, m_ref, v_ref = refs[:3]
        g_refs = refs[3:3 + ng]
        go_ref, d_ref, mo_ref, vo_ref = refs[3 + ng:]
        g = g_refs[0][...]
        for gr in g_refs[1:]:
            g = g + gr[...]
        mn = _ADAM_B1 * m_ref[...] + (1.0 - _ADAM_B1) * g
        vn = _ADAM_B2 * v_ref[...] + (1.0 - _ADAM_B2) * (g * g)
        go_ref[...] = g
        mo_ref[...] = mn
        vo_ref[...] = vn
        d_ref[...] = -_ADAM_LR * ((mn / c1) / (jnp.sqrt(vn / c2) + _ADAM_EPS) + _ADAM_WD * w_ref[...])

    spec = pl.BlockSpec((tr, c), lambda i: (i, 0))
    outs = pl.pallas_call(body, name=name, grid=(r // tr,), in_specs=[spec] * (3 + ng), out_specs=[spec] * 4,
                          out_shape=[SDS((r, c), F32)] * 4, compiler_params=_params(1))(w2, m2, v2, *gs)
    return tuple(o.reshape(w.shape) for o in outs)


_HBM = pl.BlockSpec(memory_space=pltpu.HBM)
_CHIP_FLIPS = ((1, 0, 0), (0, 1, 0), (1, 1, 0))
_ALL_FLIPS = tuple((a, b, c) for a in (0, 1) for b in (0, 1) for c in (0, 1))[1:]
_SIBLING_FLIP = ((0, 0, 1),)


def _at(ref, idx):
    return ref.at[idx] if idx else ref


def _exchange(name, flips, srcs, out_shapes, items):
    n_in, n_out = len(srcs), len(out_shapes)
    n_remote = len(items) * len(flips)
    n_local = sum(1 for it in items if it[4])

    def body(*refs):
        src_refs, dst_refs = refs[:n_in], refs[n_in:n_in + n_out]
        send_sems, recv_sems, local_sems = refs[n_in + n_out:]
        me = (lax.axis_index("x"), lax.axis_index("y"), lax.axis_index("c"))
        sends, recvs, locs = [], [], []
        n, nl = 0, 0
        for si, di, src_idx, dst_idx, local in items:
            for flip in flips:
                peer = tuple(1 - m if f else m for m, f in zip(me, flip))
                push = pltpu.make_async_remote_copy(
                    src_ref=_at(src_refs[si], src_idx(*peer)), dst_ref=_at(dst_refs[di], dst_idx(*me)),
                    send_sem=send_sems.at[n], recv_sem=recv_sems.at[n], device_id=peer, device_id_type=MESH_IDS)
                push.start()
                sends.append(push)
                recvs.append(pltpu.make_async_remote_copy(
                    src_ref=_at(src_refs[si], src_idx(*me)), dst_ref=_at(dst_refs[di], dst_idx(*peer)),
                    send_sem=send_sems.at[n], recv_sem=recv_sems.at[n], device_id=peer, device_id_type=MESH_IDS))
                n += 1
            if local:
                cp = pltpu.make_async_copy(_at(src_refs[si], src_idx(*me)), _at(dst_refs[di], dst_idx(*me)), local_sems.at[nl])
                cp.start()
                locs.append(cp)
                nl += 1
        for r in recvs:
            r.wait_recv()
        for s in sends:
            s.wait_send()
        for cp in locs:
            cp.wait()

    return pl.pallas_call(
        body, name=name, in_specs=[_HBM] * n_in, out_specs=[_HBM] * n_out, out_shape=list(out_shapes),
        scratch_shapes=[pltpu.SemaphoreType.DMA((n_remote,)), pltpu.SemaphoreType.DMA((n_remote,)),
                        pltpu.SemaphoreType.DMA((max(n_local, 1),))],
    )(*srcs)


_SEM = pl.BlockSpec(memory_space=pltpu.SEMAPHORE)
_SIDE_EFFECTS = pltpu.SideEffectType.DATAFLOW_SIDE_EFFECTING


def _split_copies(src_refs, land_refs, send_sems, recv_sems, items):
    me = (lax.axis_index("x"), lax.axis_index("y"), lax.axis_index("c"))
    pushes, arrivals, n = [], [], 0
    for si, di, src_idx, dst_idx in items:
        for flip in _CHIP_FLIPS:
            peer = tuple(1 - m if f else m for m, f in zip(me, flip))
            pushes.append(pltpu.make_async_remote_copy(
                src_ref=_at(src_refs[si], src_idx(*peer)), dst_ref=_at(land_refs[di], dst_idx(*me)),
                send_sem=send_sems.at[n], recv_sem=recv_sems.at[n], device_id=peer, device_id_type=MESH_IDS))
            arrivals.append(pltpu.make_async_remote_copy(
                src_ref=_at(src_refs[si], src_idx(*me)), dst_ref=_at(land_refs[di], dst_idx(*peer)),
                send_sem=send_sems.at[n], recv_sem=recv_sems.at[n], device_id=peer, device_id_type=MESH_IDS))
            n += 1
    return me, pushes, arrivals


def _push_start(name, srcs, land_shapes, items, after):
    n_src, n_land = len(srcs), len(land_shapes)
    n_buf = n_src + n_land
    n_remote = len(items) * len(_CHIP_FLIPS)

    def body(*refs):
        src_refs, land_refs = refs[:n_src], refs[n_src:n_buf]
        send_sems, recv_sems = refs[n_buf + 1], refs[n_buf + 2]
        local_sems = refs[2 * n_buf + 4]
        me, pushes, _ = _split_copies(src_refs, land_refs, send_sems, recv_sems, items)
        own = [pltpu.make_async_copy(_at(src_refs[si], src_idx(*me)), _at(land_refs[di], dst_idx(*me)), local_sems.at[k])
               for k, (si, di, src_idx, dst_idx) in enumerate(items)]
        for cp in own:
            cp.start()
        for cp in own:
            cp.wait()
        for p in pushes:
            p.start()

    bufs = list(srcs) + [lax.empty(s.shape, s.dtype) for s in land_shapes]
    outs = pl.pallas_call(
        body, name=name, in_specs=[_HBM] * n_buf + [pl.BlockSpec(memory_space=pl.ANY)],
        out_shape=[pltpu.SemaphoreType.DMA((n_remote,)), pltpu.SemaphoreType.DMA((n_remote,))]
        + [pltpu.HBM(a.shape, a.dtype) for a in bufs] + [pltpu.HBM((_SUBLANES, _LANES), F32)],
        out_specs=[_SEM, _SEM] + [_HBM] * (n_buf + 1),
        input_output_aliases={k: 2 + k for k in range(n_buf)},
        scratch_shapes=[pltpu.SemaphoreType.DMA((len(items),))],
        compiler_params=pltpu.CompilerParams(has_side_effects=_SIDE_EFFECTS),
    )(*[pltpu.with_memory_space_constraint(a, pltpu.HBM) for a in bufs], after)
    return dict(send=outs[0], recv=outs[1], bufs=list(outs[2:2 + n_buf]), order=outs[2 + n_buf], n_src=n_src, items=items)


def _behind(x, handles, name):
    deps = [h["order"] for h in handles]

    def body(*refs):
        refs[-1][...] = refs[0][...]

    vmem = pl.BlockSpec(memory_space=pltpu.VMEM)
    return pl.pallas_call(body, name=name, in_specs=[vmem] + [pl.BlockSpec(memory_space=pl.ANY)] * len(deps), out_specs=vmem,
                          out_shape=SDS(x.shape, x.dtype))(x, *deps)


def _push_wait(name, handle, after):
    n_src, items = handle["n_src"], handle["items"]
    n_buf = len(handle["bufs"])

    def body(*refs):
        src_refs, land_refs = refs[:n_src], refs[n_src:n_buf]
        send_sems, recv_sems = refs[n_buf], refs[n_buf + 1]
        _, pushes, arrivals = _split_copies(src_refs, land_refs, send_sems, recv_sems, items)
        for p in pushes:
            p.wait_send()
        for a in arrivals:
            a.wait_recv()

    outs = pl.pallas_call(
        body, name=name, in_specs=[_HBM] * n_buf + [_SEM, _SEM, pl.BlockSpec(memory_space=pl.ANY)],
        out_shape=[pltpu.HBM(a.shape, a.dtype) for a in handle["bufs"]], out_specs=[_HBM] * n_buf,
        input_output_aliases={k: k for k in range(n_buf)},
        compiler_params=pltpu.CompilerParams(has_side_effects=_SIDE_EFFECTS),
    )(*handle["bufs"], handle["send"], handle["recv"], after)
    return list(outs[n_src:])


def _chip_of(x, y, c):
    return 2 * x + y


def _dev_of(x, y, c):
    return 4 * x + 2 * y + c


def _window(axis, ndim, size):
    def idx(x, y, c):
        return tuple(pl.ds(_chip_of(x, y, c) * size, size) if a == axis else slice(None) for a in range(ndim))
    return idx


def _whole(x, y, c):
    return ()


def _start_gather(name, shards_axes, after):
    srcs, shapes, items = [], [], []
    for k, (a, axis) in enumerate(shards_axes):
        full = list(a.shape)
        full[axis] *= _N_CHIPS
        srcs.append(a)
        shapes.append(SDS((1,) + tuple(full), a.dtype))
        window = _window(axis, a.ndim, a.shape[axis])
        items.append((k, k, _whole, (lambda w: lambda x, y, c: (0,) + w(x, y, c))(window)))
    return _push_start(name, srcs, shapes, items, after)


def _start_scatter(name, grads_axes, after):
    srcs, shapes, items = [], [], []
    slot = lambda x, y, c: (_chip_of(x, y, c),)
    for k, (a, axis) in enumerate(grads_axes):
        shard = list(a.shape)
        shard[axis] //= _N_CHIPS
        srcs.append(a)
        shapes.append(SDS((_N_CHIPS,) + tuple(shard), a.dtype))
        items.append((k, k, _window(axis, a.ndim, shard[axis]), slot))
    return _push_start(name, srcs, shapes, items, after)


def _gather_all(name, a):
    slot = lambda x, y, c: (_dev_of(x, y, c),)
    return _exchange(name, _ALL_FLIPS, [a], [SDS((_N_DEV,) + a.shape, a.dtype)], [(0, 0, _whole, slot, True)])[0]


def _swap_chips(name, a):
    slot = lambda x, y, c: (_chip_of(x, y, c),)
    return _exchange(name, _CHIP_FLIPS, [a], [SDS(a.shape, a.dtype)], [(0, 0, slot, slot, True)])[0]


def _swap_sibling(name, arrays):
    items = [(k, k, _whole, _whole, False) for k in range(len(arrays))]
    return _exchange(name, _SIBLING_FLIP, list(arrays), [SDS(a.shape, a.dtype) for a in arrays], items)


_BIG = ("attn_wqkv", "attn_wo", "conv_w_in", "conv_w_out", "sgu_w_in", "sgu_w_out", "ffn_w_in", "ffn_w_out")
_BIG_AXIS = {"attn_wqkv": 2, "attn_wo": 1, "conv_w_in": 2, "conv_w_out": 1, "sgu_w_in": 2, "sgu_w_out": 1,
             "ffn_w_in": 2, "ffn_w_out": 1}
_SMALL = {"norm1_g": None, "norm2_g": None, "ada_b": None, "attn_bqkv": 1, "attn_sinks": None, "attn_bo": 1,
          "conv_b_in": None, "conv_dw": 2, "conv_dw_b": None, "conv_ln_g": None, "conv_ln_b": None, "conv_b_out": None,
          "sgu_b_in": 1, "sgu_ln_g": 1, "sgu_ln_b": 1, "sgu_ws": None, "sgu_bs": None, "sgu_b_out": 1,
          "ffn_dw": 2, "ffn_dw_b": None, "final_g": None}
_WEIGHTS = ['norm1_g', 'norm2_g', 'ada_w', 'ada_b', 'attn_wqkv', 'attn_bqkv', 'attn_sinks', 'attn_wo', 'attn_bo',
            'conv_w_in', 'conv_b_in', 'conv_dw', 'conv_dw_b', 'conv_ln_g', 'conv_ln_b', 'conv_w_out', 'conv_b_out',
            'sgu_w_in', 'sgu_b_in', 'sgu_ln_g', 'sgu_ln_b', 'sgu_ws', 'sgu_bs', 'sgu_w_out', 'sgu_b_out',
            'ffn_w_in', 'ffn_dw', 'ffn_dw_b', 'ffn_w_out', 'final_g']
_FLAT_COLS = 1024


def _full_shape(a, axis):
    s = list(a.shape)
    if axis is not None:
        s[axis] *= _N_CHIPS
    return tuple(s)


def _pack(arrays):
    flat = jnp.concatenate([a.reshape(-1).astype(F32) for a in arrays])
    pad = (-flat.shape[0]) % (8 * _FLAT_COLS)
    return jnp.pad(flat, (0, pad)).reshape(-1, _FLAT_COLS)


def _unpack(flat2d, shapes):
    flat = flat2d.reshape(-1)
    out, off = [], 0
    for s in shapes:
        n = math.prod(s)
        out.append(flat[off:off + n].reshape(s))
        off += n
    return out


def _shard_of(full, axis, chip):
    if axis is None:
        return full
    size = full.shape[axis] // _N_CHIPS
    return lax.dynamic_slice_in_dim(full, chip * size, size, axis)


def _unshard(gathered, axis):
    moved = jnp.moveaxis(gathered, 0, axis)
    shape = list(gathered.shape[1:])
    shape[axis] *= _N_CHIPS
    return moved.reshape(shape)


def kernel(x, c, norm1_g, norm2_g, ada_w, ada_b, attn_wqkv, attn_bqkv, attn_sinks, attn_wo, attn_bo, conv_w_in, conv_b_in, conv_dw, conv_dw_b, conv_ln_g, conv_ln_b, conv_w_out, conv_b_out, sgu_w_in, sgu_b_in, sgu_ln_g, sgu_ln_b, sgu_ws, sgu_bs, sgu_w_out, sgu_b_out, ffn_w_in, ffn_dw, ffn_dw_b, ffn_w_out, final_g, loss_target, m_norm1_g, m_norm2_g, m_ada_w, m_ada_b, m_attn_wqkv, m_attn_bqkv, m_attn_sinks, m_attn_wo, m_attn_bo, m_conv_w_in, m_conv_b_in, m_conv_dw, m_conv_dw_b, m_conv_ln_g, m_conv_ln_b, m_conv_w_out, m_conv_b_out, m_sgu_w_in, m_sgu_b_in, m_sgu_ln_g, m_sgu_ln_b, m_sgu_ws, m_sgu_bs, m_sgu_w_out, m_sgu_b_out, m_ffn_w_in, m_ffn_dw, m_ffn_dw_b, m_ffn_w_out, m_final_g, v_norm1_g, v_norm2_g, v_ada_w, v_ada_b, v_attn_wqkv, v_attn_bqkv, v_attn_sinks, v_attn_wo, v_attn_bo, v_conv_w_in, v_conv_b_in, v_conv_dw, v_conv_dw_b, v_conv_ln_g, v_conv_ln_b, v_conv_w_out, v_conv_b_out, v_sgu_w_in, v_sgu_b_in, v_sgu_ln_g, v_sgu_ln_b, v_sgu_ws, v_sgu_bs, v_sgu_w_out, v_sgu_b_out, v_ffn_w_in, v_ffn_dw, v_ffn_dw_b, v_ffn_w_out, v_final_g):
    args = dict(locals())
    wts = {n: args[n] for n in _WEIGHTS}
    mom_m = {n: args["m_" + n] for n in _WEIGHTS}
    mom_v = {n: args["v_" + n] for n in _WEIGHTS}

    ix, iy, ic = lax.axis_index("x"), lax.axis_index("y"), lax.axis_index("c")
    chip = 2 * ix + iy
    xs = x[0]
    tgt = loss_target[0]
    t, d = xs.shape
    nh = d // _HEAD_DIM
    depth = ada_w.shape[0]
    ncols = ada_w.shape[2]
    n_mod = ncols * _N_CHIPS // d

    small_sharded = [n for n in _WEIGHTS if _SMALL.get(n) is not None]
    packed_small = _pack([wts[n] for n in small_sharded])
    mixers = {0: ("attn_wqkv", "attn_wo"), 1: ("conv_w_in", "conv_w_out"), 2: ("sgu_w_in", "sgu_w_out")}
    groups = [[(n, 0) for n in mixers[0]], [("ffn_w_in", 0), ("ffn_w_out", 0)]]
    for i in range(1, depth):
        groups.append([(n, i // 3) for n in mixers[i % 3]] + [("ffn_w_in", i), ("ffn_w_out", i)])
    c_all = _gather_all("gather_c", c)[:, 0, :]
    ada_b_cols = lax.dynamic_slice_in_dim(ada_b, chip * ncols, ncols, 1)[:, None, :]
    mod_cols = _ada_mod(c_all, ada_w, ada_b_cols, "ada_mod")
    mine = lax.dynamic_index_in_dim(mod_cols.reshape(depth, _N_CHIPS, 2, ncols), ic, 2, keepdims=False)
    got = _swap_chips("swap_mod", jnp.moveaxis(mine, 1, 0))

    gathers = []
    for gi, keys in enumerate(groups):
        shards = [(_cast_bf16_layer(wts[n], j, "cast_%s_%d" % (n, j)), _BIG_AXIS[n] - 1) for n, j in keys]
        gathers.append(_start_gather("start_gather_%d" % gi, shards + ([(packed_small, 0)] if gi == 0 else []), got))
    mod = _behind(jnp.moveaxis(got, 0, 1).reshape(depth, n_mod, 1, d), gathers, "behind_gather_starts")
    wfull = {}

    def finish_gather(gi, after):
        lands = _push_wait("wait_gather_%d" % gi, gathers[gi], after)
        wfull.update(zip(groups[gi], lands))
        return lands

    small_rows = finish_gather(0, mod)[-1].reshape(_N_CHIPS, -1, _FLAT_COLS)
    pieces = [_unpack(small_rows[s], [wts[n].shape for n in small_sharded]) for s in range(_N_CHIPS)]
    full = {n: wts[n] for n in _WEIGHTS if n in _SMALL and _SMALL[n] is None}
    for k, n in enumerate(small_sharded):
        full[n] = _unshard(jnp.stack([pieces[s][k] for s in range(_N_CHIPS)]), _SMALL[n])

    tables = _attn_tables(nh)
    zeros_d = jnp.zeros((1, d), F32)
    zeros_f2 = jnp.zeros((1, ffn_w_in.shape[2] * _N_CHIPS), F32)
    row = lambda a: a.reshape(1, -1)

    saved = []
    xcur = xs
    for i in range(depth):
        sh1, sc1, g1, sh2, sc2, g2 = (mod[i, k] for k in range(n_mod))
        kind, j = i % 3, i // 3
        tag = "L%d_" % i
        if i > 0:
            finish_gather(i + 1, xcur)
        w_in, w_out = wfull[(mixers[kind][0], j)], wfull[(mixers[kind][1], j)]
        if kind == 0:
            b_in, b_out = row(full["attn_bqkv"][j]), row(full["attn_bo"][j])
        elif kind == 1:
            b_in, b_out = row(full["conv_b_in"][j]), row(full["conv_b_out"][j])
        else:
            b_in, b_out = row(full["sgu_b_in"][j]), row(full["sgu_b_out"][j])
        h1, z = _norm_mod_matmul(xcur, row(full["norm1_g"][i]), sc1, sh1, w_in, 0, b_in, tag + "mixer_in")
        if i == 0:
            finish_gather(1, z)
        if kind == 0:
            a = _attn_fwd(z, tables, full["attn_sinks"][j], nh, tag + "attn")
        elif kind == 1:
            a = _conv_mid_fwd(z, full["conv_dw"][j], row(full["conv_dw_b"][j]), row(full["conv_ln_g"][j]), row(full["conv_ln_b"][j]),
                              tag + "conv_mid")
        else:
            a = _sgu_mid_fwd(z, row(full["sgu_ln_g"][j]), row(full["sgu_ln_b"][j]), full["sgu_ws"][j], full["sgu_bs"][j].T,
                             tag + "sgu_mid")
        y1, xmid = _matmul_resid(a, w_out, 0, b_out, xcur, g1, tag + "mixer_out")
        h2, zf = _norm_mod_matmul(xmid, row(full["norm2_g"][i]), sc2, sh2, wfull[("ffn_w_in", i)], 0, zeros_f2, tag + "ffn_in")
        act = _ffn_mid_fwd(zf, full["ffn_dw"][i], row(full["ffn_dw_b"][i]), tag + "ffn_mid")
        y2, xnext = _matmul_resid(act, wfull[("ffn_w_out", i)], 0, zeros_d, xmid, g2, tag + "ffn_out")
        saved.append(dict(x=xcur, h1=h1, z=z, a=a, y1=y1, xmid=xmid, h2=h2, zf=zf, act=act, y2=y2, w_in=w_in, w_out=w_out))
        xcur = xnext

    loss_row, dx, d_final_g = _final_loss(xcur, row(final_g), tgt, "final_loss")

    small_g = {n: jnp.zeros(_full_shape(wts[n], _SMALL[n]), F32) for n in _SMALL}
    small_g["final_g"] = d_final_g[0]
    dmod = [None] * depth
    scatters = []
    pushed = None

    def put(name, j, val):
        small_g[name] = small_g[name].at[j].set(val.reshape(small_g[name].shape[1:]))

    for i in reversed(range(depth)):
        sv = saved[i]
        sh1, sc1, g1, sh2, sc2, g2 = (mod[i, k] for k in range(n_mod))
        kind, j = i % 3, i // 3
        tag = "L%d_" % i
        if pushed is not None:
            g2 = _behind(g2, [pushed], tag + "behind_scatter_start")
        dy2, dact, dg2, _ = _outproj_bwd(dx, sv["y2"], g2, wfull[("ffn_w_out", i)], 0, tag + "ffn_out_bwd")
        g_ffn_out = _matmul_tn(sv["act"], dy2, tag + "ffn_out_wgrad")
        dzf, d_fdw, d_fdwb = _ffn_mid_bwd(sv["zf"], dact, full["ffn_dw"][i], row(full["ffn_dw_b"][i]), tag + "ffn_mid_bwd")
        put("ffn_dw", i, d_fdw)
        put("ffn_dw_b", i, d_fdwb)
        g_ffn_in = _matmul_tn(sv["h2"], dzf, tag + "ffn_in_wgrad")
        dxmid, dn2, dsc2, dsh2 = _inproj_bwd(dzf, wfull[("ffn_w_in", i)], 0, sv["xmid"], dx, row(full["norm2_g"][i]), sc2,
                                             tag + "ffn_in_bwd")
        put("norm2_g", i, dn2)
        handle = _start_scatter("start_scatter_ffn_%d" % i, [(g_ffn_in, 1), (g_ffn_out, 0)], dxmid)
        scatters.append(([("ffn_w_in", i), ("ffn_w_out", i)], handle))
        g1 = _behind(g1, [handle], tag + "behind_ffn_scatter_start")
        dy1, da, dg1, dbo = _outproj_bwd(dxmid, sv["y1"], g1, sv["w_out"], 0, tag + "mixer_out_bwd")
        w_in_name, w_out_name = mixers[kind]
        g_mix_out = _matmul_tn(sv["a"], dy1, tag + "mixer_out_wgrad")
        if kind == 0:
            dz, dsink, dbin = _attn_bwd(sv["z"], da, tables, full["attn_sinks"][j], nh, tag + "attn_bwd")
            put("attn_sinks", j, dsink[0, :nh])
            put("attn_bqkv", j, dbin)
            put("attn_bo", j, dbo)
        elif kind == 1:
            dz, d_dw, d_dwb, d_lng, d_lnb, dbin = _conv_mid_bwd(
                sv["z"], da, full["conv_dw"][j], row(full["conv_dw_b"][j]), row(full["conv_ln_g"][j]), row(full["conv_ln_b"][j]),
                tag + "conv_mid_bwd")
            for nme, val in (("conv_dw", d_dw), ("conv_dw_b", d_dwb), ("conv_ln_g", d_lng), ("conv_ln_b", d_lnb),
                             ("conv_b_in", dbin), ("conv_b_out", dbo)):
                put(nme, j, val)
        else:
            dz, d_lng, d_lnb, d_ws, d_bst, dbin = _sgu_mid_bwd(
                sv["z"], da, row(full["sgu_ln_g"][j]), row(full["sgu_ln_b"][j]), full["sgu_ws"][j], full["sgu_bs"][j].T,
                tag + "sgu_mid_bwd")
            ng = sgu_ws.shape[1]
            for nme, val in (("sgu_ln_g", d_lng), ("sgu_ln_b", d_lnb), ("sgu_ws", d_ws), ("sgu_bs", d_bst[:, :ng].T),
                             ("sgu_b_in", dbin), ("sgu_b_out", dbo)):
                put(nme, j, val)
        g_mix_in = _matmul_tn(sv["h1"], dz, tag + "mixer_in_wgrad")
        dx, dn1, dsc1, dsh1 = _inproj_bwd(dz, sv["w_in"], 0, sv["x"], dxmid, row(full["norm1_g"][i]), sc1, tag + "mixer_in_bwd")
        put("norm1_g", i, dn1)
        dmod[i] = jnp.concatenate([dsh1, dsc1, dg1, dsh2, dsc2, dg2], axis=1)
        handle = _start_scatter("start_scatter_mix_%d" % i, [(g_mix_in, 1), (g_mix_out, 0)], dx)
        scatters.append(([(w_in_name, j), (w_out_name, j)], handle))
        pushed = handle

    grad_x = dx[None]
    loss = lax.psum(loss_row[0, 0], ("x", "y", "c"))

    small_names = [n for n in _WEIGHTS if n in _SMALL and n != "ada_b"]
    dmod_own = jnp.concatenate(dmod, axis=0)
    packed = _behind(_pack([small_g[n] for n in small_names] + [dmod_own]), [pushed], "behind_last_scatter_start")
    packed_all = _gather_all("gather_small_grads", packed)
    summed = _sum_slots(packed_all, "sum_small_grads")
    small_full = dict(zip(small_names, _unpack(summed, [small_g[n].shape for n in small_names])))
    n_small = sum(math.prod(small_g[n].shape) for n in small_names)
    dmod_all = packed_all.reshape(_N_DEV, -1)[:, n_small:n_small + dmod_own.size].reshape(_N_DEV, depth, n_mod * d)
    small_full["ada_b"] = _sum_slots(dmod_all, "sum_ada_b_grad")
    dmod_cols = lax.dynamic_slice_in_dim(jnp.moveaxis(dmod_all, 0, 1), chip * ncols, ncols, 2)
    g_ada_w = _ada_wgrad(c_all.T, dmod_cols, "ada_wgrad")

    slots = {}
    for k, (keys, handle) in enumerate(scatters):
        slots.update(zip(keys, _push_wait("wait_scatter_%d" % k, handle, g_ada_w)))
    partial = [_sum_slot_layers([slots[(n, j)] for j in range(wts[n].shape[0])], "sum_chips_" + n) for n in _BIG]
    other = _swap_sibling("swap_cores", partial)

    outs = {}
    for n, mine_p, theirs_p in zip(_BIG, partial, other):
        outs[n] = _adamw(wts[n], [mine_p, theirs_p], mom_m[n], mom_v[n], "adamw_" + n)
    outs["ada_w"] = _adamw(ada_w, [g_ada_w], m_ada_w, v_ada_w, "adamw_ada_w")
    sm_names = [n for n in _WEIGHTS if n in _SMALL]
    g_loc = [_shard_of(small_full[n], _SMALL[n], chip) for n in sm_names]
    packs = [_pack([src[n] for n in sm_names]) for src in (wts, mom_m, mom_v)]
    sm_out = _adamw(packs[0], [_pack(g_loc)], packs[1], packs[2], "adamw_small")
    shapes = [wts[n].shape for n in sm_names]
    unpacked = [_unpack(o, shapes) for o in sm_out]
    for k, n in enumerate(sm_names):
        outs[n] = tuple(u[k] for u in unpacked)

    result = [loss, grad_x]
    for which in range(4):
        result += [outs[n][which] for n in _WEIGHTS]
    return tuple(result)
```

```python
import math

import jax
import jax.numpy as jnp
from jax import lax
from jax.experimental import pallas as pl
from jax.experimental.pallas import tpu as pltpu

F32, BF16 = jnp.float32, jnp.bfloat16
SDS = jax.ShapeDtypeStruct
MESH_IDS = pl.DeviceIdType.MESH

_VMEM_LIMIT_BYTES = 48 * 1024 * 1024
_LANES = 128
_NORM_EPS = 1e-6
_NEG_INF = -1e30
_HEAD_DIM = 64
_N_KV = 4
_ATTN_BLOCK = 128
_ATTN_BLOCKS_PER_STEP = 2
_ROW_TILE = 512
_COL_TILE = 512
_WGRAD_TILE = 1536
_CONV_HALO = 32
_FFN_HALO = 16
_ADAM_LR, _ADAM_B1, _ADAM_B2, _ADAM_EPS, _ADAM_WD, _ADAM_STEP = 0.001, 0.9, 0.999, 1e-08, 0.01, 10
_N_CHIPS = 4
_N_DEV = 8


def _tile(n, pref, unit):
    if n <= pref:
        return n
    t = pref - pref % unit
    while t >= unit:
        if n % t == 0:
            return t
        t -= unit
    return n


def _params(n_axes):
    return pltpu.CompilerParams(dimension_semantics=("arbitrary",) * n_axes, vmem_limit_bytes=_VMEM_LIMIT_BYTES)


def _row(n):
    return pl.BlockSpec((1, n), lambda *_: (0, 0))


def _sigmoid(v):
    return 1.0 / (1.0 + jnp.exp(-v))


def _nt(a, b):
    return lax.dot_general(a, b, (((1,), (1,)), ((), ())), preferred_element_type=F32)


def _tn(a, b):
    return lax.dot_general(a, b, (((0,), (0,)), ((), ())), preferred_element_type=F32)


def _carrying(carry, body, *, name, grid, in_specs, out_specs, out_shape, scratch_shapes=(), compiler_params):
    single = not isinstance(out_shape, (list, tuple))
    if carry is None:
        call = pl.pallas_call(body, name=name, grid=grid, in_specs=in_specs, out_specs=out_specs, out_shape=out_shape,
                              scratch_shapes=list(scratch_shapes), compiler_params=compiler_params)
        return lambda *operands: (call(*operands), [])
    srcs, land_shapes, items = carry
    out_specs_l, out_shape_l = ([out_specs], [out_shape]) if single else (list(out_specs), list(out_shape))
    n_in, n_out, n_scr, n_src, n_land = len(in_specs), len(out_shape_l), len(scratch_shapes), len(srcs), len(land_shapes)
    n_remote = len(items) * len(_CHIP_FLIPS)
    last = grid[0] - 1

    def carrier(*refs):
        ins, src_refs = refs[:n_in], refs[n_in:n_in + n_src]
        o0 = n_in + n_src
        outs, land_refs = refs[o0:o0 + n_out], refs[o0 + n_out:o0 + n_out + n_land]
        s0 = o0 + n_out + n_land
        scratch, (send_sems, recv_sems, local_sems) = refs[s0:s0 + n_scr], refs[s0 + n_scr:]
        me, pushes, arrivals = _split_copies(src_refs, land_refs, send_sems, recv_sems, items)
        step = pl.program_id(0)

        @pl.when(step == 0)
        def _():
            own = [pltpu.make_async_copy(_at(src_refs[si], src_idx(*me)), _at(land_refs[di], dst_idx(*me)), local_sems.at[k])
                   for k, (si, di, src_idx, dst_idx) in enumerate(items)]
            for cp in own:
                cp.start()
            for cp in own:
                cp.wait()
            for p in pushes:
                p.start()

        body(*ins, *outs, *scratch)

        @pl.when(step == last)
        def _():
            for a in arrivals:
                a.wait_recv()
            for p in pushes:
                p.wait_send()

    def run(*operands):
        res = pl.pallas_call(
            carrier, name=name, grid=grid, in_specs=list(in_specs) + [_HBM] * n_src, out_specs=out_specs_l + [_HBM] * n_land,
            out_shape=out_shape_l + list(land_shapes),
            scratch_shapes=list(scratch_shapes) + [pltpu.SemaphoreType.DMA((n_remote,)), pltpu.SemaphoreType.DMA((n_remote,)),
                                                   pltpu.SemaphoreType.DMA((len(items),))],
            compiler_params=compiler_params)(*operands, *srcs)
        return (res[0] if single else list(res[:n_out])), list(res[n_out:])

    return run


def _resident(shape, index_map):
    return pl.BlockSpec(shape, index_map, pipeline_mode=pl.Buffered(1))


def _norm_mod_matmul(x, gn, sc, sh, w, layer, b, name, carry=None):
    t, d = x.shape
    n = w.shape[2]
    tm, tn = _tile(t, _ROW_TILE, 16), _tile(n, _COL_TILE, _LANES)

    def body(x_ref, gn_ref, sc_ref, sh_ref, w_ref, b_ref, h_ref, z_ref):
        xf = x_ref[...]
        r = lax.rsqrt(jnp.mean(xf * xf, axis=-1, keepdims=True) + _NORM_EPS)
        h_ref[...] = ((xf * r * gn_ref[...]) * (1.0 + sc_ref[...]) + sh_ref[...]).astype(BF16)
        for c0 in range(0, n, tn):
            z = jnp.dot(h_ref[...], w_ref[:, c0:c0 + tn], preferred_element_type=F32) + b_ref[:, c0:c0 + tn]
            z_ref[:, c0:c0 + tn] = z.astype(BF16)

    return _carrying(
        carry, body, name=name, grid=(t // tm,),
        in_specs=[pl.BlockSpec((tm, d), lambda i: (i, 0)), _row(d), _row(d), _row(d),
                  _resident((None, d, n), lambda i: (layer, 0, 0)), _row(n)],
        out_specs=[pl.BlockSpec((tm, d), lambda i: (i, 0)), pl.BlockSpec((tm, n), lambda i: (i, 0))],
        out_shape=[SDS((t, d), BF16), SDS((t, n), BF16)], compiler_params=_params(1),
    )(x, gn, sc, sh, w, b)


def _matmul_resid(a, w, layer, b, x, gate, name, carry=None):
    t, k = a.shape
    d = w.shape[2]
    tm, tn = _tile(t, _ROW_TILE, 16), _tile(d, _COL_TILE, _LANES)

    def body(a_ref, w_ref, b_ref, x_ref, g_ref, y_ref, xo_ref):
        for c0 in range(0, d, tn):
            cs = slice(c0, c0 + tn)
            y = jnp.dot(a_ref[...], w_ref[:, cs], preferred_element_type=F32) + b_ref[:, cs]
            y_ref[:, cs] = y.astype(BF16)
            xo_ref[:, cs] = x_ref[:, cs] + g_ref[:, cs] * y

    blk = pl.BlockSpec((tm, d), lambda i: (i, 0))
    return _carrying(
        carry, body, name=name, grid=(t // tm,),
        in_specs=[pl.BlockSpec((tm, k), lambda i: (i, 0)), _resident((None, k, d), lambda i: (layer, 0, 0)), _row(d), blk, _row(d)],
        out_specs=[blk, blk], out_shape=[SDS((t, d), BF16), SDS((t, d), F32)], compiler_params=_params(1),
    )(a, w, b, x, gate)


def _outproj_bwd(dxo, y, gate, w, layer, name):
    t, d = dxo.shape
    k = w.shape[1]
    tm, tk = _tile(t, _ROW_TILE, 16), _tile(k, _COL_TILE, _LANES)

    def body(dxo_ref, y_ref, g_ref, w_ref, dy_ref, da_ref, dg_ref, db_ref):
        @pl.when(pl.program_id(0) == 0)
        def _():
            dg_ref[...] = jnp.zeros_like(dg_ref)
            db_ref[...] = jnp.zeros_like(db_ref)

        dxf = dxo_ref[...]
        dyf = dxf * g_ref[...]
        dy_ref[...] = dyf.astype(BF16)
        dg_ref[...] += jnp.sum(dxf * y_ref[...].astype(F32), axis=0, keepdims=True)
        db_ref[...] += jnp.sum(dyf, axis=0, keepdims=True)
        for c0 in range(0, k, tk):
            da_ref[:, c0:c0 + tk] = _nt(dy_ref[...], w_ref[c0:c0 + tk, :]).astype(BF16)

    full = pl.BlockSpec((tm, d), lambda i: (i, 0))
    return pl.pallas_call(
        body, name=name, grid=(t // tm,),
        in_specs=[full, full, _row(d), _resident((None, k, d), lambda i: (layer, 0, 0))],
        out_specs=[full, pl.BlockSpec((tm, k), lambda i: (i, 0)), _row(d), _row(d)],
        out_shape=[SDS((t, d), BF16), SDS((t, k), BF16), SDS((1, d), F32), SDS((1, d), F32)],
        compiler_params=_params(1),
    )(dxo, y, gate, w)


def _matmul_tn(a, b, name):
    t, ka = a.shape
    nb = b.shape[1]
    tka, tnb, tt = _tile(ka, _WGRAD_TILE, _LANES), _tile(nb, _WGRAD_TILE, _LANES), _tile(t, 2 * _ROW_TILE, 16)
    nt = t // tt

    def body(a_ref, b_ref, o_ref, acc):
        s = pl.program_id(2)

        @pl.when(s == 0)
        def _():
            acc[...] = jnp.zeros_like(acc)

        acc[...] += _tn(a_ref[...], b_ref[...])

        @pl.when(s == nt - 1)
        def _():
            o_ref[...] = acc[...].astype(BF16)

    return pl.pallas_call(
        body, name=name, grid=(ka // tka, nb // tnb, nt),
        in_specs=[pl.BlockSpec((tt, tka), lambda i, j, s: (s, i)), pl.BlockSpec((tt, tnb), lambda i, j, s: (s, j))],
        out_specs=pl.BlockSpec((tka, tnb), lambda i, j, s: (i, j)),
        out_shape=SDS((ka, nb), BF16), scratch_shapes=[pltpu.VMEM((tka, tnb), F32)], compiler_params=_params(3),
    )(a, b)


def _inproj_bwd(dz, w, layer, x, dxo, gn, sc, name):
    t, n = dz.shape
    d = x.shape[1]
    tm, tk = _tile(t, _ROW_TILE, 16), _tile(n, _COL_TILE, _LANES)

    def body(dz_ref, w_ref, x_ref, dxo_ref, gn_ref, sc_ref, dx_ref, dgn_ref, dsc_ref, dsh_ref, acc):
        @pl.when(pl.program_id(0) == 0)
        def _():
            dgn_ref[...] = jnp.zeros_like(dgn_ref)
            dsc_ref[...] = jnp.zeros_like(dsc_ref)
            dsh_ref[...] = jnp.zeros_like(dsh_ref)

        for c0 in range(0, n, tk):
            part = _nt(dz_ref[:, c0:c0 + tk], w_ref[:, c0:c0 + tk])
            if c0 == 0:
                acc[...] = part
            else:
                acc[...] += part
        dh = acc[...]
        xf = x_ref[...]
        r = lax.rsqrt(jnp.mean(xf * xf, axis=-1, keepdims=True) + _NORM_EPS)
        xn = xf * r
        gnv = gn_ref[...]
        dsh_ref[...] += jnp.sum(dh, axis=0, keepdims=True)
        dsc_ref[...] += jnp.sum(dh * (xn * gnv), axis=0, keepdims=True)
        drn = dh * (1.0 + sc_ref[...])
        dgn_ref[...] += jnp.sum(drn * xn, axis=0, keepdims=True)
        dxn = drn * gnv
        dx_ref[...] = dxo_ref[...] + r * (dxn - xn * jnp.mean(dxn * xn, axis=-1, keepdims=True))

    full = pl.BlockSpec((tm, d), lambda i: (i, 0))
    return pl.pallas_call(
        body, name=name, grid=(t // tm,),
        in_specs=[pl.BlockSpec((tm, n), lambda i: (i, 0)), _resident((None, d, n), lambda i: (layer, 0, 0)),
                  full, full, _row(d), _row(d)],
        out_specs=[full, _row(d), _row(d), _row(d)],
        out_shape=[SDS((t, d), F32), SDS((1, d), F32), SDS((1, d), F32), SDS((1, d), F32)],
        scratch_shapes=[pltpu.VMEM((tm, d), F32)], compiler_params=_params(1),
    )(dz, w, x, dxo, gn, sc)


def _final_loss(x, g, target, name):
    t, d = x.shape
    tm = _tile(t, _ROW_TILE, 8)

    def body(x_ref, g_ref, t_ref, loss_ref, dx_ref, dg_ref):
        @pl.when(pl.program_id(0) == 0)
        def _():
            loss_ref[...] = jnp.zeros_like(loss_ref)
            dg_ref[...] = jnp.zeros_like(dg_ref)

        xf = x_ref[...]
        r = lax.rsqrt(jnp.mean(xf * xf, axis=-1, keepdims=True) + _NORM_EPS)
        xn = xf * r
        gv = g_ref[...]
        e = xn * gv - t_ref[...]
        per_row = jnp.mean(e * e, axis=-1, keepdims=True)
        loss_ref[...] += 0.5 * jnp.sum(per_row, axis=0, keepdims=True)
        dy = e * (1.0 / d)
        dg_ref[...] += jnp.sum(dy * xn, axis=0, keepdims=True)
        dxn = dy * gv
        dx_ref[...] = r * (dxn - xn * jnp.mean(dxn * xn, axis=-1, keepdims=True))

    full = pl.BlockSpec((tm, d), lambda i: (i, 0))
    return pl.pallas_call(
        body, name=name, grid=(t // tm,), in_specs=[full, _row(d), full],
        out_specs=[_row(_LANES), full, _row(d)],
        out_shape=[SDS((1, _LANES), F32), SDS((t, d), F32), SDS((1, d), F32)], compiler_params=_params(1),
    )(x, g, target)


def _attn_tables(nh):
    group = nh // _N_KV
    slopes = 2.0 ** (-8.0 * jnp.arange(1, nh + 1, dtype=F32) / nh)
    qpos = jnp.arange(_ATTN_BLOCK) + _ATTN_BLOCK
    kpos = jnp.arange(2 * _ATTN_BLOCK)
    dist = qpos[:, None] - kpos[None, :]
    band = (dist >= 0) & (dist < _ATTN_BLOCK)
    bias = jnp.where(band[None], -slopes[:, None, None] * dist.astype(F32)[None], _NEG_INF)
    first = jnp.where((kpos < _ATTN_BLOCK)[None, None, :], _NEG_INF, bias)
    return jnp.stack([first, bias]).reshape(2, _N_KV, group * _ATTN_BLOCK, 2 * _ATTN_BLOCK)


def _attn_probs(q4, k2, tab, sink_ref, kv, group):
    hd, blk = _HEAD_DIM, _ATTN_BLOCK
    s = _nt(q4, k2) * (hd ** -0.5) + tab
    sink = jnp.concatenate([jnp.full((blk, 1), sink_ref[kv * group + g], F32) for g in range(group)], axis=0)
    m = jnp.maximum(jnp.max(s, axis=-1, keepdims=True), sink)
    e = jnp.exp(s - m)
    es = jnp.exp(sink - m)
    inv = 1.0 / (jnp.sum(e, axis=-1, keepdims=True) + es)
    return e * inv, es * inv


def _attn_operands(refs, b, kv, group):
    q_ref, kp_ref, kc_ref, vp_ref, vc_ref = refs
    hd, blk = _HEAD_DIM, _ATTN_BLOCK
    rows, cs = slice(b * blk, (b + 1) * blk), slice(kv * hd, (kv + 1) * hd)
    before = slice((b - 1) * blk, b * blk)
    k2 = jnp.concatenate([kp_ref[:, cs] if b == 0 else kc_ref[before, cs], kc_ref[rows, cs]], axis=0)
    v2 = jnp.concatenate([vp_ref[:, cs] if b == 0 else vc_ref[before, cs], vc_ref[rows, cs]], axis=0)
    q4 = jnp.concatenate([q_ref[rows, (kv * group + g) * hd:(kv * group + g + 1) * hd] for g in range(group)], axis=0)
    return q4, k2, v2


def _attn_specs(nh, nblk, step):
    group = nh // _N_KV
    blk, kvw = _ATTN_BLOCK, _N_KV * _HEAD_DIM
    prev = lambda i: jnp.maximum(step(i) * nblk - 1, 0)
    return [
        pl.BlockSpec((nblk * blk, nh * _HEAD_DIM), lambda i: (step(i), 0)),
        pl.BlockSpec((blk, kvw), lambda i: (prev(i), group)),
        pl.BlockSpec((nblk * blk, kvw), lambda i: (step(i), group)),
        pl.BlockSpec((blk, kvw), lambda i: (prev(i), group + 1)),
        pl.BlockSpec((nblk * blk, kvw), lambda i: (step(i), group + 1)),
        pl.BlockSpec((2, _N_KV, group * blk, 2 * blk), lambda i: (0, 0, 0, 0)),
        pl.BlockSpec(memory_space=pltpu.SMEM),
    ]


def _attn_fwd(qkv, tables, sinks, nh, name, carry=None):
    t = qkv.shape[0]
    group, hd, blk = nh // _N_KV, _HEAD_DIM, _ATTN_BLOCK
    nblk = _ATTN_BLOCKS_PER_STEP if t % (_ATTN_BLOCKS_PER_STEP * blk) == 0 else 1

    def body(q_ref, kp_ref, kc_ref, vp_ref, vc_ref, tab_ref, sink_ref, o_ref):
        i = pl.program_id(0)
        for b in range(nblk):
            tab = tab_ref.at[jnp.minimum(i * nblk + b, 1)]
            outs = [None] * nh
            for kv in range(_N_KV):
                q4, k2, v2 = _attn_operands((q_ref, kp_ref, kc_ref, vp_ref, vc_ref), b, kv, group)
                p, _ = _attn_probs(q4, k2, tab[kv], sink_ref, kv, group)
                o4 = jnp.dot(p.astype(BF16), v2, preferred_element_type=F32)
                for g in range(group):
                    outs[kv * group + g] = o4[g * blk:(g + 1) * blk, :]
            o_ref[b * blk:(b + 1) * blk, :] = jnp.concatenate(outs, axis=1).astype(BF16)

    return _carrying(
        carry, body, name=name, grid=(t // (nblk * blk),), in_specs=_attn_specs(nh, nblk, lambda i: i),
        out_specs=pl.BlockSpec((nblk * blk, nh * hd), lambda i: (i, 0)), out_shape=SDS((t, nh * hd), BF16),
        compiler_params=_params(1),
    )(qkv, qkv, qkv, qkv, qkv, tables, sinks)


def _attn_bwd(qkv, do, tables, sinks, nh, name, carry=None):
    t, wq = qkv.shape
    group, hd, blk = nh // _N_KV, _HEAD_DIM, _ATTN_BLOCK
    nblk = _ATTN_BLOCKS_PER_STEP if t % (_ATTN_BLOCKS_PER_STEP * blk) == 0 else 1
    nsteps = t // (nblk * blk)
    kvw = _N_KV * hd
    step = lambda i: nsteps - 1 - i

    def body(q_ref, kp_ref, kc_ref, vp_ref, vc_ref, tab_ref, sink_ref, do_ref, dqkv_ref, dsink_ref, db_ref, ck, cv):
        i = pl.program_id(0)

        @pl.when(i == 0)
        def _():
            ck[...] = jnp.zeros_like(ck)
            cv[...] = jnp.zeros_like(cv)
            dsink_ref[...] = jnp.zeros_like(dsink_ref)
            db_ref[...] = jnp.zeros_like(db_ref)

        lane = lax.broadcasted_iota(jnp.int32, (1, _LANES), 1)
        carry_k = [ck[:, kv * hd:(kv + 1) * hd] for kv in range(_N_KV)]
        carry_v = [cv[:, kv * hd:(kv + 1) * hd] for kv in range(_N_KV)]
        dsink = jnp.zeros((1, _LANES), F32)
        dbias = jnp.zeros((1, wq), F32)
        for b in reversed(range(nblk)):
            rows = slice(b * blk, (b + 1) * blk)
            tab = tab_ref.at[jnp.minimum(step(i) * nblk + b, 1)]
            dq, dk, dv = [None] * nh, [None] * _N_KV, [None] * _N_KV
            for kv in range(_N_KV):
                q4, k2, v2 = _attn_operands((q_ref, kp_ref, kc_ref, vp_ref, vc_ref), b, kv, group)
                p, ps = _attn_probs(q4, k2, tab[kv], sink_ref, kv, group)
                do4 = jnp.concatenate([do_ref[rows, (kv * group + g) * hd:(kv * group + g + 1) * hd] for g in range(group)], axis=0)
                dp = _nt(do4, v2)
                dl = jnp.sum(p * dp, axis=-1, keepdims=True)
                ds = (p * (dp - dl)).astype(BF16)
                dsk = -ps * dl
                for g in range(group):
                    dsink = dsink + jnp.where(lane == kv * group + g, jnp.sum(dsk[g * blk:(g + 1) * blk, :]), 0.0)
                dq4 = jnp.dot(ds, k2, preferred_element_type=F32) * (hd ** -0.5)
                for g in range(group):
                    dq[kv * group + g] = dq4[g * blk:(g + 1) * blk, :]
                dk2 = _tn(q4, ds).T * (hd ** -0.5)
                dv2 = _tn(do4, p.astype(BF16)).T
                dk[kv] = dk2[blk:, :] + carry_k[kv]
                dv[kv] = dv2[blk:, :] + carry_v[kv]
                carry_k[kv], carry_v[kv] = dk2[:blk, :], dv2[:blk, :]
            dqkv = jnp.concatenate(dq + dk + dv, axis=1)
            dqkv_ref[rows, :] = dqkv.astype(BF16)
            dbias = dbias + jnp.sum(dqkv, axis=0, keepdims=True)
        for kv in range(_N_KV):
            ck[:, kv * hd:(kv + 1) * hd] = carry_k[kv]
            cv[:, kv * hd:(kv + 1) * hd] = carry_v[kv]
        db_ref[...] += dbias
        dsink_ref[...] += dsink

    return _carrying(
        carry, body, name=name, grid=(nsteps,),
        in_specs=_attn_specs(nh, nblk, step) + [pl.BlockSpec((nblk * blk, nh * hd), lambda i: (step(i), 0))],
        out_specs=[pl.BlockSpec((nblk * blk, wq), lambda i: (step(i), 0)), _row(_LANES), _row(wq)],
        out_shape=[SDS((t, wq), BF16), SDS((1, _LANES), F32), SDS((1, wq), F32)],
        scratch_shapes=[pltpu.VMEM((blk, kvw), F32), pltpu.VMEM((blk, kvw), F32)], compiler_params=_params(1),
    )(qkv, qkv, qkv, qkv, qkv, tables, sinks, do)


_SUBLANES = 8


_ALL_SHIFTS = tuple(range(1, _SUBLANES))


def _shift_copies(src_ref, sh_ref, shifts=_ALL_SHIFTS):
    rows = src_ref.shape[0]
    full = src_ref[...]
    for n, b in enumerate(shifts):
        sh_ref[n] = pltpu.roll(full, rows - b, axis=0)
    return sh_ref, shifts


def _rows_at(src_ref, shifted, off, r0, rg, cs):
    b = off % _SUBLANES
    if shifted is None or b not in shifted[1]:
        return src_ref[r0 + off:r0 + off + rg, cs]
    return shifted[0][shifted[1].index(b), r0 + off - b:r0 + off - b + rg, cs]


def _taps(src_ref, w_ref, dst_ref, n_rows, width, offs, rg, shifted=None):
    cg = _tile(width, 512, _LANES)
    for c0 in range(0, width, cg):
        cs = slice(c0, c0 + cg)
        wk = [w_ref[k:k + 1, cs] for k, _ in offs]
        for r0 in range(0, n_rows, rg):
            acc = None
            for (_, off), wv in zip(offs, wk):
                term = wv * _rows_at(src_ref, shifted, off, r0, rg, cs)
                acc = term if acc is None else acc + term
            dst_ref[r0:r0 + rg, cs] = acc


def _tap_grads(dy_ref, z_ref, out_ref, n_rows, width, offs, rg, shifted=None):
    cg = _tile(width, 512, _LANES)
    for c0 in range(0, width, cg):
        cs = slice(c0, c0 + cg)
        for k, off in offs:
            acc = None
            for r0 in range(0, n_rows, rg):
                term = dy_ref[r0:r0 + rg, cs] * _rows_at(z_ref, shifted, off, r0, rg, cs)
                acc = term if acc is None else acc + term
            out_ref[k:k + 1, cs] += jnp.sum(acc, axis=0, keepdims=True)


def _conv_mid_fwd(ag, dw, dwb, lng, lnb, name, carry=None):
    t, c2 = ag.shape
    c = c2 // 2
    kw = dw.shape[0]
    hl = _CONV_HALO
    tm = _tile(t, 256, hl)
    per = tm // hl

    def body(agp_ref, ag_ref, dw_ref, dwb_ref, lng_ref, lnb_ref, o_ref, zext, yb, zsh):
        i = pl.program_id(0)
        glu = lambda ref: ref[:, :c].astype(F32) * _sigmoid(ref[:, c:].astype(F32))
        zext[0:hl, :] = jnp.where(i > 0, glu(agp_ref), 0.0)
        zext[hl:, :] = glu(ag_ref)
        _taps(zext, dw_ref, yb, tm, c, [(k, hl - (kw - 1) + k) for k in range(kw)], 32, _shift_copies(zext, zsh))
        y = yb[...] + dwb_ref[...]
        mu = jnp.mean(y, axis=-1, keepdims=True)
        yc = y - mu
        rstd = lax.rsqrt(jnp.mean(yc * yc, axis=-1, keepdims=True) + _NORM_EPS)
        ln = yc * rstd * lng_ref[...] + lnb_ref[...]
        o_ref[...] = (ln * _sigmoid(ln)).astype(BF16)

    return _carrying(
        carry, body, name=name, grid=(t // tm,),
        in_specs=[pl.BlockSpec((hl, c2), lambda i: (jnp.maximum(i * per - 1, 0), 0)), pl.BlockSpec((tm, c2), lambda i: (i, 0)),
                  pl.BlockSpec((kw, c), lambda i: (0, 0)), _row(c), _row(c), _row(c)],
        out_specs=pl.BlockSpec((tm, c), lambda i: (i, 0)), out_shape=SDS((t, c), BF16),
        scratch_shapes=[pltpu.VMEM((hl + tm, c), F32), pltpu.VMEM((tm, c), F32), pltpu.VMEM((_SUBLANES - 1, hl + tm, c), F32)],
        compiler_params=_params(1),
    )(ag, ag, dw, dwb, lng, lnb)


def _conv_mid_bwd(ag, dzc, dw, dwb, lng, lnb, name, carry=None):
    t, c2 = ag.shape
    c = c2 // 2
    kw = dw.shape[0]
    hl = _CONV_HALO
    tm = _tile(t, 256, hl)
    per = tm // hl
    nt = t // tm
    last_halo = t // hl - 1

    def body(agp_ref, ag_ref, agn_ref, dzc_ref, dzcn_ref, dw_ref, dwb_ref, lng_ref, lnb_ref,
             dag_ref, ddw_ref, ddwb_ref, dlng_ref, dlnb_ref, dbin_ref, zext, yext, dyext, dzb, zsh, dysh):
        i = pl.program_id(0)

        @pl.when(i == 0)
        def _():
            for r in (ddw_ref, ddwb_ref, dlng_ref, dlnb_ref, dbin_ref):
                r[...] = jnp.zeros_like(r)

        glu = lambda ref: ref[:, :c].astype(F32) * _sigmoid(ref[:, c:].astype(F32))
        zext[0:hl, :] = jnp.where(i > 0, glu(agp_ref), 0.0)
        zext[hl:hl + tm, :] = glu(ag_ref)
        zext[hl + tm:, :] = glu(agn_ref)
        fwd_offs = [(k, hl - (kw - 1) + k) for k in range(kw)]
        z_shifted = _shift_copies(zext, zsh)
        _taps(zext, dw_ref, yext, tm + hl, c, fwd_offs, 32, z_shifted)
        y = yext[...] + dwb_ref[...]
        mu = jnp.mean(y, axis=-1, keepdims=True)
        yc = y - mu
        rstd = lax.rsqrt(jnp.mean(yc * yc, axis=-1, keepdims=True) + _NORM_EPS)
        xhat = yc * rstd
        lngv = lng_ref[...]
        ln = xhat * lngv + lnb_ref[...]
        sg = _sigmoid(ln)
        dz_out = jnp.concatenate([dzc_ref[...].astype(F32), jnp.where(i < nt - 1, dzcn_ref[...].astype(F32), 0.0)], axis=0)
        dln = dz_out * (sg * (1.0 + ln * (1.0 - sg)))
        dlng_ref[...] += jnp.sum((dln * xhat)[:tm], axis=0, keepdims=True)
        dlnb_ref[...] += jnp.sum(dln[:tm], axis=0, keepdims=True)
        dxh = dln * lngv
        dy = rstd * (dxh - jnp.mean(dxh, axis=-1, keepdims=True) - xhat * jnp.mean(dxh * xhat, axis=-1, keepdims=True))
        dyext[...] = dy
        ddwb_ref[...] += jnp.sum(dy[:tm], axis=0, keepdims=True)
        _taps(dyext, dw_ref, dzb, tm, c, [(k, kw - 1 - k) for k in range(kw)], 32, _shift_copies(dyext, dysh))
        _tap_grads(dyext, zext, ddw_ref, tm, c, fwd_offs, 32, z_shifted)
        a = ag_ref[:, :c].astype(F32)
        sgg = _sigmoid(ag_ref[:, c:].astype(F32))
        dz = dzb[...]
        da = dz * sgg
        dg = dz * a * sgg * (1.0 - sgg)
        dag_ref[:, :c] = da.astype(BF16)
        dag_ref[:, c:] = dg.astype(BF16)
        dbin_ref[:, :c] += jnp.sum(da, axis=0, keepdims=True)
        dbin_ref[:, c:] += jnp.sum(dg, axis=0, keepdims=True)

    prev = lambda i: (jnp.maximum(i * per - 1, 0), 0)
    nxt = lambda i: (jnp.minimum((i + 1) * per, last_halo), 0)
    return _carrying(
        carry, body, name=name, grid=(nt,),
        in_specs=[pl.BlockSpec((hl, c2), prev), pl.BlockSpec((tm, c2), lambda i: (i, 0)), pl.BlockSpec((hl, c2), nxt),
                  pl.BlockSpec((tm, c), lambda i: (i, 0)), pl.BlockSpec((hl, c), nxt),
                  pl.BlockSpec((kw, c), lambda i: (0, 0)), _row(c), _row(c), _row(c)],
        out_specs=[pl.BlockSpec((tm, c2), lambda i: (i, 0)), pl.BlockSpec((kw, c), lambda i: (0, 0)), _row(c), _row(c), _row(c), _row(c2)],
        out_shape=[SDS((t, c2), BF16), SDS((kw, c), F32), SDS((1, c), F32), SDS((1, c), F32), SDS((1, c), F32), SDS((1, c2), F32)],
        scratch_shapes=[pltpu.VMEM((hl + tm + hl, c), F32), pltpu.VMEM((tm + hl, c), F32), pltpu.VMEM((tm + hl, c), F32),
                        pltpu.VMEM((tm, c), F32), pltpu.VMEM((_SUBLANES - 1, hl + tm + hl, c), F32),
                        pltpu.VMEM((_SUBLANES - 1, tm + hl, c), F32)],
        compiler_params=_params(1),
    )(ag, ag, ag, dzc, dzc, dw, dwb, lng, lnb)


def _ffn_mid_fwd(zf, dw, dwb, name):
    t, f2 = zf.shape
    f = f2 // 2
    kw = dw.shape[0]
    hl = _FFN_HALO
    tm = _tile(t, 256, hl)
    per = tm // hl
    tc = _tile(f, 256, _LANES)
    offs = [(k, hl - (kw - 1) + k) for k in range(kw)]
    shifts = tuple(sorted({off % _SUBLANES for _, off in offs} - {0}))

    def body(zp_ref, z_ref, dw_ref, dwb_ref, o_ref, zext, cb, zsh):
        i = pl.program_id(0)
        zext[0:hl, :] = jnp.where(i > 0, zp_ref[...].astype(F32), 0.0)
        zext[hl:, :] = z_ref[...].astype(F32)
        _taps(zext, dw_ref, cb, tm, f2, offs, 16, _shift_copies(zext, zsh, shifts))
        for c0 in range(0, f, tc):
            g = cb[:, c0:c0 + tc] + dwb_ref[:, c0:c0 + tc]
            u = cb[:, f + c0:f + c0 + tc] + dwb_ref[:, f + c0:f + c0 + tc]
            o_ref[:, c0:c0 + tc] = (g * _sigmoid(g) * u).astype(BF16)

    return pl.pallas_call(
        body, name=name, grid=(t // tm,),
        in_specs=[pl.BlockSpec((hl, f2), lambda i: (jnp.maximum(i * per - 1, 0), 0)), pl.BlockSpec((tm, f2), lambda i: (i, 0)),
                  pl.BlockSpec((kw, f2), lambda i: (0, 0)), _row(f2)],
        out_specs=pl.BlockSpec((tm, f), lambda i: (i, 0)), out_shape=SDS((t, f), BF16),
        scratch_shapes=[pltpu.VMEM((hl + tm, f2), F32), pltpu.VMEM((tm, f2), F32), pltpu.VMEM((len(shifts), hl + tm, f2), F32)],
        compiler_params=_params(1),
    )(zf, zf, dw, dwb)


def _ffn_mid_bwd(zf, dact, dw, dwb, name):
    t, f2 = zf.shape
    f = f2 // 2
    kw = dw.shape[0]
    hl = _FFN_HALO
    tm = _tile(t, 128, hl)
    per = tm // hl
    nt = t // tm
    last_halo = t // hl - 1
    tc = _tile(f, 256, _LANES)
    fwd_offs = [(k, hl - (kw - 1) + k) for k in range(kw)]
    bwd_offs = [(k, kw - 1 - k) for k in range(kw)]
    fwd_shifts = tuple(sorted({off % _SUBLANES for _, off in fwd_offs} - {0}))
    bwd_shifts = tuple(sorted({off % _SUBLANES for _, off in bwd_offs} - {0}))

    def body(zp_ref, z_ref, zn_ref, da_ref, dan_ref, dw_ref, dwb_ref, dzf_ref, ddw_ref, ddwb_ref, zext, cext, dcext, dzb,
             zsh, dcsh):
        i = pl.program_id(0)

        @pl.when(i == 0)
        def _():
            ddw_ref[...] = jnp.zeros_like(ddw_ref)
            ddwb_ref[...] = jnp.zeros_like(ddwb_ref)

        zext[0:hl, :] = jnp.where(i > 0, zp_ref[...].astype(F32), 0.0)
        zext[hl:hl + tm, :] = z_ref[...].astype(F32)
        zext[hl + tm:, :] = zn_ref[...].astype(F32)
        z_shifted = _shift_copies(zext, zsh, fwd_shifts)
        _taps(zext, dw_ref, cext, tm + hl, f2, fwd_offs, 16, z_shifted)
        for c0 in range(0, f, tc):
            g = cext[:, c0:c0 + tc] + dwb_ref[:, c0:c0 + tc]
            u = cext[:, f + c0:f + c0 + tc] + dwb_ref[:, f + c0:f + c0 + tc]
            da = jnp.concatenate([da_ref[:, c0:c0 + tc].astype(F32),
                                  jnp.where(i < nt - 1, dan_ref[:, c0:c0 + tc].astype(F32), 0.0)], axis=0)
            sg = _sigmoid(g)
            dcg = da * u * (sg * (1.0 + g * (1.0 - sg)))
            dcu = da * (g * sg)
            dcext[:, c0:c0 + tc] = dcg
            dcext[:, f + c0:f + c0 + tc] = dcu
            ddwb_ref[:, c0:c0 + tc] += jnp.sum(dcg[:tm], axis=0, keepdims=True)
            ddwb_ref[:, f + c0:f + c0 + tc] += jnp.sum(dcu[:tm], axis=0, keepdims=True)
        _taps(dcext, dw_ref, dzb, tm, f2, bwd_offs, 16, _shift_copies(dcext, dcsh, bwd_shifts))
        _tap_grads(dcext, zext, ddw_ref, tm, f2, fwd_offs, 16, z_shifted)
        dzf_ref[...] = dzb[...].astype(BF16)

    prev = lambda i: (jnp.maximum(i * per - 1, 0), 0)
    nxt = lambda i: (jnp.minimum((i + 1) * per, last_halo), 0)
    return pl.pallas_call(
        body, name=name, grid=(nt,),
        in_specs=[pl.BlockSpec((hl, f2), prev), pl.BlockSpec((tm, f2), lambda i: (i, 0)), pl.BlockSpec((hl, f2), nxt),
                  pl.BlockSpec((tm, f), lambda i: (i, 0)), pl.BlockSpec((hl, f), nxt),
                  pl.BlockSpec((kw, f2), lambda i: (0, 0)), _row(f2)],
        out_specs=[pl.BlockSpec((tm, f2), lambda i: (i, 0)), pl.BlockSpec((kw, f2), lambda i: (0, 0)), _row(f2)],
        out_shape=[SDS((t, f2), BF16), SDS((kw, f2), F32), SDS((1, f2), F32)],
        scratch_shapes=[pltpu.VMEM((hl + tm + hl, f2), F32), pltpu.VMEM((tm + hl, f2), F32), pltpu.VMEM((tm + hl, f2), F32),
                        pltpu.VMEM((tm, f2), F32), pltpu.VMEM((len(fwd_shifts), hl + tm + hl, f2), F32),
                        pltpu.VMEM((len(bwd_shifts), tm + hl, f2), F32)],
        compiler_params=_params(1),
    )(zf, zf, zf, dact, dact, dw, dwb)


_INV_SQRT2 = 0.7071067811865476
_INV_SQRT_2PI = 0.3989422804014327


def _sgu_common(zin_ref, lng_ref, lnb_ref, hh):
    z = zin_ref[...].astype(F32)
    cdf = 0.5 * (1.0 + lax.erf(z * _INV_SQRT2))
    ge = z * cdf
    u, v = ge[:, :hh], ge[:, hh:]
    mu = jnp.mean(v, axis=-1, keepdims=True)
    vc = v - mu
    rstd = lax.rsqrt(jnp.mean(vc * vc, axis=-1, keepdims=True) + _NORM_EPS)
    vhat = vc * rstd
    vn = vhat * lng_ref[...] + lnb_ref[...]
    return z, cdf, u, vhat, rstd, vn


def _sgu_wm(ws_ref, g, ch):
    rows = lax.broadcasted_iota(jnp.int32, (ch, ch), 0)
    cols = lax.broadcasted_iota(jnp.int32, (ch, ch), 1)
    return jnp.where(rows >= cols, ws_ref[g], 0.0).astype(BF16)


def _sgu_mid_fwd(zin, lng, lnb, ws, bs_t, name, carry=None):
    t, h2 = zin.shape
    hh = h2 // 2
    ng, ch = ws.shape[0], ws.shape[1]
    hg = hh // ng
    tm = _tile(t, 256, ch)

    def body(zin_ref, lng_ref, lnb_ref, ws_ref, bs_ref, o_ref):
        _, _, u, _, _, vn = _sgu_common(zin_ref, lng_ref, lnb_ref, hh)
        vnb = vn.astype(BF16)
        for g in range(ng):
            wm = _sgu_wm(ws_ref, g, ch)
            for cc in range(tm // ch):
                rs, cs = slice(cc * ch, (cc + 1) * ch), slice(g * hg, (g + 1) * hg)
                vv = jnp.dot(wm, vnb[rs, cs], preferred_element_type=F32) + bs_ref[:, g:g + 1]
                o_ref[rs, cs] = (u[rs, cs] * vv).astype(BF16)

    return _carrying(
        carry, body, name=name, grid=(t // tm,),
        in_specs=[pl.BlockSpec((tm, h2), lambda i: (i, 0)), _row(hh), _row(hh),
                  pl.BlockSpec((ng, ch, ch), lambda i: (0, 0, 0)), pl.BlockSpec((ch, ng), lambda i: (0, 0))],
        out_specs=pl.BlockSpec((tm, hh), lambda i: (i, 0)), out_shape=SDS((t, hh), BF16), compiler_params=_params(1),
    )(zin, lng, lnb, ws, bs_t)


def _sgu_mid_bwd(zin, duv, lng, lnb, ws, bs_t, name, carry=None):
    t, h2 = zin.shape
    hh = h2 // 2
    ng, ch = ws.shape[0], ws.shape[1]
    hg = hh // ng
    tm = _tile(t, 128, ch)

    def body(zin_ref, duv_ref, lng_ref, lnb_ref, ws_ref, bs_ref, dzin_ref, dlng_ref, dlnb_ref, dws_ref, dbs_ref, dbin_ref, dvn_s, du_s):
        @pl.when(pl.program_id(0) == 0)
        def _():
            for r in (dlng_ref, dlnb_ref, dws_ref, dbs_ref, dbin_ref):
                r[...] = jnp.zeros_like(r)

        z, cdf, u, vhat, rstd, vn = _sgu_common(zin_ref, lng_ref, lnb_ref, hh)
        vnb = vn.astype(BF16)
        duv = duv_ref[...].astype(F32)
        dvv = (duv * u).astype(BF16)
        lane = lax.broadcasted_iota(jnp.int32, (1, _LANES), 1)
        rows = lax.broadcasted_iota(jnp.int32, (ch, ch), 0)
        cols = lax.broadcasted_iota(jnp.int32, (ch, ch), 1)
        dbs = jnp.zeros((ch, _LANES), F32)
        for g in range(ng):
            wm = _sgu_wm(ws_ref, g, ch)
            dwm = jnp.zeros((ch, ch), F32)
            for cc in range(tm // ch):
                rs, cs = slice(cc * ch, (cc + 1) * ch), slice(g * hg, (g + 1) * hg)
                vv = jnp.dot(wm, vnb[rs, cs], preferred_element_type=F32) + bs_ref[:, g:g + 1]
                du_s[rs, cs] = duv[rs, cs] * vv
                dvn_s[rs, cs] = _tn(wm, dvv[rs, cs])
                dwm = dwm + _nt(dvv[rs, cs], vnb[rs, cs])
                dbs = dbs + jnp.where(lane == g, jnp.sum(dvv[rs, cs].astype(F32), axis=-1, keepdims=True), 0.0)
            dws_ref[g] += jnp.where(rows >= cols, dwm, 0.0)
        dbs_ref[...] += dbs
        dvn = dvn_s[...]
        dlng_ref[...] += jnp.sum(dvn * vhat, axis=0, keepdims=True)
        dlnb_ref[...] += jnp.sum(dvn, axis=0, keepdims=True)
        dxh = dvn * lng_ref[...]
        dv = rstd * (dxh - jnp.mean(dxh, axis=-1, keepdims=True) - vhat * jnp.mean(dxh * vhat, axis=-1, keepdims=True))
        dgelu = cdf + z * (_INV_SQRT_2PI * jnp.exp(-0.5 * z * z))
        dzu = du_s[...] * dgelu[:, :hh]
        dzv = dv * dgelu[:, hh:]
        dzin_ref[:, :hh] = dzu.astype(BF16)
        dzin_ref[:, hh:] = dzv.astype(BF16)
        dbin_ref[:, :hh] += jnp.sum(dzu, axis=0, keepdims=True)
        dbin_ref[:, hh:] += jnp.sum(dzv, axis=0, keepdims=True)

    return _carrying(
        carry, body, name=name, grid=(t // tm,),
        in_specs=[pl.BlockSpec((tm, h2), lambda i: (i, 0)), pl.BlockSpec((tm, hh), lambda i: (i, 0)), _row(hh), _row(hh),
                  pl.BlockSpec((ng, ch, ch), lambda i: (0, 0, 0)), pl.BlockSpec((ch, ng), lambda i: (0, 0))],
        out_specs=[pl.BlockSpec((tm, h2), lambda i: (i, 0)), _row(hh), _row(hh), pl.BlockSpec((ng, ch, ch), lambda i: (0, 0, 0)),
                   pl.BlockSpec((ch, _LANES), lambda i: (0, 0)), _row(h2)],
        out_shape=[SDS((t, h2), BF16), SDS((1, hh), F32), SDS((1, hh), F32), SDS((ng, ch, ch), F32), SDS((ch, _LANES), F32),
                   SDS((1, h2), F32)],
        scratch_shapes=[pltpu.VMEM((tm, hh), F32), pltpu.VMEM((tm, hh), F32)], compiler_params=_params(1),
    )(zin, duv, lng, lnb, ws, bs_t)


def _ada_mod(c_all, ada_w, ada_b, name):
    nl, d, n = ada_w.shape
    nb = c_all.shape[0]
    tn = _tile(n, _COL_TILE, _LANES)

    def body(c_ref, w_ref, b_ref, o_ref):
        cv = c_ref[...]
        ca = cv * _sigmoid(cv)
        o_ref[...] = jnp.dot(ca, w_ref[...], preferred_element_type=F32, precision=lax.Precision.HIGHEST) + b_ref[...]

    return pl.pallas_call(
        body, name=name, grid=(nl, n // tn),
        in_specs=[pl.BlockSpec((nb, d), lambda l, j: (0, 0)), pl.BlockSpec((None, d, tn), lambda l, j: (l, 0, j)),
                  pl.BlockSpec((None, 1, tn), lambda l, j: (l, 0, j))],
        out_specs=pl.BlockSpec((None, nb, tn), lambda l, j: (l, 0, j)), out_shape=SDS((nl, nb, n), F32),
        compiler_params=_params(2),
    )(c_all, ada_w, ada_b)


def _ada_wgrad(c_all_t, dmod, name):
    d, nb = c_all_t.shape
    nl, _, n = dmod.shape
    tn = _tile(n, _COL_TILE, _LANES)

    def body(c_ref, dm_ref, o_ref):
        cv = c_ref[...]
        ca = cv * _sigmoid(cv)
        acc = ca[:, 0:1] * dm_ref[0:1, :]
        for b in range(1, nb):
            acc = acc + ca[:, b:b + 1] * dm_ref[b:b + 1, :]
        o_ref[...] = acc

    return pl.pallas_call(
        body, name=name, grid=(nl, n // tn),
        in_specs=[pl.BlockSpec((d, nb), lambda l, j: (0, 0)), pl.BlockSpec((None, nb, tn), lambda l, j: (l, 0, j))],
        out_specs=pl.BlockSpec((None, d, tn), lambda l, j: (l, 0, j)), out_shape=SDS((nl, d, n), F32),
        compiler_params=_params(2),
    )(c_all_t, dmod)


def _as_rows(a):
    return a.reshape(-1, a.shape[-1])


def _row_tile(r, c, n_arrays):
    budget = _VMEM_LIMIT_BYTES // (4 * 2 * n_arrays * 4)
    return _tile(r, max(8, budget // max(c, 1)), 8)


def _cast_bf16_layer(a, layer, name):
    _, r, c = a.shape
    tr = _row_tile(r, c, 2)

    def body(a_ref, o_ref):
        o_ref[...] = a_ref[...].astype(BF16)

    return pl.pallas_call(body, name=name, grid=(r // tr,), in_specs=[pl.BlockSpec((None, tr, c), lambda i: (layer, i, 0))],
                          out_specs=pl.BlockSpec((tr, c), lambda i: (i, 0)), out_shape=SDS((r, c), BF16),
                          compiler_params=_params(1))(a)


def _sum_slot_layers(parts, name):
    nl = len(parts)
    n, r, c = parts[0].shape
    tr = _row_tile(r, c, nl * (n + 1))

    def body(*refs):
        o_ref = refs[nl]
        for lay in range(nl):
            acc = refs[lay][0].astype(F32)
            for s in range(1, n):
                acc = acc + refs[lay][s].astype(F32)
            o_ref[lay] = acc

    return pl.pallas_call(body, name=name, grid=(r // tr,), in_specs=[pl.BlockSpec((n, tr, c), lambda i: (0, i, 0))] * nl,
                          out_specs=pl.BlockSpec((nl, tr, c), lambda i: (0, i, 0)), out_shape=SDS((nl, r, c), F32),
                          compiler_params=_params(1))(*parts)


def _sum_slots(parts, name):
    n = parts.shape[0]
    p2 = parts.reshape(n, -1, parts.shape[-1])
    r, c = p2.shape[1:]
    tr = _row_tile(r, c, n + 1)

    def body(p_ref, o_ref):
        acc = p_ref[0].astype(F32)
        for s in range(1, n):
            acc = acc + p_ref[s].astype(F32)
        o_ref[...] = acc

    out = pl.pallas_call(body, name=name, grid=(r // tr,), in_specs=[pl.BlockSpec((n, tr, c), lambda i: (0, i, 0))],
                         out_specs=pl.BlockSpec((tr, c), lambda i: (i, 0)), out_shape=SDS((r, c), F32),
                         compiler_params=_params(1))(p2)
    return out.reshape(parts.shape[1:])


def _adamw(w, g_parts, m, v, name):
    w2, m2, v2 = _as_rows(w), _as_rows(m), _as_rows(v)
    gs = [_as_rows(g) for g in g_parts]
    r, c = w2.shape
    ng = len(gs)
    tr = _row_tile(r, c, 7 + ng)
    c1 = 1.0 - _ADAM_B1 ** _ADAM_STEP
    c2 = 1.0 - _ADAM_B2 ** _ADAM_STEP

    def body(*refs):
        w_ref, m_ref, v_ref = refs[:3]
        g_refs = refs[3:3 + ng]
        go_ref, d_ref, mo_ref, vo_ref = refs[3 + ng:]
        g = g_refs[0][...]
        for gr in g_refs[1:]:
            g = g + gr[...]
        mn = _ADAM_B1 * m_ref[...] + (1.0 - _ADAM_B1) * g
        vn = _ADAM_B2 * v_ref[...] + (1.0 - _ADAM_B2) * (g * g)
        go_ref[...] = g
        mo_ref[...] = mn
        vo_ref[...] = vn
        d_ref[...] = -_ADAM_LR * ((mn / c1) / (jnp.sqrt(vn / c2) + _ADAM_EPS) + _ADAM_WD * w_ref[...])

    spec = pl.BlockSpec((tr, c), lambda i: (i, 0))
    outs = pl.pallas_call(body, name=name, grid=(r // tr,), in_specs=[spec] * (3 + ng), out_specs=[spec] * 4,
                          out_shape=[SDS((r, c), F32)] * 4, compiler_params=_params(1))(w2, m2, v2, *gs)
    return tuple(o.reshape(w.shape) for o in outs)


_HBM = pl.BlockSpec(memory_space=pltpu.HBM)
_CHIP_FLIPS = ((1, 0, 0), (0, 1, 0), (1, 1, 0))
_ALL_FLIPS = tuple((a, b, c) for a in (0, 1) for b in (0, 1) for c in (0, 1))[1:]
_SIBLING_FLIP = ((0, 0, 1),)


def _at(ref, idx):
    return ref.at[idx] if idx else ref


def _exchange(name, flips, srcs, out_shapes, items):
    n_in, n_out = len(srcs), len(out_shapes)
    n_remote = len(items) * len(flips)
    n_local = sum(1 for it in items if it[4])

    def body(*refs):
        src_refs, dst_refs = refs[:n_in], refs[n_in:n_in + n_out]
        send_sems, recv_sems, local_sems = refs[n_in + n_out:]
        me = (lax.axis_index("x"), lax.axis_index("y"), lax.axis_index("c"))
        sends, recvs, locs = [], [], []
        n, nl = 0, 0
        for si, di, src_idx, dst_idx, local in items:
            for flip in flips:
                peer = tuple(1 - m if f else m for m, f in zip(me, flip))
                push = pltpu.make_async_remote_copy(
                    src_ref=_at(src_refs[si], src_idx(*peer)), dst_ref=_at(dst_refs[di], dst_idx(*me)),
                    send_sem=send_sems.at[n], recv_sem=recv_sems.at[n], device_id=peer, device_id_type=MESH_IDS)
                push.start()
                sends.append(push)
                recvs.append(pltpu.make_async_remote_copy(
                    src_ref=_at(src_refs[si], src_idx(*me)), dst_ref=_at(dst_refs[di], dst_idx(*peer)),
                    send_sem=send_sems.at[n], recv_sem=recv_sems.at[n], device_id=peer, device_id_type=MESH_IDS))
                n += 1
            if local:
                cp = pltpu.make_async_copy(_at(src_refs[si], src_idx(*me)), _at(dst_refs[di], dst_idx(*me)), local_sems.at[nl])
                cp.start()
                locs.append(cp)
                nl += 1
        for r in recvs:
            r.wait_recv()
        for s in sends:
            s.wait_send()
        for cp in locs:
            cp.wait()

    return pl.pallas_call(
        body, name=name, in_specs=[_HBM] * n_in, out_specs=[_HBM] * n_out, out_shape=list(out_shapes),
        scratch_shapes=[pltpu.SemaphoreType.DMA((n_remote,)), pltpu.SemaphoreType.DMA((n_remote,)),
                        pltpu.SemaphoreType.DMA((max(n_local, 1),))],
    )(*srcs)


_SEM = pl.BlockSpec(memory_space=pltpu.SEMAPHORE)
_SIDE_EFFECTS = pltpu.SideEffectType.DATAFLOW_SIDE_EFFECTING


def _split_copies(src_refs, land_refs, send_sems, recv_sems, items):
    me = (lax.axis_index("x"), lax.axis_index("y"), lax.axis_index("c"))
    pushes, arrivals, n = [], [], 0
    for si, di, src_idx, dst_idx in items:
        for flip in _CHIP_FLIPS:
            peer = tuple(1 - m if f else m for m, f in zip(me, flip))
            pushes.append(pltpu.make_async_remote_copy(
                src_ref=_at(src_refs[si], src_idx(*peer)), dst_ref=_at(land_refs[di], dst_idx(*me)),
                send_sem=send_sems.at[n], recv_sem=recv_sems.at[n], device_id=peer, device_id_type=MESH_IDS))
            arrivals.append(pltpu.make_async_remote_copy(
                src_ref=_at(src_refs[si], src_idx(*me)), dst_ref=_at(land_refs[di], dst_idx(*peer)),
                send_sem=send_sems.at[n], recv_sem=recv_sems.at[n], device_id=peer, device_id_type=MESH_IDS))
            n += 1
    return me, pushes, arrivals


def _push_start(name, srcs, land_shapes, items, after):
    n_src, n_land = len(srcs), len(land_shapes)
    n_buf = n_src + n_land
    n_remote = len(items) * len(_CHIP_FLIPS)

    def body(*refs):
        src_refs, land_refs = refs[:n_src], refs[n_src:n_buf]
        send_sems, recv_sems = refs[n_buf + 1], refs[n_buf + 2]
        local_sems = refs[2 * n_buf + 4]
        me, pushes, _ = _split_copies(src_refs, land_refs, send_sems, recv_sems, items)
        own = [pltpu.make_async_copy(_at(src_refs[si], src_idx(*me)), _at(land_refs[di], dst_idx(*me)), local_sems.at[k])
               for k, (si, di, src_idx, dst_idx) in enumerate(items)]
        for cp in own:
            cp.start()
        for cp in own:
            cp.wait()
        for p in pushes:
            p.start()

    bufs = list(srcs) + [lax.empty(s.shape, s.dtype) for s in land_shapes]
    outs = pl.pallas_call(
        body, name=name, in_specs=[_HBM] * n_buf + [pl.BlockSpec(memory_space=pl.ANY)],
        out_shape=[pltpu.SemaphoreType.DMA((n_remote,)), pltpu.SemaphoreType.DMA((n_remote,))]
        + [pltpu.HBM(a.shape, a.dtype) for a in bufs] + [pltpu.HBM((_SUBLANES, _LANES), F32)],
        out_specs=[_SEM, _SEM] + [_HBM] * (n_buf + 1),
        input_output_aliases={k: 2 + k for k in range(n_buf)},
        scratch_shapes=[pltpu.SemaphoreType.DMA((len(items),))],
        compiler_params=pltpu.CompilerParams(has_side_effects=_SIDE_EFFECTS),
    )(*[pltpu.with_memory_space_constraint(a, pltpu.HBM) for a in bufs], after)
    return dict(send=outs[0], recv=outs[1], bufs=list(outs[2:2 + n_buf]), order=outs[2 + n_buf], n_src=n_src, items=items)


def _behind(x, handles, name):
    deps = [h["order"] for h in handles]

    def body(*refs):
        refs[-1][...] = refs[0][...]

    vmem = pl.BlockSpec(memory_space=pltpu.VMEM)
    return pl.pallas_call(body, name=name, in_specs=[vmem] + [pl.BlockSpec(memory_space=pl.ANY)] * len(deps), out_specs=vmem,
                          out_shape=SDS(x.shape, x.dtype))(x, *deps)


def _push_wait(name, handle, after):
    n_src, items = handle["n_src"], handle["items"]
    n_buf = len(handle["bufs"])

    def body(*refs):
        src_refs, land_refs = refs[:n_src], refs[n_src:n_buf]
        send_sems, recv_sems = refs[n_buf], refs[n_buf + 1]
        _, pushes, arrivals = _split_copies(src_refs, land_refs, send_sems, recv_sems, items)
        for p in pushes:
            p.wait_send()
        for a in arrivals:
            a.wait_recv()

    outs = pl.pallas_call(
        body, name=name, in_specs=[_HBM] * n_buf + [_SEM, _SEM, pl.BlockSpec(memory_space=pl.ANY)],
        out_shape=[pltpu.HBM(a.shape, a.dtype) for a in handle["bufs"]], out_specs=[_HBM] * n_buf,
        input_output_aliases={k: k for k in range(n_buf)},
        compiler_params=pltpu.CompilerParams(has_side_effects=_SIDE_EFFECTS),
    )(*handle["bufs"], handle["send"], handle["recv"], after)
    return list(outs[n_src:])


def _chip_of(x, y, c):
    return 2 * x + y


def _dev_of(x, y, c):
    return 4 * x + 2 * y + c


def _window(axis, ndim, size):
    def idx(x, y, c):
        return tuple(pl.ds(_chip_of(x, y, c) * size, size) if a == axis else slice(None) for a in range(ndim))
    return idx


def _whole(x, y, c):
    return ()


def _gather_pushes(shards_axes):
    srcs, shapes, items = [], [], []
    for k, (a, axis) in enumerate(shards_axes):
        full = list(a.shape)
        full[axis] *= _N_CHIPS
        srcs.append(a)
        shapes.append(SDS((1,) + tuple(full), a.dtype))
        window = _window(axis, a.ndim, a.shape[axis])
        items.append((k, k, _whole, (lambda w: lambda x, y, c: (0,) + w(x, y, c))(window)))
    return srcs, shapes, items


def _scatter_pushes(grads_axes):
    srcs, shapes, items = [], [], []
    slot = lambda x, y, c: (_chip_of(x, y, c),)
    for k, (a, axis) in enumerate(grads_axes):
        shard = list(a.shape)
        shard[axis] //= _N_CHIPS
        srcs.append(a)
        shapes.append(SDS((_N_CHIPS,) + tuple(shard), a.dtype))
        items.append((k, k, _window(axis, a.ndim, shard[axis]), slot))
    return srcs, shapes, items


def _gather_all(name, a):
    slot = lambda x, y, c: (_dev_of(x, y, c),)
    return _exchange(name, _ALL_FLIPS, [a], [SDS((_N_DEV,) + a.shape, a.dtype)], [(0, 0, _whole, slot, True)])[0]


def _swap_chips(name, a):
    slot = lambda x, y, c: (_chip_of(x, y, c),)
    return _exchange(name, _CHIP_FLIPS, [a], [SDS(a.shape, a.dtype)], [(0, 0, slot, slot, True)])[0]


def _swap_sibling(name, arrays):
    items = [(k, k, _whole, _whole, False) for k in range(len(arrays))]
    return _exchange(name, _SIBLING_FLIP, list(arrays), [SDS(a.shape, a.dtype) for a in arrays], items)


_BIG = ("attn_wqkv", "attn_wo", "conv_w_in", "conv_w_out", "sgu_w_in", "sgu_w_out", "ffn_w_in", "ffn_w_out")
_BIG_AXIS = {"attn_wqkv": 2, "attn_wo": 1, "conv_w_in": 2, "conv_w_out": 1, "sgu_w_in": 2, "sgu_w_out": 1,
             "ffn_w_in": 2, "ffn_w_out": 1}
_SMALL = {"norm1_g": None, "norm2_g": None, "ada_b": None, "attn_bqkv": 1, "attn_sinks": None, "attn_bo": 1,
          "conv_b_in": None, "conv_dw": 2, "conv_dw_b": None, "conv_ln_g": None, "conv_ln_b": None, "conv_b_out": None,
          "sgu_b_in": 1, "sgu_ln_g": 1, "sgu_ln_b": 1, "sgu_ws": None, "sgu_bs": None, "sgu_b_out": 1,
          "ffn_dw": 2, "ffn_dw_b": None, "final_g": None}
_WEIGHTS = ['norm1_g', 'norm2_g', 'ada_w', 'ada_b', 'attn_wqkv', 'attn_bqkv', 'attn_sinks', 'attn_wo', 'attn_bo',
            'conv_w_in', 'conv_b_in', 'conv_dw', 'conv_dw_b', 'conv_ln_g', 'conv_ln_b', 'conv_w_out', 'conv_b_out',
            'sgu_w_in', 'sgu_b_in', 'sgu_ln_g', 'sgu_ln_b', 'sgu_ws', 'sgu_bs', 'sgu_w_out', 'sgu_b_out',
            'ffn_w_in', 'ffn_dw', 'ffn_dw_b', 'ffn_w_out', 'final_g']
_FLAT_COLS = 1024


def _full_shape(a, axis):
    s = list(a.shape)
    if axis is not None:
        s[axis] *= _N_CHIPS
    return tuple(s)


def _pack(arrays):
    flat = jnp.concatenate([a.reshape(-1).astype(F32) for a in arrays])
    pad = (-flat.shape[0]) % (8 * _FLAT_COLS)
    return jnp.pad(flat, (0, pad)).reshape(-1, _FLAT_COLS)


def _unpack(flat2d, shapes):
    flat = flat2d.reshape(-1)
    out, off = [], 0
    for s in shapes:
        n = math.prod(s)
        out.append(flat[off:off + n].reshape(s))
        off += n
    return out


def _shard_of(full, axis, chip):
    if axis is None:
        return full
    size = full.shape[axis] // _N_CHIPS
    return lax.dynamic_slice_in_dim(full, chip * size, size, axis)


def _unshard(gathered, axis):
    moved = jnp.moveaxis(gathered, 0, axis)
    shape = list(gathered.shape[1:])
    shape[axis] *= _N_CHIPS
    return moved.reshape(shape)


def kernel(x, c, norm1_g, norm2_g, ada_w, ada_b, attn_wqkv, attn_bqkv, attn_sinks, attn_wo, attn_bo, conv_w_in, conv_b_in, conv_dw, conv_dw_b, conv_ln_g, conv_ln_b, conv_w_out, conv_b_out, sgu_w_in, sgu_b_in, sgu_ln_g, sgu_ln_b, sgu_ws, sgu_bs, sgu_w_out, sgu_b_out, ffn_w_in, ffn_dw, ffn_dw_b, ffn_w_out, final_g, loss_target, m_norm1_g, m_norm2_g, m_ada_w, m_ada_b, m_attn_wqkv, m_attn_bqkv, m_attn_sinks, m_attn_wo, m_attn_bo, m_conv_w_in, m_conv_b_in, m_conv_dw, m_conv_dw_b, m_conv_ln_g, m_conv_ln_b, m_conv_w_out, m_conv_b_out, m_sgu_w_in, m_sgu_b_in, m_sgu_ln_g, m_sgu_ln_b, m_sgu_ws, m_sgu_bs, m_sgu_w_out, m_sgu_b_out, m_ffn_w_in, m_ffn_dw, m_ffn_dw_b, m_ffn_w_out, m_final_g, v_norm1_g, v_norm2_g, v_ada_w, v_ada_b, v_attn_wqkv, v_attn_bqkv, v_attn_sinks, v_attn_wo, v_attn_bo, v_conv_w_in, v_conv_b_in, v_conv_dw, v_conv_dw_b, v_conv_ln_g, v_conv_ln_b, v_conv_w_out, v_conv_b_out, v_sgu_w_in, v_sgu_b_in, v_sgu_ln_g, v_sgu_ln_b, v_sgu_ws, v_sgu_bs, v_sgu_w_out, v_sgu_b_out, v_ffn_w_in, v_ffn_dw, v_ffn_dw_b, v_ffn_w_out, v_final_g):
    args = dict(locals())
    wts = {n: args[n] for n in _WEIGHTS}
    mom_m = {n: args["m_" + n] for n in _WEIGHTS}
    mom_v = {n: args["v_" + n] for n in _WEIGHTS}

    ix, iy, ic = lax.axis_index("x"), lax.axis_index("y"), lax.axis_index("c")
    chip = 2 * ix + iy
    xs = x[0]
    tgt = loss_target[0]
    t, d = xs.shape
    nh = d // _HEAD_DIM
    depth = ada_w.shape[0]
    ncols = ada_w.shape[2]
    n_mod = ncols * _N_CHIPS // d

    small_sharded = [n for n in _WEIGHTS if _SMALL.get(n) is not None]
    packed_small = _pack([wts[n] for n in small_sharded])
    mixers = {0: ("attn_wqkv", "attn_wo"), 1: ("conv_w_in", "conv_w_out"), 2: ("sgu_w_in", "sgu_w_out")}

    def shards_of(keys):
        return [(_cast_bf16_layer(wts[n], j, "cast_%s_%d" % (n, j)), _BIG_AXIS[n] - 1) for n, j in keys]

    c_all = _gather_all("gather_c", c)[:, 0, :]
    ada_b_cols = lax.dynamic_slice_in_dim(ada_b, chip * ncols, ncols, 1)[:, None, :]
    mod_cols = _ada_mod(c_all, ada_w, ada_b_cols, "ada_mod")
    mine = lax.dynamic_index_in_dim(mod_cols.reshape(depth, _N_CHIPS, 2, ncols), ic, 2, keepdims=False)
    got = _swap_chips("swap_mod", jnp.moveaxis(mine, 1, 0))

    first = [(n, 0) for n in mixers[0]]
    gather0 = _push_start("start_gather_0", *_gather_pushes(shards_of(first) + [(packed_small, 0)]), got)
    mod = _behind(jnp.moveaxis(got, 0, 1).reshape(depth, n_mod, 1, d), [gather0], "behind_gather_start")
    lands0 = _push_wait("wait_gather_0", gather0, mod)
    wfull = dict(zip(first, lands0))
    small_rows = lands0[-1].reshape(_N_CHIPS, -1, _FLAT_COLS)
    pieces = [_unpack(small_rows[s], [wts[n].shape for n in small_sharded]) for s in range(_N_CHIPS)]
    full = {n: wts[n] for n in _WEIGHTS if n in _SMALL and _SMALL[n] is None}
    for k, n in enumerate(small_sharded):
        full[n] = _unshard(jnp.stack([pieces[s][k] for s in range(_N_CHIPS)]), _SMALL[n])

    tables = _attn_tables(nh)
    zeros_d = jnp.zeros((1, d), F32)
    zeros_f2 = jnp.zeros((1, ffn_w_in.shape[2] * _N_CHIPS), F32)
    row = lambda a: a.reshape(1, -1)

    saved = []
    xcur = xs
    for i in range(depth):
        sh1, sc1, g1, sh2, sc2, g2 = (mod[i, k] for k in range(n_mod))
        kind, j = i % 3, i // 3
        tag = "L%d_" % i
        more = i + 1 < depth
        mid_keys = ([("ffn_w_in", 0), ("ffn_w_out", 0)] if i == 0 else []) + \
                   ([(n, (i + 1) // 3) for n in mixers[(i + 1) % 3]] if more else [])
        mid_carry = _gather_pushes(shards_of(mid_keys)) if mid_keys else None
        in_carry = _gather_pushes(shards_of([("ffn_w_in", i + 1)])) if more else None
        out_carry = _gather_pushes(shards_of([("ffn_w_out", i + 1)])) if more else None
        w_in, w_out = wfull[(mixers[kind][0], j)], wfull[(mixers[kind][1], j)]
        if kind == 0:
            b_in, b_out = row(full["attn_bqkv"][j]), row(full["attn_bo"][j])
        elif kind == 1:
            b_in, b_out = row(full["conv_b_in"][j]), row(full["conv_b_out"][j])
        else:
            b_in, b_out = row(full["sgu_b_in"][j]), row(full["sgu_b_out"][j])
        (h1, z), _ = _norm_mod_matmul(xcur, row(full["norm1_g"][i]), sc1, sh1, w_in, 0, b_in, tag + "mixer_in")
        if kind == 0:
            a, lands = _attn_fwd(z, tables, full["attn_sinks"][j], nh, tag + "attn", mid_carry)
        elif kind == 1:
            a, lands = _conv_mid_fwd(z, full["conv_dw"][j], row(full["conv_dw_b"][j]), row(full["conv_ln_g"][j]),
                                     row(full["conv_ln_b"][j]), tag + "conv_mid", mid_carry)
        else:
            a, lands = _sgu_mid_fwd(z, row(full["sgu_ln_g"][j]), row(full["sgu_ln_b"][j]), full["sgu_ws"][j], full["sgu_bs"][j].T,
                                    tag + "sgu_mid", mid_carry)
        wfull.update(zip(mid_keys, lands))
        (y1, xmid), _ = _matmul_resid(a, w_out, 0, b_out, xcur, g1, tag + "mixer_out")
        (h2, zf), lands = _norm_mod_matmul(xmid, row(full["norm2_g"][i]), sc2, sh2, wfull[("ffn_w_in", i)], 0, zeros_f2,
                                           tag + "ffn_in", in_carry)
        wfull.update(zip([("ffn_w_in", i + 1)], lands))
        act = _ffn_mid_fwd(zf, full["ffn_dw"][i], row(full["ffn_dw_b"][i]), tag + "ffn_mid")
        (y2, xnext), lands = _matmul_resid(act, wfull[("ffn_w_out", i)], 0, zeros_d, xmid, g2, tag + "ffn_out", out_carry)
        wfull.update(zip([("ffn_w_out", i + 1)], lands))
        saved.append(dict(x=xcur, h1=h1, z=z, a=a, y1=y1, xmid=xmid, h2=h2, zf=zf, act=act, y2=y2, w_in=w_in, w_out=w_out))
        xcur = xnext

    loss_row, dx, d_final_g = _final_loss(xcur, row(final_g), tgt, "final_loss")

    small_g = {n: jnp.zeros(_full_shape(wts[n], _SMALL[n]), F32) for n in _SMALL}
    small_g["final_g"] = d_final_g[0]
    dmod = [None] * depth
    pending = []
    slots = {}

    def put(name, j, val):
        small_g[name] = small_g[name].at[j].set(val.reshape(small_g[name].shape[1:]))

    for i in reversed(range(depth)):
        sv = saved[i]
        sh1, sc1, g1, sh2, sc2, g2 = (mod[i, k] for k in range(n_mod))
        kind, j = i % 3, i // 3
        tag = "L%d_" % i
        dy2, dact, dg2, _ = _outproj_bwd(dx, sv["y2"], g2, wfull[("ffn_w_out", i)], 0, tag + "ffn_out_bwd")
        g_ffn_out = _matmul_tn(sv["act"], dy2, tag + "ffn_out_wgrad")
        dzf, d_fdw, d_fdwb = _ffn_mid_bwd(sv["zf"], dact, full["ffn_dw"][i], row(full["ffn_dw_b"][i]), tag + "ffn_mid_bwd")
        put("ffn_dw", i, d_fdw)
        put("ffn_dw_b", i, d_fdwb)
        g_ffn_in = _matmul_tn(sv["h2"], dzf, tag + "ffn_in_wgrad")
        dxmid, dn2, dsc2, dsh2 = _inproj_bwd(dzf, wfull[("ffn_w_in", i)], 0, sv["xmid"], dx, row(full["norm2_g"][i]), sc2,
                                             tag + "ffn_in_bwd")
        put("norm2_g", i, dn2)
        pending += [(("ffn_w_in", i), g_ffn_in, 1), (("ffn_w_out", i), g_ffn_out, 0)]
        dy1, da, dg1, dbo = _outproj_bwd(dxmid, sv["y1"], g1, sv["w_out"], 0, tag + "mixer_out_bwd")
        w_in_name, w_out_name = mixers[kind]
        g_mix_out = _matmul_tn(sv["a"], dy1, tag + "mixer_out_wgrad")
        carry = _scatter_pushes([(g, axis) for _, g, axis in pending])
        if kind == 0:
            (dz, dsink, dbin), lands = _attn_bwd(sv["z"], da, tables, full["attn_sinks"][j], nh, tag + "attn_bwd", carry)
            put("attn_sinks", j, dsink[0, :nh])
            put("attn_bqkv", j, dbin)
            put("attn_bo", j, dbo)
        elif kind == 1:
            (dz, d_dw, d_dwb, d_lng, d_lnb, dbin), lands = _conv_mid_bwd(
                sv["z"], da, full["conv_dw"][j], row(full["conv_dw_b"][j]), row(full["conv_ln_g"][j]), row(full["conv_ln_b"][j]),
                tag + "conv_mid_bwd", carry)
            for nme, val in (("conv_dw", d_dw), ("conv_dw_b", d_dwb), ("conv_ln_g", d_lng), ("conv_ln_b", d_lnb),
                             ("conv_b_in", dbin), ("conv_b_out", dbo)):
                put(nme, j, val)
        else:
            (dz, d_lng, d_lnb, d_ws, d_bst, dbin), lands = _sgu_mid_bwd(
                sv["z"], da, row(full["sgu_ln_g"][j]), row(full["sgu_ln_b"][j]), full["sgu_ws"][j], full["sgu_bs"][j].T,
                tag + "sgu_mid_bwd", carry)
            ng = sgu_ws.shape[1]
            for nme, val in (("sgu_ln_g", d_lng), ("sgu_ln_b", d_lnb), ("sgu_ws", d_ws), ("sgu_bs", d_bst[:, :ng].T),
                             ("sgu_b_in", dbin), ("sgu_b_out", dbo)):
                put(nme, j, val)
        slots.update(zip([key for key, _, _ in pending], lands))
        pending = []
        g_mix_in = _matmul_tn(sv["h1"], dz, tag + "mixer_in_wgrad")
        dx, dn1, dsc1, dsh1 = _inproj_bwd(dz, sv["w_in"], 0, sv["x"], dxmid, row(full["norm1_g"][i]), sc1, tag + "mixer_in_bwd")
        put("norm1_g", i, dn1)
        dmod[i] = jnp.concatenate([dsh1, dsc1, dg1, dsh2, dsc2, dg2], axis=1)
        pending += [((w_in_name, j), g_mix_in, 1), ((w_out_name, j), g_mix_out, 0)]

    last_scatter = _push_start("start_scatter_last", *_scatter_pushes([(g, axis) for _, g, axis in pending]), dx)
    grad_x = dx[None]
    loss = lax.psum(loss_row[0, 0], ("x", "y", "c"))

    small_names = [n for n in _WEIGHTS if n in _SMALL and n != "ada_b"]
    dmod_own = jnp.concatenate(dmod, axis=0)
    packed = _behind(_pack([small_g[n] for n in small_names] + [dmod_own]), [last_scatter], "behind_last_scatter_start")
    packed_all = _gather_all("gather_small_grads", packed)
    summed = _sum_slots(packed_all, "sum_small_grads")
    small_full = dict(zip(small_names, _unpack(summed, [small_g[n].shape for n in small_names])))
    n_small = sum(math.prod(small_g[n].shape) for n in small_names)
    dmod_all = packed_all.reshape(_N_DEV, -1)[:, n_small:n_small + dmod_own.size].reshape(_N_DEV, depth, n_mod * d)
    small_full["ada_b"] = _sum_slots(dmod_all, "sum_ada_b_grad")
    dmod_cols = lax.dynamic_slice_in_dim(jnp.moveaxis(dmod_all, 0, 1), chip * ncols, ncols, 2)
    g_ada_w = _ada_wgrad(c_all.T, dmod_cols, "ada_wgrad")

    slots.update(zip([key for key, _, _ in pending], _push_wait("wait_scatter_last", last_scatter, g_ada_w)))
    partial =[_sum_slot_layers([slots[(n, j)] for j in range(wts[n].shape[0])], "sum_chips_" + n) for n in _BIG]
    other = _swap_sibling("swap_cores", partial)

    outs = {}
    for n, mine_p, theirs_p in zip(_BIG, partial, other):
        outs[n] = _adamw(wts[n], [mine_p, theirs_p], mom_m[n], mom_v[n], "adamw_" + n)
    outs["ada_w"] = _adamw(ada_w, [g_ada_w], m_ada_w, v_ada_w, "adamw_ada_w")
    sm_names = [n for n in _WEIGHTS if n in _SMALL]
    g_loc = [_shard_of(small_full[n], _SMALL[n], chip) for n in sm_names]
    packs = [_pack([src[n] for n in sm_names]) for src in (wts, mom_m, mom_v)]
    sm_out = _adamw(packs[0], [_pack(g_loc)], packs[1], packs[2], "adamw_small")
    shapes = [wts[n].shape for n in sm_names]
    unpacked = [_unpack(o, shapes) for o in sm_out]
    for k, n in enumerate(sm_names):
        outs[n] = tuple(u[k] for u in unpacked)

    result = [loss, grad_x]
    for which in range(4):
        result += [outs[n][which] for n in _WEIGHTS]
    return tuple(result)
```

```python
import math

import jax
import jax.numpy as jnp
from jax import lax
from jax.experimental import pallas as pl
from jax.experimental.pallas import tpu as pltpu

F32, BF16 = jnp.float32, jnp.bfloat16
SDS = jax.ShapeDtypeStruct
MESH_IDS = pl.DeviceIdType.MESH

_VMEM_LIMIT_BYTES = 48 * 1024 * 1024
_LANES = 128
_NORM_EPS = 1e-6
_NEG_INF = -1e30
_HEAD_DIM = 64
_N_KV = 4
_ATTN_BLOCK = 128
_ATTN_BLOCKS_PER_STEP = 2
_ROW_TILE = 512
_COL_TILE = 512
_WGRAD_TILE = 1536
_CONV_HALO = 32
_FFN_HALO = 16
_ADAM_LR, _ADAM_B1, _ADAM_B2, _ADAM_EPS, _ADAM_WD, _ADAM_STEP = 0.001, 0.9, 0.999, 1e-08, 0.01, 10
_N_CHIPS = 4
_N_DEV = 8


def _tile(n, pref, unit):
    if n <= pref:
        return n
    t = pref - pref % unit
    while t >= unit:
        if n % t == 0:
            return t
        t -= unit
    return n


def _params(n_axes):
    return pltpu.CompilerParams(dimension_semantics=("arbitrary",) * n_axes, vmem_limit_bytes=_VMEM_LIMIT_BYTES)


def _row(n):
    return pl.BlockSpec((1, n), lambda *_: (0, 0))


def _sigmoid(v):
    return 1.0 / (1.0 + jnp.exp(-v))


def _nt(a, b):
    return lax.dot_general(a, b, (((1,), (1,)), ((), ())), preferred_element_type=F32)


def _tn(a, b):
    return lax.dot_general(a, b, (((0,), (0,)), ((), ())), preferred_element_type=F32)


def _carrying(carry, body, *, name, grid, in_specs, out_specs, out_shape, scratch_shapes=(), compiler_params):
    single = not isinstance(out_shape, (list, tuple))
    if carry is None:
        call = pl.pallas_call(body, name=name, grid=grid, in_specs=in_specs, out_specs=out_specs, out_shape=out_shape,
                              scratch_shapes=list(scratch_shapes), compiler_params=compiler_params)
        return lambda *operands: (call(*operands), [])
    srcs, land_shapes, items = carry
    out_specs_l, out_shape_l = ([out_specs], [out_shape]) if single else (list(out_specs), list(out_shape))
    n_in, n_out, n_scr, n_src, n_land = len(in_specs), len(out_shape_l), len(scratch_shapes), len(srcs), len(land_shapes)
    n_remote = len(items) * len(_CHIP_FLIPS)
    last = grid[0] - 1

    def carrier(*refs):
        ins, src_refs = refs[:n_in], refs[n_in:n_in + n_src]
        o0 = n_in + n_src
        outs, land_refs = refs[o0:o0 + n_out], refs[o0 + n_out:o0 + n_out + n_land]
        s0 = o0 + n_out + n_land
        scratch, (send_sems, recv_sems, local_sems) = refs[s0:s0 + n_scr], refs[s0 + n_scr:]
        me, plan = _push_plan(src_refs, land_refs, items)
        step = pl.program_id(0)

        @pl.when(step == 0)
        def _():
            own = [pltpu.make_async_copy(_at(src_refs[si], src_idx(*me)), _at(land_refs[di], dst_idx(*me)), local_sems.at[k])
                   for k, (si, di, src_idx, dst_idx) in enumerate(items)]
            for cp in own:
                cp.start()
            for cp in own:
                cp.wait()

        for n, (peer, mine_src, mine_dst, _, _) in enumerate(plan):
            rows = mine_src.shape[0]
            chunks = _chunks_of(rows, grid[0])
            per = rows // chunks

            @pl.when(step < chunks)
            def _(n=n, peer=peer, mine_src=mine_src, mine_dst=mine_dst, per=per):
                part = pl.ds(pl.multiple_of(step * per, per), per)
                _remote_copy(mine_src.at[part], mine_dst.at[part], send_sems, recv_sems, n, peer).start()

        body(*ins, *outs, *scratch)

        @pl.when(step == last)
        def _():
            for n, (peer, mine_src, mine_dst, their_src, their_dst) in enumerate(plan):
                _remote_copy(their_src, their_dst, send_sems, recv_sems, n, peer).wait_recv()
            for n, (peer, mine_src, mine_dst, their_src, their_dst) in enumerate(plan):
                _remote_copy(mine_src, mine_dst, send_sems, recv_sems, n, peer).wait_send()

    def run(*operands):
        res = pl.pallas_call(
            carrier, name=name, grid=grid, in_specs=list(in_specs) + [_HBM] * n_src, out_specs=out_specs_l + [_HBM] * n_land,
            out_shape=out_shape_l + list(land_shapes),
            scratch_shapes=list(scratch_shapes) + [pltpu.SemaphoreType.DMA((n_remote,)), pltpu.SemaphoreType.DMA((n_remote,)),
                                                   pltpu.SemaphoreType.DMA((len(items),))],
            compiler_params=compiler_params)(*operands, *srcs)
        return (res[0] if single else list(res[:n_out])), list(res[n_out:])

    return run


def _resident(shape, index_map):
    return pl.BlockSpec(shape, index_map, pipeline_mode=pl.Buffered(1))


def _norm_mod_matmul(x, gn, sc, sh, w, layer, b, name, carry=None):
    t, d = x.shape
    n = w.shape[2]
    tm, tn = _tile(t, _ROW_TILE, 16), _tile(n, _COL_TILE, _LANES)

    def body(x_ref, gn_ref, sc_ref, sh_ref, w_ref, b_ref, h_ref, z_ref):
        xf = x_ref[...]
        r = lax.rsqrt(jnp.mean(xf * xf, axis=-1, keepdims=True) + _NORM_EPS)
        h_ref[...] = ((xf * r * gn_ref[...]) * (1.0 + sc_ref[...]) + sh_ref[...]).astype(BF16)
        for c0 in range(0, n, tn):
            z = jnp.dot(h_ref[...], w_ref[:, c0:c0 + tn], preferred_element_type=F32) + b_ref[:, c0:c0 + tn]
            z_ref[:, c0:c0 + tn] = z.astype(BF16)

    return _carrying(
        carry, body, name=name, grid=(t // tm,),
        in_specs=[pl.BlockSpec((tm, d), lambda i: (i, 0)), _row(d), _row(d), _row(d),
                  _resident((None, d, n), lambda i: (layer, 0, 0)), _row(n)],
        out_specs=[pl.BlockSpec((tm, d), lambda i: (i, 0)), pl.BlockSpec((tm, n), lambda i: (i, 0))],
        out_shape=[SDS((t, d), BF16), SDS((t, n), BF16)], compiler_params=_params(1),
    )(x, gn, sc, sh, w, b)


def _matmul_resid(a, w, layer, b, x, gate, name, carry=None):
    t, k = a.shape
    d = w.shape[2]
    tm, tn = _tile(t, _ROW_TILE, 16), _tile(d, _COL_TILE, _LANES)

    def body(a_ref, w_ref, b_ref, x_ref, g_ref, y_ref, xo_ref):
        for c0 in range(0, d, tn):
            cs = slice(c0, c0 + tn)
            y = jnp.dot(a_ref[...], w_ref[:, cs], preferred_element_type=F32) + b_ref[:, cs]
            y_ref[:, cs] = y.astype(BF16)
            xo_ref[:, cs] = x_ref[:, cs] + g_ref[:, cs] * y

    blk = pl.BlockSpec((tm, d), lambda i: (i, 0))
    return _carrying(
        carry, body, name=name, grid=(t // tm,),
        in_specs=[pl.BlockSpec((tm, k), lambda i: (i, 0)), _resident((None, k, d), lambda i: (layer, 0, 0)), _row(d), blk, _row(d)],
        out_specs=[blk, blk], out_shape=[SDS((t, d), BF16), SDS((t, d), F32)], compiler_params=_params(1),
    )(a, w, b, x, gate)


def _outproj_bwd(dxo, y, gate, w, layer, name):
    t, d = dxo.shape
    k = w.shape[1]
    tm, tk = _tile(t, _ROW_TILE, 16), _tile(k, _COL_TILE, _LANES)

    def body(dxo_ref, y_ref, g_ref, w_ref, dy_ref, da_ref, dg_ref, db_ref):
        @pl.when(pl.program_id(0) == 0)
        def _():
            dg_ref[...] = jnp.zeros_like(dg_ref)
            db_ref[...] = jnp.zeros_like(db_ref)

        dxf = dxo_ref[...]
        dyf = dxf * g_ref[...]
        dy_ref[...] = dyf.astype(BF16)
        dg_ref[...] += jnp.sum(dxf * y_ref[...].astype(F32), axis=0, keepdims=True)
        db_ref[...] += jnp.sum(dyf, axis=0, keepdims=True)
        for c0 in range(0, k, tk):
            da_ref[:, c0:c0 + tk] = _nt(dy_ref[...], w_ref[c0:c0 + tk, :]).astype(BF16)

    full = pl.BlockSpec((tm, d), lambda i: (i, 0))
    return pl.pallas_call(
        body, name=name, grid=(t // tm,),
        in_specs=[full, full, _row(d), _resident((None, k, d), lambda i: (layer, 0, 0))],
        out_specs=[full, pl.BlockSpec((tm, k), lambda i: (i, 0)), _row(d), _row(d)],
        out_shape=[SDS((t, d), BF16), SDS((t, k), BF16), SDS((1, d), F32), SDS((1, d), F32)],
        compiler_params=_params(1),
    )(dxo, y, gate, w)


def _matmul_tn(a, b, name):
    t, ka = a.shape
    nb = b.shape[1]
    tka, tnb, tt = _tile(ka, _WGRAD_TILE, _LANES), _tile(nb, _WGRAD_TILE, _LANES), _tile(t, 2 * _ROW_TILE, 16)
    nt = t // tt

    def body(a_ref, b_ref, o_ref, acc):
        s = pl.program_id(2)

        @pl.when(s == 0)
        def _():
            acc[...] = jnp.zeros_like(acc)

        acc[...] += _tn(a_ref[...], b_ref[...])

        @pl.when(s == nt - 1)
        def _():
            o_ref[...] = acc[...].astype(BF16)

    return pl.pallas_call(
        body, name=name, grid=(ka // tka, nb // tnb, nt),
        in_specs=[pl.BlockSpec((tt, tka), lambda i, j, s: (s, i)), pl.BlockSpec((tt, tnb), lambda i, j, s: (s, j))],
        out_specs=pl.BlockSpec((tka, tnb), lambda i, j, s: (i, j)),
        out_shape=SDS((ka, nb), BF16), scratch_shapes=[pltpu.VMEM((tka, tnb), F32)], compiler_params=_params(3),
    )(a, b)


def _inproj_bwd(dz, w, layer, x, dxo, gn, sc, name):
    t, n = dz.shape
    d = x.shape[1]
    tm, tk = _tile(t, _ROW_TILE, 16), _tile(n, _COL_TILE, _LANES)

    def body(dz_ref, w_ref, x_ref, dxo_ref, gn_ref, sc_ref, dx_ref, dgn_ref, dsc_ref, dsh_ref, acc):
        @pl.when(pl.program_id(0) == 0)
        def _():
            dgn_ref[...] = jnp.zeros_like(dgn_ref)
            dsc_ref[...] = jnp.zeros_like(dsc_ref)
            dsh_ref[...] = jnp.zeros_like(dsh_ref)

        for c0 in range(0, n, tk):
            part = _nt(dz_ref[:, c0:c0 + tk], w_ref[:, c0:c0 + tk])
            if c0 == 0:
                acc[...] = part
            else:
                acc[...] += part
        dh = acc[...]
        xf = x_ref[...]
        r = lax.rsqrt(jnp.mean(xf * xf, axis=-1, keepdims=True) + _NORM_EPS)
        xn = xf * r
        gnv = gn_ref[...]
        dsh_ref[...] += jnp.sum(dh, axis=0, keepdims=True)
        dsc_ref[...] += jnp.sum(dh * (xn * gnv), axis=0, keepdims=True)
        drn = dh * (1.0 + sc_ref[...])
        dgn_ref[...] += jnp.sum(drn * xn, axis=0, keepdims=True)
        dxn = drn * gnv
        dx_ref[...] = dxo_ref[...] + r * (dxn - xn * jnp.mean(dxn * xn, axis=-1, keepdims=True))

    full = pl.BlockSpec((tm, d), lambda i: (i, 0))
    return pl.pallas_call(
        body, name=name, grid=(t // tm,),
        in_specs=[pl.BlockSpec((tm, n), lambda i: (i, 0)), _resident((None, d, n), lambda i: (layer, 0, 0)),
                  full, full, _row(d), _row(d)],
        out_specs=[full, _row(d), _row(d), _row(d)],
        out_shape=[SDS((t, d), F32), SDS((1, d), F32), SDS((1, d), F32), SDS((1, d), F32)],
        scratch_shapes=[pltpu.VMEM((tm, d), F32)], compiler_params=_params(1),
    )(dz, w, x, dxo, gn, sc)


def _final_loss(x, g, target, name):
    t, d = x.shape
    tm = _tile(t, _ROW_TILE, 8)

    def body(x_ref, g_ref, t_ref, loss_ref, dx_ref, dg_ref):
        @pl.when(pl.program_id(0) == 0)
        def _():
            loss_ref[...] = jnp.zeros_like(loss_ref)
            dg_ref[...] = jnp.zeros_like(dg_ref)

        xf = x_ref[...]
        r = lax.rsqrt(jnp.mean(xf * xf, axis=-1, keepdims=True) + _NORM_EPS)
        xn = xf * r
        gv = g_ref[...]
        e = xn * gv - t_ref[...]
        per_row = jnp.mean(e * e, axis=-1, keepdims=True)
        loss_ref[...] += 0.5 * jnp.sum(per_row, axis=0, keepdims=True)
        dy = e * (1.0 / d)
        dg_ref[...] += jnp.sum(dy * xn, axis=0, keepdims=True)
        dxn = dy * gv
        dx_ref[...] = r * (dxn - xn * jnp.mean(dxn * xn, axis=-1, keepdims=True))

    full = pl.BlockSpec((tm, d), lambda i: (i, 0))
    return pl.pallas_call(
        body, name=name, grid=(t // tm,), in_specs=[full, _row(d), full],
        out_specs=[_row(_LANES), full, _row(d)],
        out_shape=[SDS((1, _LANES), F32), SDS((t, d), F32), SDS((1, d), F32)], compiler_params=_params(1),
    )(x, g, target)


def _attn_tables(nh):
    group = nh // _N_KV
    slopes = 2.0 ** (-8.0 * jnp.arange(1, nh + 1, dtype=F32) / nh)
    qpos = jnp.arange(_ATTN_BLOCK) + _ATTN_BLOCK
    kpos = jnp.arange(2 * _ATTN_BLOCK)
    dist = qpos[:, None] - kpos[None, :]
    band = (dist >= 0) & (dist < _ATTN_BLOCK)
    bias = jnp.where(band[None], -slopes[:, None, None] * dist.astype(F32)[None], _NEG_INF)
    first = jnp.where((kpos < _ATTN_BLOCK)[None, None, :], _NEG_INF, bias)
    return jnp.stack([first, bias]).reshape(2, _N_KV, group * _ATTN_BLOCK, 2 * _ATTN_BLOCK)


def _attn_probs(q4, k2, tab, sink_ref, kv, group):
    hd, blk = _HEAD_DIM, _ATTN_BLOCK
    s = _nt(q4, k2) * (hd ** -0.5) + tab
    sink = jnp.concatenate([jnp.full((blk, 1), sink_ref[kv * group + g], F32) for g in range(group)], axis=0)
    m = jnp.maximum(jnp.max(s, axis=-1, keepdims=True), sink)
    e = jnp.exp(s - m)
    es = jnp.exp(sink - m)
    inv = 1.0 / (jnp.sum(e, axis=-1, keepdims=True) + es)
    return e * inv, es * inv


def _attn_operands(refs, b, kv, group):
    q_ref, kp_ref, kc_ref, vp_ref, vc_ref = refs
    hd, blk = _HEAD_DIM, _ATTN_BLOCK
    rows, cs = slice(b * blk, (b + 1) * blk), slice(kv * hd, (kv + 1) * hd)
    before = slice((b - 1) * blk, b * blk)
    k2 = jnp.concatenate([kp_ref[:, cs] if b == 0 else kc_ref[before, cs], kc_ref[rows, cs]], axis=0)
    v2 = jnp.concatenate([vp_ref[:, cs] if b == 0 else vc_ref[before, cs], vc_ref[rows, cs]], axis=0)
    q4 = jnp.concatenate([q_ref[rows, (kv * group + g) * hd:(kv * group + g + 1) * hd] for g in range(group)], axis=0)
    return q4, k2, v2


def _attn_specs(nh, nblk, step):
    group = nh // _N_KV
    blk, kvw = _ATTN_BLOCK, _N_KV * _HEAD_DIM
    prev = lambda i: jnp.maximum(step(i) * nblk - 1, 0)
    return [
        pl.BlockSpec((nblk * blk, nh * _HEAD_DIM), lambda i: (step(i), 0)),
        pl.BlockSpec((blk, kvw), lambda i: (prev(i), group)),
        pl.BlockSpec((nblk * blk, kvw), lambda i: (step(i), group)),
        pl.BlockSpec((blk, kvw), lambda i: (prev(i), group + 1)),
        pl.BlockSpec((nblk * blk, kvw), lambda i: (step(i), group + 1)),
        pl.BlockSpec((2, _N_KV, group * blk, 2 * blk), lambda i: (0, 0, 0, 0)),
        pl.BlockSpec(memory_space=pltpu.SMEM),
    ]


def _attn_fwd(qkv, tables, sinks, nh, name, carry=None):
    t = qkv.shape[0]
    group, hd, blk = nh // _N_KV, _HEAD_DIM, _ATTN_BLOCK
    nblk = _ATTN_BLOCKS_PER_STEP if t % (_ATTN_BLOCKS_PER_STEP * blk) == 0 else 1

    def body(q_ref, kp_ref, kc_ref, vp_ref, vc_ref, tab_ref, sink_ref, o_ref):
        i = pl.program_id(0)
        for b in range(nblk):
            tab = tab_ref.at[jnp.minimum(i * nblk + b, 1)]
            outs = [None] * nh
            for kv in range(_N_KV):
                q4, k2, v2 = _attn_operands((q_ref, kp_ref, kc_ref, vp_ref, vc_ref), b, kv, group)
                p, _ = _attn_probs(q4, k2, tab[kv], sink_ref, kv, group)
                o4 = jnp.dot(p.astype(BF16), v2, preferred_element_type=F32)
                for g in range(group):
                    outs[kv * group + g] = o4[g * blk:(g + 1) * blk, :]
            o_ref[b * blk:(b + 1) * blk, :] = jnp.concatenate(outs, axis=1).astype(BF16)

    return _carrying(
        carry, body, name=name, grid=(t // (nblk * blk),), in_specs=_attn_specs(nh, nblk, lambda i: i),
        out_specs=pl.BlockSpec((nblk * blk, nh * hd), lambda i: (i, 0)), out_shape=SDS((t, nh * hd), BF16),
        compiler_params=_params(1),
    )(qkv, qkv, qkv, qkv, qkv, tables, sinks)


def _attn_bwd(qkv, do, tables, sinks, nh, name, carry=None):
    t, wq = qkv.shape
    group, hd, blk = nh // _N_KV, _HEAD_DIM, _ATTN_BLOCK
    nblk = _ATTN_BLOCKS_PER_STEP if t % (_ATTN_BLOCKS_PER_STEP * blk) == 0 else 1
    nsteps = t // (nblk * blk)
    kvw = _N_KV * hd
    step = lambda i: nsteps - 1 - i

    def body(q_ref, kp_ref, kc_ref, vp_ref, vc_ref, tab_ref, sink_ref, do_ref, dqkv_ref, dsink_ref, db_ref, ck, cv):
        i = pl.program_id(0)

        @pl.when(i == 0)
        def _():
            ck[...] = jnp.zeros_like(ck)
            cv[...] = jnp.zeros_like(cv)
            dsink_ref[...] = jnp.zeros_like(dsink_ref)
            db_ref[...] = jnp.zeros_like(db_ref)

        lane = lax.broadcasted_iota(jnp.int32, (1, _LANES), 1)
        carry_k = [ck[:, kv * hd:(kv + 1) * hd] for kv in range(_N_KV)]
        carry_v = [cv[:, kv * hd:(kv + 1) * hd] for kv in range(_N_KV)]
        dsink = jnp.zeros((1, _LANES), F32)
        dbias = jnp.zeros((1, wq), F32)
        for b in reversed(range(nblk)):
            rows = slice(b * blk, (b + 1) * blk)
            tab = tab_ref.at[jnp.minimum(step(i) * nblk + b, 1)]
            dq, dk, dv = [None] * nh, [None] * _N_KV, [None] * _N_KV
            for kv in range(_N_KV):
                q4, k2, v2 = _attn_operands((q_ref, kp_ref, kc_ref, vp_ref, vc_ref), b, kv, group)
                p, ps = _attn_probs(q4, k2, tab[kv], sink_ref, kv, group)
                do4 = jnp.concatenate([do_ref[rows, (kv * group + g) * hd:(kv * group + g + 1) * hd] for g in range(group)], axis=0)
                dp = _nt(do4, v2)
                dl = jnp.sum(p * dp, axis=-1, keepdims=True)
                ds = (p * (dp - dl)).astype(BF16)
                dsk = -ps * dl
                for g in range(group):
                    dsink = dsink + jnp.where(lane == kv * group + g, jnp.sum(dsk[g * blk:(g + 1) * blk, :]), 0.0)
                dq4 = jnp.dot(ds, k2, preferred_element_type=F32) * (hd ** -0.5)
                for g in range(group):
                    dq[kv * group + g] = dq4[g * blk:(g + 1) * blk, :]
                dk2 = _tn(q4, ds).T * (hd ** -0.5)
                dv2 = _tn(do4, p.astype(BF16)).T
                dk[kv] = dk2[blk:, :] + carry_k[kv]
                dv[kv] = dv2[blk:, :] + carry_v[kv]
                carry_k[kv], carry_v[kv] = dk2[:blk, :], dv2[:blk, :]
            dqkv = jnp.concatenate(dq + dk + dv, axis=1)
            dqkv_ref[rows, :] = dqkv.astype(BF16)
            dbias = dbias + jnp.sum(dqkv, axis=0, keepdims=True)
        for kv in range(_N_KV):
            ck[:, kv * hd:(kv + 1) * hd] = carry_k[kv]
            cv[:, kv * hd:(kv + 1) * hd] = carry_v[kv]
        db_ref[...] += dbias
        dsink_ref[...] += dsink

    return _carrying(
        carry, body, name=name, grid=(nsteps,),
        in_specs=_attn_specs(nh, nblk, step) + [pl.BlockSpec((nblk * blk, nh * hd), lambda i: (step(i), 0))],
        out_specs=[pl.BlockSpec((nblk * blk, wq), lambda i: (step(i), 0)), _row(_LANES), _row(wq)],
        out_shape=[SDS((t, wq), BF16), SDS((1, _LANES), F32), SDS((1, wq), F32)],
        scratch_shapes=[pltpu.VMEM((blk, kvw), F32), pltpu.VMEM((blk, kvw), F32)], compiler_params=_params(1),
    )(qkv, qkv, qkv, qkv, qkv, tables, sinks, do)


_SUBLANES = 8


_ALL_SHIFTS = tuple(range(1, _SUBLANES))


def _shift_copies(src_ref, sh_ref, shifts=_ALL_SHIFTS):
    rows = src_ref.shape[0]
    full = src_ref[...]
    for n, b in enumerate(shifts):
        sh_ref[n] = pltpu.roll(full, rows - b, axis=0)
    return sh_ref, shifts


def _rows_at(src_ref, shifted, off, r0, rg, cs):
    b = off % _SUBLANES
    if shifted is None or b not in shifted[1]:
        return src_ref[r0 + off:r0 + off + rg, cs]
    return shifted[0][shifted[1].index(b), r0 + off - b:r0 + off - b + rg, cs]


def _taps(src_ref, w_ref, dst_ref, n_rows, width, offs, rg, shifted=None):
    cg = _tile(width, 512, _LANES)
    for c0 in range(0, width, cg):
        cs = slice(c0, c0 + cg)
        wk = [w_ref[k:k + 1, cs] for k, _ in offs]
        for r0 in range(0, n_rows, rg):
            acc = None
            for (_, off), wv in zip(offs, wk):
                term = wv * _rows_at(src_ref, shifted, off, r0, rg, cs)
                acc = term if acc is None else acc + term
            dst_ref[r0:r0 + rg, cs] = acc


def _tap_grads(dy_ref, z_ref, out_ref, n_rows, width, offs, rg, shifted=None):
    cg = _tile(width, 512, _LANES)
    for c0 in range(0, width, cg):
        cs = slice(c0, c0 + cg)
        for k, off in offs:
            acc = None
            for r0 in range(0, n_rows, rg):
                term = dy_ref[r0:r0 + rg, cs] * _rows_at(z_ref, shifted, off, r0, rg, cs)
                acc = term if acc is None else acc + term
            out_ref[k:k + 1, cs] += jnp.sum(acc, axis=0, keepdims=True)


def _conv_mid_fwd(ag, dw, dwb, lng, lnb, name, carry=None):
    t, c2 = ag.shape
    c = c2 // 2
    kw = dw.shape[0]
    hl = _CONV_HALO
    tm = _tile(t, 256, hl)
    per = tm // hl

    def body(agp_ref, ag_ref, dw_ref, dwb_ref, lng_ref, lnb_ref, o_ref, zext, yb, zsh):
        i = pl.program_id(0)
        glu = lambda ref: ref[:, :c].astype(F32) * _sigmoid(ref[:, c:].astype(F32))
        zext[0:hl, :] = jnp.where(i > 0, glu(agp_ref), 0.0)
        zext[hl:, :] = glu(ag_ref)
        _taps(zext, dw_ref, yb, tm, c, [(k, hl - (kw - 1) + k) for k in range(kw)], 32, _shift_copies(zext, zsh))
        y = yb[...] + dwb_ref[...]
        mu = jnp.mean(y, axis=-1, keepdims=True)
        yc = y - mu
        rstd = lax.rsqrt(jnp.mean(yc * yc, axis=-1, keepdims=True) + _NORM_EPS)
        ln = yc * rstd * lng_ref[...] + lnb_ref[...]
        o_ref[...] = (ln * _sigmoid(ln)).astype(BF16)

    return _carrying(
        carry, body, name=name, grid=(t // tm,),
        in_specs=[pl.BlockSpec((hl, c2), lambda i: (jnp.maximum(i * per - 1, 0), 0)), pl.BlockSpec((tm, c2), lambda i: (i, 0)),
                  pl.BlockSpec((kw, c), lambda i: (0, 0)), _row(c), _row(c), _row(c)],
        out_specs=pl.BlockSpec((tm, c), lambda i: (i, 0)), out_shape=SDS((t, c), BF16),
        scratch_shapes=[pltpu.VMEM((hl + tm, c), F32), pltpu.VMEM((tm, c), F32), pltpu.VMEM((_SUBLANES - 1, hl + tm, c), F32)],
        compiler_params=_params(1),
    )(ag, ag, dw, dwb, lng, lnb)


def _conv_mid_bwd(ag, dzc, dw, dwb, lng, lnb, name, carry=None):
    t, c2 = ag.shape
    c = c2 // 2
    kw = dw.shape[0]
    hl = _CONV_HALO
    tm = _tile(t, 256, hl)
    per = tm // hl
    nt = t // tm
    last_halo = t // hl - 1

    def body(agp_ref, ag_ref, agn_ref, dzc_ref, dzcn_ref, dw_ref, dwb_ref, lng_ref, lnb_ref,
             dag_ref, ddw_ref, ddwb_ref, dlng_ref, dlnb_ref, dbin_ref, zext, yext, dyext, dzb, zsh, dysh):
        i = pl.program_id(0)

        @pl.when(i == 0)
        def _():
            for r in (ddw_ref, ddwb_ref, dlng_ref, dlnb_ref, dbin_ref):
                r[...] = jnp.zeros_like(r)

        glu = lambda ref: ref[:, :c].astype(F32) * _sigmoid(ref[:, c:].astype(F32))
        zext[0:hl, :] = jnp.where(i > 0, glu(agp_ref), 0.0)
        zext[hl:hl + tm, :] = glu(ag_ref)
        zext[hl + tm:, :] = glu(agn_ref)
        fwd_offs = [(k, hl - (kw - 1) + k) for k in range(kw)]
        z_shifted = _shift_copies(zext, zsh)
        _taps(zext, dw_ref, yext, tm + hl, c, fwd_offs, 32, z_shifted)
        y = yext[...] + dwb_ref[...]
        mu = jnp.mean(y, axis=-1, keepdims=True)
        yc = y - mu
        rstd = lax.rsqrt(jnp.mean(yc * yc, axis=-1, keepdims=True) + _NORM_EPS)
        xhat = yc * rstd
        lngv = lng_ref[...]
        ln = xhat * lngv + lnb_ref[...]
        sg = _sigmoid(ln)
        dz_out = jnp.concatenate([dzc_ref[...].astype(F32), jnp.where(i < nt - 1, dzcn_ref[...].astype(F32), 0.0)], axis=0)
        dln = dz_out * (sg * (1.0 + ln * (1.0 - sg)))
        dlng_ref[...] += jnp.sum((dln * xhat)[:tm], axis=0, keepdims=True)
        dlnb_ref[...] += jnp.sum(dln[:tm], axis=0, keepdims=True)
        dxh = dln * lngv
        dy = rstd * (dxh - jnp.mean(dxh, axis=-1, keepdims=True) - xhat * jnp.mean(dxh * xhat, axis=-1, keepdims=True))
        dyext[...] = dy
        ddwb_ref[...] += jnp.sum(dy[:tm], axis=0, keepdims=True)
        _taps(dyext, dw_ref, dzb, tm, c, [(k, kw - 1 - k) for k in range(kw)], 32, _shift_copies(dyext, dysh))
        _tap_grads(dyext, zext, ddw_ref, tm, c, fwd_offs, 32, z_shifted)
        a = ag_ref[:, :c].astype(F32)
        sgg = _sigmoid(ag_ref[:, c:].astype(F32))
        dz = dzb[...]
        da = dz * sgg
        dg = dz * a * sgg * (1.0 - sgg)
        dag_ref[:, :c] = da.astype(BF16)
        dag_ref[:, c:] = dg.astype(BF16)
        dbin_ref[:, :c] += jnp.sum(da, axis=0, keepdims=True)
        dbin_ref[:, c:] += jnp.sum(dg, axis=0, keepdims=True)

    prev = lambda i: (jnp.maximum(i * per - 1, 0), 0)
    nxt = lambda i: (jnp.minimum((i + 1) * per, last_halo), 0)
    return _carrying(
        carry, body, name=name, grid=(nt,),
        in_specs=[pl.BlockSpec((hl, c2), prev), pl.BlockSpec((tm, c2), lambda i: (i, 0)), pl.BlockSpec((hl, c2), nxt),
                  pl.BlockSpec((tm, c), lambda i: (i, 0)), pl.BlockSpec((hl, c), nxt),
                  pl.BlockSpec((kw, c), lambda i: (0, 0)), _row(c), _row(c), _row(c)],
        out_specs=[pl.BlockSpec((tm, c2), lambda i: (i, 0)), pl.BlockSpec((kw, c), lambda i: (0, 0)), _row(c), _row(c), _row(c), _row(c2)],
        out_shape=[SDS((t, c2), BF16), SDS((kw, c), F32), SDS((1, c), F32), SDS((1, c), F32), SDS((1, c), F32), SDS((1, c2), F32)],
        scratch_shapes=[pltpu.VMEM((hl + tm + hl, c), F32), pltpu.VMEM((tm + hl, c), F32), pltpu.VMEM((tm + hl, c), F32),
                        pltpu.VMEM((tm, c), F32), pltpu.VMEM((_SUBLANES - 1, hl + tm + hl, c), F32),
                        pltpu.VMEM((_SUBLANES - 1, tm + hl, c), F32)],
        compiler_params=_params(1),
    )(ag, ag, ag, dzc, dzc, dw, dwb, lng, lnb)


def _ffn_mid_fwd(zf, dw, dwb, name):
    t, f2 = zf.shape
    f = f2 // 2
    kw = dw.shape[0]
    hl = _FFN_HALO
    tm = _tile(t, 256, hl)
    per = tm // hl
    tc = _tile(f, 256, _LANES)
    offs = [(k, hl - (kw - 1) + k) for k in range(kw)]
    shifts = tuple(sorted({off % _SUBLANES for _, off in offs} - {0}))

    def body(zp_ref, z_ref, dw_ref, dwb_ref, o_ref, zext, cb, zsh):
        i = pl.program_id(0)
        zext[0:hl, :] = jnp.where(i > 0, zp_ref[...].astype(F32), 0.0)
        zext[hl:, :] = z_ref[...].astype(F32)
        _taps(zext, dw_ref, cb, tm, f2, offs, 16, _shift_copies(zext, zsh, shifts))
        for c0 in range(0, f, tc):
            g = cb[:, c0:c0 + tc] + dwb_ref[:, c0:c0 + tc]
            u = cb[:, f + c0:f + c0 + tc] + dwb_ref[:, f + c0:f + c0 + tc]
            o_ref[:, c0:c0 + tc] = (g * _sigmoid(g) * u).astype(BF16)

    return pl.pallas_call(
        body, name=name, grid=(t // tm,),
        in_specs=[pl.BlockSpec((hl, f2), lambda i: (jnp.maximum(i * per - 1, 0), 0)), pl.BlockSpec((tm, f2), lambda i: (i, 0)),
                  pl.BlockSpec((kw, f2), lambda i: (0, 0)), _row(f2)],
        out_specs=pl.BlockSpec((tm, f), lambda i: (i, 0)), out_shape=SDS((t, f), BF16),
        scratch_shapes=[pltpu.VMEM((hl + tm, f2), F32), pltpu.VMEM((tm, f2), F32), pltpu.VMEM((len(shifts), hl + tm, f2), F32)],
        compiler_params=_params(1),
    )(zf, zf, dw, dwb)


def _ffn_mid_bwd(zf, dact, dw, dwb, name):
    t, f2 = zf.shape
    f = f2 // 2
    kw = dw.shape[0]
    hl = _FFN_HALO
    tm = _tile(t, 128, hl)
    per = tm // hl
    nt = t // tm
    last_halo = t // hl - 1
    tc = _tile(f, 256, _LANES)
    fwd_offs = [(k, hl - (kw - 1) + k) for k in range(kw)]
    bwd_offs = [(k, kw - 1 - k) for k in range(kw)]
    fwd_shifts = tuple(sorted({off % _SUBLANES for _, off in fwd_offs} - {0}))
    bwd_shifts = tuple(sorted({off % _SUBLANES for _, off in bwd_offs} - {0}))

    def body(zp_ref, z_ref, zn_ref, da_ref, dan_ref, dw_ref, dwb_ref, dzf_ref, ddw_ref, ddwb_ref, zext, cext, dcext, dzb,
             zsh, dcsh):
        i = pl.program_id(0)

        @pl.when(i == 0)
        def _():
            ddw_ref[...] = jnp.zeros_like(ddw_ref)
            ddwb_ref[...] = jnp.zeros_like(ddwb_ref)

        zext[0:hl, :] = jnp.where(i > 0, zp_ref[...].astype(F32), 0.0)
        zext[hl:hl + tm, :] = z_ref[...].astype(F32)
        zext[hl + tm:, :] = zn_ref[...].astype(F32)
        z_shifted = _shift_copies(zext, zsh, fwd_shifts)
        _taps(zext, dw_ref, cext, tm + hl, f2, fwd_offs, 16, z_shifted)
        for c0 in range(0, f, tc):
            g = cext[:, c0:c0 + tc] + dwb_ref[:, c0:c0 + tc]
            u = cext[:, f + c0:f + c0 + tc] + dwb_ref[:, f + c0:f + c0 + tc]
            da = jnp.concatenate([da_ref[:, c0:c0 + tc].astype(F32),
                                  jnp.where(i < nt - 1, dan_ref[:, c0:c0 + tc].astype(F32), 0.0)], axis=0)
            sg = _sigmoid(g)
            dcg = da * u * (sg * (1.0 + g * (1.0 - sg)))
            dcu = da * (g * sg)
            dcext[:, c0:c0 + tc] = dcg
            dcext[:, f + c0:f + c0 + tc] = dcu
            ddwb_ref[:, c0:c0 + tc] += jnp.sum(dcg[:tm], axis=0, keepdims=True)
            ddwb_ref[:, f + c0:f + c0 + tc] += jnp.sum(dcu[:tm], axis=0, keepdims=True)
        _taps(dcext, dw_ref, dzb, tm, f2, bwd_offs, 16, _shift_copies(dcext, dcsh, bwd_shifts))
        _tap_grads(dcext, zext, ddw_ref, tm, f2, fwd_offs, 16, z_shifted)
        dzf_ref[...] = dzb[...].astype(BF16)

    prev = lambda i: (jnp.maximum(i * per - 1, 0), 0)
    nxt = lambda i: (jnp.minimum((i + 1) * per, last_halo), 0)
    return pl.pallas_call(
        body, name=name, grid=(nt,),
        in_specs=[pl.BlockSpec((hl, f2), prev), pl.BlockSpec((tm, f2), lambda i: (i, 0)), pl.BlockSpec((hl, f2), nxt),
                  pl.BlockSpec((tm, f), lambda i: (i, 0)), pl.BlockSpec((hl, f), nxt),
                  pl.BlockSpec((kw, f2), lambda i: (0, 0)), _row(f2)],
        out_specs=[pl.BlockSpec((tm, f2), lambda i: (i, 0)), pl.BlockSpec((kw, f2), lambda i: (0, 0)), _row(f2)],
        out_shape=[SDS((t, f2), BF16), SDS((kw, f2), F32), SDS((1, f2), F32)],
        scratch_shapes=[pltpu.VMEM((hl + tm + hl, f2), F32), pltpu.VMEM((tm + hl, f2), F32), pltpu.VMEM((tm + hl, f2), F32),
                        pltpu.VMEM((tm, f2), F32), pltpu.VMEM((len(fwd_shifts), hl + tm + hl, f2), F32),
                        pltpu.VMEM((len(bwd_shifts), tm + hl, f2), F32)],
        compiler_params=_params(1),
    )(zf, zf, zf, dact, dact, dw, dwb)


_INV_SQRT2 = 0.7071067811865476
_INV_SQRT_2PI = 0.3989422804014327


def _sgu_common(zin_ref, lng_ref, lnb_ref, hh):
    z = zin_ref[...].astype(F32)
    cdf = 0.5 * (1.0 + lax.erf(z * _INV_SQRT2))
    ge = z * cdf
    u, v = ge[:, :hh], ge[:, hh:]
    mu = jnp.mean(v, axis=-1, keepdims=True)
    vc = v - mu
    rstd = lax.rsqrt(jnp.mean(vc * vc, axis=-1, keepdims=True) + _NORM_EPS)
    vhat = vc * rstd
    vn = vhat * lng_ref[...] + lnb_ref[...]
    return z, cdf, u, vhat, rstd, vn


def _sgu_wm(ws_ref, g, ch):
    rows = lax.broadcasted_iota(jnp.int32, (ch, ch), 0)
    cols = lax.broadcasted_iota(jnp.int32, (ch, ch), 1)
    return jnp.where(rows >= cols, ws_ref[g], 0.0).astype(BF16)


def _sgu_mid_fwd(zin, lng, lnb, ws, bs_t, name, carry=None):
    t, h2 = zin.shape
    hh = h2 // 2
    ng, ch = ws.shape[0], ws.shape[1]
    hg = hh // ng
    tm = _tile(t, 256, ch)

    def body(zin_ref, lng_ref, lnb_ref, ws_ref, bs_ref, o_ref):
        _, _, u, _, _, vn = _sgu_common(zin_ref, lng_ref, lnb_ref, hh)
        vnb = vn.astype(BF16)
        for g in range(ng):
            wm = _sgu_wm(ws_ref, g, ch)
            for cc in range(tm // ch):
                rs, cs = slice(cc * ch, (cc + 1) * ch), slice(g * hg, (g + 1) * hg)
                vv = jnp.dot(wm, vnb[rs, cs], preferred_element_type=F32) + bs_ref[:, g:g + 1]
                o_ref[rs, cs] = (u[rs, cs] * vv).astype(BF16)

    return _carrying(
        carry, body, name=name, grid=(t // tm,),
        in_specs=[pl.BlockSpec((tm, h2), lambda i: (i, 0)), _row(hh), _row(hh),
                  pl.BlockSpec((ng, ch, ch), lambda i: (0, 0, 0)), pl.BlockSpec((ch, ng), lambda i: (0, 0))],
        out_specs=pl.BlockSpec((tm, hh), lambda i: (i, 0)), out_shape=SDS((t, hh), BF16), compiler_params=_params(1),
    )(zin, lng, lnb, ws, bs_t)


def _sgu_mid_bwd(zin, duv, lng, lnb, ws, bs_t, name, carry=None):
    t, h2 = zin.shape
    hh = h2 // 2
    ng, ch = ws.shape[0], ws.shape[1]
    hg = hh // ng
    tm = _tile(t, 128, ch)

    def body(zin_ref, duv_ref, lng_ref, lnb_ref, ws_ref, bs_ref, dzin_ref, dlng_ref, dlnb_ref, dws_ref, dbs_ref, dbin_ref, dvn_s, du_s):
        @pl.when(pl.program_id(0) == 0)
        def _():
            for r in (dlng_ref, dlnb_ref, dws_ref, dbs_ref, dbin_ref):
                r[...] = jnp.zeros_like(r)

        z, cdf, u, vhat, rstd, vn = _sgu_common(zin_ref, lng_ref, lnb_ref, hh)
        vnb = vn.astype(BF16)
        duv = duv_ref[...].astype(F32)
        dvv = (duv * u).astype(BF16)
        lane = lax.broadcasted_iota(jnp.int32, (1, _LANES), 1)
        rows = lax.broadcasted_iota(jnp.int32, (ch, ch), 0)
        cols = lax.broadcasted_iota(jnp.int32, (ch, ch), 1)
        dbs = jnp.zeros((ch, _LANES), F32)
        for g in range(ng):
            wm = _sgu_wm(ws_ref, g, ch)
            dwm = jnp.zeros((ch, ch), F32)
            for cc in range(tm // ch):
                rs, cs = slice(cc * ch, (cc + 1) * ch), slice(g * hg, (g + 1) * hg)
                vv = jnp.dot(wm, vnb[rs, cs], preferred_element_type=F32) + bs_ref[:, g:g + 1]
                du_s[rs, cs] = duv[rs, cs] * vv
                dvn_s[rs, cs] = _tn(wm, dvv[rs, cs])
                dwm = dwm + _nt(dvv[rs, cs], vnb[rs, cs])
                dbs = dbs + jnp.where(lane == g, jnp.sum(dvv[rs, cs].astype(F32), axis=-1, keepdims=True), 0.0)
            dws_ref[g] += jnp.where(rows >= cols, dwm, 0.0)
        dbs_ref[...] += dbs
        dvn = dvn_s[...]
        dlng_ref[...] += jnp.sum(dvn * vhat, axis=0, keepdims=True)
        dlnb_ref[...] += jnp.sum(dvn, axis=0, keepdims=True)
        dxh = dvn * lng_ref[...]
        dv = rstd * (dxh - jnp.mean(dxh, axis=-1, keepdims=True) - vhat * jnp.mean(dxh * vhat, axis=-1, keepdims=True))
        dgelu = cdf + z * (_INV_SQRT_2PI * jnp.exp(-0.5 * z * z))
        dzu = du_s[...] * dgelu[:, :hh]
        dzv = dv * dgelu[:, hh:]
        dzin_ref[:, :hh] = dzu.astype(BF16)
        dzin_ref[:, hh:] = dzv.astype(BF16)
        dbin_ref[:, :hh] += jnp.sum(dzu, axis=0, keepdims=True)
        dbin_ref[:, hh:] += jnp.sum(dzv, axis=0, keepdims=True)

    return _carrying(
        carry, body, name=name, grid=(t // tm,),
        in_specs=[pl.BlockSpec((tm, h2), lambda i: (i, 0)), pl.BlockSpec((tm, hh), lambda i: (i, 0)), _row(hh), _row(hh),
                  pl.BlockSpec((ng, ch, ch), lambda i: (0, 0, 0)), pl.BlockSpec((ch, ng), lambda i: (0, 0))],
        out_specs=[pl.BlockSpec((tm, h2), lambda i: (i, 0)), _row(hh), _row(hh), pl.BlockSpec((ng, ch, ch), lambda i: (0, 0, 0)),
                   pl.BlockSpec((ch, _LANES), lambda i: (0, 0)), _row(h2)],
        out_shape=[SDS((t, h2), BF16), SDS((1, hh), F32), SDS((1, hh), F32), SDS((ng, ch, ch), F32), SDS((ch, _LANES), F32),
                   SDS((1, h2), F32)],
        scratch_shapes=[pltpu.VMEM((tm, hh), F32), pltpu.VMEM((tm, hh), F32)], compiler_params=_params(1),
    )(zin, duv, lng, lnb, ws, bs_t)


def _ada_mod(c_all, ada_w, ada_b, name):
    nl, d, n = ada_w.shape
    nb = c_all.shape[0]
    tn = _tile(n, _COL_TILE, _LANES)

    def body(c_ref, w_ref, b_ref, o_ref):
        cv = c_ref[...]
        ca = cv * _sigmoid(cv)
        o_ref[...] = jnp.dot(ca, w_ref[...], preferred_element_type=F32, precision=lax.Precision.HIGHEST) + b_ref[...]

    return pl.pallas_call(
        body, name=name, grid=(nl, n // tn),
        in_specs=[pl.BlockSpec((nb, d), lambda l, j: (0, 0)), pl.BlockSpec((None, d, tn), lambda l, j: (l, 0, j)),
                  pl.BlockSpec((None, 1, tn), lambda l, j: (l, 0, j))],
        out_specs=pl.BlockSpec((None, nb, tn), lambda l, j: (l, 0, j)), out_shape=SDS((nl, nb, n), F32),
        compiler_params=_params(2),
    )(c_all, ada_w, ada_b)


def _ada_wgrad(c_all_t, dmod, name):
    d, nb = c_all_t.shape
    nl, _, n = dmod.shape
    tn = _tile(n, _COL_TILE, _LANES)

    def body(c_ref, dm_ref, o_ref):
        cv = c_ref[...]
        ca = cv * _sigmoid(cv)
        acc = ca[:, 0:1] * dm_ref[0:1, :]
        for b in range(1, nb):
            acc = acc + ca[:, b:b + 1] * dm_ref[b:b + 1, :]
        o_ref[...] = acc

    return pl.pallas_call(
        body, name=name, grid=(nl, n // tn),
        in_specs=[pl.BlockSpec((d, nb), lambda l, j: (0, 0)), pl.BlockSpec((None, nb, tn), lambda l, j: (l, 0, j))],
        out_specs=pl.BlockSpec((None, d, tn), lambda l, j: (l, 0, j)), out_shape=SDS((nl, d, n), F32),
        compiler_params=_params(2),
    )(c_all_t, dmod)


def _as_rows(a):
    return a.reshape(-1, a.shape[-1])


def _row_tile(r, c, n_arrays):
    budget = _VMEM_LIMIT_BYTES // (4 * 2 * n_arrays * 4)
    return _tile(r, max(8, budget // max(c, 1)), 8)


def _cast_bf16_layer(a, layer, name):
    _, r, c = a.shape
    tr = _row_tile(r, c, 2)

    def body(a_ref, o_ref):
        o_ref[...] = a_ref[...].astype(BF16)

    return pl.pallas_call(body, name=name, grid=(r // tr,), in_specs=[pl.BlockSpec((None, tr, c), lambda i: (layer, i, 0))],
                          out_specs=pl.BlockSpec((tr, c), lambda i: (i, 0)), out_shape=SDS((r, c), BF16),
                          compiler_params=_params(1))(a)


def _sum_slot_layers(parts, name):
    nl = len(parts)
    n, r, c = parts[0].shape
    tr = _row_tile(r, c, nl * (n + 1))

    def body(*refs):
        o_ref = refs[nl]
        for lay in range(nl):
            acc = refs[lay][0].astype(F32)
            for s in range(1, n):
                acc = acc + refs[lay][s].astype(F32)
            o_ref[lay] = acc

    return pl.pallas_call(body, name=name, grid=(r // tr,), in_specs=[pl.BlockSpec((n, tr, c), lambda i: (0, i, 0))] * nl,
                          out_specs=pl.BlockSpec((nl, tr, c), lambda i: (0, i, 0)), out_shape=SDS((nl, r, c), F32),
                          compiler_params=_params(1))(*parts)


def _sum_slots(parts, name):
    n = parts.shape[0]
    p2 = parts.reshape(n, -1, parts.shape[-1])
    r, c = p2.shape[1:]
    tr = _row_tile(r, c, n + 1)

    def body(p_ref, o_ref):
        acc = p_ref[0].astype(F32)
        for s in range(1, n):
            acc = acc + p_ref[s].astype(F32)
        o_ref[...] = acc

    out = pl.pallas_call(body, name=name, grid=(r // tr,), in_specs=[pl.BlockSpec((n, tr, c), lambda i: (0, i, 0))],
                         out_specs=pl.BlockSpec((tr, c), lambda i: (i, 0)), out_shape=SDS((r, c), F32),
                         compiler_params=_params(1))(p2)
    return out.reshape(parts.shape[1:])


def _adamw(w, g_parts, m, v, name):
    w2, m2, v2 = _as_rows(w), _as_rows(m), _as_rows(v)
    gs = [_as_rows(g) for g in g_parts]
    r, c = w2.shape
    ng = len(gs)
    tr = _row_tile(r, c, 7 + ng)
    c1 = 1.0 - _ADAM_B1 ** _ADAM_STEP
    c2 = 1.0 - _ADAM_B2 ** _ADAM_STEP

    def body(*refs):
        w_ref, m_ref, v_ref = refs[:3]
        g_refs = refs[3:3 + ng]
        go_ref, d_ref, mo_ref, vo_ref = refs[3 + ng:]
        g = g_refs[0][...]
        for gr in g_refs[1:]:
            g = g + gr[...]
        mn = _ADAM_B1 * m_ref[...] + (1.0 - _ADAM_B1) * g
        vn = _ADAM_B2 * v_ref[...] + (1.0 - _ADAM_B2) * (g * g)
        go_ref[...] = g
        mo_ref[...] = mn
        vo_ref[...] = vn
        d_ref[...] = -_ADAM_LR * ((mn / c1) / (jnp.sqrt(vn / c2) + _ADAM_EPS) + _ADAM_WD * w_ref[...])

    spec = pl.BlockSpec((tr, c), lambda i: (i, 0))
    outs = pl.pallas_call(body, name=name, grid=(r // tr,), in_specs=[spec] * (3 + ng), out_specs=[spec] * 4,
                          out_shape=[SDS((r, c), F32)] * 4, compiler_params=_params(1))(w2, m2, v2, *gs)
    return tuple(o.reshape(w.shape) for o in outs)


_HBM = pl.BlockSpec(memory_space=pltpu.HBM)
_CHIP_FLIPS = ((1, 0, 0), (0, 1, 0), (1, 1, 0))
_ALL_FLIPS = tuple((a, b, c) for a in (0, 1) for b in (0, 1) for c in (0, 1))[1:]
_SIBLING_FLIP = ((0, 0, 1),)


def _at(ref, idx):
    return ref.at[idx] if idx else ref


def _exchange(name, flips, srcs, out_shapes, items):
    n_in, n_out = len(srcs), len(out_shapes)
    n_remote = len(items) * len(flips)
    n_local = sum(1 for it in items if it[4])

    def body(*refs):
        src_refs, dst_refs = refs[:n_in], refs[n_in:n_in + n_out]
        send_sems, recv_sems, local_sems = refs[n_in + n_out:]
        me = (lax.axis_index("x"), lax.axis_index("y"), lax.axis_index("c"))
        sends, recvs, locs = [], [], []
        n, nl = 0, 0
        for si, di, src_idx, dst_idx, local in items:
            for flip in flips:
                peer = tuple(1 - m if f else m for m, f in zip(me, flip))
                push = pltpu.make_async_remote_copy(
                    src_ref=_at(src_refs[si], src_idx(*peer)), dst_ref=_at(dst_refs[di], dst_idx(*me)),
                    send_sem=send_sems.at[n], recv_sem=recv_sems.at[n], device_id=peer, device_id_type=MESH_IDS)
                push.start()
                sends.append(push)
                recvs.append(pltpu.make_async_remote_copy(
                    src_ref=_at(src_refs[si], src_idx(*me)), dst_ref=_at(dst_refs[di], dst_idx(*peer)),
                    send_sem=send_sems.at[n], recv_sem=recv_sems.at[n], device_id=peer, device_id_type=MESH_IDS))
                n += 1
            if local:
                cp = pltpu.make_async_copy(_at(src_refs[si], src_idx(*me)), _at(dst_refs[di], dst_idx(*me)), local_sems.at[nl])
                cp.start()
                locs.append(cp)
                nl += 1
        for r in recvs:
            r.wait_recv()
        for s in sends:
            s.wait_send()
        for cp in locs:
            cp.wait()

    return pl.pallas_call(
        body, name=name, in_specs=[_HBM] * n_in, out_specs=[_HBM] * n_out, out_shape=list(out_shapes),
        scratch_shapes=[pltpu.SemaphoreType.DMA((n_remote,)), pltpu.SemaphoreType.DMA((n_remote,)),
                        pltpu.SemaphoreType.DMA((max(n_local, 1),))],
    )(*srcs)


_SEM = pl.BlockSpec(memory_space=pltpu.SEMAPHORE)
_SIDE_EFFECTS = pltpu.SideEffectType.DATAFLOW_SIDE_EFFECTING


def _push_plan(src_refs, land_refs, items):
    me = (lax.axis_index("x"), lax.axis_index("y"), lax.axis_index("c"))
    plan = []
    for si, di, src_idx, dst_idx in items:
        for flip in _CHIP_FLIPS:
            peer = tuple(1 - m if f else m for m, f in zip(me, flip))
            plan.append((peer, _at(src_refs[si], src_idx(*peer)), _at(land_refs[di], dst_idx(*me)),
                         _at(src_refs[si], src_idx(*me)), _at(land_refs[di], dst_idx(*peer))))
    return me, plan


def _remote_copy(src, dst, send_sems, recv_sems, n, peer):
    return pltpu.make_async_remote_copy(src_ref=src, dst_ref=dst, send_sem=send_sems.at[n], recv_sem=recv_sems.at[n],
                                        device_id=peer, device_id_type=MESH_IDS)


def _chunks_of(rows, steps):
    units = rows // 16 if rows % 16 == 0 else 1
    return max(k for k in range(1, min(units, steps) + 1) if units % k == 0)


def _split_copies(src_refs, land_refs, send_sems, recv_sems, items):
    me, plan = _push_plan(src_refs, land_refs, items)
    pushes = [_remote_copy(ms, md, send_sems, recv_sems, n, peer) for n, (peer, ms, md, _, _) in enumerate(plan)]
    arrivals = [_remote_copy(ts, td, send_sems, recv_sems, n, peer) for n, (peer, _, _, ts, td) in enumerate(plan)]
    return me, pushes, arrivals


def _push_start(name, srcs, land_shapes, items, after):
    n_src, n_land = len(srcs), len(land_shapes)
    n_buf = n_src + n_land
    n_remote = len(items) * len(_CHIP_FLIPS)

    def body(*refs):
        src_refs, land_refs = refs[:n_src], refs[n_src:n_buf]
        send_sems, recv_sems = refs[n_buf + 1], refs[n_buf + 2]
        local_sems = refs[2 * n_buf + 4]
        me, pushes, _ = _split_copies(src_refs, land_refs, send_sems, recv_sems, items)
        own = [pltpu.make_async_copy(_at(src_refs[si], src_idx(*me)), _at(land_refs[di], dst_idx(*me)), local_sems.at[k])
               for k, (si, di, src_idx, dst_idx) in enumerate(items)]
        for cp in own:
            cp.start()
        for cp in own:
            cp.wait()
        for p in pushes:
            p.start()

    bufs = list(srcs) + [lax.empty(s.shape, s.dtype) for s in land_shapes]
    outs = pl.pallas_call(
        body, name=name, in_specs=[_HBM] * n_buf + [pl.BlockSpec(memory_space=pl.ANY)],
        out_shape=[pltpu.SemaphoreType.DMA((n_remote,)), pltpu.SemaphoreType.DMA((n_remote,))]
        + [pltpu.HBM(a.shape, a.dtype) for a in bufs] + [pltpu.HBM((_SUBLANES, _LANES), F32)],
        out_specs=[_SEM, _SEM] + [_HBM] * (n_buf + 1),
        input_output_aliases={k: 2 + k for k in range(n_buf)},
        scratch_shapes=[pltpu.SemaphoreType.DMA((len(items),))],
        compiler_params=pltpu.CompilerParams(has_side_effects=_SIDE_EFFECTS),
    )(*[pltpu.with_memory_space_constraint(a, pltpu.HBM) for a in bufs], after)
    return dict(send=outs[0], recv=outs[1], bufs=list(outs[2:2 + n_buf]), order=outs[2 + n_buf], n_src=n_src, items=items)


def _behind(x, handles, name):
    deps = [h["order"] for h in handles]

    def body(*refs):
        refs[-1][...] = refs[0][...]

    vmem = pl.BlockSpec(memory_space=pltpu.VMEM)
    return pl.pallas_call(body, name=name, in_specs=[vmem] + [pl.BlockSpec(memory_space=pl.ANY)] * len(deps), out_specs=vmem,
                          out_shape=SDS(x.shape, x.dtype))(x, *deps)


def _push_wait(name, handle, after):
    n_src, items = handle["n_src"], handle["items"]
    n_buf = len(handle["bufs"])

    def body(*refs):
        src_refs, land_refs = refs[:n_src], refs[n_src:n_buf]
        send_sems, recv_sems = refs[n_buf], refs[n_buf + 1]
        _, pushes, arrivals = _split_copies(src_refs, land_refs, send_sems, recv_sems, items)
        for p in pushes:
            p.wait_send()
        for a in arrivals:
            a.wait_recv()

    outs = pl.pallas_call(
        body, name=name, in_specs=[_HBM] * n_buf + [_SEM, _SEM, pl.BlockSpec(memory_space=pl.ANY)],
        out_shape=[pltpu.HBM(a.shape, a.dtype) for a in handle["bufs"]], out_specs=[_HBM] * n_buf,
        input_output_aliases={k: k for k in range(n_buf)},
        compiler_params=pltpu.CompilerParams(has_side_effects=_SIDE_EFFECTS),
    )(*handle["bufs"], handle["send"], handle["recv"], after)
    return list(outs[n_src:])


def _chip_of(x, y, c):
    return 2 * x + y


def _dev_of(x, y, c):
    return 4 * x + 2 * y + c


def _window(axis, ndim, size):
    def idx(x, y, c):
        return tuple(pl.ds(_chip_of(x, y, c) * size, size) if a == axis else slice(None) for a in range(ndim))
    return idx


def _whole(x, y, c):
    return ()


def _gather_pushes(shards_axes):
    srcs, shapes, items = [], [], []
    for k, (a, axis) in enumerate(shards_axes):
        full = list(a.shape)
        full[axis] *= _N_CHIPS
        srcs.append(a)
        shapes.append(SDS((1,) + tuple(full), a.dtype))
        window = _window(axis, a.ndim, a.shape[axis])
        items.append((k, k, _whole, (lambda w: lambda x, y, c: (0,) + w(x, y, c))(window)))
    return srcs, shapes, items


def _scatter_pushes(grads_axes):
    srcs, shapes, items = [], [], []
    slot = lambda x, y, c: (_chip_of(x, y, c),)
    for k, (a, axis) in enumerate(grads_axes):
        shard = list(a.shape)
        shard[axis] //= _N_CHIPS
        srcs.append(a)
        shapes.append(SDS((_N_CHIPS,) + tuple(shard), a.dtype))
        items.append((k, k, _window(axis, a.ndim, shard[axis]), slot))
    return srcs, shapes, items


def _gather_all(name, a):
    slot = lambda x, y, c: (_dev_of(x, y, c),)
    return _exchange(name, _ALL_FLIPS, [a], [SDS((_N_DEV,) + a.shape, a.dtype)], [(0, 0, _whole, slot, True)])[0]


def _swap_chips(name, a):
    slot = lambda x, y, c: (_chip_of(x, y, c),)
    return _exchange(name, _CHIP_FLIPS, [a], [SDS(a.shape, a.dtype)], [(0, 0, slot, slot, True)])[0]


def _swap_sibling(name, arrays):
    items = [(k, k, _whole, _whole, False) for k in range(len(arrays))]
    return _exchange(name, _SIBLING_FLIP, list(arrays), [SDS(a.shape, a.dtype) for a in arrays], items)


_BIG = ("attn_wqkv", "attn_wo", "conv_w_in", "conv_w_out", "sgu_w_in", "sgu_w_out", "ffn_w_in", "ffn_w_out")
_BIG_AXIS = {"attn_wqkv": 2, "attn_wo": 1, "conv_w_in": 2, "conv_w_out": 1, "sgu_w_in": 2, "sgu_w_out": 1,
             "ffn_w_in": 2, "ffn_w_out": 1}
_SMALL = {"norm1_g": None, "norm2_g": None, "ada_b": None, "attn_bqkv": 1, "attn_sinks": None, "attn_bo": 1,
          "conv_b_in": None, "conv_dw": 2, "conv_dw_b": None, "conv_ln_g": None, "conv_ln_b": None, "conv_b_out": None,
          "sgu_b_in": 1, "sgu_ln_g": 1, "sgu_ln_b": 1, "sgu_ws": None, "sgu_bs": None, "sgu_b_out": 1,
          "ffn_dw": 2, "ffn_dw_b": None, "final_g": None}
_WEIGHTS = ['norm1_g', 'norm2_g', 'ada_w', 'ada_b', 'attn_wqkv', 'attn_bqkv', 'attn_sinks', 'attn_wo', 'attn_bo',
            'conv_w_in', 'conv_b_in', 'conv_dw', 'conv_dw_b', 'conv_ln_g', 'conv_ln_b', 'conv_w_out', 'conv_b_out',
            'sgu_w_in', 'sgu_b_in', 'sgu_ln_g', 'sgu_ln_b', 'sgu_ws', 'sgu_bs', 'sgu_w_out', 'sgu_b_out',
            'ffn_w_in', 'ffn_dw', 'ffn_dw_b', 'ffn_w_out', 'final_g']
_FLAT_COLS = 1024


def _full_shape(a, axis):
    s = list(a.shape)
    if axis is not None:
        s[axis] *= _N_CHIPS
    return tuple(s)


def _pack(arrays):
    flat = jnp.concatenate([a.reshape(-1).astype(F32) for a in arrays])
    pad = (-flat.shape[0]) % (8 * _FLAT_COLS)
    return jnp.pad(flat, (0, pad)).reshape(-1, _FLAT_COLS)


def _unpack(flat2d, shapes):
    flat = flat2d.reshape(-1)
    out, off = [], 0
    for s in shapes:
        n = math.prod(s)
        out.append(flat[off:off + n].reshape(s))
        off += n
    return out


def _shard_of(full, axis, chip):
    if axis is None:
        return full
    size = full.shape[axis] // _N_CHIPS
    return lax.dynamic_slice_in_dim(full, chip * size, size, axis)


def _unshard(gathered, axis):
    moved = jnp.moveaxis(gathered, 0, axis)
    shape = list(gathered.shape[1:])
    shape[axis] *= _N_CHIPS
    return moved.reshape(shape)


def kernel(x, c, norm1_g, norm2_g, ada_w, ada_b, attn_wqkv, attn_bqkv, attn_sinks, attn_wo, attn_bo, conv_w_in, conv_b_in, conv_dw, conv_dw_b, conv_ln_g, conv_ln_b, conv_w_out, conv_b_out, sgu_w_in, sgu_b_in, sgu_ln_g, sgu_ln_b, sgu_ws, sgu_bs, sgu_w_out, sgu_b_out, ffn_w_in, ffn_dw, ffn_dw_b, ffn_w_out, final_g, loss_target, m_norm1_g, m_norm2_g, m_ada_w, m_ada_b, m_attn_wqkv, m_attn_bqkv, m_attn_sinks, m_attn_wo, m_attn_bo, m_conv_w_in, m_conv_b_in, m_conv_dw, m_conv_dw_b, m_conv_ln_g, m_conv_ln_b, m_conv_w_out, m_conv_b_out, m_sgu_w_in, m_sgu_b_in, m_sgu_ln_g, m_sgu_ln_b, m_sgu_ws, m_sgu_bs, m_sgu_w_out, m_sgu_b_out, m_ffn_w_in, m_ffn_dw, m_ffn_dw_b, m_ffn_w_out, m_final_g, v_norm1_g, v_norm2_g, v_ada_w, v_ada_b, v_attn_wqkv, v_attn_bqkv, v_attn_sinks, v_attn_wo, v_attn_bo, v_conv_w_in, v_conv_b_in, v_conv_dw, v_conv_dw_b, v_conv_ln_g, v_conv_ln_b, v_conv_w_out, v_conv_b_out, v_sgu_w_in, v_sgu_b_in, v_sgu_ln_g, v_sgu_ln_b, v_sgu_ws, v_sgu_bs, v_sgu_w_out, v_sgu_b_out, v_ffn_w_in, v_ffn_dw, v_ffn_dw_b, v_ffn_w_out, v_final_g):
    args = dict(locals())
    wts = {n: args[n] for n in _WEIGHTS}
    mom_m = {n: args["m_" + n] for n in _WEIGHTS}
    mom_v = {n: args["v_" + n] for n in _WEIGHTS}

    ix, iy, ic = lax.axis_index("x"), lax.axis_index("y"), lax.axis_index("c")
    chip = 2 * ix + iy
    xs = x[0]
    tgt = loss_target[0]
    t, d = xs.shape
    nh = d // _HEAD_DIM
    depth = ada_w.shape[0]
    ncols = ada_w.shape[2]
    n_mod = ncols * _N_CHIPS // d

    small_sharded = [n for n in _WEIGHTS if _SMALL.get(n) is not None]
    packed_small = _pack([wts[n] for n in small_sharded])
    mixers = {0: ("attn_wqkv", "attn_wo"), 1: ("conv_w_in", "conv_w_out"), 2: ("sgu_w_in", "sgu_w_out")}

    def shards_of(keys):
        return [(_cast_bf16_layer(wts[n], j, "cast_%s_%d" % (n, j)), _BIG_AXIS[n] - 1) for n, j in keys]

    c_all = _gather_all("gather_c", c)[:, 0, :]
    ada_b_cols = lax.dynamic_slice_in_dim(ada_b, chip * ncols, ncols, 1)[:, None, :]
    mod_cols = _ada_mod(c_all, ada_w, ada_b_cols, "ada_mod")
    mine = lax.dynamic_index_in_dim(mod_cols.reshape(depth, _N_CHIPS, 2, ncols), ic, 2, keepdims=False)
    got = _swap_chips("swap_mod", jnp.moveaxis(mine, 1, 0))

    first = [(n, 0) for n in mixers[0]]
    gather0 = _push_start("start_gather_0", *_gather_pushes(shards_of(first) + [(packed_small, 0)]), got)
    mod = _behind(jnp.moveaxis(got, 0, 1).reshape(depth, n_mod, 1, d), [gather0], "behind_gather_start")
    lands0 = _push_wait("wait_gather_0", gather0, mod)
    wfull = dict(zip(first, lands0))
    small_rows = lands0[-1].reshape(_N_CHIPS, -1, _FLAT_COLS)
    pieces = [_unpack(small_rows[s], [wts[n].shape for n in small_sharded]) for s in range(_N_CHIPS)]
    full = {n: wts[n] for n in _WEIGHTS if n in _SMALL and _SMALL[n] is None}
    for k, n in enumerate(small_sharded):
        full[n] = _unshard(jnp.stack([pieces[s][k] for s in range(_N_CHIPS)]), _SMALL[n])

    tables = _attn_tables(nh)
    zeros_d = jnp.zeros((1, d), F32)
    zeros_f2 = jnp.zeros((1, ffn_w_in.shape[2] * _N_CHIPS), F32)
    row = lambda a: a.reshape(1, -1)

    saved = []
    xcur = xs
    for i in range(depth):
        sh1, sc1, g1, sh2, sc2, g2 = (mod[i, k] for k in range(n_mod))
        kind, j = i % 3, i // 3
        tag = "L%d_" % i
        more = i + 1 < depth
        mid_keys = ([("ffn_w_in", 0), ("ffn_w_out", 0)] if i == 0 else []) + \
                   ([(n, (i + 1) // 3) for n in mixers[(i + 1) % 3]] if more else [])
        mid_carry = _gather_pushes(shards_of(mid_keys)) if mid_keys else None
        in_carry = _gather_pushes(shards_of([("ffn_w_in", i + 1)])) if more else None
        out_carry = _gather_pushes(shards_of([("ffn_w_out", i + 1)])) if more else None
        w_in, w_out = wfull[(mixers[kind][0], j)], wfull[(mixers[kind][1], j)]
        if kind == 0:
            b_in, b_out = row(full["attn_bqkv"][j]), row(full["attn_bo"][j])
        elif kind == 1:
            b_in, b_out = row(full["conv_b_in"][j]), row(full["conv_b_out"][j])
        else:
            b_in, b_out = row(full["sgu_b_in"][j]), row(full["sgu_b_out"][j])
        (h1, z), _ = _norm_mod_matmul(xcur, row(full["norm1_g"][i]), sc1, sh1, w_in, 0, b_in, tag + "mixer_in")
        if kind == 0:
            a, lands = _attn_fwd(z, tables, full["attn_sinks"][j], nh, tag + "attn", mid_carry)
        elif kind == 1:
            a, lands = _conv_mid_fwd(z, full["conv_dw"][j], row(full["conv_dw_b"][j]), row(full["conv_ln_g"][j]),
                                     row(full["conv_ln_b"][j]), tag + "conv_mid", mid_carry)
        else:
            a, lands = _sgu_mid_fwd(z, row(full["sgu_ln_g"][j]), row(full["sgu_ln_b"][j]), full["sgu_ws"][j], full["sgu_bs"][j].T,
                                    tag + "sgu_mid", mid_carry)
        wfull.update(zip(mid_keys, lands))
        (y1, xmid), _ = _matmul_resid(a, w_out, 0, b_out, xcur, g1, tag + "mixer_out")
        (h2, zf), lands = _norm_mod_matmul(xmid, row(full["norm2_g"][i]), sc2, sh2, wfull[("ffn_w_in", i)], 0, zeros_f2,
                                           tag + "ffn_in", in_carry)
        wfull.update(zip([("ffn_w_in", i + 1)], lands))
        act = _ffn_mid_fwd(zf, full["ffn_dw"][i], row(full["ffn_dw_b"][i]), tag + "ffn_mid")
        (y2, xnext), lands = _matmul_resid(act, wfull[("ffn_w_out", i)], 0, zeros_d, xmid, g2, tag + "ffn_out", out_carry)
        wfull.update(zip([("ffn_w_out", i + 1)], lands))
        saved.append(dict(x=xcur, h1=h1, z=z, a=a, y1=y1, xmid=xmid, h2=h2, zf=zf, act=act, y2=y2, w_in=w_in, w_out=w_out))
        xcur = xnext

    loss_row, dx, d_final_g = _final_loss(xcur, row(final_g), tgt, "final_loss")

    small_g = {n: jnp.zeros(_full_shape(wts[n], _SMALL[n]), F32) for n in _SMALL}
    small_g["final_g"] = d_final_g[0]
    dmod = [None] * depth
    pending = []
    slots = {}

    def put(name, j, val):
        small_g[name] = small_g[name].at[j].set(val.reshape(small_g[name].shape[1:]))

    for i in reversed(range(depth)):
        sv = saved[i]
        sh1, sc1, g1, sh2, sc2, g2 = (mod[i, k] for k in range(n_mod))
        kind, j = i % 3, i // 3
        tag = "L%d_" % i
        dy2, dact, dg2, _ = _outproj_bwd(dx, sv["y2"], g2, wfull[("ffn_w_out", i)], 0, tag + "ffn_out_bwd")
        g_ffn_out = _matmul_tn(sv["act"], dy2, tag + "ffn_out_wgrad")
        dzf, d_fdw, d_fdwb = _ffn_mid_bwd(sv["zf"], dact, full["ffn_dw"][i], row(full["ffn_dw_b"][i]), tag + "ffn_mid_bwd")
        put("ffn_dw", i, d_fdw)
        put("ffn_dw_b", i, d_fdwb)
        g_ffn_in = _matmul_tn(sv["h2"], dzf, tag + "ffn_in_wgrad")
        dxmid, dn2, dsc2, dsh2 = _inproj_bwd(dzf, wfull[("ffn_w_in", i)], 0, sv["xmid"], dx, row(full["norm2_g"][i]), sc2,
                                             tag + "ffn_in_bwd")
        put("norm2_g", i, dn2)
        pending += [(("ffn_w_in", i), g_ffn_in, 1), (("ffn_w_out", i), g_ffn_out, 0)]
        dy1, da, dg1, dbo = _outproj_bwd(dxmid, sv["y1"], g1, sv["w_out"], 0, tag + "mixer_out_bwd")
        w_in_name, w_out_name = mixers[kind]
        g_mix_out = _matmul_tn(sv["a"], dy1, tag + "mixer_out_wgrad")
        carry = _scatter_pushes([(g, axis) for _, g, axis in pending])
        if kind == 0:
            (dz, dsink, dbin), lands = _attn_bwd(sv["z"], da, tables, full["attn_sinks"][j], nh, tag + "attn_bwd", carry)
            put("attn_sinks", j, dsink[0, :nh])
            put("attn_bqkv", j, dbin)
            put("attn_bo", j, dbo)
        elif kind == 1:
            (dz, d_dw, d_dwb, d_lng, d_lnb, dbin), lands = _conv_mid_bwd(
                sv["z"], da, full["conv_dw"][j], row(full["conv_dw_b"][j]), row(full["conv_ln_g"][j]), row(full["conv_ln_b"][j]),
                tag + "conv_mid_bwd", carry)
            for nme, val in (("conv_dw", d_dw), ("conv_dw_b", d_dwb), ("conv_ln_g", d_lng), ("conv_ln_b", d_lnb),
                             ("conv_b_in", dbin), ("conv_b_out", dbo)):
                put(nme, j, val)
        else:
            (dz, d_lng, d_lnb, d_ws, d_bst, dbin), lands = _sgu_mid_bwd(
                sv["z"], da, row(full["sgu_ln_g"][j]), row(full["sgu_ln_b"][j]), full["sgu_ws"][j], full["sgu_bs"][j].T,
                tag + "sgu_mid_bwd", carry)
            ng = sgu_ws.shape[1]
            for nme, val in (("sgu_ln_g", d_lng), ("sgu_ln_b", d_lnb), ("sgu_ws", d_ws), ("sgu_bs", d_bst[:, :ng].T),
                             ("sgu_b_in", dbin), ("sgu_b_out", dbo)):
                put(nme, j, val)
        slots.update(zip([key for key, _, _ in pending], lands))
        pending = []
        g_mix_in = _matmul_tn(sv["h1"], dz, tag + "mixer_in_wgrad")
        dx, dn1, dsc1, dsh1 = _inproj_bwd(dz, sv["w_in"], 0, sv["x"], dxmid, row(full["norm1_g"][i]), sc1, tag + "mixer_in_bwd")
        put("norm1_g", i, dn1)
        dmod[i] = jnp.concatenate([dsh1, dsc1, dg1, dsh2, dsc2, dg2], axis=1)
        pending += [((w_in_name, j), g_mix_in, 1), ((w_out_name, j), g_mix_out, 0)]

    last_scatter = _push_start("start_scatter_last", *_scatter_pushes([(g, axis) for _, g, axis in pending]), dx)
    grad_x = dx[None]
    loss = lax.psum(loss_row[0, 0], ("x", "y", "c"))

    small_names = [n for n in _WEIGHTS if n in _SMALL and n != "ada_b"]
    dmod_own = jnp.concatenate(dmod, axis=0)
    packed = _behind(_pack([small_g[n] for n in small_names] + [dmod_own]), [last_scatter], "behind_last_scatter_start")
    packed_all = _gather_all("gather_small_grads", packed)
    summed = _sum_slots(packed_all, "sum_small_grads")
    small_full = dict(zip(small_names, _unpack(summed, [small_g[n].shape for n in small_names])))
    n_small = sum(math.prod(small_g[n].shape) for n in small_names)
    dmod_all = packed_all.reshape(_N_DEV, -1)[:, n_small:n_small + dmod_own.size].reshape(_N_DEV, depth, n_mod * d)
    small_full["ada_b"] = _sum_slots(dmod_all, "sum_ada_b_grad")
    dmod_cols = lax.dynamic_slice_in_dim(jnp.moveaxis(dmod_all, 0, 1), chip * ncols, ncols, 2)
    g_ada_w = _ada_wgrad(c_all.T, dmod_cols, "ada_wgrad")

    slots.update(zip([key for key, _, _ in pending], _push_wait("wait_scatter_last", last_scatter, g_ada_w)))
    partial =[_sum_slot_layers([slots[(n, j)] for j in range(wts[n].shape[0])], "sum_chips_" + n) for n in _BIG]
    other = _swap_sibling("swap_cores", partial)

    outs = {}
    for n, mine_p, theirs_p in zip(_BIG, partial, other):
        outs[n] = _adamw(wts[n], [mine_p, theirs_p], mom_m[n], mom_v[n], "adamw_" + n)
    outs["ada_w"] = _adamw(ada_w, [g_ada_w], m_ada_w, v_ada_w, "adamw_ada_w")
    sm_names = [n for n in _WEIGHTS if n in _SMALL]
    g_loc = [_shard_of(small_full[n], _SMALL[n], chip) for n in sm_names]
    packs = [_pack([src[n] for n in sm_names]) for src in (wts, mom_m, mom_v)]
    sm_out = _adamw(packs[0], [_pack(g_loc)], packs[1], packs[2], "adamw_small")
    shapes = [wts[n].shape for n in sm_names]
    unpacked = [_unpack(o, shapes) for o in sm_out]
    for k, n in enumerate(sm_names):
        outs[n] = tuple(u[k] for u in unpacked)

    result = [loss, grad_x]
    for which in range(4):
        result += [outs[n][which] for n in _WEIGHTS]
    return tuple(result)
```

```python
import math

import jax
import jax.numpy as jnp
from jax import lax
from jax.experimental import pallas as pl
from jax.experimental.pallas import tpu as pltpu

F32, BF16 = jnp.float32, jnp.bfloat16
SDS = jax.ShapeDtypeStruct
MESH_IDS = pl.DeviceIdType.MESH

_VMEM_LIMIT_BYTES = 48 * 1024 * 1024
_LANES = 128
_NORM_EPS = 1e-6
_NEG_INF = -1e30
_HEAD_DIM = 64
_N_KV = 4
_ATTN_BLOCK = 128
_ATTN_BLOCKS_PER_STEP = 2
_ROW_TILE = 512
_COL_TILE = 512
_WGRAD_TILE = 1536
_CONV_HALO = 32
_FFN_HALO = 16
_ADAM_LR, _ADAM_B1, _ADAM_B2, _ADAM_EPS, _ADAM_WD, _ADAM_STEP = 0.001, 0.9, 0.999, 1e-08, 0.01, 10
_N_CHIPS = 4
_N_DEV = 8


def _tile(n, pref, unit):
    if n <= pref:
        return n
    t = pref - pref % unit
    while t >= unit:
        if n % t == 0:
            return t
        t -= unit
    return n


def _params(n_axes):
    return pltpu.CompilerParams(dimension_semantics=("arbitrary",) * n_axes, vmem_limit_bytes=_VMEM_LIMIT_BYTES)


def _row(n):
    return pl.BlockSpec((1, n), lambda *_: (0, 0))


def _sigmoid(v):
    return 1.0 / (1.0 + jnp.exp(-v))


def _nt(a, b):
    return lax.dot_general(a, b, (((1,), (1,)), ((), ())), preferred_element_type=F32)


def _tn(a, b):
    return lax.dot_general(a, b, (((0,), (0,)), ((), ())), preferred_element_type=F32)


def _carrying(carry, body, *, name, grid, in_specs, out_specs, out_shape, scratch_shapes=(), compiler_params):
    single = not isinstance(out_shape, (list, tuple))
    if carry is None:
        call = pl.pallas_call(body, name=name, grid=grid, in_specs=in_specs, out_specs=out_specs, out_shape=out_shape,
                              scratch_shapes=list(scratch_shapes), compiler_params=compiler_params)
        return lambda *operands: (call(*operands), [])
    srcs, land_shapes, items = carry
    out_specs_l, out_shape_l = ([out_specs], [out_shape]) if single else (list(out_specs), list(out_shape))
    n_in, n_out, n_scr, n_src, n_land = len(in_specs), len(out_shape_l), len(scratch_shapes), len(srcs), len(land_shapes)
    n_remote = len(items) * len(_CHIP_FLIPS)
    last = grid[0] - 1

    def carrier(*refs):
        ins, src_refs = refs[:n_in], refs[n_in:n_in + n_src]
        o0 = n_in + n_src
        outs, land_refs = refs[o0:o0 + n_out], refs[o0 + n_out:o0 + n_out + n_land]
        s0 = o0 + n_out + n_land
        scratch, (send_sems, recv_sems, local_sems) = refs[s0:s0 + n_scr], refs[s0 + n_scr:]
        me, plan = _push_plan(src_refs, land_refs, items)
        step = pl.program_id(0)

        @pl.when(step == 0)
        def _():
            own = [pltpu.make_async_copy(_at(src_refs[si], src_idx(*me)), _at(land_refs[di], dst_idx(*me)), local_sems.at[k])
                   for k, (si, di, src_idx, dst_idx) in enumerate(items)]
            for cp in own:
                cp.start()
            for cp in own:
                cp.wait()

        for n, (peer, mine_src, mine_dst, _, _) in enumerate(plan):
            rows = mine_src.shape[0]
            chunks = _chunks_of(rows, grid[0])
            per = rows // chunks

            @pl.when(step < chunks)
            def _(n=n, peer=peer, mine_src=mine_src, mine_dst=mine_dst, per=per):
                part = pl.ds(pl.multiple_of(step * per, per), per)
                _remote_copy(mine_src.at[part], mine_dst.at[part], send_sems, recv_sems, n, peer).start()

        body(*ins, *outs, *scratch)

        @pl.when(step == last)
        def _():
            for n, (peer, mine_src, mine_dst, their_src, their_dst) in enumerate(plan):
                _remote_copy(their_src, their_dst, send_sems, recv_sems, n, peer).wait_recv()
            for n, (peer, mine_src, mine_dst, their_src, their_dst) in enumerate(plan):
                _remote_copy(mine_src, mine_dst, send_sems, recv_sems, n, peer).wait_send()

    def run(*operands):
        res = pl.pallas_call(
            carrier, name=name, grid=grid, in_specs=list(in_specs) + [_HBM] * n_src, out_specs=out_specs_l + [_HBM] * n_land,
            out_shape=out_shape_l + list(land_shapes),
            scratch_shapes=list(scratch_shapes) + [pltpu.SemaphoreType.DMA((n_remote,)), pltpu.SemaphoreType.DMA((n_remote,)),
                                                   pltpu.SemaphoreType.DMA((len(items),))],
            compiler_params=compiler_params)(*operands, *srcs)
        return (res[0] if single else list(res[:n_out])), list(res[n_out:])

    return run


def _resident(shape, index_map):
    return pl.BlockSpec(shape, index_map, pipeline_mode=pl.Buffered(1))


def _norm_mod_matmul(x, gn, sc, sh, w, layer, b, name, carry=None):
    t, d = x.shape
    n = w.shape[2]
    tm, tn = _tile(t, _ROW_TILE, 16), _tile(n, _COL_TILE, _LANES)

    def body(x_ref, gn_ref, sc_ref, sh_ref, w_ref, b_ref, h_ref, z_ref):
        xf = x_ref[...]
        r = lax.rsqrt(jnp.mean(xf * xf, axis=-1, keepdims=True) + _NORM_EPS)
        h_ref[...] = ((xf * r * gn_ref[...]) * (1.0 + sc_ref[...]) + sh_ref[...]).astype(BF16)
        for c0 in range(0, n, tn):
            z = jnp.dot(h_ref[...], w_ref[:, c0:c0 + tn], preferred_element_type=F32) + b_ref[:, c0:c0 + tn]
            z_ref[:, c0:c0 + tn] = z.astype(BF16)

    return _carrying(
        carry, body, name=name, grid=(t // tm,),
        in_specs=[pl.BlockSpec((tm, d), lambda i: (i, 0)), _row(d), _row(d), _row(d),
                  _resident((None, d, n), lambda i: (layer, 0, 0)), _row(n)],
        out_specs=[pl.BlockSpec((tm, d), lambda i: (i, 0)), pl.BlockSpec((tm, n), lambda i: (i, 0))],
        out_shape=[SDS((t, d), BF16), SDS((t, n), BF16)], compiler_params=_params(1),
    )(x, gn, sc, sh, w, b)


def _matmul_resid(a, w, layer, b, x, gate, name, carry=None):
    t, k = a.shape
    d = w.shape[2]
    tm, tn = _tile(t, _ROW_TILE, 16), _tile(d, _COL_TILE, _LANES)

    def body(a_ref, w_ref, b_ref, x_ref, g_ref, y_ref, xo_ref):
        for c0 in range(0, d, tn):
            cs = slice(c0, c0 + tn)
            y = jnp.dot(a_ref[...], w_ref[:, cs], preferred_element_type=F32) + b_ref[:, cs]
            y_ref[:, cs] = y.astype(BF16)
            xo_ref[:, cs] = x_ref[:, cs] + g_ref[:, cs] * y

    blk = pl.BlockSpec((tm, d), lambda i: (i, 0))
    return _carrying(
        carry, body, name=name, grid=(t // tm,),
        in_specs=[pl.BlockSpec((tm, k), lambda i: (i, 0)), _resident((None, k, d), lambda i: (layer, 0, 0)), _row(d), blk, _row(d)],
        out_specs=[blk, blk], out_shape=[SDS((t, d), BF16), SDS((t, d), F32)], compiler_params=_params(1),
    )(a, w, b, x, gate)


def _outproj_bwd(dxo, y, gate, w, layer, name):
    t, d = dxo.shape
    k = w.shape[1]
    tm, tk = _tile(t, _ROW_TILE, 16), _tile(k, _COL_TILE, _LANES)

    def body(dxo_ref, y_ref, g_ref, w_ref, dy_ref, da_ref, dg_ref, db_ref):
        @pl.when(pl.program_id(0) == 0)
        def _():
            dg_ref[...] = jnp.zeros_like(dg_ref)
            db_ref[...] = jnp.zeros_like(db_ref)

        dxf = dxo_ref[...]
        dyf = dxf * g_ref[...]
        dy_ref[...] = dyf.astype(BF16)
        dg_ref[...] += jnp.sum(dxf * y_ref[...].astype(F32), axis=0, keepdims=True)
        db_ref[...] += jnp.sum(dyf, axis=0, keepdims=True)
        for c0 in range(0, k, tk):
            da_ref[:, c0:c0 + tk] = _nt(dy_ref[...], w_ref[c0:c0 + tk, :]).astype(BF16)

    full = pl.BlockSpec((tm, d), lambda i: (i, 0))
    return pl.pallas_call(
        body, name=name, grid=(t // tm,),
        in_specs=[full, full, _row(d), _resident((None, k, d), lambda i: (layer, 0, 0))],
        out_specs=[full, pl.BlockSpec((tm, k), lambda i: (i, 0)), _row(d), _row(d)],
        out_shape=[SDS((t, d), BF16), SDS((t, k), BF16), SDS((1, d), F32), SDS((1, d), F32)],
        compiler_params=_params(1),
    )(dxo, y, gate, w)


def _matmul_tn(a, b, name):
    t, ka = a.shape
    nb = b.shape[1]
    tka, tnb, tt = _tile(ka, _WGRAD_TILE, _LANES), _tile(nb, _WGRAD_TILE, _LANES), _tile(t, 2 * _ROW_TILE, 16)
    nt = t // tt

    def body(a_ref, b_ref, o_ref, acc):
        s = pl.program_id(2)

        @pl.when(s == 0)
        def _():
            acc[...] = jnp.zeros_like(acc)

        acc[...] += _tn(a_ref[...], b_ref[...])

        @pl.when(s == nt - 1)
        def _():
            o_ref[...] = acc[...].astype(BF16)

    return pl.pallas_call(
        body, name=name, grid=(ka // tka, nb // tnb, nt),
        in_specs=[pl.BlockSpec((tt, tka), lambda i, j, s: (s, i)), pl.BlockSpec((tt, tnb), lambda i, j, s: (s, j))],
        out_specs=pl.BlockSpec((tka, tnb), lambda i, j, s: (i, j)),
        out_shape=SDS((ka, nb), BF16), scratch_shapes=[pltpu.VMEM((tka, tnb), F32)], compiler_params=_params(3),
    )(a, b)


def _inproj_bwd(dz, w, layer, x, dxo, gn, sc, name, carry=None):
    t, n = dz.shape
    d = x.shape[1]
    tm, tk = _tile(t, _ROW_TILE, 16), _tile(n, _COL_TILE, _LANES)

    def body(dz_ref, w_ref, x_ref, dxo_ref, gn_ref, sc_ref, dx_ref, dgn_ref, dsc_ref, dsh_ref, acc):
        @pl.when(pl.program_id(0) == 0)
        def _():
            dgn_ref[...] = jnp.zeros_like(dgn_ref)
            dsc_ref[...] = jnp.zeros_like(dsc_ref)
            dsh_ref[...] = jnp.zeros_like(dsh_ref)

        for c0 in range(0, n, tk):
            part = _nt(dz_ref[:, c0:c0 + tk], w_ref[:, c0:c0 + tk])
            if c0 == 0:
                acc[...] = part
            else:
                acc[...] += part
        dh = acc[...]
        xf = x_ref[...]
        r = lax.rsqrt(jnp.mean(xf * xf, axis=-1, keepdims=True) + _NORM_EPS)
        xn = xf * r
        gnv = gn_ref[...]
        dsh_ref[...] += jnp.sum(dh, axis=0, keepdims=True)
        dsc_ref[...] += jnp.sum(dh * (xn * gnv), axis=0, keepdims=True)
        drn = dh * (1.0 + sc_ref[...])
        dgn_ref[...] += jnp.sum(drn * xn, axis=0, keepdims=True)
        dxn = drn * gnv
        dx_ref[...] = dxo_ref[...] + r * (dxn - xn * jnp.mean(dxn * xn, axis=-1, keepdims=True))

    full = pl.BlockSpec((tm, d), lambda i: (i, 0))
    return _carrying(
        carry, body, name=name, grid=(t // tm,),
        in_specs=[pl.BlockSpec((tm, n), lambda i: (i, 0)), _resident((None, d, n), lambda i: (layer, 0, 0)),
                  full, full, _row(d), _row(d)],
        out_specs=[full, _row(d), _row(d), _row(d)],
        out_shape=[SDS((t, d), F32), SDS((1, d), F32), SDS((1, d), F32), SDS((1, d), F32)],
        scratch_shapes=[pltpu.VMEM((tm, d), F32)], compiler_params=_params(1),
    )(dz, w, x, dxo, gn, sc)


def _final_loss(x, g, target, name):
    t, d = x.shape
    tm = _tile(t, _ROW_TILE, 8)

    def body(x_ref, g_ref, t_ref, loss_ref, dx_ref, dg_ref):
        @pl.when(pl.program_id(0) == 0)
        def _():
            loss_ref[...] = jnp.zeros_like(loss_ref)
            dg_ref[...] = jnp.zeros_like(dg_ref)

        xf = x_ref[...]
        r = lax.rsqrt(jnp.mean(xf * xf, axis=-1, keepdims=True) + _NORM_EPS)
        xn = xf * r
        gv = g_ref[...]
        e = xn * gv - t_ref[...]
        per_row = jnp.mean(e * e, axis=-1, keepdims=True)
        loss_ref[...] += 0.5 * jnp.sum(per_row, axis=0, keepdims=True)
        dy = e * (1.0 / d)
        dg_ref[...] += jnp.sum(dy * xn, axis=0, keepdims=True)
        dxn = dy * gv
        dx_ref[...] = r * (dxn - xn * jnp.mean(dxn * xn, axis=-1, keepdims=True))

    full = pl.BlockSpec((tm, d), lambda i: (i, 0))
    return pl.pallas_call(
        body, name=name, grid=(t // tm,), in_specs=[full, _row(d), full],
        out_specs=[_row(_LANES), full, _row(d)],
        out_shape=[SDS((1, _LANES), F32), SDS((t, d), F32), SDS((1, d), F32)], compiler_params=_params(1),
    )(x, g, target)


def _attn_tables(nh):
    group = nh // _N_KV
    slopes = 2.0 ** (-8.0 * jnp.arange(1, nh + 1, dtype=F32) / nh)
    qpos = jnp.arange(_ATTN_BLOCK) + _ATTN_BLOCK
    kpos = jnp.arange(2 * _ATTN_BLOCK)
    dist = qpos[:, None] - kpos[None, :]
    band = (dist >= 0) & (dist < _ATTN_BLOCK)
    bias = jnp.where(band[None], -slopes[:, None, None] * dist.astype(F32)[None], _NEG_INF)
    first = jnp.where((kpos < _ATTN_BLOCK)[None, None, :], _NEG_INF, bias)
    return jnp.stack([first, bias]).reshape(2, _N_KV, group * _ATTN_BLOCK, 2 * _ATTN_BLOCK)


def _attn_probs(q4, k2, tab, sink_ref, kv, group):
    hd, blk = _HEAD_DIM, _ATTN_BLOCK
    s = _nt(q4, k2) * (hd ** -0.5) + tab
    sink = jnp.concatenate([jnp.full((blk, 1), sink_ref[kv * group + g], F32) for g in range(group)], axis=0)
    m = jnp.maximum(jnp.max(s, axis=-1, keepdims=True), sink)
    e = jnp.exp(s - m)
    es = jnp.exp(sink - m)
    inv = 1.0 / (jnp.sum(e, axis=-1, keepdims=True) + es)
    return e * inv, es * inv


def _attn_operands(refs, b, kv, group):
    q_ref, kp_ref, kc_ref, vp_ref, vc_ref = refs
    hd, blk = _HEAD_DIM, _ATTN_BLOCK
    rows, cs = slice(b * blk, (b + 1) * blk), slice(kv * hd, (kv + 1) * hd)
    before = slice((b - 1) * blk, b * blk)
    k2 = jnp.concatenate([kp_ref[:, cs] if b == 0 else kc_ref[before, cs], kc_ref[rows, cs]], axis=0)
    v2 = jnp.concatenate([vp_ref[:, cs] if b == 0 else vc_ref[before, cs], vc_ref[rows, cs]], axis=0)
    q4 = jnp.concatenate([q_ref[rows, (kv * group + g) * hd:(kv * group + g + 1) * hd] for g in range(group)], axis=0)
    return q4, k2, v2


def _attn_specs(nh, nblk, step):
    group = nh // _N_KV
    blk, kvw = _ATTN_BLOCK, _N_KV * _HEAD_DIM
    prev = lambda i: jnp.maximum(step(i) * nblk - 1, 0)
    return [
        pl.BlockSpec((nblk * blk, nh * _HEAD_DIM), lambda i: (step(i), 0)),
        pl.BlockSpec((blk, kvw), lambda i: (prev(i), group)),
        pl.BlockSpec((nblk * blk, kvw), lambda i: (step(i), group)),
        pl.BlockSpec((blk, kvw), lambda i: (prev(i), group + 1)),
        pl.BlockSpec((nblk * blk, kvw), lambda i: (step(i), group + 1)),
        pl.BlockSpec((2, _N_KV, group * blk, 2 * blk), lambda i: (0, 0, 0, 0)),
        pl.BlockSpec(memory_space=pltpu.SMEM),
    ]


def _attn_fwd(qkv, tables, sinks, nh, name, carry=None):
    t = qkv.shape[0]
    group, hd, blk = nh // _N_KV, _HEAD_DIM, _ATTN_BLOCK
    nblk = _ATTN_BLOCKS_PER_STEP if t % (_ATTN_BLOCKS_PER_STEP * blk) == 0 else 1

    def body(q_ref, kp_ref, kc_ref, vp_ref, vc_ref, tab_ref, sink_ref, o_ref):
        i = pl.program_id(0)
        for b in range(nblk):
            tab = tab_ref.at[jnp.minimum(i * nblk + b, 1)]
            outs = [None] * nh
            for kv in range(_N_KV):
                q4, k2, v2 = _attn_operands((q_ref, kp_ref, kc_ref, vp_ref, vc_ref), b, kv, group)
                p, _ = _attn_probs(q4, k2, tab[kv], sink_ref, kv, group)
                o4 = jnp.dot(p.astype(BF16), v2, preferred_element_type=F32)
                for g in range(group):
                    outs[kv * group + g] = o4[g * blk:(g + 1) * blk, :]
            o_ref[b * blk:(b + 1) * blk, :] = jnp.concatenate(outs, axis=1).astype(BF16)

    return _carrying(
        carry, body, name=name, grid=(t // (nblk * blk),), in_specs=_attn_specs(nh, nblk, lambda i: i),
        out_specs=pl.BlockSpec((nblk * blk, nh * hd), lambda i: (i, 0)), out_shape=SDS((t, nh * hd), BF16),
        compiler_params=_params(1),
    )(qkv, qkv, qkv, qkv, qkv, tables, sinks)


def _attn_bwd(qkv, do, tables, sinks, nh, name, carry=None):
    t, wq = qkv.shape
    group, hd, blk = nh // _N_KV, _HEAD_DIM, _ATTN_BLOCK
    nblk = _ATTN_BLOCKS_PER_STEP if t % (_ATTN_BLOCKS_PER_STEP * blk) == 0 else 1
    nsteps = t // (nblk * blk)
    kvw = _N_KV * hd
    step = lambda i: nsteps - 1 - i

    def body(q_ref, kp_ref, kc_ref, vp_ref, vc_ref, tab_ref, sink_ref, do_ref, dqkv_ref, dsink_ref, db_ref, ck, cv):
        i = pl.program_id(0)

        @pl.when(i == 0)
        def _():
            ck[...] = jnp.zeros_like(ck)
            cv[...] = jnp.zeros_like(cv)
            dsink_ref[...] = jnp.zeros_like(dsink_ref)
            db_ref[...] = jnp.zeros_like(db_ref)

        lane = lax.broadcasted_iota(jnp.int32, (1, _LANES), 1)
        carry_k = [ck[:, kv * hd:(kv + 1) * hd] for kv in range(_N_KV)]
        carry_v = [cv[:, kv * hd:(kv + 1) * hd] for kv in range(_N_KV)]
        dsink = jnp.zeros((1, _LANES), F32)
        dbias = jnp.zeros((1, wq), F32)
        for b in reversed(range(nblk)):
            rows = slice(b * blk, (b + 1) * blk)
            tab = tab_ref.at[jnp.minimum(step(i) * nblk + b, 1)]
            dq, dk, dv = [None] * nh, [None] * _N_KV, [None] * _N_KV
            for kv in range(_N_KV):
                q4, k2, v2 = _attn_operands((q_ref, kp_ref, kc_ref, vp_ref, vc_ref), b, kv, group)
                p, ps = _attn_probs(q4, k2, tab[kv], sink_ref, kv, group)
                do4 = jnp.concatenate([do_ref[rows, (kv * group + g) * hd:(kv * group + g + 1) * hd] for g in range(group)], axis=0)
                dp = _nt(do4, v2)
                dl = jnp.sum(p * dp, axis=-1, keepdims=True)
                ds = (p * (dp - dl)).astype(BF16)
                dsk = -ps * dl
                for g in range(group):
                    dsink = dsink + jnp.where(lane == kv * group + g, jnp.sum(dsk[g * blk:(g + 1) * blk, :]), 0.0)
                dq4 = jnp.dot(ds, k2, preferred_element_type=F32) * (hd ** -0.5)
                for g in range(group):
                    dq[kv * group + g] = dq4[g * blk:(g + 1) * blk, :]
                dk2 = _tn(q4, ds).T * (hd ** -0.5)
                dv2 = _tn(do4, p.astype(BF16)).T
                dk[kv] = dk2[blk:, :] + carry_k[kv]
                dv[kv] = dv2[blk:, :] + carry_v[kv]
                carry_k[kv], carry_v[kv] = dk2[:blk, :], dv2[:blk, :]
            dqkv = jnp.concatenate(dq + dk + dv, axis=1)
            dqkv_ref[rows, :] = dqkv.astype(BF16)
            dbias = dbias + jnp.sum(dqkv, axis=0, keepdims=True)
        for kv in range(_N_KV):
            ck[:, kv * hd:(kv + 1) * hd] = carry_k[kv]
            cv[:, kv * hd:(kv + 1) * hd] = carry_v[kv]
        db_ref[...] += dbias
        dsink_ref[...] += dsink

    return _carrying(
        carry, body, name=name, grid=(nsteps,),
        in_specs=_attn_specs(nh, nblk, step) + [pl.BlockSpec((nblk * blk, nh * hd), lambda i: (step(i), 0))],
        out_specs=[pl.BlockSpec((nblk * blk, wq), lambda i: (step(i), 0)), _row(_LANES), _row(wq)],
        out_shape=[SDS((t, wq), BF16), SDS((1, _LANES), F32), SDS((1, wq), F32)],
        scratch_shapes=[pltpu.VMEM((blk, kvw), F32), pltpu.VMEM((blk, kvw), F32)], compiler_params=_params(1),
    )(qkv, qkv, qkv, qkv, qkv, tables, sinks, do)


_SUBLANES = 8


_ALL_SHIFTS = tuple(range(1, _SUBLANES))


def _shift_copies(src_ref, sh_ref, shifts=_ALL_SHIFTS):
    rows = src_ref.shape[0]
    full = src_ref[...]
    for n, b in enumerate(shifts):
        sh_ref[n] = pltpu.roll(full, rows - b, axis=0)
    return sh_ref, shifts


def _rows_at(src_ref, shifted, off, r0, rg, cs):
    b = off % _SUBLANES
    if shifted is None or b not in shifted[1]:
        return src_ref[r0 + off:r0 + off + rg, cs]
    return shifted[0][shifted[1].index(b), r0 + off - b:r0 + off - b + rg, cs]


def _taps(src_ref, w_ref, dst_ref, n_rows, width, offs, rg, shifted=None):
    cg = _tile(width, 512, _LANES)
    for c0 in range(0, width, cg):
        cs = slice(c0, c0 + cg)
        wk = [w_ref[k:k + 1, cs] for k, _ in offs]
        for r0 in range(0, n_rows, rg):
            acc = None
            for (_, off), wv in zip(offs, wk):
                term = wv * _rows_at(src_ref, shifted, off, r0, rg, cs)
                acc = term if acc is None else acc + term
            dst_ref[r0:r0 + rg, cs] = acc


def _tap_grads(dy_ref, z_ref, out_ref, n_rows, width, offs, rg, shifted=None):
    cg = _tile(width, 512, _LANES)
    for c0 in range(0, width, cg):
        cs = slice(c0, c0 + cg)
        for k, off in offs:
            acc = None
            for r0 in range(0, n_rows, rg):
                term = dy_ref[r0:r0 + rg, cs] * _rows_at(z_ref, shifted, off, r0, rg, cs)
                acc = term if acc is None else acc + term
            out_ref[k:k + 1, cs] += jnp.sum(acc, axis=0, keepdims=True)


def _conv_mid_fwd(ag, dw, dwb, lng, lnb, name, carry=None):
    t, c2 = ag.shape
    c = c2 // 2
    kw = dw.shape[0]
    hl = _CONV_HALO
    tm = _tile(t, 256, hl)
    per = tm // hl

    def body(agp_ref, ag_ref, dw_ref, dwb_ref, lng_ref, lnb_ref, o_ref, zext, yb, zsh):
        i = pl.program_id(0)
        glu = lambda ref: ref[:, :c].astype(F32) * _sigmoid(ref[:, c:].astype(F32))
        zext[0:hl, :] = jnp.where(i > 0, glu(agp_ref), 0.0)
        zext[hl:, :] = glu(ag_ref)
        _taps(zext, dw_ref, yb, tm, c, [(k, hl - (kw - 1) + k) for k in range(kw)], 32, _shift_copies(zext, zsh))
        y = yb[...] + dwb_ref[...]
        mu = jnp.mean(y, axis=-1, keepdims=True)
        yc = y - mu
        rstd = lax.rsqrt(jnp.mean(yc * yc, axis=-1, keepdims=True) + _NORM_EPS)
        ln = yc * rstd * lng_ref[...] + lnb_ref[...]
        o_ref[...] = (ln * _sigmoid(ln)).astype(BF16)

    return _carrying(
        carry, body, name=name, grid=(t // tm,),
        in_specs=[pl.BlockSpec((hl, c2), lambda i: (jnp.maximum(i * per - 1, 0), 0)), pl.BlockSpec((tm, c2), lambda i: (i, 0)),
                  pl.BlockSpec((kw, c), lambda i: (0, 0)), _row(c), _row(c), _row(c)],
        out_specs=pl.BlockSpec((tm, c), lambda i: (i, 0)), out_shape=SDS((t, c), BF16),
        scratch_shapes=[pltpu.VMEM((hl + tm, c), F32), pltpu.VMEM((tm, c), F32), pltpu.VMEM((_SUBLANES - 1, hl + tm, c), F32)],
        compiler_params=_params(1),
    )(ag, ag, dw, dwb, lng, lnb)


def _conv_mid_bwd(ag, dzc, dw, dwb, lng, lnb, name, carry=None):
    t, c2 = ag.shape
    c = c2 // 2
    kw = dw.shape[0]
    hl = _CONV_HALO
    tm = _tile(t, 256, hl)
    per = tm // hl
    nt = t // tm
    last_halo = t // hl - 1

    def body(agp_ref, ag_ref, agn_ref, dzc_ref, dzcn_ref, dw_ref, dwb_ref, lng_ref, lnb_ref,
             dag_ref, ddw_ref, ddwb_ref, dlng_ref, dlnb_ref, dbin_ref, zext, yext, dyext, dzb, zsh, dysh):
        i = pl.program_id(0)

        @pl.when(i == 0)
        def _():
            for r in (ddw_ref, ddwb_ref, dlng_ref, dlnb_ref, dbin_ref):
                r[...] = jnp.zeros_like(r)

        glu = lambda ref: ref[:, :c].astype(F32) * _sigmoid(ref[:, c:].astype(F32))
        zext[0:hl, :] = jnp.where(i > 0, glu(agp_ref), 0.0)
        zext[hl:hl + tm, :] = glu(ag_ref)
        zext[hl + tm:, :] = glu(agn_ref)
        fwd_offs = [(k, hl - (kw - 1) + k) for k in range(kw)]
        z_shifted = _shift_copies(zext, zsh)
        _taps(zext, dw_ref, yext, tm + hl, c, fwd_offs, 32, z_shifted)
        y = yext[...] + dwb_ref[...]
        mu = jnp.mean(y, axis=-1, keepdims=True)
        yc = y - mu
        rstd = lax.rsqrt(jnp.mean(yc * yc, axis=-1, keepdims=True) + _NORM_EPS)
        xhat = yc * rstd
        lngv = lng_ref[...]
        ln = xhat * lngv + lnb_ref[...]
        sg = _sigmoid(ln)
        dz_out = jnp.concatenate([dzc_ref[...].astype(F32), jnp.where(i < nt - 1, dzcn_ref[...].astype(F32), 0.0)], axis=0)
        dln = dz_out * (sg * (1.0 + ln * (1.0 - sg)))
        dlng_ref[...] += jnp.sum((dln * xhat)[:tm], axis=0, keepdims=True)
        dlnb_ref[...] += jnp.sum(dln[:tm], axis=0, keepdims=True)
        dxh = dln * lngv
        dy = rstd * (dxh - jnp.mean(dxh, axis=-1, keepdims=True) - xhat * jnp.mean(dxh * xhat, axis=-1, keepdims=True))
        dyext[...] = dy
        ddwb_ref[...] += jnp.sum(dy[:tm], axis=0, keepdims=True)
        _taps(dyext, dw_ref, dzb, tm, c, [(k, kw - 1 - k) for k in range(kw)], 32, _shift_copies(dyext, dysh))
        _tap_grads(dyext, zext, ddw_ref, tm, c, fwd_offs, 32, z_shifted)
        a = ag_ref[:, :c].astype(F32)
        sgg = _sigmoid(ag_ref[:, c:].astype(F32))
        dz = dzb[...]
        da = dz * sgg
        dg = dz * a * sgg * (1.0 - sgg)
        dag_ref[:, :c] = da.astype(BF16)
        dag_ref[:, c:] = dg.astype(BF16)
        dbin_ref[:, :c] += jnp.sum(da, axis=0, keepdims=True)
        dbin_ref[:, c:] += jnp.sum(dg, axis=0, keepdims=True)

    prev = lambda i: (jnp.maximum(i * per - 1, 0), 0)
    nxt = lambda i: (jnp.minimum((i + 1) * per, last_halo), 0)
    return _carrying(
        carry, body, name=name, grid=(nt,),
        in_specs=[pl.BlockSpec((hl, c2), prev), pl.BlockSpec((tm, c2), lambda i: (i, 0)), pl.BlockSpec((hl, c2), nxt),
                  pl.BlockSpec((tm, c), lambda i: (i, 0)), pl.BlockSpec((hl, c), nxt),
                  pl.BlockSpec((kw, c), lambda i: (0, 0)), _row(c), _row(c), _row(c)],
        out_specs=[pl.BlockSpec((tm, c2), lambda i: (i, 0)), pl.BlockSpec((kw, c), lambda i: (0, 0)), _row(c), _row(c), _row(c), _row(c2)],
        out_shape=[SDS((t, c2), BF16), SDS((kw, c), F32), SDS((1, c), F32), SDS((1, c), F32), SDS((1, c), F32), SDS((1, c2), F32)],
        scratch_shapes=[pltpu.VMEM((hl + tm + hl, c), F32), pltpu.VMEM((tm + hl, c), F32), pltpu.VMEM((tm + hl, c), F32),
                        pltpu.VMEM((tm, c), F32), pltpu.VMEM((_SUBLANES - 1, hl + tm + hl, c), F32),
                        pltpu.VMEM((_SUBLANES - 1, tm + hl, c), F32)],
        compiler_params=_params(1),
    )(ag, ag, ag, dzc, dzc, dw, dwb, lng, lnb)


def _ffn_mid_fwd(zf, dw, dwb, name):
    t, f2 = zf.shape
    f = f2 // 2
    kw = dw.shape[0]
    hl = _FFN_HALO
    tm = _tile(t, 256, hl)
    per = tm // hl
    tc = _tile(f, 256, _LANES)
    offs = [(k, hl - (kw - 1) + k) for k in range(kw)]
    shifts = tuple(sorted({off % _SUBLANES for _, off in offs} - {0}))

    def body(zp_ref, z_ref, dw_ref, dwb_ref, o_ref, zext, cb, zsh):
        i = pl.program_id(0)
        zext[0:hl, :] = jnp.where(i > 0, zp_ref[...].astype(F32), 0.0)
        zext[hl:, :] = z_ref[...].astype(F32)
        _taps(zext, dw_ref, cb, tm, f2, offs, 16, _shift_copies(zext, zsh, shifts))
        for c0 in range(0, f, tc):
            g = cb[:, c0:c0 + tc] + dwb_ref[:, c0:c0 + tc]
            u = cb[:, f + c0:f + c0 + tc] + dwb_ref[:, f + c0:f + c0 + tc]
            o_ref[:, c0:c0 + tc] = (g * _sigmoid(g) * u).astype(BF16)

    return pl.pallas_call(
        body, name=name, grid=(t // tm,),
        in_specs=[pl.BlockSpec((hl, f2), lambda i: (jnp.maximum(i * per - 1, 0), 0)), pl.BlockSpec((tm, f2), lambda i: (i, 0)),
                  pl.BlockSpec((kw, f2), lambda i: (0, 0)), _row(f2)],
        out_specs=pl.BlockSpec((tm, f), lambda i: (i, 0)), out_shape=SDS((t, f), BF16),
        scratch_shapes=[pltpu.VMEM((hl + tm, f2), F32), pltpu.VMEM((tm, f2), F32), pltpu.VMEM((len(shifts), hl + tm, f2), F32)],
        compiler_params=_params(1),
    )(zf, zf, dw, dwb)


def _ffn_mid_bwd(zf, dact, dw, dwb, name):
    t, f2 = zf.shape
    f = f2 // 2
    kw = dw.shape[0]
    hl = _FFN_HALO
    tm = _tile(t, 128, hl)
    per = tm // hl
    nt = t // tm
    last_halo = t // hl - 1
    tc = _tile(f, 256, _LANES)
    fwd_offs = [(k, hl - (kw - 1) + k) for k in range(kw)]
    bwd_offs = [(k, kw - 1 - k) for k in range(kw)]
    fwd_shifts = tuple(sorted({off % _SUBLANES for _, off in fwd_offs} - {0}))
    bwd_shifts = tuple(sorted({off % _SUBLANES for _, off in bwd_offs} - {0}))

    def body(zp_ref, z_ref, zn_ref, da_ref, dan_ref, dw_ref, dwb_ref, dzf_ref, ddw_ref, ddwb_ref, zext, cext, dcext, dzb,
             zsh, dcsh):
        i = pl.program_id(0)

        @pl.when(i == 0)
        def _():
            ddw_ref[...] = jnp.zeros_like(ddw_ref)
            ddwb_ref[...] = jnp.zeros_like(ddwb_ref)

        zext[0:hl, :] = jnp.where(i > 0, zp_ref[...].astype(F32), 0.0)
        zext[hl:hl + tm, :] = z_ref[...].astype(F32)
        zext[hl + tm:, :] = zn_ref[...].astype(F32)
        z_shifted = _shift_copies(zext, zsh, fwd_shifts)
        _taps(zext, dw_ref, cext, tm + hl, f2, fwd_offs, 16, z_shifted)
        for c0 in range(0, f, tc):
            g = cext[:, c0:c0 + tc] + dwb_ref[:, c0:c0 + tc]
            u = cext[:, f + c0:f + c0 + tc] + dwb_ref[:, f + c0:f + c0 + tc]
            da = jnp.concatenate([da_ref[:, c0:c0 + tc].astype(F32),
                                  jnp.where(i < nt - 1, dan_ref[:, c0:c0 + tc].astype(F32), 0.0)], axis=0)
            sg = _sigmoid(g)
            dcg = da * u * (sg * (1.0 + g * (1.0 - sg)))
            dcu = da * (g * sg)
            dcext[:, c0:c0 + tc] = dcg
            dcext[:, f + c0:f + c0 + tc] = dcu
            ddwb_ref[:, c0:c0 + tc] += jnp.sum(dcg[:tm], axis=0, keepdims=True)
            ddwb_ref[:, f + c0:f + c0 + tc] += jnp.sum(dcu[:tm], axis=0, keepdims=True)
        _taps(dcext, dw_ref, dzb, tm, f2, bwd_offs, 16, _shift_copies(dcext, dcsh, bwd_shifts))
        _tap_grads(dcext, zext, ddw_ref, tm, f2, fwd_offs, 16, z_shifted)
        dzf_ref[...] = dzb[...].astype(BF16)

    prev = lambda i: (jnp.maximum(i * per - 1, 0), 0)
    nxt = lambda i: (jnp.minimum((i + 1) * per, last_halo), 0)
    return pl.pallas_call(
        body, name=name, grid=(nt,),
        in_specs=[pl.BlockSpec((hl, f2), prev), pl.BlockSpec((tm, f2), lambda i: (i, 0)), pl.BlockSpec((hl, f2), nxt),
                  pl.BlockSpec((tm, f), lambda i: (i, 0)), pl.BlockSpec((hl, f), nxt),
                  pl.BlockSpec((kw, f2), lambda i: (0, 0)), _row(f2)],
        out_specs=[pl.BlockSpec((tm, f2), lambda i: (i, 0)), pl.BlockSpec((kw, f2), lambda i: (0, 0)), _row(f2)],
        out_shape=[SDS((t, f2), BF16), SDS((kw, f2), F32), SDS((1, f2), F32)],
        scratch_shapes=[pltpu.VMEM((hl + tm + hl, f2), F32), pltpu.VMEM((tm + hl, f2), F32), pltpu.VMEM((tm + hl, f2), F32),
                        pltpu.VMEM((tm, f2), F32), pltpu.VMEM((len(fwd_shifts), hl + tm + hl, f2), F32),
                        pltpu.VMEM((len(bwd_shifts), tm + hl, f2), F32)],
        compiler_params=_params(1),
    )(zf, zf, zf, dact, dact, dw, dwb)


_INV_SQRT2 = 0.7071067811865476
_INV_SQRT_2PI = 0.3989422804014327


def _sgu_common(zin_ref, lng_ref, lnb_ref, hh):
    z = zin_ref[...].astype(F32)
    cdf = 0.5 * (1.0 + lax.erf(z * _INV_SQRT2))
    ge = z * cdf
    u, v = ge[:, :hh], ge[:, hh:]
    mu = jnp.mean(v, axis=-1, keepdims=True)
    vc = v - mu
    rstd = lax.rsqrt(jnp.mean(vc * vc, axis=-1, keepdims=True) + _NORM_EPS)
    vhat = vc * rstd
    vn = vhat * lng_ref[...] + lnb_ref[...]
    return z, cdf, u, vhat, rstd, vn


def _sgu_wm(ws_ref, g, ch):
    rows = lax.broadcasted_iota(jnp.int32, (ch, ch), 0)
    cols = lax.broadcasted_iota(jnp.int32, (ch, ch), 1)
    return jnp.where(rows >= cols, ws_ref[g], 0.0).astype(BF16)


def _sgu_mid_fwd(zin, lng, lnb, ws, bs_t, name, carry=None):
    t, h2 = zin.shape
    hh = h2 // 2
    ng, ch = ws.shape[0], ws.shape[1]
    hg = hh // ng
    tm = _tile(t, 256, ch)

    def body(zin_ref, lng_ref, lnb_ref, ws_ref, bs_ref, o_ref):
        _, _, u, _, _, vn = _sgu_common(zin_ref, lng_ref, lnb_ref, hh)
        vnb = vn.astype(BF16)
        for g in range(ng):
            wm = _sgu_wm(ws_ref, g, ch)
            for cc in range(tm // ch):
                rs, cs = slice(cc * ch, (cc + 1) * ch), slice(g * hg, (g + 1) * hg)
                vv = jnp.dot(wm, vnb[rs, cs], preferred_element_type=F32) + bs_ref[:, g:g + 1]
                o_ref[rs, cs] = (u[rs, cs] * vv).astype(BF16)

    return _carrying(
        carry, body, name=name, grid=(t // tm,),
        in_specs=[pl.BlockSpec((tm, h2), lambda i: (i, 0)), _row(hh), _row(hh),
                  pl.BlockSpec((ng, ch, ch), lambda i: (0, 0, 0)), pl.BlockSpec((ch, ng), lambda i: (0, 0))],
        out_specs=pl.BlockSpec((tm, hh), lambda i: (i, 0)), out_shape=SDS((t, hh), BF16), compiler_params=_params(1),
    )(zin, lng, lnb, ws, bs_t)


def _sgu_mid_bwd(zin, duv, lng, lnb, ws, bs_t, name, carry=None):
    t, h2 = zin.shape
    hh = h2 // 2
    ng, ch = ws.shape[0], ws.shape[1]
    hg = hh // ng
    tm = _tile(t, 128, ch)

    def body(zin_ref, duv_ref, lng_ref, lnb_ref, ws_ref, bs_ref, dzin_ref, dlng_ref, dlnb_ref, dws_ref, dbs_ref, dbin_ref, dvn_s, du_s):
        @pl.when(pl.program_id(0) == 0)
        def _():
            for r in (dlng_ref, dlnb_ref, dws_ref, dbs_ref, dbin_ref):
                r[...] = jnp.zeros_like(r)

        z, cdf, u, vhat, rstd, vn = _sgu_common(zin_ref, lng_ref, lnb_ref, hh)
        vnb = vn.astype(BF16)
        duv = duv_ref[...].astype(F32)
        dvv = (duv * u).astype(BF16)
        lane = lax.broadcasted_iota(jnp.int32, (1, _LANES), 1)
        rows = lax.broadcasted_iota(jnp.int32, (ch, ch), 0)
        cols = lax.broadcasted_iota(jnp.int32, (ch, ch), 1)
        dbs = jnp.zeros((ch, _LANES), F32)
        for g in range(ng):
            wm = _sgu_wm(ws_ref, g, ch)
            dwm = jnp.zeros((ch, ch), F32)
            for cc in range(tm // ch):
                rs, cs = slice(cc * ch, (cc + 1) * ch), slice(g * hg, (g + 1) * hg)
                vv = jnp.dot(wm, vnb[rs, cs], preferred_element_type=F32) + bs_ref[:, g:g + 1]
                du_s[rs, cs] = duv[rs, cs] * vv
                dvn_s[rs, cs] = _tn(wm, dvv[rs, cs])
                dwm = dwm + _nt(dvv[rs, cs], vnb[rs, cs])
                dbs = dbs + jnp.where(lane == g, jnp.sum(dvv[rs, cs].astype(F32), axis=-1, keepdims=True), 0.0)
            dws_ref[g] += jnp.where(rows >= cols, dwm, 0.0)
        dbs_ref[...] += dbs
        dvn = dvn_s[...]
        dlng_ref[...] += jnp.sum(dvn * vhat, axis=0, keepdims=True)
        dlnb_ref[...] += jnp.sum(dvn, axis=0, keepdims=True)
        dxh = dvn * lng_ref[...]
        dv = rstd * (dxh - jnp.mean(dxh, axis=-1, keepdims=True) - vhat * jnp.mean(dxh * vhat, axis=-1, keepdims=True))
        dgelu = cdf + z * (_INV_SQRT_2PI * jnp.exp(-0.5 * z * z))
        dzu = du_s[...] * dgelu[:, :hh]
        dzv = dv * dgelu[:, hh:]
        dzin_ref[:, :hh] = dzu.astype(BF16)
        dzin_ref[:, hh:] = dzv.astype(BF16)
        dbin_ref[:, :hh] += jnp.sum(dzu, axis=0, keepdims=True)
        dbin_ref[:, hh:] += jnp.sum(dzv, axis=0, keepdims=True)

    return _carrying(
        carry, body, name=name, grid=(t // tm,),
        in_specs=[pl.BlockSpec((tm, h2), lambda i: (i, 0)), pl.BlockSpec((tm, hh), lambda i: (i, 0)), _row(hh), _row(hh),
                  pl.BlockSpec((ng, ch, ch), lambda i: (0, 0, 0)), pl.BlockSpec((ch, ng), lambda i: (0, 0))],
        out_specs=[pl.BlockSpec((tm, h2), lambda i: (i, 0)), _row(hh), _row(hh), pl.BlockSpec((ng, ch, ch), lambda i: (0, 0, 0)),
                   pl.BlockSpec((ch, _LANES), lambda i: (0, 0)), _row(h2)],
        out_shape=[SDS((t, h2), BF16), SDS((1, hh), F32), SDS((1, hh), F32), SDS((ng, ch, ch), F32), SDS((ch, _LANES), F32),
                   SDS((1, h2), F32)],
        scratch_shapes=[pltpu.VMEM((tm, hh), F32), pltpu.VMEM((tm, hh), F32)], compiler_params=_params(1),
    )(zin, duv, lng, lnb, ws, bs_t)


def _ada_mod(c_all, ada_w, ada_b, name):
    nl, d, n = ada_w.shape
    nb = c_all.shape[0]
    tn = _tile(n, _COL_TILE, _LANES)

    def body(c_ref, w_ref, b_ref, o_ref):
        cv = c_ref[...]
        ca = cv * _sigmoid(cv)
        o_ref[...] = jnp.dot(ca, w_ref[...], preferred_element_type=F32, precision=lax.Precision.HIGHEST) + b_ref[...]

    return pl.pallas_call(
        body, name=name, grid=(nl, n // tn),
        in_specs=[pl.BlockSpec((nb, d), lambda l, j: (0, 0)), pl.BlockSpec((None, d, tn), lambda l, j: (l, 0, j)),
                  pl.BlockSpec((None, 1, tn), lambda l, j: (l, 0, j))],
        out_specs=pl.BlockSpec((None, nb, tn), lambda l, j: (l, 0, j)), out_shape=SDS((nl, nb, n), F32),
        compiler_params=_params(2),
    )(c_all, ada_w, ada_b)


def _ada_wgrad(c_all_t, dmod, name):
    d, nb = c_all_t.shape
    nl, _, n = dmod.shape
    tn = _tile(n, _COL_TILE, _LANES)

    def body(c_ref, dm_ref, o_ref):
        cv = c_ref[...]
        ca = cv * _sigmoid(cv)
        acc = ca[:, 0:1] * dm_ref[0:1, :]
        for b in range(1, nb):
            acc = acc + ca[:, b:b + 1] * dm_ref[b:b + 1, :]
        o_ref[...] = acc

    return pl.pallas_call(
        body, name=name, grid=(nl, n // tn),
        in_specs=[pl.BlockSpec((d, nb), lambda l, j: (0, 0)), pl.BlockSpec((None, nb, tn), lambda l, j: (l, 0, j))],
        out_specs=pl.BlockSpec((None, d, tn), lambda l, j: (l, 0, j)), out_shape=SDS((nl, d, n), F32),
        compiler_params=_params(2),
    )(c_all_t, dmod)


def _as_rows(a):
    return a.reshape(-1, a.shape[-1])


def _row_tile(r, c, n_arrays):
    budget = _VMEM_LIMIT_BYTES // (4 * 2 * n_arrays * 4)
    return _tile(r, max(8, budget // max(c, 1)), 8)


def _cast_bf16_layer(a, layer, name):
    _, r, c = a.shape
    tr = _row_tile(r, c, 2)

    def body(a_ref, o_ref):
        o_ref[...] = a_ref[...].astype(BF16)

    return pl.pallas_call(body, name=name, grid=(r // tr,), in_specs=[pl.BlockSpec((None, tr, c), lambda i: (layer, i, 0))],
                          out_specs=pl.BlockSpec((tr, c), lambda i: (i, 0)), out_shape=SDS((r, c), BF16),
                          compiler_params=_params(1))(a)


def _sum_slot_layers(parts, name):
    nl = len(parts)
    n, r, c = parts[0].shape
    tr = _row_tile(r, c, nl * (n + 1))

    def body(*refs):
        o_ref = refs[nl]
        for lay in range(nl):
            acc = refs[lay][0].astype(F32)
            for s in range(1, n):
                acc = acc + refs[lay][s].astype(F32)
            o_ref[lay] = acc

    return pl.pallas_call(body, name=name, grid=(r // tr,), in_specs=[pl.BlockSpec((n, tr, c), lambda i: (0, i, 0))] * nl,
                          out_specs=pl.BlockSpec((nl, tr, c), lambda i: (0, i, 0)), out_shape=SDS((nl, r, c), F32),
                          compiler_params=_params(1))(*parts)


def _sum_slots(parts, name):
    n = parts.shape[0]
    p2 = parts.reshape(n, -1, parts.shape[-1])
    r, c = p2.shape[1:]
    tr = _row_tile(r, c, n + 1)

    def body(p_ref, o_ref):
        acc = p_ref[0].astype(F32)
        for s in range(1, n):
            acc = acc + p_ref[s].astype(F32)
        o_ref[...] = acc

    out = pl.pallas_call(body, name=name, grid=(r // tr,), in_specs=[pl.BlockSpec((n, tr, c), lambda i: (0, i, 0))],
                         out_specs=pl.BlockSpec((tr, c), lambda i: (i, 0)), out_shape=SDS((r, c), F32),
                         compiler_params=_params(1))(p2)
    return out.reshape(parts.shape[1:])


def _adamw(w, g_parts, m, v, name):
    w2, m2, v2 = _as_rows(w), _as_rows(m), _as_rows(v)
    gs = [_as_rows(g) for g in g_parts]
    r, c = w2.shape
    ng = len(gs)
    tr = _row_tile(r, c, 7 + ng)
    c1 = 1.0 - _ADAM_B1 ** _ADAM_STEP
    c2 = 1.0 - _ADAM_B2 ** _ADAM_STEP

    def body(*refs):
        w_ref, m_ref, v_ref = refs[:3]
        g_refs = refs[3:3 + ng]
        go_ref, d_ref, mo_ref, vo_ref = refs[3 + ng:]
        g = g_refs[0][...]
        for gr in g_refs[1:]:
            g = g + gr[...]
        mn = _ADAM_B1 * m_ref[...] + (1.0 - _ADAM_B1) * g
        vn = _ADAM_B2 * v_ref[...] + (1.0 - _ADAM_B2) * (g * g)
        go_ref[...] = g
        mo_ref[...] = mn
        vo_ref[...] = vn
        d_ref[...] = -_ADAM_LR * ((mn / c1) / (jnp.sqrt(vn / c2) + _ADAM_EPS) + _ADAM_WD * w_ref[...])

    spec = pl.BlockSpec((tr, c), lambda i: (i, 0))
    outs = pl.pallas_call(body, name=name, grid=(r // tr,), in_specs=[spec] * (3 + ng), out_specs=[spec] * 4,
                          out_shape=[SDS((r, c), F32)] * 4, compiler_params=_params(1))(w2, m2, v2, *gs)
    return tuple(o.reshape(w.shape) for o in outs)


_HBM = pl.BlockSpec(memory_space=pltpu.HBM)
_CHIP_FLIPS = ((1, 0, 0), (0, 1, 0), (1, 1, 0))
_ALL_FLIPS = tuple((a, b, c) for a in (0, 1) for b in (0, 1) for c in (0, 1))[1:]
_SIBLING_FLIP = ((0, 0, 1),)


def _at(ref, idx):
    return ref.at[idx] if idx else ref


def _exchange(name, flips, srcs, out_shapes, items):
    n_in, n_out = len(srcs), len(out_shapes)
    n_remote = len(items) * len(flips)
    n_local = sum(1 for it in items if it[4])

    def body(*refs):
        src_refs, dst_refs = refs[:n_in], refs[n_in:n_in + n_out]
        send_sems, recv_sems, local_sems = refs[n_in + n_out:]
        me = (lax.axis_index("x"), lax.axis_index("y"), lax.axis_index("c"))
        sends, recvs, locs = [], [], []
        n, nl = 0, 0
        for si, di, src_idx, dst_idx, local in items:
            for flip in flips:
                peer = tuple(1 - m if f else m for m, f in zip(me, flip))
                push = pltpu.make_async_remote_copy(
                    src_ref=_at(src_refs[si], src_idx(*peer)), dst_ref=_at(dst_refs[di], dst_idx(*me)),
                    send_sem=send_sems.at[n], recv_sem=recv_sems.at[n], device_id=peer, device_id_type=MESH_IDS)
                push.start()
                sends.append(push)
                recvs.append(pltpu.make_async_remote_copy(
                    src_ref=_at(src_refs[si], src_idx(*me)), dst_ref=_at(dst_refs[di], dst_idx(*peer)),
                    send_sem=send_sems.at[n], recv_sem=recv_sems.at[n], device_id=peer, device_id_type=MESH_IDS))
                n += 1
            if local:
                cp = pltpu.make_async_copy(_at(src_refs[si], src_idx(*me)), _at(dst_refs[di], dst_idx(*me)), local_sems.at[nl])
                cp.start()
                locs.append(cp)
                nl += 1
        for r in recvs:
            r.wait_recv()
        for s in sends:
            s.wait_send()
        for cp in locs:
            cp.wait()

    return pl.pallas_call(
        body, name=name, in_specs=[_HBM] * n_in, out_specs=[_HBM] * n_out, out_shape=list(out_shapes),
        scratch_shapes=[pltpu.SemaphoreType.DMA((n_remote,)), pltpu.SemaphoreType.DMA((n_remote,)),
                        pltpu.SemaphoreType.DMA((max(n_local, 1),))],
    )(*srcs)


_SEM = pl.BlockSpec(memory_space=pltpu.SEMAPHORE)
_SIDE_EFFECTS = pltpu.SideEffectType.DATAFLOW_SIDE_EFFECTING


def _push_plan(src_refs, land_refs, items):
    me = (lax.axis_index("x"), lax.axis_index("y"), lax.axis_index("c"))
    plan = []
    for si, di, src_idx, dst_idx in items:
        for flip in _CHIP_FLIPS:
            peer = tuple(1 - m if f else m for m, f in zip(me, flip))
            plan.append((peer, _at(src_refs[si], src_idx(*peer)), _at(land_refs[di], dst_idx(*me)),
                         _at(src_refs[si], src_idx(*me)), _at(land_refs[di], dst_idx(*peer))))
    return me, plan


def _remote_copy(src, dst, send_sems, recv_sems, n, peer):
    return pltpu.make_async_remote_copy(src_ref=src, dst_ref=dst, send_sem=send_sems.at[n], recv_sem=recv_sems.at[n],
                                        device_id=peer, device_id_type=MESH_IDS)


def _chunks_of(rows, steps):
    units = rows // 16 if rows % 16 == 0 else 1
    return max(k for k in range(1, min(units, steps) + 1) if units % k == 0)


def _split_copies(src_refs, land_refs, send_sems, recv_sems, items):
    me, plan = _push_plan(src_refs, land_refs, items)
    pushes = [_remote_copy(ms, md, send_sems, recv_sems, n, peer) for n, (peer, ms, md, _, _) in enumerate(plan)]
    arrivals = [_remote_copy(ts, td, send_sems, recv_sems, n, peer) for n, (peer, _, _, ts, td) in enumerate(plan)]
    return me, pushes, arrivals


def _push_start(name, srcs, land_shapes, items, after):
    n_src, n_land = len(srcs), len(land_shapes)
    n_buf = n_src + n_land
    n_remote = len(items) * len(_CHIP_FLIPS)

    def body(*refs):
        src_refs, land_refs = refs[:n_src], refs[n_src:n_buf]
        send_sems, recv_sems = refs[n_buf + 1], refs[n_buf + 2]
        local_sems = refs[2 * n_buf + 4]
        me, pushes, _ = _split_copies(src_refs, land_refs, send_sems, recv_sems, items)
        own = [pltpu.make_async_copy(_at(src_refs[si], src_idx(*me)), _at(land_refs[di], dst_idx(*me)), local_sems.at[k])
               for k, (si, di, src_idx, dst_idx) in enumerate(items)]
        for cp in own:
            cp.start()
        for cp in own:
            cp.wait()
        for p in pushes:
            p.start()

    bufs = list(srcs) + [lax.empty(s.shape, s.dtype) for s in land_shapes]
    outs = pl.pallas_call(
        body, name=name, in_specs=[_HBM] * n_buf + [pl.BlockSpec(memory_space=pl.ANY)],
        out_shape=[pltpu.SemaphoreType.DMA((n_remote,)), pltpu.SemaphoreType.DMA((n_remote,))]
        + [pltpu.HBM(a.shape, a.dtype) for a in bufs] + [pltpu.HBM((_SUBLANES, _LANES), F32)],
        out_specs=[_SEM, _SEM] + [_HBM] * (n_buf + 1),
        input_output_aliases={k: 2 + k for k in range(n_buf)},
        scratch_shapes=[pltpu.SemaphoreType.DMA((len(items),))],
        compiler_params=pltpu.CompilerParams(has_side_effects=_SIDE_EFFECTS),
    )(*[pltpu.with_memory_space_constraint(a, pltpu.HBM) for a in bufs], after)
    return dict(send=outs[0], recv=outs[1], bufs=list(outs[2:2 + n_buf]), order=outs[2 + n_buf], n_src=n_src, items=items)


def _behind(x, handles, name):
    deps = [h["order"] for h in handles]

    def body(*refs):
        refs[-1][...] = refs[0][...]

    vmem = pl.BlockSpec(memory_space=pltpu.VMEM)
    return pl.pallas_call(body, name=name, in_specs=[vmem] + [pl.BlockSpec(memory_space=pl.ANY)] * len(deps), out_specs=vmem,
                          out_shape=SDS(x.shape, x.dtype))(x, *deps)


def _push_wait(name, handle, after):
    n_src, items = handle["n_src"], handle["items"]
    n_buf = len(handle["bufs"])

    def body(*refs):
        src_refs, land_refs = refs[:n_src], refs[n_src:n_buf]
        send_sems, recv_sems = refs[n_buf], refs[n_buf + 1]
        _, pushes, arrivals = _split_copies(src_refs, land_refs, send_sems, recv_sems, items)
        for p in pushes:
            p.wait_send()
        for a in arrivals:
            a.wait_recv()

    outs = pl.pallas_call(
        body, name=name, in_specs=[_HBM] * n_buf + [_SEM, _SEM, pl.BlockSpec(memory_space=pl.ANY)],
        out_shape=[pltpu.HBM(a.shape, a.dtype) for a in handle["bufs"]], out_specs=[_HBM] * n_buf,
        input_output_aliases={k: k for k in range(n_buf)},
        compiler_params=pltpu.CompilerParams(has_side_effects=_SIDE_EFFECTS),
    )(*handle["bufs"], handle["send"], handle["recv"], after)
    return list(outs[n_src:])


def _chip_of(x, y, c):
    return 2 * x + y


def _dev_of(x, y, c):
    return 4 * x + 2 * y + c


def _window(axis, ndim, size):
    def idx(x, y, c):
        return tuple(pl.ds(_chip_of(x, y, c) * size, size) if a == axis else slice(None) for a in range(ndim))
    return idx


def _whole(x, y, c):
    return ()


def _gather_pushes(shards_axes):
    srcs, shapes, items = [], [], []
    for k, (a, axis) in enumerate(shards_axes):
        full = list(a.shape)
        full[axis] *= _N_CHIPS
        srcs.append(a)
        shapes.append(SDS((1,) + tuple(full), a.dtype))
        window = _window(axis, a.ndim, a.shape[axis])
        items.append((k, k, _whole, (lambda w: lambda x, y, c: (0,) + w(x, y, c))(window)))
    return srcs, shapes, items


def _scatter_pushes(grads_axes):
    srcs, shapes, items = [], [], []
    slot = lambda x, y, c: (_chip_of(x, y, c),)
    for k, (a, axis) in enumerate(grads_axes):
        shard = list(a.shape)
        shard[axis] //= _N_CHIPS
        srcs.append(a)
        shapes.append(SDS((_N_CHIPS,) + tuple(shard), a.dtype))
        items.append((k, k, _window(axis, a.ndim, shard[axis]), slot))
    return srcs, shapes, items


def _gather_all(name, a):
    slot = lambda x, y, c: (_dev_of(x, y, c),)
    return _exchange(name, _ALL_FLIPS, [a], [SDS((_N_DEV,) + a.shape, a.dtype)], [(0, 0, _whole, slot, True)])[0]


def _swap_chips(name, a):
    slot = lambda x, y, c: (_chip_of(x, y, c),)
    return _exchange(name, _CHIP_FLIPS, [a], [SDS(a.shape, a.dtype)], [(0, 0, slot, slot, True)])[0]


def _swap_sibling(name, arrays):
    items = [(k, k, _whole, _whole, False) for k in range(len(arrays))]
    return _exchange(name, _SIBLING_FLIP, list(arrays), [SDS(a.shape, a.dtype) for a in arrays], items)


_BIG = ("attn_wqkv", "attn_wo", "conv_w_in", "conv_w_out", "sgu_w_in", "sgu_w_out", "ffn_w_in", "ffn_w_out")
_BIG_AXIS = {"attn_wqkv": 2, "attn_wo": 1, "conv_w_in": 2, "conv_w_out": 1, "sgu_w_in": 2, "sgu_w_out": 1,
             "ffn_w_in": 2, "ffn_w_out": 1}
_SMALL = {"norm1_g": None, "norm2_g": None, "ada_b": None, "attn_bqkv": 1, "attn_sinks": None, "attn_bo": 1,
          "conv_b_in": None, "conv_dw": 2, "conv_dw_b": None, "conv_ln_g": None, "conv_ln_b": None, "conv_b_out": None,
          "sgu_b_in": 1, "sgu_ln_g": 1, "sgu_ln_b": 1, "sgu_ws": None, "sgu_bs": None, "sgu_b_out": 1,
          "ffn_dw": 2, "ffn_dw_b": None, "final_g": None}
_WEIGHTS = ['norm1_g', 'norm2_g', 'ada_w', 'ada_b', 'attn_wqkv', 'attn_bqkv', 'attn_sinks', 'attn_wo', 'attn_bo',
            'conv_w_in', 'conv_b_in', 'conv_dw', 'conv_dw_b', 'conv_ln_g', 'conv_ln_b', 'conv_w_out', 'conv_b_out',
            'sgu_w_in', 'sgu_b_in', 'sgu_ln_g', 'sgu_ln_b', 'sgu_ws', 'sgu_bs', 'sgu_w_out', 'sgu_b_out',
            'ffn_w_in', 'ffn_dw', 'ffn_dw_b', 'ffn_w_out', 'final_g']
_FLAT_COLS = 1024


def _full_shape(a, axis):
    s = list(a.shape)
    if axis is not None:
        s[axis] *= _N_CHIPS
    return tuple(s)


def _pack(arrays):
    flat = jnp.concatenate([a.reshape(-1).astype(F32) for a in arrays])
    pad = (-flat.shape[0]) % (8 * _FLAT_COLS)
    return jnp.pad(flat, (0, pad)).reshape(-1, _FLAT_COLS)


def _unpack(flat2d, shapes):
    flat = flat2d.reshape(-1)
    out, off = [], 0
    for s in shapes:
        n = math.prod(s)
        out.append(flat[off:off + n].reshape(s))
        off += n
    return out


def _shard_of(full, axis, chip):
    if axis is None:
        return full
    size = full.shape[axis] // _N_CHIPS
    return lax.dynamic_slice_in_dim(full, chip * size, size, axis)


def _unshard(gathered, axis):
    moved = jnp.moveaxis(gathered, 0, axis)
    shape = list(gathered.shape[1:])
    shape[axis] *= _N_CHIPS
    return moved.reshape(shape)


def kernel(x, c, norm1_g, norm2_g, ada_w, ada_b, attn_wqkv, attn_bqkv, attn_sinks, attn_wo, attn_bo, conv_w_in, conv_b_in, conv_dw, conv_dw_b, conv_ln_g, conv_ln_b, conv_w_out, conv_b_out, sgu_w_in, sgu_b_in, sgu_ln_g, sgu_ln_b, sgu_ws, sgu_bs, sgu_w_out, sgu_b_out, ffn_w_in, ffn_dw, ffn_dw_b, ffn_w_out, final_g, loss_target, m_norm1_g, m_norm2_g, m_ada_w, m_ada_b, m_attn_wqkv, m_attn_bqkv, m_attn_sinks, m_attn_wo, m_attn_bo, m_conv_w_in, m_conv_b_in, m_conv_dw, m_conv_dw_b, m_conv_ln_g, m_conv_ln_b, m_conv_w_out, m_conv_b_out, m_sgu_w_in, m_sgu_b_in, m_sgu_ln_g, m_sgu_ln_b, m_sgu_ws, m_sgu_bs, m_sgu_w_out, m_sgu_b_out, m_ffn_w_in, m_ffn_dw, m_ffn_dw_b, m_ffn_w_out, m_final_g, v_norm1_g, v_norm2_g, v_ada_w, v_ada_b, v_attn_wqkv, v_attn_bqkv, v_attn_sinks, v_attn_wo, v_attn_bo, v_conv_w_in, v_conv_b_in, v_conv_dw, v_conv_dw_b, v_conv_ln_g, v_conv_ln_b, v_conv_w_out, v_conv_b_out, v_sgu_w_in, v_sgu_b_in, v_sgu_ln_g, v_sgu_ln_b, v_sgu_ws, v_sgu_bs, v_sgu_w_out, v_sgu_b_out, v_ffn_w_in, v_ffn_dw, v_ffn_dw_b, v_ffn_w_out, v_final_g):
    args = dict(locals())
    wts = {n: args[n] for n in _WEIGHTS}
    mom_m = {n: args["m_" + n] for n in _WEIGHTS}
    mom_v = {n: args["v_" + n] for n in _WEIGHTS}

    ix, iy, ic = lax.axis_index("x"), lax.axis_index("y"), lax.axis_index("c")
    chip = 2 * ix + iy
    xs = x[0]
    tgt = loss_target[0]
    t, d = xs.shape
    nh = d // _HEAD_DIM
    depth = ada_w.shape[0]
    ncols = ada_w.shape[2]
    n_mod = ncols * _N_CHIPS // d

    small_sharded = [n for n in _WEIGHTS if _SMALL.get(n) is not None]
    packed_small = _pack([wts[n] for n in small_sharded])
    mixers = {0: ("attn_wqkv", "attn_wo"), 1: ("conv_w_in", "conv_w_out"), 2: ("sgu_w_in", "sgu_w_out")}

    def shards_of(keys):
        return [(_cast_bf16_layer(wts[n], j, "cast_%s_%d" % (n, j)), _BIG_AXIS[n] - 1) for n, j in keys]

    c_all = _gather_all("gather_c", c)[:, 0, :]
    ada_b_cols = lax.dynamic_slice_in_dim(ada_b, chip * ncols, ncols, 1)[:, None, :]
    mod_cols = _ada_mod(c_all, ada_w, ada_b_cols, "ada_mod")
    mine = lax.dynamic_index_in_dim(mod_cols.reshape(depth, _N_CHIPS, 2, ncols), ic, 2, keepdims=False)
    got = _swap_chips("swap_mod", jnp.moveaxis(mine, 1, 0))

    first = [(n, 0) for n in mixers[0]]
    gather0 = _push_start("start_gather_0", *_gather_pushes(shards_of(first) + [(packed_small, 0)]), got)
    mod = _behind(jnp.moveaxis(got, 0, 1).reshape(depth, n_mod, 1, d), [gather0], "behind_gather_start")
    lands0 = _push_wait("wait_gather_0", gather0, mod)
    wfull = dict(zip(first, lands0))
    small_rows = lands0[-1].reshape(_N_CHIPS, -1, _FLAT_COLS)
    pieces = [_unpack(small_rows[s], [wts[n].shape for n in small_sharded]) for s in range(_N_CHIPS)]
    full = {n: wts[n] for n in _WEIGHTS if n in _SMALL and _SMALL[n] is None}
    for k, n in enumerate(small_sharded):
        full[n] = _unshard(jnp.stack([pieces[s][k] for s in range(_N_CHIPS)]), _SMALL[n])

    tables = _attn_tables(nh)
    zeros_d = jnp.zeros((1, d), F32)
    zeros_f2 = jnp.zeros((1, ffn_w_in.shape[2] * _N_CHIPS), F32)
    row = lambda a: a.reshape(1, -1)

    saved = []
    xcur = xs
    for i in range(depth):
        sh1, sc1, g1, sh2, sc2, g2 = (mod[i, k] for k in range(n_mod))
        kind, j = i % 3, i // 3
        tag = "L%d_" % i
        more = i + 1 < depth
        mid_keys = ([("ffn_w_in", 0), ("ffn_w_out", 0)] if i == 0 else []) + \
                   ([(n, (i + 1) // 3) for n in mixers[(i + 1) % 3]] if more else [])
        mid_carry = _gather_pushes(shards_of(mid_keys)) if mid_keys else None
        in_carry = _gather_pushes(shards_of([("ffn_w_in", i + 1)])) if more else None
        out_carry = _gather_pushes(shards_of([("ffn_w_out", i + 1)])) if more else None
        w_in, w_out = wfull[(mixers[kind][0], j)], wfull[(mixers[kind][1], j)]
        if kind == 0:
            b_in, b_out = row(full["attn_bqkv"][j]), row(full["attn_bo"][j])
        elif kind == 1:
            b_in, b_out = row(full["conv_b_in"][j]), row(full["conv_b_out"][j])
        else:
            b_in, b_out = row(full["sgu_b_in"][j]), row(full["sgu_b_out"][j])
        (h1, z), _ = _norm_mod_matmul(xcur, row(full["norm1_g"][i]), sc1, sh1, w_in, 0, b_in, tag + "mixer_in")
        if kind == 0:
            a, lands = _attn_fwd(z, tables, full["attn_sinks"][j], nh, tag + "attn", mid_carry)
        elif kind == 1:
            a, lands = _conv_mid_fwd(z, full["conv_dw"][j], row(full["conv_dw_b"][j]), row(full["conv_ln_g"][j]),
                                     row(full["conv_ln_b"][j]), tag + "conv_mid", mid_carry)
        else:
            a, lands = _sgu_mid_fwd(z, row(full["sgu_ln_g"][j]), row(full["sgu_ln_b"][j]), full["sgu_ws"][j], full["sgu_bs"][j].T,
                                    tag + "sgu_mid", mid_carry)
        wfull.update(zip(mid_keys, lands))
        (y1, xmid), _ = _matmul_resid(a, w_out, 0, b_out, xcur, g1, tag + "mixer_out")
        (h2, zf), lands = _norm_mod_matmul(xmid, row(full["norm2_g"][i]), sc2, sh2, wfull[("ffn_w_in", i)], 0, zeros_f2,
                                           tag + "ffn_in", in_carry)
        wfull.update(zip([("ffn_w_in", i + 1)], lands))
        act = _ffn_mid_fwd(zf, full["ffn_dw"][i], row(full["ffn_dw_b"][i]), tag + "ffn_mid")
        (y2, xnext), lands = _matmul_resid(act, wfull[("ffn_w_out", i)], 0, zeros_d, xmid, g2, tag + "ffn_out", out_carry)
        wfull.update(zip([("ffn_w_out", i + 1)], lands))
        saved.append(dict(x=xcur, h1=h1, z=z, a=a, y1=y1, xmid=xmid, h2=h2, zf=zf, act=act, y2=y2, w_in=w_in, w_out=w_out))
        xcur = xnext

    loss_row, dx, d_final_g = _final_loss(xcur, row(final_g), tgt, "final_loss")

    small_g = {n: jnp.zeros(_full_shape(wts[n], _SMALL[n]), F32) for n in _SMALL}
    small_g["final_g"] = d_final_g[0]
    dmod = [None] * depth
    pending = []
    slots = {}

    def put(name, j, val):
        small_g[name] = small_g[name].at[j].set(val.reshape(small_g[name].shape[1:]))

    for i in reversed(range(depth)):
        sv = saved[i]
        sh1, sc1, g1, sh2, sc2, g2 = (mod[i, k] for k in range(n_mod))
        kind, j = i % 3, i // 3
        tag = "L%d_" % i
        dy2, dact, dg2, _ = _outproj_bwd(dx, sv["y2"], g2, wfull[("ffn_w_out", i)], 0, tag + "ffn_out_bwd")
        g_ffn_out = _matmul_tn(sv["act"], dy2, tag + "ffn_out_wgrad")
        dzf, d_fdw, d_fdwb = _ffn_mid_bwd(sv["zf"], dact, full["ffn_dw"][i], row(full["ffn_dw_b"][i]), tag + "ffn_mid_bwd")
        put("ffn_dw", i, d_fdw)
        put("ffn_dw_b", i, d_fdwb)
        g_ffn_in = _matmul_tn(sv["h2"], dzf, tag + "ffn_in_wgrad")
        carry = _scatter_pushes([(g, axis) for _, g, axis in pending]) if pending else None
        (dxmid, dn2, dsc2, dsh2), lands = _inproj_bwd(dzf, wfull[("ffn_w_in", i)], 0, sv["xmid"], dx, row(full["norm2_g"][i]),
                                                      sc2, tag + "ffn_in_bwd", carry)
        slots.update(zip([key for key, _, _ in pending], lands))
        put("norm2_g", i, dn2)
        pending = [(("ffn_w_in", i), g_ffn_in, 1), (("ffn_w_out", i), g_ffn_out, 0)]
        dy1, da, dg1, dbo = _outproj_bwd(dxmid, sv["y1"], g1, sv["w_out"], 0, tag + "mixer_out_bwd")
        w_in_name, w_out_name = mixers[kind]
        g_mix_out = _matmul_tn(sv["a"], dy1, tag + "mixer_out_wgrad")
        if kind == 0:
            (dz, dsink, dbin), _ = _attn_bwd(sv["z"], da, tables, full["attn_sinks"][j], nh, tag + "attn_bwd")
            put("attn_sinks", j, dsink[0, :nh])
            put("attn_bqkv", j, dbin)
            put("attn_bo", j, dbo)
        elif kind == 1:
            (dz, d_dw, d_dwb, d_lng, d_lnb, dbin), _ = _conv_mid_bwd(
                sv["z"], da, full["conv_dw"][j], row(full["conv_dw_b"][j]), row(full["conv_ln_g"][j]), row(full["conv_ln_b"][j]),
                tag + "conv_mid_bwd")
            for nme, val in (("conv_dw", d_dw), ("conv_dw_b", d_dwb), ("conv_ln_g", d_lng), ("conv_ln_b", d_lnb),
                             ("conv_b_in", dbin), ("conv_b_out", dbo)):
                put(nme, j, val)
        else:
            (dz, d_lng, d_lnb, d_ws, d_bst, dbin), _ = _sgu_mid_bwd(
                sv["z"], da, row(full["sgu_ln_g"][j]), row(full["sgu_ln_b"][j]), full["sgu_ws"][j], full["sgu_bs"][j].T,
                tag + "sgu_mid_bwd")
            ng = sgu_ws.shape[1]
            for nme, val in (("sgu_ln_g", d_lng), ("sgu_ln_b", d_lnb), ("sgu_ws", d_ws), ("sgu_bs", d_bst[:, :ng].T),
                             ("sgu_b_in", dbin), ("sgu_b_out", dbo)):
                put(nme, j, val)
        g_mix_in = _matmul_tn(sv["h1"], dz, tag + "mixer_in_wgrad")
        carry = _scatter_pushes([(g, axis) for _, g, axis in pending])
        (dx, dn1, dsc1, dsh1), lands = _inproj_bwd(dz, sv["w_in"], 0, sv["x"], dxmid, row(full["norm1_g"][i]), sc1,
                                                   tag + "mixer_in_bwd", carry)
        slots.update(zip([key for key, _, _ in pending], lands))
        put("norm1_g", i, dn1)
        dmod[i] = jnp.concatenate([dsh1, dsc1, dg1, dsh2, dsc2, dg2], axis=1)
        pending = [((w_in_name, j), g_mix_in, 1), ((w_out_name, j), g_mix_out, 0)]

    last_scatter = _push_start("start_scatter_last", *_scatter_pushes([(g, axis) for _, g, axis in pending]), dx)
    grad_x = dx[None]
    loss = lax.psum(loss_row[0, 0], ("x", "y", "c"))

    small_names = [n for n in _WEIGHTS if n in _SMALL and n != "ada_b"]
    dmod_own = jnp.concatenate(dmod, axis=0)
    packed = _behind(_pack([small_g[n] for n in small_names] + [dmod_own]), [last_scatter], "behind_last_scatter_start")
    packed_all = _gather_all("gather_small_grads", packed)
    summed = _sum_slots(packed_all, "sum_small_grads")
    small_full = dict(zip(small_names, _unpack(summed, [small_g[n].shape for n in small_names])))
    n_small = sum(math.prod(small_g[n].shape) for n in small_names)
    dmod_all = packed_all.reshape(_N_DEV, -1)[:, n_small:n_small + dmod_own.size].reshape(_N_DEV, depth, n_mod * d)
    small_full["ada_b"] = _sum_slots(dmod_all, "sum_ada_b_grad")
    dmod_cols = lax.dynamic_slice_in_dim(jnp.moveaxis(dmod_all, 0, 1), chip * ncols, ncols, 2)
    g_ada_w = _ada_wgrad(c_all.T, dmod_cols, "ada_wgrad")

    slots.update(zip([key for key, _, _ in pending], _push_wait("wait_scatter_last", last_scatter, g_ada_w)))
    partial =[_sum_slot_layers([slots[(n, j)] for j in range(wts[n].shape[0])], "sum_chips_" + n) for n in _BIG]
    other = _swap_sibling("swap_cores", partial)

    outs = {}
    for n, mine_p, theirs_p in zip(_BIG, partial, other):
        outs[n] = _adamw(wts[n], [mine_p, theirs_p], mom_m[n], mom_v[n], "adamw_" + n)
    outs["ada_w"] = _adamw(ada_w, [g_ada_w], m_ada_w, v_ada_w, "adamw_ada_w")
    sm_names = [n for n in _WEIGHTS if n in _SMALL]
    g_loc = [_shard_of(small_full[n], _SMALL[n], chip) for n in sm_names]
    packs = [_pack([src[n] for n in sm_names]) for src in (wts, mom_m, mom_v)]
    sm_out = _adamw(packs[0], [_pack(g_loc)], packs[1], packs[2], "adamw_small")
    shapes = [wts[n].shape for n in sm_names]
    unpacked = [_unpack(o, shapes) for o in sm_out]
    for k, n in enumerate(sm_names):
        outs[n] = tuple(u[k] for u in unpacked)

    result = [loss, grad_x]
    for which in range(4):
        result += [outs[n][which] for n in _WEIGHTS]
    return tuple(result)
```

```python
import math

import jax
import jax.numpy as jnp
from jax import lax
from jax.experimental import pallas as pl
from jax.experimental.pallas import tpu as pltpu

F32, BF16 = jnp.float32, jnp.bfloat16
SDS = jax.ShapeDtypeStruct
MESH_IDS = pl.DeviceIdType.MESH

_VMEM_LIMIT_BYTES = 48 * 1024 * 1024
_LANES = 128
_NORM_EPS = 1e-6
_NEG_INF = -1e30
_HEAD_DIM = 64
_N_KV = 4
_ATTN_BLOCK = 128
_ATTN_BLOCKS_PER_STEP = 4
_ROW_TILE = 512
_COL_TILE = 512
_WGRAD_TILE = 1536
_CONV_HALO = 32
_FFN_HALO = 16
_ADAM_LR, _ADAM_B1, _ADAM_B2, _ADAM_EPS, _ADAM_WD, _ADAM_STEP = 0.001, 0.9, 0.999, 1e-08, 0.01, 10
_N_CHIPS = 4
_N_DEV = 8


def _tile(n, pref, unit):
    if n <= pref:
        return n
    t = pref - pref % unit
    while t >= unit:
        if n % t == 0:
            return t
        t -= unit
    return n


def _params(n_axes):
    return pltpu.CompilerParams(dimension_semantics=("arbitrary",) * n_axes, vmem_limit_bytes=_VMEM_LIMIT_BYTES)


def _row(n):
    return pl.BlockSpec((1, n), lambda *_: (0, 0))


def _sigmoid(v):
    return 1.0 / (1.0 + jnp.exp(-v))


def _nt(a, b):
    return lax.dot_general(a, b, (((1,), (1,)), ((), ())), preferred_element_type=F32)


def _tn(a, b):
    return lax.dot_general(a, b, (((0,), (0,)), ((), ())), preferred_element_type=F32)


def _carrying(carry, body, *, name, grid, in_specs, out_specs, out_shape, scratch_shapes=(), compiler_params):
    single = not isinstance(out_shape, (list, tuple))
    if carry is None:
        call = pl.pallas_call(body, name=name, grid=grid, in_specs=in_specs, out_specs=out_specs, out_shape=out_shape,
                              scratch_shapes=list(scratch_shapes), compiler_params=compiler_params)
        return lambda *operands: (call(*operands), [])
    srcs, land_shapes, items = carry
    out_specs_l, out_shape_l = ([out_specs], [out_shape]) if single else (list(out_specs), list(out_shape))
    n_in, n_out, n_scr, n_src, n_land = len(in_specs), len(out_shape_l), len(scratch_shapes), len(srcs), len(land_shapes)
    n_remote = len(items) * len(_CHIP_FLIPS)
    last = grid[0] - 1

    def carrier(*refs):
        ins, src_refs = refs[:n_in], refs[n_in:n_in + n_src]
        o0 = n_in + n_src
        outs, land_refs = refs[o0:o0 + n_out], refs[o0 + n_out:o0 + n_out + n_land]
        s0 = o0 + n_out + n_land
        scratch, (send_sems, recv_sems, local_sems) = refs[s0:s0 + n_scr], refs[s0 + n_scr:]
        me, pushes, arrivals = _split_copies(src_refs, land_refs, send_sems, recv_sems, items)
        step = pl.program_id(0)

        @pl.when(step == 0)
        def _():
            own = [pltpu.make_async_copy(_at(src_refs[si], src_idx(*me)), _at(land_refs[di], dst_idx(*me)), local_sems.at[k])
                   for k, (si, di, src_idx, dst_idx) in enumerate(items)]
            for cp in own:
                cp.start()
            for cp in own:
                cp.wait()
            for p in pushes:
                p.start()

        body(*ins, *outs, *scratch)

        @pl.when(step == last)
        def _():
            for a in arrivals:
                a.wait_recv()
            for p in pushes:
                p.wait_send()

    def run(*operands):
        res = pl.pallas_call(
            carrier, name=name, grid=grid, in_specs=list(in_specs) + [_HBM] * n_src, out_specs=out_specs_l + [_HBM] * n_land,
            out_shape=out_shape_l + list(land_shapes),
            scratch_shapes=list(scratch_shapes) + [pltpu.SemaphoreType.DMA((n_remote,)), pltpu.SemaphoreType.DMA((n_remote,)),
                                                   pltpu.SemaphoreType.DMA((len(items),))],
            compiler_params=compiler_params)(*operands, *srcs)
        return (res[0] if single else list(res[:n_out])), list(res[n_out:])

    return run


def _resident(shape, index_map):
    return pl.BlockSpec(shape, index_map, pipeline_mode=pl.Buffered(1))


def _norm_mod_matmul(x, gn, sc, sh, w, layer, b, name, carry=None):
    t, d = x.shape
    n = w.shape[2]
    tm, tn = _tile(t, _ROW_TILE, 16), _tile(n, _COL_TILE, _LANES)

    def body(x_ref, gn_ref, sc_ref, sh_ref, w_ref, b_ref, h_ref, z_ref):
        xf = x_ref[...]
        r = lax.rsqrt(jnp.mean(xf * xf, axis=-1, keepdims=True) + _NORM_EPS)
        h_ref[...] = ((xf * r * gn_ref[...]) * (1.0 + sc_ref[...]) + sh_ref[...]).astype(BF16)
        for c0 in range(0, n, tn):
            z = jnp.dot(h_ref[...], w_ref[:, c0:c0 + tn], preferred_element_type=F32) + b_ref[:, c0:c0 + tn]
            z_ref[:, c0:c0 + tn] = z.astype(BF16)

    return _carrying(
        carry, body, name=name, grid=(t // tm,),
        in_specs=[pl.BlockSpec((tm, d), lambda i: (i, 0)), _row(d), _row(d), _row(d),
                  _resident((None, d, n), lambda i: (layer, 0, 0)), _row(n)],
        out_specs=[pl.BlockSpec((tm, d), lambda i: (i, 0)), pl.BlockSpec((tm, n), lambda i: (i, 0))],
        out_shape=[SDS((t, d), BF16), SDS((t, n), BF16)], compiler_params=_params(1),
    )(x, gn, sc, sh, w, b)


def _matmul_resid(a, w, layer, b, x, gate, name, carry=None):
    t, k = a.shape
    d = w.shape[2]
    tm, tn = _tile(t, _ROW_TILE, 16), _tile(d, _COL_TILE, _LANES)

    def body(a_ref, w_ref, b_ref, x_ref, g_ref, y_ref, xo_ref):
        for c0 in range(0, d, tn):
            cs = slice(c0, c0 + tn)
            y = jnp.dot(a_ref[...], w_ref[:, cs], preferred_element_type=F32) + b_ref[:, cs]
            y_ref[:, cs] = y.astype(BF16)
            xo_ref[:, cs] = x_ref[:, cs] + g_ref[:, cs] * y

    blk = pl.BlockSpec((tm, d), lambda i: (i, 0))
    return _carrying(
        carry, body, name=name, grid=(t // tm,),
        in_specs=[pl.BlockSpec((tm, k), lambda i: (i, 0)), _resident((None, k, d), lambda i: (layer, 0, 0)), _row(d), blk, _row(d)],
        out_specs=[blk, blk], out_shape=[SDS((t, d), BF16), SDS((t, d), F32)], compiler_params=_params(1),
    )(a, w, b, x, gate)


def _outproj_bwd(dxo, y, gate, w, layer, name):
    t, d = dxo.shape
    k = w.shape[1]
    tm, tk = _tile(t, _ROW_TILE, 16), _tile(k, _COL_TILE, _LANES)

    def body(dxo_ref, y_ref, g_ref, w_ref, dy_ref, da_ref, dg_ref, db_ref):
        @pl.when(pl.program_id(0) == 0)
        def _():
            dg_ref[...] = jnp.zeros_like(dg_ref)
            db_ref[...] = jnp.zeros_like(db_ref)

        dxf = dxo_ref[...]
        dyf = dxf * g_ref[...]
        dy_ref[...] = dyf.astype(BF16)
        dg_ref[...] += jnp.sum(dxf * y_ref[...].astype(F32), axis=0, keepdims=True)
        db_ref[...] += jnp.sum(dyf, axis=0, keepdims=True)
        for c0 in range(0, k, tk):
            da_ref[:, c0:c0 + tk] = _nt(dy_ref[...], w_ref[c0:c0 + tk, :]).astype(BF16)

    full = pl.BlockSpec((tm, d), lambda i: (i, 0))
    return pl.pallas_call(
        body, name=name, grid=(t // tm,),
        in_specs=[full, full, _row(d), _resident((None, k, d), lambda i: (layer, 0, 0))],
        out_specs=[full, pl.BlockSpec((tm, k), lambda i: (i, 0)), _row(d), _row(d)],
        out_shape=[SDS((t, d), BF16), SDS((t, k), BF16), SDS((1, d), F32), SDS((1, d), F32)],
        compiler_params=_params(1),
    )(dxo, y, gate, w)


def _matmul_tn(a, b, name):
    t, ka = a.shape
    nb = b.shape[1]
    tka, tnb, tt = _tile(ka, _WGRAD_TILE, _LANES), _tile(nb, _WGRAD_TILE, _LANES), _tile(t, 2 * _ROW_TILE, 16)
    nt = t // tt

    def body(a_ref, b_ref, o_ref, acc):
        s = pl.program_id(2)

        @pl.when(s == 0)
        def _():
            acc[...] = jnp.zeros_like(acc)

        acc[...] += _tn(a_ref[...], b_ref[...])

        @pl.when(s == nt - 1)
        def _():
            o_ref[...] = acc[...].astype(BF16)

    return pl.pallas_call(
        body, name=name, grid=(ka // tka, nb // tnb, nt),
        in_specs=[pl.BlockSpec((tt, tka), lambda i, j, s: (s, i)), pl.BlockSpec((tt, tnb), lambda i, j, s: (s, j))],
        out_specs=pl.BlockSpec((tka, tnb), lambda i, j, s: (i, j)),
        out_shape=SDS((ka, nb), BF16), scratch_shapes=[pltpu.VMEM((tka, tnb), F32)], compiler_params=_params(3),
    )(a, b)


def _inproj_bwd(dz, w, layer, x, dxo, gn, sc, name):
    t, n = dz.shape
    d = x.shape[1]
    tm, tk = _tile(t, _ROW_TILE, 16), _tile(n, _COL_TILE, _LANES)

    def body(dz_ref, w_ref, x_ref, dxo_ref, gn_ref, sc_ref, dx_ref, dgn_ref, dsc_ref, dsh_ref, acc):
        @pl.when(pl.program_id(0) == 0)
        def _():
            dgn_ref[...] = jnp.zeros_like(dgn_ref)
            dsc_ref[...] = jnp.zeros_like(dsc_ref)
            dsh_ref[...] = jnp.zeros_like(dsh_ref)

        for c0 in range(0, n, tk):
            part = _nt(dz_ref[:, c0:c0 + tk], w_ref[:, c0:c0 + tk])
            if c0 == 0:
                acc[...] = part
            else:
                acc[...] += part
        dh = acc[...]
        xf = x_ref[...]
        r = lax.rsqrt(jnp.mean(xf * xf, axis=-1, keepdims=True) + _NORM_EPS)
        xn = xf * r
        gnv = gn_ref[...]
        dsh_ref[...] += jnp.sum(dh, axis=0, keepdims=True)
        dsc_ref[...] += jnp.sum(dh * (xn * gnv), axis=0, keepdims=True)
        drn = dh * (1.0 + sc_ref[...])
        dgn_ref[...] += jnp.sum(drn * xn, axis=0, keepdims=True)
        dxn = drn * gnv
        dx_ref[...] = dxo_ref[...] + r * (dxn - xn * jnp.mean(dxn * xn, axis=-1, keepdims=True))

    full = pl.BlockSpec((tm, d), lambda i: (i, 0))
    return pl.pallas_call(
        body, name=name, grid=(t // tm,),
        in_specs=[pl.BlockSpec((tm, n), lambda i: (i, 0)), _resident((None, d, n), lambda i: (layer, 0, 0)),
                  full, full, _row(d), _row(d)],
        out_specs=[full, _row(d), _row(d), _row(d)],
        out_shape=[SDS((t, d), F32), SDS((1, d), F32), SDS((1, d), F32), SDS((1, d), F32)],
        scratch_shapes=[pltpu.VMEM((tm, d), F32)], compiler_params=_params(1),
    )(dz, w, x, dxo, gn, sc)


def _final_loss(x, g, target, name):
    t, d = x.shape
    tm = _tile(t, _ROW_TILE, 8)

    def body(x_ref, g_ref, t_ref, loss_ref, dx_ref, dg_ref):
        @pl.when(pl.program_id(0) == 0)
        def _():
            loss_ref[...] = jnp.zeros_like(loss_ref)
            dg_ref[...] = jnp.zeros_like(dg_ref)

        xf = x_ref[...]
        r = lax.rsqrt(jnp.mean(xf * xf, axis=-1, keepdims=True) + _NORM_EPS)
        xn = xf * r
        gv = g_ref[...]
        e = xn * gv - t_ref[...]
        per_row = jnp.mean(e * e, axis=-1, keepdims=True)
        loss_ref[...] += 0.5 * jnp.sum(per_row, axis=0, keepdims=True)
        dy = e * (1.0 / d)
        dg_ref[...] += jnp.sum(dy * xn, axis=0, keepdims=True)
        dxn = dy * gv
        dx_ref[...] = r * (dxn - xn * jnp.mean(dxn * xn, axis=-1, keepdims=True))

    full = pl.BlockSpec((tm, d), lambda i: (i, 0))
    return pl.pallas_call(
        body, name=name, grid=(t // tm,), in_specs=[full, _row(d), full],
        out_specs=[_row(_LANES), full, _row(d)],
        out_shape=[SDS((1, _LANES), F32), SDS((t, d), F32), SDS((1, d), F32)], compiler_params=_params(1),
    )(x, g, target)


def _attn_tables(nh):
    group = nh // _N_KV
    slopes = 2.0 ** (-8.0 * jnp.arange(1, nh + 1, dtype=F32) / nh)
    qpos = jnp.arange(_ATTN_BLOCK) + _ATTN_BLOCK
    kpos = jnp.arange(2 * _ATTN_BLOCK)
    dist = qpos[:, None] - kpos[None, :]
    band = (dist >= 0) & (dist < _ATTN_BLOCK)
    bias = jnp.where(band[None], -slopes[:, None, None] * dist.astype(F32)[None], _NEG_INF)
    first = jnp.where((kpos < _ATTN_BLOCK)[None, None, :], _NEG_INF, bias)
    return jnp.stack([first, bias]).reshape(2, _N_KV, group * _ATTN_BLOCK, 2 * _ATTN_BLOCK)


def _attn_probs(q4, k2, tab, sink_ref, kv, group):
    hd, blk = _HEAD_DIM, _ATTN_BLOCK
    s = _nt(q4, k2) * (hd ** -0.5) + tab
    sink = jnp.concatenate([jnp.full((blk, 1), sink_ref[kv * group + g], F32) for g in range(group)], axis=0)
    m = jnp.maximum(jnp.max(s, axis=-1, keepdims=True), sink)
    e = jnp.exp(s - m)
    es = jnp.exp(sink - m)
    inv = 1.0 / (jnp.sum(e, axis=-1, keepdims=True) + es)
    return e * inv, es * inv


def _attn_operands(refs, b, kv, group):
    q_ref, kp_ref, kc_ref, vp_ref, vc_ref = refs
    hd, blk = _HEAD_DIM, _ATTN_BLOCK
    rows, cs = slice(b * blk, (b + 1) * blk), slice(kv * hd, (kv + 1) * hd)
    before = slice((b - 1) * blk, b * blk)
    k2 = jnp.concatenate([kp_ref[:, cs] if b == 0 else kc_ref[before, cs], kc_ref[rows, cs]], axis=0)
    v2 = jnp.concatenate([vp_ref[:, cs] if b == 0 else vc_ref[before, cs], vc_ref[rows, cs]], axis=0)
    q4 = jnp.concatenate([q_ref[rows, (kv * group + g) * hd:(kv * group + g + 1) * hd] for g in range(group)], axis=0)
    return q4, k2, v2


def _attn_specs(nh, nblk, step):
    group = nh // _N_KV
    blk, kvw = _ATTN_BLOCK, _N_KV * _HEAD_DIM
    prev = lambda i: jnp.maximum(step(i) * nblk - 1, 0)
    return [
        pl.BlockSpec((nblk * blk, nh * _HEAD_DIM), lambda i: (step(i), 0)),
        pl.BlockSpec((blk, kvw), lambda i: (prev(i), group)),
        pl.BlockSpec((nblk * blk, kvw), lambda i: (step(i), group)),
        pl.BlockSpec((blk, kvw), lambda i: (prev(i), group + 1)),
        pl.BlockSpec((nblk * blk, kvw), lambda i: (step(i), group + 1)),
        pl.BlockSpec((2, _N_KV, group * blk, 2 * blk), lambda i: (0, 0, 0, 0)),
        pl.BlockSpec(memory_space=pltpu.SMEM),
    ]


def _attn_fwd(qkv, tables, sinks, nh, name, carry=None):
    t = qkv.shape[0]
    group, hd, blk = nh // _N_KV, _HEAD_DIM, _ATTN_BLOCK
    nblk = _ATTN_BLOCKS_PER_STEP if t % (_ATTN_BLOCKS_PER_STEP * blk) == 0 else 1

    def body(q_ref, kp_ref, kc_ref, vp_ref, vc_ref, tab_ref, sink_ref, o_ref):
        i = pl.program_id(0)
        for b in range(nblk):
            tab = tab_ref.at[jnp.minimum(i * nblk + b, 1)]
            outs = [None] * nh
            for kv in range(_N_KV):
                q4, k2, v2 = _attn_operands((q_ref, kp_ref, kc_ref, vp_ref, vc_ref), b, kv, group)
                p, _ = _attn_probs(q4, k2, tab[kv], sink_ref, kv, group)
                o4 = jnp.dot(p.astype(BF16), v2, preferred_element_type=F32)
                for g in range(group):
                    outs[kv * group + g] = o4[g * blk:(g + 1) * blk, :]
            o_ref[b * blk:(b + 1) * blk, :] = jnp.concatenate(outs, axis=1).astype(BF16)

    return _carrying(
        carry, body, name=name, grid=(t // (nblk * blk),), in_specs=_attn_specs(nh, nblk, lambda i: i),
        out_specs=pl.BlockSpec((nblk * blk, nh * hd), lambda i: (i, 0)), out_shape=SDS((t, nh * hd), BF16),
        compiler_params=_params(1),
    )(qkv, qkv, qkv, qkv, qkv, tables, sinks)


def _attn_bwd(qkv, do, tables, sinks, nh, name, carry=None):
    t, wq = qkv.shape
    group, hd, blk = nh // _N_KV, _HEAD_DIM, _ATTN_BLOCK
    nblk = _ATTN_BLOCKS_PER_STEP if t % (_ATTN_BLOCKS_PER_STEP * blk) == 0 else 1
    nsteps = t // (nblk * blk)
    kvw = _N_KV * hd
    step = lambda i: nsteps - 1 - i

    def body(q_ref, kp_ref, kc_ref, vp_ref, vc_ref, tab_ref, sink_ref, do_ref, dqkv_ref, dsink_ref, db_ref, ck, cv):
        i = pl.program_id(0)

        @pl.when(i == 0)
        def _():
            ck[...] = jnp.zeros_like(ck)
            cv[...] = jnp.zeros_like(cv)
            dsink_ref[...] = jnp.zeros_like(dsink_ref)
            db_ref[...] = jnp.zeros_like(db_ref)

        lane = lax.broadcasted_iota(jnp.int32, (1, _LANES), 1)
        carry_k = [ck[:, kv * hd:(kv + 1) * hd] for kv in range(_N_KV)]
        carry_v = [cv[:, kv * hd:(kv + 1) * hd] for kv in range(_N_KV)]
        dsink = jnp.zeros((1, _LANES), F32)
        dbias = jnp.zeros((1, wq), F32)
        for b in reversed(range(nblk)):
            rows = slice(b * blk, (b + 1) * blk)
            tab = tab_ref.at[jnp.minimum(step(i) * nblk + b, 1)]
            dq, dk, dv = [None] * nh, [None] * _N_KV, [None] * _N_KV
            for kv in range(_N_KV):
                q4, k2, v2 = _attn_operands((q_ref, kp_ref, kc_ref, vp_ref, vc_ref), b, kv, group)
                p, ps = _attn_probs(q4, k2, tab[kv], sink_ref, kv, group)
                do4 = jnp.concatenate([do_ref[rows, (kv * group + g) * hd:(kv * group + g + 1) * hd] for g in range(group)], axis=0)
                dp = _nt(do4, v2)
                dl = jnp.sum(p * dp, axis=-1, keepdims=True)
                ds = (p * (dp - dl)).astype(BF16)
                dsk = -ps * dl
                for g in range(group):
                    dsink = dsink + jnp.where(lane == kv * group + g, jnp.sum(dsk[g * blk:(g + 1) * blk, :]), 0.0)
                dq4 = jnp.dot(ds, k2, preferred_element_type=F32) * (hd ** -0.5)
                for g in range(group):
                    dq[kv * group + g] = dq4[g * blk:(g + 1) * blk, :]
                dk2 = _tn(q4, ds).T * (hd ** -0.5)
                dv2 = _tn(do4, p.astype(BF16)).T
                dk[kv] = dk2[blk:, :] + carry_k[kv]
                dv[kv] = dv2[blk:, :] + carry_v[kv]
                carry_k[kv], carry_v[kv] = dk2[:blk, :], dv2[:blk, :]
            dqkv = jnp.concatenate(dq + dk + dv, axis=1)
            dqkv_ref[rows, :] = dqkv.astype(BF16)
            dbias = dbias + jnp.sum(dqkv, axis=0, keepdims=True)
        for kv in range(_N_KV):
            ck[:, kv * hd:(kv + 1) * hd] = carry_k[kv]
            cv[:, kv * hd:(kv + 1) * hd] = carry_v[kv]
        db_ref[...] += dbias
        dsink_ref[...] += dsink

    return _carrying(
        carry, body, name=name, grid=(nsteps,),
        in_specs=_attn_specs(nh, nblk, step) + [pl.BlockSpec((nblk * blk, nh * hd), lambda i: (step(i), 0))],
        out_specs=[pl.BlockSpec((nblk * blk, wq), lambda i: (step(i), 0)), _row(_LANES), _row(wq)],
        out_shape=[SDS((t, wq), BF16), SDS((1, _LANES), F32), SDS((1, wq), F32)],
        scratch_shapes=[pltpu.VMEM((blk, kvw), F32), pltpu.VMEM((blk, kvw), F32)], compiler_params=_params(1),
    )(qkv, qkv, qkv, qkv, qkv, tables, sinks, do)


_SUBLANES = 8


_ALL_SHIFTS = tuple(range(1, _SUBLANES))


def _shift_copies(src_ref, sh_ref, shifts=_ALL_SHIFTS):
    rows = src_ref.shape[0]
    full = src_ref[...]
    for n, b in enumerate(shifts):
        sh_ref[n] = pltpu.roll(full, rows - b, axis=0)
    return sh_ref, shifts


def _rows_at(src_ref, shifted, off, r0, rg, cs):
    b = off % _SUBLANES
    if shifted is None or b not in shifted[1]:
        return src_ref[r0 + off:r0 + off + rg, cs]
    return shifted[0][shifted[1].index(b), r0 + off - b:r0 + off - b + rg, cs]


def _taps(src_ref, w_ref, dst_ref, n_rows, width, offs, rg, shifted=None):
    cg = _tile(width, 512, _LANES)
    for c0 in range(0, width, cg):
        cs = slice(c0, c0 + cg)
        wk = [w_ref[k:k + 1, cs] for k, _ in offs]
        for r0 in range(0, n_rows, rg):
            acc = None
            for (_, off), wv in zip(offs, wk):
                term = wv * _rows_at(src_ref, shifted, off, r0, rg, cs)
                acc = term if acc is None else acc + term
            dst_ref[r0:r0 + rg, cs] = acc


def _tap_grads(dy_ref, z_ref, out_ref, n_rows, width, offs, rg, shifted=None):
    cg = _tile(width, 512, _LANES)
    for c0 in range(0, width, cg):
        cs = slice(c0, c0 + cg)
        for k, off in offs:
            acc = None
            for r0 in range(0, n_rows, rg):
                term = dy_ref[r0:r0 + rg, cs] * _rows_at(z_ref, shifted, off, r0, rg, cs)
                acc = term if acc is None else acc + term
            out_ref[k:k + 1, cs] += jnp.sum(acc, axis=0, keepdims=True)


def _conv_mid_fwd(ag, dw, dwb, lng, lnb, name, carry=None):
    t, c2 = ag.shape
    c = c2 // 2
    kw = dw.shape[0]
    hl = _CONV_HALO
    tm = _tile(t, 256, hl)
    per = tm // hl

    def body(agp_ref, ag_ref, dw_ref, dwb_ref, lng_ref, lnb_ref, o_ref, zext, yb, zsh):
        i = pl.program_id(0)
        glu = lambda ref: ref[:, :c].astype(F32) * _sigmoid(ref[:, c:].astype(F32))
        zext[0:hl, :] = jnp.where(i > 0, glu(agp_ref), 0.0)
        zext[hl:, :] = glu(ag_ref)
        _taps(zext, dw_ref, yb, tm, c, [(k, hl - (kw - 1) + k) for k in range(kw)], 32, _shift_copies(zext, zsh))
        y = yb[...] + dwb_ref[...]
        mu = jnp.mean(y, axis=-1, keepdims=True)
        yc = y - mu
        rstd = lax.rsqrt(jnp.mean(yc * yc, axis=-1, keepdims=True) + _NORM_EPS)
        ln = yc * rstd * lng_ref[...] + lnb_ref[...]
        o_ref[...] = (ln * _sigmoid(ln)).astype(BF16)

    return _carrying(
        carry, body, name=name, grid=(t // tm,),
        in_specs=[pl.BlockSpec((hl, c2), lambda i: (jnp.maximum(i * per - 1, 0), 0)), pl.BlockSpec((tm, c2), lambda i: (i, 0)),
                  pl.BlockSpec((kw, c), lambda i: (0, 0)), _row(c), _row(c), _row(c)],
        out_specs=pl.BlockSpec((tm, c), lambda i: (i, 0)), out_shape=SDS((t, c), BF16),
        scratch_shapes=[pltpu.VMEM((hl + tm, c), F32), pltpu.VMEM((tm, c), F32), pltpu.VMEM((_SUBLANES - 1, hl + tm, c), F32)],
        compiler_params=_params(1),
    )(ag, ag, dw, dwb, lng, lnb)


def _conv_mid_bwd(ag, dzc, dw, dwb, lng, lnb, name, carry=None):
    t, c2 = ag.shape
    c = c2 // 2
    kw = dw.shape[0]
    hl = _CONV_HALO
    tm = _tile(t, 256, hl)
    per = tm // hl
    nt = t // tm
    last_halo = t // hl - 1

    def body(agp_ref, ag_ref, agn_ref, dzc_ref, dzcn_ref, dw_ref, dwb_ref, lng_ref, lnb_ref,
             dag_ref, ddw_ref, ddwb_ref, dlng_ref, dlnb_ref, dbin_ref, zext, yext, dyext, dzb, zsh, dysh):
        i = pl.program_id(0)

        @pl.when(i == 0)
        def _():
            for r in (ddw_ref, ddwb_ref, dlng_ref, dlnb_ref, dbin_ref):
                r[...] = jnp.zeros_like(r)

        glu = lambda ref: ref[:, :c].astype(F32) * _sigmoid(ref[:, c:].astype(F32))
        zext[0:hl, :] = jnp.where(i > 0, glu(agp_ref), 0.0)
        zext[hl:hl + tm, :] = glu(ag_ref)
        zext[hl + tm:, :] = glu(agn_ref)
        fwd_offs = [(k, hl - (kw - 1) + k) for k in range(kw)]
        z_shifted = _shift_copies(zext, zsh)
        _taps(zext, dw_ref, yext, tm + hl, c, fwd_offs, 32, z_shifted)
        y = yext[...] + dwb_ref[...]
        mu = jnp.mean(y, axis=-1, keepdims=True)
        yc = y - mu
        rstd = lax.rsqrt(jnp.mean(yc * yc, axis=-1, keepdims=True) + _NORM_EPS)
        xhat = yc * rstd
        lngv = lng_ref[...]
        ln = xhat * lngv + lnb_ref[...]
        sg = _sigmoid(ln)
        dz_out = jnp.concatenate([dzc_ref[...].astype(F32), jnp.where(i < nt - 1, dzcn_ref[...].astype(F32), 0.0)], axis=0)
        dln = dz_out * (sg * (1.0 + ln * (1.0 - sg)))
        dlng_ref[...] += jnp.sum((dln * xhat)[:tm], axis=0, keepdims=True)
        dlnb_ref[...] += jnp.sum(dln[:tm], axis=0, keepdims=True)
        dxh = dln * lngv
        dy = rstd * (dxh - jnp.mean(dxh, axis=-1, keepdims=True) - xhat * jnp.mean(dxh * xhat, axis=-1, keepdims=True))
        dyext[...] = dy
        ddwb_ref[...] += jnp.sum(dy[:tm], axis=0, keepdims=True)
        _taps(dyext, dw_ref, dzb, tm, c, [(k, kw - 1 - k) for k in range(kw)], 32, _shift_copies(dyext, dysh))
        _tap_grads(dyext, zext, ddw_ref, tm, c, fwd_offs, 32, z_shifted)
        a = ag_ref[:, :c].astype(F32)
        sgg = _sigmoid(ag_ref[:, c:].astype(F32))
        dz = dzb[...]
        da = dz * sgg
        dg = dz * a * sgg * (1.0 - sgg)
        dag_ref[:, :c] = da.astype(BF16)
        dag_ref[:, c:] = dg.astype(BF16)
        dbin_ref[:, :c] += jnp.sum(da, axis=0, keepdims=True)
        dbin_ref[:, c:] += jnp.sum(dg, axis=0, keepdims=True)

    prev = lambda i: (jnp.maximum(i * per - 1, 0), 0)
    nxt = lambda i: (jnp.minimum((i + 1) * per, last_halo), 0)
    return _carrying(
        carry, body, name=name, grid=(nt,),
        in_specs=[pl.BlockSpec((hl, c2), prev), pl.BlockSpec((tm, c2), lambda i: (i, 0)), pl.BlockSpec((hl, c2), nxt),
                  pl.BlockSpec((tm, c), lambda i: (i, 0)), pl.BlockSpec((hl, c), nxt),
                  pl.BlockSpec((kw, c), lambda i: (0, 0)), _row(c), _row(c), _row(c)],
        out_specs=[pl.BlockSpec((tm, c2), lambda i: (i, 0)), pl.BlockSpec((kw, c), lambda i: (0, 0)), _row(c), _row(c), _row(c), _row(c2)],
        out_shape=[SDS((t, c2), BF16), SDS((kw, c), F32), SDS((1, c), F32), SDS((1, c), F32), SDS((1, c), F32), SDS((1, c2), F32)],
        scratch_shapes=[pltpu.VMEM((hl + tm + hl, c), F32), pltpu.VMEM((tm + hl, c), F32), pltpu.VMEM((tm + hl, c), F32),
                        pltpu.VMEM((tm, c), F32), pltpu.VMEM((_SUBLANES - 1, hl + tm + hl, c), F32),
                        pltpu.VMEM((_SUBLANES - 1, tm + hl, c), F32)],
        compiler_params=_params(1),
    )(ag, ag, ag, dzc, dzc, dw, dwb, lng, lnb)


def _ffn_mid_fwd(zf, dw, dwb, name):
    t, f2 = zf.shape
    f = f2 // 2
    kw = dw.shape[0]
    hl = _FFN_HALO
    tm = _tile(t, 256, hl)
    per = tm // hl
    tc = _tile(f, 256, _LANES)
    offs = [(k, hl - (kw - 1) + k) for k in range(kw)]
    shifts = tuple(sorted({off % _SUBLANES for _, off in offs} - {0}))

    def body(zp_ref, z_ref, dw_ref, dwb_ref, o_ref, zext, cb, zsh):
        i = pl.program_id(0)
        zext[0:hl, :] = jnp.where(i > 0, zp_ref[...].astype(F32), 0.0)
        zext[hl:, :] = z_ref[...].astype(F32)
        _taps(zext, dw_ref, cb, tm, f2, offs, 16, _shift_copies(zext, zsh, shifts))
        for c0 in range(0, f, tc):
            g = cb[:, c0:c0 + tc] + dwb_ref[:, c0:c0 + tc]
            u = cb[:, f + c0:f + c0 + tc] + dwb_ref[:, f + c0:f + c0 + tc]
            o_ref[:, c0:c0 + tc] = (g * _sigmoid(g) * u).astype(BF16)

    return pl.pallas_call(
        body, name=name, grid=(t // tm,),
        in_specs=[pl.BlockSpec((hl, f2), lambda i: (jnp.maximum(i * per - 1, 0), 0)), pl.BlockSpec((tm, f2), lambda i: (i, 0)),
                  pl.BlockSpec((kw, f2), lambda i: (0, 0)), _row(f2)],
        out_specs=pl.BlockSpec((tm, f), lambda i: (i, 0)), out_shape=SDS((t, f), BF16),
        scratch_shapes=[pltpu.VMEM((hl + tm, f2), F32), pltpu.VMEM((tm, f2), F32), pltpu.VMEM((len(shifts), hl + tm, f2), F32)],
        compiler_params=_params(1),
    )(zf, zf, dw, dwb)


def _ffn_mid_bwd(zf, dact, dw, dwb, name):
    t, f2 = zf.shape
    f = f2 // 2
    kw = dw.shape[0]
    hl = _FFN_HALO
    tm = _tile(t, 128, hl)
    per = tm // hl
    nt = t // tm
    last_halo = t // hl - 1
    tc = _tile(f, 256, _LANES)
    fwd_offs = [(k, hl - (kw - 1) + k) for k in range(kw)]
    bwd_offs = [(k, kw - 1 - k) for k in range(kw)]
    fwd_shifts = tuple(sorted({off % _SUBLANES for _, off in fwd_offs} - {0}))
    bwd_shifts = tuple(sorted({off % _SUBLANES for _, off in bwd_offs} - {0}))

    def body(zp_ref, z_ref, zn_ref, da_ref, dan_ref, dw_ref, dwb_ref, dzf_ref, ddw_ref, ddwb_ref, zext, cext, dcext, dzb,
             zsh, dcsh):
        i = pl.program_id(0)

        @pl.when(i == 0)
        def _():
            ddw_ref[...] = jnp.zeros_like(ddw_ref)
            ddwb_ref[...] = jnp.zeros_like(ddwb_ref)

        zext[0:hl, :] = jnp.where(i > 0, zp_ref[...].astype(F32), 0.0)
        zext[hl:hl + tm, :] = z_ref[...].astype(F32)
        zext[hl + tm:, :] = zn_ref[...].astype(F32)
        z_shifted = _shift_copies(zext, zsh, fwd_shifts)
        _taps(zext, dw_ref, cext, tm + hl, f2, fwd_offs, 16, z_shifted)
        for c0 in range(0, f, tc):
            g = cext[:, c0:c0 + tc] + dwb_ref[:, c0:c0 + tc]
            u = cext[:, f + c0:f + c0 + tc] + dwb_ref[:, f + c0:f + c0 + tc]
            da = jnp.concatenate([da_ref[:, c0:c0 + tc].astype(F32),
                                  jnp.where(i < nt - 1, dan_ref[:, c0:c0 + tc].astype(F32), 0.0)], axis=0)
            sg = _sigmoid(g)
            dcg = da * u * (sg * (1.0 + g * (1.0 - sg)))
            dcu = da * (g * sg)
            dcext[:, c0:c0 + tc] = dcg
            dcext[:, f + c0:f + c0 + tc] = dcu
            ddwb_ref[:, c0:c0 + tc] += jnp.sum(dcg[:tm], axis=0, keepdims=True)
            ddwb_ref[:, f + c0:f + c0 + tc] += jnp.sum(dcu[:tm], axis=0, keepdims=True)
        _taps(dcext, dw_ref, dzb, tm, f2, bwd_offs, 16, _shift_copies(dcext, dcsh, bwd_shifts))
        _tap_grads(dcext, zext, ddw_ref, tm, f2, fwd_offs, 16, z_shifted)
        dzf_ref[...] = dzb[...].astype(BF16)

    prev = lambda i: (jnp.maximum(i * per - 1, 0), 0)
    nxt = lambda i: (jnp.minimum((i + 1) * per, last_halo), 0)
    return pl.pallas_call(
        body, name=name, grid=(nt,),
        in_specs=[pl.BlockSpec((hl, f2), prev), pl.BlockSpec((tm, f2), lambda i: (i, 0)), pl.BlockSpec((hl, f2), nxt),
                  pl.BlockSpec((tm, f), lambda i: (i, 0)), pl.BlockSpec((hl, f), nxt),
                  pl.BlockSpec((kw, f2), lambda i: (0, 0)), _row(f2)],
        out_specs=[pl.BlockSpec((tm, f2), lambda i: (i, 0)), pl.BlockSpec((kw, f2), lambda i: (0, 0)), _row(f2)],
        out_shape=[SDS((t, f2), BF16), SDS((kw, f2), F32), SDS((1, f2), F32)],
        scratch_shapes=[pltpu.VMEM((hl + tm + hl, f2), F32), pltpu.VMEM((tm + hl, f2), F32), pltpu.VMEM((tm + hl, f2), F32),
                        pltpu.VMEM((tm, f2), F32), pltpu.VMEM((len(fwd_shifts), hl + tm + hl, f2), F32),
                        pltpu.VMEM((len(bwd_shifts), tm + hl, f2), F32)],
        compiler_params=_params(1),
    )(zf, zf, zf, dact, dact, dw, dwb)


_INV_SQRT2 = 0.7071067811865476
_INV_SQRT_2PI = 0.3989422804014327


def _sgu_common(zin_ref, lng_ref, lnb_ref, hh):
    z = zin_ref[...].astype(F32)
    cdf = 0.5 * (1.0 + lax.erf(z * _INV_SQRT2))
    ge = z * cdf
    u, v = ge[:, :hh], ge[:, hh:]
    mu = jnp.mean(v, axis=-1, keepdims=True)
    vc = v - mu
    rstd = lax.rsqrt(jnp.mean(vc * vc, axis=-1, keepdims=True) + _NORM_EPS)
    vhat = vc * rstd
    vn = vhat * lng_ref[...] + lnb_ref[...]
    return z, cdf, u, vhat, rstd, vn


def _sgu_wm(ws_ref, g, ch):
    rows = lax.broadcasted_iota(jnp.int32, (ch, ch), 0)
    cols = lax.broadcasted_iota(jnp.int32, (ch, ch), 1)
    return jnp.where(rows >= cols, ws_ref[g], 0.0).astype(BF16)


def _sgu_mid_fwd(zin, lng, lnb, ws, bs_t, name, carry=None):
    t, h2 = zin.shape
    hh = h2 // 2
    ng, ch = ws.shape[0], ws.shape[1]
    hg = hh // ng
    tm = _tile(t, 256, ch)

    def body(zin_ref, lng_ref, lnb_ref, ws_ref, bs_ref, o_ref):
        _, _, u, _, _, vn = _sgu_common(zin_ref, lng_ref, lnb_ref, hh)
        vnb = vn.astype(BF16)
        for g in range(ng):
            wm = _sgu_wm(ws_ref, g, ch)
            for cc in range(tm // ch):
                rs, cs = slice(cc * ch, (cc + 1) * ch), slice(g * hg, (g + 1) * hg)
                vv = jnp.dot(wm, vnb[rs, cs], preferred_element_type=F32) + bs_ref[:, g:g + 1]
                o_ref[rs, cs] = (u[rs, cs] * vv).astype(BF16)

    return _carrying(
        carry, body, name=name, grid=(t // tm,),
        in_specs=[pl.BlockSpec((tm, h2), lambda i: (i, 0)), _row(hh), _row(hh),
                  pl.BlockSpec((ng, ch, ch), lambda i: (0, 0, 0)), pl.BlockSpec((ch, ng), lambda i: (0, 0))],
        out_specs=pl.BlockSpec((tm, hh), lambda i: (i, 0)), out_shape=SDS((t, hh), BF16), compiler_params=_params(1),
    )(zin, lng, lnb, ws, bs_t)


def _sgu_mid_bwd(zin, duv, lng, lnb, ws, bs_t, name, carry=None):
    t, h2 = zin.shape
    hh = h2 // 2
    ng, ch = ws.shape[0], ws.shape[1]
    hg = hh // ng
    tm = _tile(t, 128, ch)

    def body(zin_ref, duv_ref, lng_ref, lnb_ref, ws_ref, bs_ref, dzin_ref, dlng_ref, dlnb_ref, dws_ref, dbs_ref, dbin_ref, dvn_s, du_s):
        @pl.when(pl.program_id(0) == 0)
        def _():
            for r in (dlng_ref, dlnb_ref, dws_ref, dbs_ref, dbin_ref):
                r[...] = jnp.zeros_like(r)

        z, cdf, u, vhat, rstd, vn = _sgu_common(zin_ref, lng_ref, lnb_ref, hh)
        vnb = vn.astype(BF16)
        duv = duv_ref[...].astype(F32)
        dvv = (duv * u).astype(BF16)
        lane = lax.broadcasted_iota(jnp.int32, (1, _LANES), 1)
        rows = lax.broadcasted_iota(jnp.int32, (ch, ch), 0)
        cols = lax.broadcasted_iota(jnp.int32, (ch, ch), 1)
        dbs = jnp.zeros((ch, _LANES), F32)
        for g in range(ng):
            wm = _sgu_wm(ws_ref, g, ch)
            dwm = jnp.zeros((ch, ch), F32)
            for cc in range(tm // ch):
                rs, cs = slice(cc * ch, (cc + 1) * ch), slice(g * hg, (g + 1) * hg)
                vv = jnp.dot(wm, vnb[rs, cs], preferred_element_type=F32) + bs_ref[:, g:g + 1]
                du_s[rs, cs] = duv[rs, cs] * vv
                dvn_s[rs, cs] = _tn(wm, dvv[rs, cs])
                dwm = dwm + _nt(dvv[rs, cs], vnb[rs, cs])
                dbs = dbs + jnp.where(lane == g, jnp.sum(dvv[rs, cs].astype(F32), axis=-1, keepdims=True), 0.0)
            dws_ref[g] += jnp.where(rows >= cols, dwm, 0.0)
        dbs_ref[...] += dbs
        dvn = dvn_s[...]
        dlng_ref[...] += jnp.sum(dvn * vhat, axis=0, keepdims=True)
        dlnb_ref[...] += jnp.sum(dvn, axis=0, keepdims=True)
        dxh = dvn * lng_ref[...]
        dv = rstd * (dxh - jnp.mean(dxh, axis=-1, keepdims=True) - vhat * jnp.mean(dxh * vhat, axis=-1, keepdims=True))
        dgelu = cdf + z * (_INV_SQRT_2PI * jnp.exp(-0.5 * z * z))
        dzu = du_s[...] * dgelu[:, :hh]
        dzv = dv * dgelu[:, hh:]
        dzin_ref[:, :hh] = dzu.astype(BF16)
        dzin_ref[:, hh:] = dzv.astype(BF16)
        dbin_ref[:, :hh] += jnp.sum(dzu, axis=0, keepdims=True)
        dbin_ref[:, hh:] += jnp.sum(dzv, axis=0, keepdims=True)

    return _carrying(
        carry, body, name=name, grid=(t // tm,),
        in_specs=[pl.BlockSpec((tm, h2), lambda i: (i, 0)), pl.BlockSpec((tm, hh), lambda i: (i, 0)), _row(hh), _row(hh),
                  pl.BlockSpec((ng, ch, ch), lambda i: (0, 0, 0)), pl.BlockSpec((ch, ng), lambda i: (0, 0))],
        out_specs=[pl.BlockSpec((tm, h2), lambda i: (i, 0)), _row(hh), _row(hh), pl.BlockSpec((ng, ch, ch), lambda i: (0, 0, 0)),
                   pl.BlockSpec((ch, _LANES), lambda i: (0, 0)), _row(h2)],
        out_shape=[SDS((t, h2), BF16), SDS((1, hh), F32), SDS((1, hh), F32), SDS((ng, ch, ch), F32), SDS((ch, _LANES), F32),
                   SDS((1, h2), F32)],
        scratch_shapes=[pltpu.VMEM((tm, hh), F32), pltpu.VMEM((tm, hh), F32)], compiler_params=_params(1),
    )(zin, duv, lng, lnb, ws, bs_t)


def _ada_mod(c_all, ada_w, ada_b, name):
    nl, d, n = ada_w.shape
    nb = c_all.shape[0]
    tn = _tile(n, _COL_TILE, _LANES)

    def body(c_ref, w_ref, b_ref, o_ref):
        cv = c_ref[...]
        ca = cv * _sigmoid(cv)
        o_ref[...] = jnp.dot(ca, w_ref[...], preferred_element_type=F32, precision=lax.Precision.HIGHEST) + b_ref[...]

    return pl.pallas_call(
        body, name=name, grid=(nl, n // tn),
        in_specs=[pl.BlockSpec((nb, d), lambda l, j: (0, 0)), pl.BlockSpec((None, d, tn), lambda l, j: (l, 0, j)),
                  pl.BlockSpec((None, 1, tn), lambda l, j: (l, 0, j))],
        out_specs=pl.BlockSpec((None, nb, tn), lambda l, j: (l, 0, j)), out_shape=SDS((nl, nb, n), F32),
        compiler_params=_params(2),
    )(c_all, ada_w, ada_b)


def _ada_wgrad(c_all_t, dmod, name):
    d, nb = c_all_t.shape
    nl, _, n = dmod.shape
    tn = _tile(n, _COL_TILE, _LANES)

    def body(c_ref, dm_ref, o_ref):
        cv = c_ref[...]
        ca = cv * _sigmoid(cv)
        acc = ca[:, 0:1] * dm_ref[0:1, :]
        for b in range(1, nb):
            acc = acc + ca[:, b:b + 1] * dm_ref[b:b + 1, :]
        o_ref[...] = acc

    return pl.pallas_call(
        body, name=name, grid=(nl, n // tn),
        in_specs=[pl.BlockSpec((d, nb), lambda l, j: (0, 0)), pl.BlockSpec((None, nb, tn), lambda l, j: (l, 0, j))],
        out_specs=pl.BlockSpec((None, d, tn), lambda l, j: (l, 0, j)), out_shape=SDS((nl, d, n), F32),
        compiler_params=_params(2),
    )(c_all_t, dmod)


def _as_rows(a):
    return a.reshape(-1, a.shape[-1])


def _row_tile(r, c, n_arrays):
    budget = _VMEM_LIMIT_BYTES // (4 * 2 * n_arrays * 4)
    return _tile(r, max(8, budget // max(c, 1)), 8)


def _cast_bf16_layer(a, layer, name):
    _, r, c = a.shape
    tr = _row_tile(r, c, 2)

    def body(a_ref, o_ref):
        o_ref[...] = a_ref[...].astype(BF16)

    return pl.pallas_call(body, name=name, grid=(r // tr,), in_specs=[pl.BlockSpec((None, tr, c), lambda i: (layer, i, 0))],
                          out_specs=pl.BlockSpec((tr, c), lambda i: (i, 0)), out_shape=SDS((r, c), BF16),
                          compiler_params=_params(1))(a)


def _sum_slot_layers(parts, name):
    nl = len(parts)
    n, r, c = parts[0].shape
    tr = _row_tile(r, c, nl * (n + 1))

    def body(*refs):
        o_ref = refs[nl]
        for lay in range(nl):
            acc = refs[lay][0].astype(F32)
            for s in range(1, n):
                acc = acc + refs[lay][s].astype(F32)
            o_ref[lay] = acc

    return pl.pallas_call(body, name=name, grid=(r // tr,), in_specs=[pl.BlockSpec((n, tr, c), lambda i: (0, i, 0))] * nl,
                          out_specs=pl.BlockSpec((nl, tr, c), lambda i: (0, i, 0)), out_shape=SDS((nl, r, c), F32),
                          compiler_params=_params(1))(*parts)


def _sum_slots(parts, name):
    n = parts.shape[0]
    p2 = parts.reshape(n, -1, parts.shape[-1])
    r, c = p2.shape[1:]
    tr = _row_tile(r, c, n + 1)

    def body(p_ref, o_ref):
        acc = p_ref[0].astype(F32)
        for s in range(1, n):
            acc = acc + p_ref[s].astype(F32)
        o_ref[...] = acc

    out = pl.pallas_call(body, name=name, grid=(r // tr,), in_specs=[pl.BlockSpec((n, tr, c), lambda i: (0, i, 0))],
                         out_specs=pl.BlockSpec((tr, c), lambda i: (i, 0)), out_shape=SDS((r, c), F32),
                         compiler_params=_params(1))(p2)
    return out.reshape(parts.shape[1:])


def _adamw(w, g_parts, m, v, name):
    w2, m2, v2 = _as_rows(w), _as_rows(m), _as_rows(v)
    gs = [_as_rows(g) for g in g_parts]
    r, c = w2.shape
    ng = len(gs)
    tr = _row_tile(r, c, 7 + ng)
    c1 = 1.0 - _ADAM_B1 ** _ADAM_STEP
    c2 = 1.0 - _ADAM_B2 ** _ADAM_STEP

    def body(*refs):
        w_ref, m_ref, v_ref = refs[:3]
        g_refs = refs[3:3 + ng]
        go_ref, d_ref, mo_ref, vo_ref = refs[3 + ng:]
        g = g_refs[0][...]
        for gr in g_refs[1:]:
            g = g + gr[...]
        mn = _ADAM_B1 * m_ref[...] + (1.0 - _ADAM_B1) * g
        vn = _ADAM_B2 * v_ref[...] + (1.0 - _ADAM_B2) * (g * g)
        go_ref[...] = g
        mo_ref[...] = mn
        vo_ref[...] = vn
        d_ref[...] = -_ADAM_LR * ((mn / c1) / (jnp.sqrt(vn / c2) + _ADAM_EPS) + _ADAM_WD * w_ref[...])

    spec = pl.BlockSpec((tr, c), lambda i: (i, 0))
    outs = pl.pallas_call(body, name=name, grid=(r // tr,), in_specs=[spec] * (3 + ng), out_specs=[spec] * 4,
                          out_shape=[SDS((r, c), F32)] * 4, compiler_params=_params(1))(w2, m2, v2, *gs)
    return tuple(o.reshape(w.shape) for o in outs)


_HBM = pl.BlockSpec(memory_space=pltpu.HBM)
_CHIP_FLIPS = ((1, 0, 0), (0, 1, 0), (1, 1, 0))
_ALL_FLIPS = tuple((a, b, c) for a in (0, 1) for b in (0, 1) for c in (0, 1))[1:]
_SIBLING_FLIP = ((0, 0, 1),)


def _at(ref, idx):
    return ref.at[idx] if idx else ref


def _exchange(name, flips, srcs, out_shapes, items):
    n_in, n_out = len(srcs), len(out_shapes)
    n_remote = len(items) * len(flips)
    n_local = sum(1 for it in items if it[4])

    def body(*refs):
        src_refs, dst_refs = refs[:n_in], refs[n_in:n_in + n_out]
        send_sems, recv_sems, local_sems = refs[n_in + n_out:]
        me = (lax.axis_index("x"), lax.axis_index("y"), lax.axis_index("c"))
        sends, recvs, locs = [], [], []
        n, nl = 0, 0
        for si, di, src_idx, dst_idx, local in items:
            for flip in flips:
                peer = tuple(1 - m if f else m for m, f in zip(me, flip))
                push = pltpu.make_async_remote_copy(
                    src_ref=_at(src_refs[si], src_idx(*peer)), dst_ref=_at(dst_refs[di], dst_idx(*me)),
                    send_sem=send_sems.at[n], recv_sem=recv_sems.at[n], device_id=peer, device_id_type=MESH_IDS)
                push.start()
                sends.append(push)
                recvs.append(pltpu.make_async_remote_copy(
                    src_ref=_at(src_refs[si], src_idx(*me)), dst_ref=_at(dst_refs[di], dst_idx(*peer)),
                    send_sem=send_sems.at[n], recv_sem=recv_sems.at[n], device_id=peer, device_id_type=MESH_IDS))
                n += 1
            if local:
                cp = pltpu.make_async_copy(_at(src_refs[si], src_idx(*me)), _at(dst_refs[di], dst_idx(*me)), local_sems.at[nl])
                cp.start()
                locs.append(cp)
                nl += 1
        for r in recvs:
            r.wait_recv()
        for s in sends:
            s.wait_send()
        for cp in locs:
            cp.wait()

    return pl.pallas_call(
        body, name=name, in_specs=[_HBM] * n_in, out_specs=[_HBM] * n_out, out_shape=list(out_shapes),
        scratch_shapes=[pltpu.SemaphoreType.DMA((n_remote,)), pltpu.SemaphoreType.DMA((n_remote,)),
                        pltpu.SemaphoreType.DMA((max(n_local, 1),))],
    )(*srcs)


_SEM = pl.BlockSpec(memory_space=pltpu.SEMAPHORE)
_SIDE_EFFECTS = pltpu.SideEffectType.DATAFLOW_SIDE_EFFECTING


def _split_copies(src_refs, land_refs, send_sems, recv_sems, items):
    me = (lax.axis_index("x"), lax.axis_index("y"), lax.axis_index("c"))
    pushes, arrivals, n = [], [], 0
    for si, di, src_idx, dst_idx in items:
        for flip in _CHIP_FLIPS:
            peer = tuple(1 - m if f else m for m, f in zip(me, flip))
            pushes.append(pltpu.make_async_remote_copy(
                src_ref=_at(src_refs[si], src_idx(*peer)), dst_ref=_at(land_refs[di], dst_idx(*me)),
                send_sem=send_sems.at[n], recv_sem=recv_sems.at[n], device_id=peer, device_id_type=MESH_IDS))
            arrivals.append(pltpu.make_async_remote_copy(
                src_ref=_at(src_refs[si], src_idx(*me)), dst_ref=_at(land_refs[di], dst_idx(*peer)),
                send_sem=send_sems.at[n], recv_sem=recv_sems.at[n], device_id=peer, device_id_type=MESH_IDS))
            n += 1
    return me, pushes, arrivals


def _push_start(name, srcs, land_shapes, items, after):
    n_src, n_land = len(srcs), len(land_shapes)
    n_buf = n_src + n_land
    n_remote = len(items) * len(_CHIP_FLIPS)

    def body(*refs):
        src_refs, land_refs = refs[:n_src], refs[n_src:n_buf]
        send_sems, recv_sems = refs[n_buf + 1], refs[n_buf + 2]
        local_sems = refs[2 * n_buf + 4]
        me, pushes, _ = _split_copies(src_refs, land_refs, send_sems, recv_sems, items)
        own = [pltpu.make_async_copy(_at(src_refs[si], src_idx(*me)), _at(land_refs[di], dst_idx(*me)), local_sems.at[k])
               for k, (si, di, src_idx, dst_idx) in enumerate(items)]
        for cp in own:
            cp.start()
        for cp in own:
            cp.wait()
        for p in pushes:
            p.start()

    bufs = list(srcs) + [lax.empty(s.shape, s.dtype) for s in land_shapes]
    outs = pl.pallas_call(
        body, name=name, in_specs=[_HBM] * n_buf + [pl.BlockSpec(memory_space=pl.ANY)],
        out_shape=[pltpu.SemaphoreType.DMA((n_remote,)), pltpu.SemaphoreType.DMA((n_remote,))]
        + [pltpu.HBM(a.shape, a.dtype) for a in bufs] + [pltpu.HBM((_SUBLANES, _LANES), F32)],
        out_specs=[_SEM, _SEM] + [_HBM] * (n_buf + 1),
        input_output_aliases={k: 2 + k for k in range(n_buf)},
        scratch_shapes=[pltpu.SemaphoreType.DMA((len(items),))],
        compiler_params=pltpu.CompilerParams(has_side_effects=_SIDE_EFFECTS),
    )(*[pltpu.with_memory_space_constraint(a, pltpu.HBM) for a in bufs], after)
    return dict(send=outs[0], recv=outs[1], bufs=list(outs[2:2 + n_buf]), order=outs[2 + n_buf], n_src=n_src, items=items)


def _behind(x, handles, name):
    deps = [h["order"] for h in handles]

    def body(*refs):
        refs[-1][...] = refs[0][...]

    vmem = pl.BlockSpec(memory_space=pltpu.VMEM)
    return pl.pallas_call(body, name=name, in_specs=[vmem] + [pl.BlockSpec(memory_space=pl.ANY)] * len(deps), out_specs=vmem,
                          out_shape=SDS(x.shape, x.dtype))(x, *deps)


def _push_wait(name, handle, after):
    n_src, items = handle["n_src"], handle["items"]
    n_buf = len(handle["bufs"])

    def body(*refs):
        src_refs, land_refs = refs[:n_src], refs[n_src:n_buf]
        send_sems, recv_sems = refs[n_buf], refs[n_buf + 1]
        _, pushes, arrivals = _split_copies(src_refs, land_refs, send_sems, recv_sems, items)
        for p in pushes:
            p.wait_send()
        for a in arrivals:
            a.wait_recv()

    outs = pl.pallas_call(
        body, name=name, in_specs=[_HBM] * n_buf + [_SEM, _SEM, pl.BlockSpec(memory_space=pl.ANY)],
        out_shape=[pltpu.HBM(a.shape, a.dtype) for a in handle["bufs"]], out_specs=[_HBM] * n_buf,
        input_output_aliases={k: k for k in range(n_buf)},
        compiler_params=pltpu.CompilerParams(has_side_effects=_SIDE_EFFECTS),
    )(*handle["bufs"], handle["send"], handle["recv"], after)
    return list(outs[n_src:])


def _chip_of(x, y, c):
    return 2 * x + y


def _dev_of(x, y, c):
    return 4 * x + 2 * y + c


def _window(axis, ndim, size):
    def idx(x, y, c):
        return tuple(pl.ds(_chip_of(x, y, c) * size, size) if a == axis else slice(None) for a in range(ndim))
    return idx


def _whole(x, y, c):
    return ()


def _gather_pushes(shards_axes):
    srcs, shapes, items = [], [], []
    for k, (a, axis) in enumerate(shards_axes):
        full = list(a.shape)
        full[axis] *= _N_CHIPS
        srcs.append(a)
        shapes.append(SDS((1,) + tuple(full), a.dtype))
        window = _window(axis, a.ndim, a.shape[axis])
        items.append((k, k, _whole, (lambda w: lambda x, y, c: (0,) + w(x, y, c))(window)))
    return srcs, shapes, items


def _scatter_pushes(grads_axes):
    srcs, shapes, items = [], [], []
    slot = lambda x, y, c: (_chip_of(x, y, c),)
    for k, (a, axis) in enumerate(grads_axes):
        shard = list(a.shape)
        shard[axis] //= _N_CHIPS
        srcs.append(a)
        shapes.append(SDS((_N_CHIPS,) + tuple(shard), a.dtype))
        items.append((k, k, _window(axis, a.ndim, shard[axis]), slot))
    return srcs, shapes, items


def _gather_all(name, a):
    slot = lambda x, y, c: (_dev_of(x, y, c),)
    return _exchange(name, _ALL_FLIPS, [a], [SDS((_N_DEV,) + a.shape, a.dtype)], [(0, 0, _whole, slot, True)])[0]


def _swap_chips(name, a):
    slot = lambda x, y, c: (_chip_of(x, y, c),)
    return _exchange(name, _CHIP_FLIPS, [a], [SDS(a.shape, a.dtype)], [(0, 0, slot, slot, True)])[0]


def _swap_sibling(name, arrays):
    items = [(k, k, _whole, _whole, False) for k in range(len(arrays))]
    return _exchange(name, _SIBLING_FLIP, list(arrays), [SDS(a.shape, a.dtype) for a in arrays], items)


_BIG = ("attn_wqkv", "attn_wo", "conv_w_in", "conv_w_out", "sgu_w_in", "sgu_w_out", "ffn_w_in", "ffn_w_out")
_BIG_AXIS = {"attn_wqkv": 2, "attn_wo": 1, "conv_w_in": 2, "conv_w_out": 1, "sgu_w_in": 2, "sgu_w_out": 1,
             "ffn_w_in": 2, "ffn_w_out": 1}
_SMALL = {"norm1_g": None, "norm2_g": None, "ada_b": None, "attn_bqkv": 1, "attn_sinks": None, "attn_bo": 1,
          "conv_b_in": None, "conv_dw": 2, "conv_dw_b": None, "conv_ln_g": None, "conv_ln_b": None, "conv_b_out": None,
          "sgu_b_in": 1, "sgu_ln_g": 1, "sgu_ln_b": 1, "sgu_ws": None, "sgu_bs": None, "sgu_b_out": 1,
          "ffn_dw": 2, "ffn_dw_b": None, "final_g": None}
_WEIGHTS = ['norm1_g', 'norm2_g', 'ada_w', 'ada_b', 'attn_wqkv', 'attn_bqkv', 'attn_sinks', 'attn_wo', 'attn_bo',
            'conv_w_in', 'conv_b_in', 'conv_dw', 'conv_dw_b', 'conv_ln_g', 'conv_ln_b', 'conv_w_out', 'conv_b_out',
            'sgu_w_in', 'sgu_b_in', 'sgu_ln_g', 'sgu_ln_b', 'sgu_ws', 'sgu_bs', 'sgu_w_out', 'sgu_b_out',
            'ffn_w_in', 'ffn_dw', 'ffn_dw_b', 'ffn_w_out', 'final_g']
_FLAT_COLS = 1024


def _full_shape(a, axis):
    s = list(a.shape)
    if axis is not None:
        s[axis] *= _N_CHIPS
    return tuple(s)


def _pack(arrays):
    flat = jnp.concatenate([a.reshape(-1).astype(F32) for a in arrays])
    pad = (-flat.shape[0]) % (8 * _FLAT_COLS)
    return jnp.pad(flat, (0, pad)).reshape(-1, _FLAT_COLS)


def _unpack(flat2d, shapes):
    flat = flat2d.reshape(-1)
    out, off = [], 0
    for s in shapes:
        n = math.prod(s)
        out.append(flat[off:off + n].reshape(s))
        off += n
    return out


def _shard_of(full, axis, chip):
    if axis is None:
        return full
    size = full.shape[axis] // _N_CHIPS
    return lax.dynamic_slice_in_dim(full, chip * size, size, axis)


def _unshard(gathered, axis):
    moved = jnp.moveaxis(gathered, 0, axis)
    shape = list(gathered.shape[1:])
    shape[axis] *= _N_CHIPS
    return moved.reshape(shape)


def kernel(x, c, norm1_g, norm2_g, ada_w, ada_b, attn_wqkv, attn_bqkv, attn_sinks, attn_wo, attn_bo, conv_w_in, conv_b_in, conv_dw, conv_dw_b, conv_ln_g, conv_ln_b, conv_w_out, conv_b_out, sgu_w_in, sgu_b_in, sgu_ln_g, sgu_ln_b, sgu_ws, sgu_bs, sgu_w_out, sgu_b_out, ffn_w_in, ffn_dw, ffn_dw_b, ffn_w_out, final_g, loss_target, m_norm1_g, m_norm2_g, m_ada_w, m_ada_b, m_attn_wqkv, m_attn_bqkv, m_attn_sinks, m_attn_wo, m_attn_bo, m_conv_w_in, m_conv_b_in, m_conv_dw, m_conv_dw_b, m_conv_ln_g, m_conv_ln_b, m_conv_w_out, m_conv_b_out, m_sgu_w_in, m_sgu_b_in, m_sgu_ln_g, m_sgu_ln_b, m_sgu_ws, m_sgu_bs, m_sgu_w_out, m_sgu_b_out, m_ffn_w_in, m_ffn_dw, m_ffn_dw_b, m_ffn_w_out, m_final_g, v_norm1_g, v_norm2_g, v_ada_w, v_ada_b, v_attn_wqkv, v_attn_bqkv, v_attn_sinks, v_attn_wo, v_attn_bo, v_conv_w_in, v_conv_b_in, v_conv_dw, v_conv_dw_b, v_conv_ln_g, v_conv_ln_b, v_conv_w_out, v_conv_b_out, v_sgu_w_in, v_sgu_b_in, v_sgu_ln_g, v_sgu_ln_b, v_sgu_ws, v_sgu_bs, v_sgu_w_out, v_sgu_b_out, v_ffn_w_in, v_ffn_dw, v_ffn_dw_b, v_ffn_w_out, v_final_g):
    args = dict(locals())
    wts = {n: args[n] for n in _WEIGHTS}
    mom_m = {n: args["m_" + n] for n in _WEIGHTS}
    mom_v = {n: args["v_" + n] for n in _WEIGHTS}

    ix, iy, ic = lax.axis_index("x"), lax.axis_index("y"), lax.axis_index("c")
    chip = 2 * ix + iy
    xs = x[0]
    tgt = loss_target[0]
    t, d = xs.shape
    nh = d // _HEAD_DIM
    depth = ada_w.shape[0]
    ncols = ada_w.shape[2]
    n_mod = ncols * _N_CHIPS // d

    small_sharded = [n for n in _WEIGHTS if _SMALL.get(n) is not None]
    packed_small = _pack([wts[n] for n in small_sharded])
    mixers = {0: ("attn_wqkv", "attn_wo"), 1: ("conv_w_in", "conv_w_out"), 2: ("sgu_w_in", "sgu_w_out")}

    def shards_of(keys):
        return [(_cast_bf16_layer(wts[n], j, "cast_%s_%d" % (n, j)), _BIG_AXIS[n] - 1) for n, j in keys]

    c_all = _gather_all("gather_c", c)[:, 0, :]
    ada_b_cols = lax.dynamic_slice_in_dim(ada_b, chip * ncols, ncols, 1)[:, None, :]
    mod_cols = _ada_mod(c_all, ada_w, ada_b_cols, "ada_mod")
    mine = lax.dynamic_index_in_dim(mod_cols.reshape(depth, _N_CHIPS, 2, ncols), ic, 2, keepdims=False)
    got = _swap_chips("swap_mod", jnp.moveaxis(mine, 1, 0))

    first = [(n, 0) for n in mixers[0]]
    gather0 = _push_start("start_gather_0", *_gather_pushes(shards_of(first) + [(packed_small, 0)]), got)
    mod = _behind(jnp.moveaxis(got, 0, 1).reshape(depth, n_mod, 1, d), [gather0], "behind_gather_start")
    lands0 = _push_wait("wait_gather_0", gather0, mod)
    wfull = dict(zip(first, lands0))
    small_rows = lands0[-1].reshape(_N_CHIPS, -1, _FLAT_COLS)
    pieces = [_unpack(small_rows[s], [wts[n].shape for n in small_sharded]) for s in range(_N_CHIPS)]
    full = {n: wts[n] for n in _WEIGHTS if n in _SMALL and _SMALL[n] is None}
    for k, n in enumerate(small_sharded):
        full[n] = _unshard(jnp.stack([pieces[s][k] for s in range(_N_CHIPS)]), _SMALL[n])

    tables = _attn_tables(nh)
    zeros_d = jnp.zeros((1, d), F32)
    zeros_f2 = jnp.zeros((1, ffn_w_in.shape[2] * _N_CHIPS), F32)
    row = lambda a: a.reshape(1, -1)

    saved = []
    xcur = xs
    for i in range(depth):
        sh1, sc1, g1, sh2, sc2, g2 = (mod[i, k] for k in range(n_mod))
        kind, j = i % 3, i // 3
        tag = "L%d_" % i
        more = i + 1 < depth
        mid_keys = ([("ffn_w_in", 0), ("ffn_w_out", 0)] if i == 0 else []) + \
                   ([(n, (i + 1) // 3) for n in mixers[(i + 1) % 3]] if more else [])
        mid_carry = _gather_pushes(shards_of(mid_keys)) if mid_keys else None
        in_carry = _gather_pushes(shards_of([("ffn_w_in", i + 1)])) if more else None
        out_carry = _gather_pushes(shards_of([("ffn_w_out", i + 1)])) if more else None
        w_in, w_out = wfull[(mixers[kind][0], j)], wfull[(mixers[kind][1], j)]
        if kind == 0:
            b_in, b_out = row(full["attn_bqkv"][j]), row(full["attn_bo"][j])
        elif kind == 1:
            b_in, b_out = row(full["conv_b_in"][j]), row(full["conv_b_out"][j])
        else:
            b_in, b_out = row(full["sgu_b_in"][j]), row(full["sgu_b_out"][j])
        (h1, z), _ = _norm_mod_matmul(xcur, row(full["norm1_g"][i]), sc1, sh1, w_in, 0, b_in, tag + "mixer_in")
        if kind == 0:
            a, lands = _attn_fwd(z, tables, full["attn_sinks"][j], nh, tag + "attn", mid_carry)
        elif kind == 1:
            a, lands = _conv_mid_fwd(z, full["conv_dw"][j], row(full["conv_dw_b"][j]), row(full["conv_ln_g"][j]),
                                     row(full["conv_ln_b"][j]), tag + "conv_mid", mid_carry)
        else:
            a, lands = _sgu_mid_fwd(z, row(full["sgu_ln_g"][j]), row(full["sgu_ln_b"][j]), full["sgu_ws"][j], full["sgu_bs"][j].T,
                                    tag + "sgu_mid", mid_carry)
        wfull.update(zip(mid_keys, lands))
        (y1, xmid), _ = _matmul_resid(a, w_out, 0, b_out, xcur, g1, tag + "mixer_out")
        (h2, zf), lands = _norm_mod_matmul(xmid, row(full["norm2_g"][i]), sc2, sh2, wfull[("ffn_w_in", i)], 0, zeros_f2,
                                           tag + "ffn_in", in_carry)
        wfull.update(zip([("ffn_w_in", i + 1)], lands))
        act = _ffn_mid_fwd(zf, full["ffn_dw"][i], row(full["ffn_dw_b"][i]), tag + "ffn_mid")
        (y2, xnext), lands = _matmul_resid(act, wfull[("ffn_w_out", i)], 0, zeros_d, xmid, g2, tag + "ffn_out", out_carry)
        wfull.update(zip([("ffn_w_out", i + 1)], lands))
        saved.append(dict(x=xcur, h1=h1, z=z, a=a, y1=y1, xmid=xmid, h2=h2, zf=zf, act=act, y2=y2, w_in=w_in, w_out=w_out))
        xcur = xnext

    loss_row, dx, d_final_g = _final_loss(xcur, row(final_g), tgt, "final_loss")

    small_g = {n: jnp.zeros(_full_shape(wts[n], _SMALL[n]), F32) for n in _SMALL}
    small_g["final_g"] = d_final_g[0]
    dmod = [None] * depth
    pending = []
    slots = {}

    def put(name, j, val):
        small_g[name] = small_g[name].at[j].set(val.reshape(small_g[name].shape[1:]))

    for i in reversed(range(depth)):
        sv = saved[i]
        sh1, sc1, g1, sh2, sc2, g2 = (mod[i, k] for k in range(n_mod))
        kind, j = i % 3, i // 3
        tag = "L%d_" % i
        dy2, dact, dg2, _ = _outproj_bwd(dx, sv["y2"], g2, wfull[("ffn_w_out", i)], 0, tag + "ffn_out_bwd")
        g_ffn_out = _matmul_tn(sv["act"], dy2, tag + "ffn_out_wgrad")
        dzf, d_fdw, d_fdwb = _ffn_mid_bwd(sv["zf"], dact, full["ffn_dw"][i], row(full["ffn_dw_b"][i]), tag + "ffn_mid_bwd")
        put("ffn_dw", i, d_fdw)
        put("ffn_dw_b", i, d_fdwb)
        g_ffn_in = _matmul_tn(sv["h2"], dzf, tag + "ffn_in_wgrad")
        dxmid, dn2, dsc2, dsh2 = _inproj_bwd(dzf, wfull[("ffn_w_in", i)], 0, sv["xmid"], dx, row(full["norm2_g"][i]), sc2,
                                             tag + "ffn_in_bwd")
        put("norm2_g", i, dn2)
        pending += [(("ffn_w_in", i), g_ffn_in, 1), (("ffn_w_out", i), g_ffn_out, 0)]
        dy1, da, dg1, dbo = _outproj_bwd(dxmid, sv["y1"], g1, sv["w_out"], 0, tag + "mixer_out_bwd")
        w_in_name, w_out_name = mixers[kind]
        g_mix_out = _matmul_tn(sv["a"], dy1, tag + "mixer_out_wgrad")
        carry = _scatter_pushes([(g, axis) for _, g, axis in pending])
        if kind == 0:
            (dz, dsink, dbin), lands = _attn_bwd(sv["z"], da, tables, full["attn_sinks"][j], nh, tag + "attn_bwd", carry)
            put("attn_sinks", j, dsink[0, :nh])
            put("attn_bqkv", j, dbin)
            put("attn_bo", j, dbo)
        elif kind == 1:
            (dz, d_dw, d_dwb, d_lng, d_lnb, dbin), lands = _conv_mid_bwd(
                sv["z"], da, full["conv_dw"][j], row(full["conv_dw_b"][j]), row(full["conv_ln_g"][j]), row(full["conv_ln_b"][j]),
                tag + "conv_mid_bwd", carry)
            for nme, val in (("conv_dw", d_dw), ("conv_dw_b", d_dwb), ("conv_ln_g", d_lng), ("conv_ln_b", d_lnb),
                             ("conv_b_in", dbin), ("conv_b_out", dbo)):
                put(nme, j, val)
        else:
            (dz, d_lng, d_lnb, d_ws, d_bst, dbin), lands = _sgu_mid_bwd(
                sv["z"], da, row(full["sgu_ln_g"][j]), row(full["sgu_ln_b"][j]), full["sgu_ws"][j], full["sgu_bs"][j].T,
                tag + "sgu_mid_bwd", carry)
            ng = sgu_ws.shape[1]
            for nme, val in (("sgu_ln_g", d_lng), ("sgu_ln_b", d_lnb), ("sgu_ws", d_ws), ("sgu_bs", d_bst[:, :ng].T),
                             ("sgu_b_in", dbin), ("sgu_b_out", dbo)):
                put(nme, j, val)
        slots.update(zip([key for key, _, _ in pending], lands))
        pending = []
        g_mix_in = _matmul_tn(sv["h1"], dz, tag + "mixer_in_wgrad")
        dx, dn1, dsc1, dsh1 = _inproj_bwd(dz, sv["w_in"], 0, sv["x"], dxmid, row(full["norm1_g"][i]), sc1, tag + "mixer_in_bwd")
        put("norm1_g", i, dn1)
        dmod[i] = jnp.concatenate([dsh1, dsc1, dg1, dsh2, dsc2, dg2], axis=1)
        pending += [((w_in_name, j), g_mix_in, 1), ((w_out_name, j), g_mix_out, 0)]

    last_scatter = _push_start("start_scatter_last", *_scatter_pushes([(g, axis) for _, g, axis in pending]), dx)
    grad_x = dx[None]
    loss = lax.psum(loss_row[0, 0], ("x", "y", "c"))

    small_names = [n for n in _WEIGHTS if n in _SMALL and n != "ada_b"]
    dmod_own = jnp.concatenate(dmod, axis=0)
    packed = _behind(_pack([small_g[n] for n in small_names] + [dmod_own]), [last_scatter], "behind_last_scatter_start")
    packed_all = _gather_all("gather_small_grads", packed)
    summed = _sum_slots(packed_all, "sum_small_grads")
    small_full = dict(zip(small_names, _unpack(summed, [small_g[n].shape for n in small_names])))
    n_small = sum(math.prod(small_g[n].shape) for n in small_names)
    dmod_all = packed_all.reshape(_N_DEV, -1)[:, n_small:n_small + dmod_own.size].reshape(_N_DEV, depth, n_mod * d)
    small_full["ada_b"] = _sum_slots(dmod_all, "sum_ada_b_grad")
    dmod_cols = lax.dynamic_slice_in_dim(jnp.moveaxis(dmod_all, 0, 1), chip * ncols, ncols, 2)
    g_ada_w = _ada_wgrad(c_all.T, dmod_cols, "ada_wgrad")

    slots.update(zip([key for key, _, _ in pending], _push_wait("wait_scatter_last", last_scatter, g_ada_w)))
    partial =[_sum_slot_layers([slots[(n, j)] for j in range(wts[n].shape[0])], "sum_chips_" + n) for n in _BIG]
    other = _swap_sibling("swap_cores", partial)

    outs = {}
    for n, mine_p, theirs_p in zip(_BIG, partial, other):
        outs[n] = _adamw(wts[n], [mine_p, theirs_p], mom_m[n], mom_v[n], "adamw_" + n)
    outs["ada_w"] = _adamw(ada_w, [g_ada_w], m_ada_w, v_ada_w, "adamw_ada_w")
    sm_names = [n for n in _WEIGHTS if n in _SMALL]
    g_loc = [_shard_of(small_full[n], _SMALL[n], chip) for n in sm_names]
    packs = [_pack([src[n] for n in sm_names]) for src in (wts, mom_m, mom_v)]
    sm_out = _adamw(packs[0], [_pack(g_loc)], packs[1], packs[2], "adamw_small")
    shapes = [wts[n].shape for n in sm_names]
    unpacked = [_unpack(o, shapes) for o in sm_out]
    for k, n in enumerate(sm_names):
        outs[n] = tuple(u[k] for u in unpacked)

    result = [loss, grad_x]
    for which in range(4):
        result += [outs[n][which] for n in _WEIGHTS]
    return tuple(result)
```
